```python
import math
import jax, jax.numpy as jnp
from jax import lax
import numpy as np

D_MODEL = 2048
BATCH = 8
SEQ = 2048
DEPTH = 2
DEC_BATCH = 32
DEC_SEQ = 64
PAST_LEN = 1024

CHUNK = 64
N_A = DEPTH // 2
N_B = DEPTH - N_A
D_FF = 5632
RET_HEADS = 8
RET_DK = D_MODEL // RET_HEADS
RET_DV = 2 * D_MODEL // RET_HEADS
MLA_HEADS = 16
QK_NOPE = 128
QK_ROPE = 64
QK_HEAD = QK_NOPE + QK_ROPE
V_HEAD = 128
KV_LORA = 512
Q_LORA = 512
Q_BLOCK = 128
ROPE_BASE = 10000.0
EPS = 1e-6
NEG_INF = -1e30

kernel_name = "yoco_retention_mla_macaron_stream_step"

F32 = jnp.float32


def rmsnorm(x, g):
    xf = x.astype(F32)
    y = xf * lax.rsqrt(jnp.mean(xf * xf, axis=-1, keepdims=True) + EPS)
    return (y * g.astype(F32)).astype(x.dtype)


def rope(x, pos):
    half = x.shape[-1] // 2
    inv = jnp.power(ROPE_BASE, -jnp.arange(half, dtype=F32) / half)
    ang = pos.astype(F32)[:, None] * inv[None, :]
    cos = jnp.cos(ang)[:, None, :]
    sin = jnp.sin(ang)[:, None, :]
    x1 = x[..., :half].astype(F32)
    x2 = x[..., half:].astype(F32)
    return jnp.concatenate([x1 * cos - x2 * sin, x1 * sin + x2 * cos], axis=-1).astype(x.dtype)


def swiglu(x, g, w_in, w_out):
    a, b = jnp.split(rmsnorm(x, g) @ w_in, 2, axis=-1)
    return (jax.nn.silu(a) * b) @ w_out


def retention_block(q, k, v, S, log_gamma):
    L = q.shape[2]
    n = jnp.arange(L, dtype=F32)
    diff = n[:, None] - n[None, :]
    lg = log_gamma[:, None, None]
    decay = jnp.where(diff >= 0, jnp.exp(lg * jnp.maximum(diff, 0.0)), 0.0).astype(q.dtype)
    inner = jnp.einsum('bhqd,bhkd->bhqk', q, k) * decay
    cross = jnp.exp(log_gamma[:, None] * (n + 1.0))[:, :, None].astype(q.dtype)
    o = jnp.einsum('bhqk,bhkv->bhqv', inner, v) + cross * jnp.einsum('bhqd,bhdv->bhqv', q, S)
    kdec = jnp.exp(log_gamma[:, None] * (L - 1.0 - n))[:, :, None].astype(q.dtype)
    S_new = jnp.exp(log_gamma * L)[:, None, None].astype(S.dtype) * S + \
        jnp.einsum('bhkd,bhkv->bhdv', k * kdec, v).astype(S.dtype)
    return o, S_new


def retention(xn, pos, state, w_in, gn_gain, w_out, prompt):
    B, T, _ = xn.shape
    hk, hv = RET_HEADS * RET_DK, RET_HEADS * RET_DV
    q, k, v, g = jnp.split(xn @ w_in, [hk, 2 * hk, 2 * hk + hv], axis=-1)
    q = rope(q.reshape(B, T, RET_HEADS, RET_DK), pos) * (RET_DK ** -0.5)
    k = rope(k.reshape(B, T, RET_HEADS, RET_DK), pos)
    v = v.reshape(B, T, RET_HEADS, RET_DV)
    q, k, v = jnp.swapaxes(q, 1, 2), jnp.swapaxes(k, 1, 2), jnp.swapaxes(v, 1, 2)
    log_gamma = jnp.log1p(-jnp.exp2(-5.0 - jnp.arange(RET_HEADS, dtype=F32)))
    if prompt:
        nC = T // CHUNK

        def to_chunks(t):
            return jnp.moveaxis(t.reshape(B, RET_HEADS, nC, CHUNK, t.shape[-1]), 2, 0)

        def step(S, blk):
            qb, kb, vb = blk
            o_b, S = retention_block(qb, kb, vb, S, log_gamma)
            return S, o_b

        S0 = jnp.zeros((B, RET_HEADS, RET_DK, RET_DV), q.dtype)
        S_fin, o = lax.scan(step, S0, (to_chunks(q), to_chunks(k), to_chunks(v)))
        o = jnp.moveaxis(o, 0, 2).reshape(B, RET_HEADS, T, RET_DV)
    else:
        o, S_fin = retention_block(q, k, v, state, log_gamma)
    of = jnp.swapaxes(o, 1, 2).astype(F32)
    mu = jnp.mean(of, axis=-1, keepdims=True)
    var = jnp.mean(jnp.square(of - mu), axis=-1, keepdims=True)
    of = ((of - mu) * lax.rsqrt(var + EPS)).reshape(B, T, hv) * gn_gain.astype(F32)
    y = jax.nn.silu(g) * of.astype(g.dtype)
    return y @ w_out, S_fin


def shared_latent(h, pos, kv_norm, w_dkv, kv_lat_norm):
    ckr = rmsnorm(h, kv_norm) @ w_dkv
    c = rmsnorm(ckr[..., :KV_LORA], kv_lat_norm)
    kr = rope(ckr[..., KV_LORA:][:, :, None, :], pos)[:, :, 0, :]
    return c, kr


def mla_keys_values(c, kr, w_ukv, k_norm):
    B, T, _ = c.shape
    kv = (c @ w_ukv).reshape(B, T, MLA_HEADS, QK_NOPE + V_HEAD)
    k = jnp.concatenate([kv[..., :QK_NOPE],
                         jnp.broadcast_to(kr[:, :, None, :], (B, T, MLA_HEADS, QK_ROPE))], axis=-1)
    return rmsnorm(k, k_norm), kv[..., QK_NOPE:]


def mla_queries(hn, pos, w_dq, q_lat_norm, w_uq, q_norm):
    B, T, _ = hn.shape
    q = (rmsnorm(hn @ w_dq, q_lat_norm) @ w_uq).reshape(B, T, MLA_HEADS, QK_HEAD)
    q = jnp.concatenate([q[..., :QK_NOPE], rope(q[..., QK_NOPE:], pos)], axis=-1)
    return rmsnorm(q, q_norm) * (QK_HEAD ** -0.5)


def attend(q, k, v, mask):
    s = jnp.einsum('bqhd,bkhd->bhqk', q, k).astype(F32)
    if mask is not None:
        s = jnp.where(mask[None, None], s, NEG_INF)
    p = jax.nn.softmax(s, axis=-1).astype(v.dtype)
    return jnp.einsum('bhqk,bkhd->bqhd', p, v)


def prompt_attention(q, k, v):
    B, T, _, _ = q.shape
    nq = T // Q_BLOCK
    kchunk = jnp.arange(T) // CHUNK
    qb = jnp.moveaxis(q.reshape(B, nq, Q_BLOCK, MLA_HEADS, QK_HEAD), 1, 0)
    starts = jnp.arange(nq) * Q_BLOCK

    def one(args):
        qi, s0 = args
        qchunk = (s0 + jnp.arange(Q_BLOCK)) // CHUNK
        return attend(qi, k, v, kchunk[None, :] <= qchunk[:, None])

    o = lax.map(one, (qb, starts))
    return jnp.moveaxis(o, 0, 1).reshape(B, T, MLA_HEADS * V_HEAD)


def run_trunk(x, pos, ret_state, ckv_past, krope_past, prompt, p):
    h = x
    new_ret = []
    c_new = kr_new = k_sh = v_sh = None
    for l in range(DEPTH):
        h = h + 0.5 * swiglu(h, p['ffn1_norm'][l], p['ffn1_w_in'][l], p['ffn1_w_out'][l])
        hn = rmsnorm(h, p['mix_norm'][l])
        if l < N_A:
            o, S = retention(hn, pos, None if prompt else ret_state[l],
                             p['ret_w_in'][l], p['ret_gn_gain'][l], p['ret_w_out'][l], prompt)
            new_ret.append(S)
        else:
            j = l - N_A
            q = mla_queries(hn, pos, p['w_dq'][j], p['q_lat_norm'][j], p['w_uq'][j], p['q_norm'][j])
            if prompt:
                o = prompt_attention(q, k_sh, v_sh)
            else:
                B, T = q.shape[0], q.shape[1]
                o = attend(q, k_sh, v_sh, None).reshape(B, T, MLA_HEADS * V_HEAD)
            o = o @ p['w_o'][j]
        h = h + o
        h = h + 0.5 * swiglu(h, p['ffn2_norm'][l], p['ffn2_w_in'][l], p['ffn2_w_out'][l])
        if l == N_A - 1:
            c_new, kr_new = shared_latent(h, pos, p['kv_norm'], p['w_dkv'], p['kv_lat_norm'])
            if prompt:
                c_all, kr_all = c_new, kr_new
            else:
                c_all = jnp.concatenate([ckv_past.astype(c_new.dtype), c_new], axis=1)
                kr_all = jnp.concatenate([krope_past.astype(kr_new.dtype), kr_new], axis=1)
            k_sh, v_sh = mla_keys_values(c_all, kr_all, p['w_ukv'], p['k_norm'])
    return h, jnp.stack(new_ret), c_new, kr_new


def setup_inputs(seed: int = 0) -> dict:
    key = jax.random.key(seed)
    ks = iter(jax.random.split(key, 40))

    def nrm(shape, scale):
        return jax.random.normal(next(ks), shape, F32) * scale

    def gain(shape):
        return 1.0 + 0.02 * jax.random.normal(next(ks), shape, F32)

    D = D_MODEL
    ret_in = 2 * RET_HEADS * RET_DK + 2 * RET_HEADS * RET_DV
    return {
        "x_prompt": nrm((BATCH, SEQ, D), 1.0),
        "x_sample": nrm((DEC_BATCH, DEC_SEQ, D), 1.0),
        "state_ret": nrm((N_A, DEC_BATCH, RET_HEADS, RET_DK, RET_DV), 1.0),
        "cache_ckv": nrm((DEC_BATCH, PAST_LEN, KV_LORA), 1.0),
        "cache_krope": nrm((DEC_BATCH, PAST_LEN, QK_ROPE), 1.0),
        "ffn1_norm": gain((DEPTH, D)),
        "ffn1_w_in": nrm((DEPTH, D, 2 * D_FF), D ** -0.5),
        "ffn1_w_out": nrm((DEPTH, D_FF, D), D_FF ** -0.5),
        "mix_norm": gain((DEPTH, D)),
        "ffn2_norm": gain((DEPTH, D)),
        "ffn2_w_in": nrm((DEPTH, D, 2 * D_FF), D ** -0.5),
        "ffn2_w_out": nrm((DEPTH, D_FF, D), D_FF ** -0.5),
        "ret_w_in": nrm((N_A, D, ret_in), D ** -0.5),
        "ret_gn_gain": gain((N_A, RET_HEADS * RET_DV)),
        "ret_w_out": nrm((N_A, RET_HEADS * RET_DV, D), (RET_HEADS * RET_DV) ** -0.5),
        "kv_norm": gain((D,)),
        "w_dkv": nrm((D, KV_LORA + QK_ROPE), D ** -0.5),
        "kv_lat_norm": gain((KV_LORA,)),
        "w_ukv": nrm((KV_LORA, MLA_HEADS * (QK_NOPE + V_HEAD)), KV_LORA ** -0.5),
        "k_norm": gain((QK_HEAD,)),
        "w_dq": nrm((N_B, D, Q_LORA), D ** -0.5),
        "q_lat_norm": gain((N_B, Q_LORA)),
        "w_uq": nrm((N_B, Q_LORA, MLA_HEADS * QK_HEAD), Q_LORA ** -0.5),
        "q_norm": gain((N_B, QK_HEAD)),
        "w_o": nrm((N_B, MLA_HEADS * V_HEAD, D), (MLA_HEADS * V_HEAD) ** -0.5),
    }


def reference(x_prompt, x_sample, state_ret, cache_ckv, cache_krope,
              ffn1_norm, ffn1_w_in, ffn1_w_out, mix_norm, ffn2_norm, ffn2_w_in, ffn2_w_out,
              ret_w_in, ret_gn_gain, ret_w_out,
              kv_norm, w_dkv, kv_lat_norm, w_ukv, k_norm,
              w_dq, q_lat_norm, w_uq, q_norm, w_o):
    p = dict(ffn1_norm=ffn1_norm, ffn1_w_in=ffn1_w_in, ffn1_w_out=ffn1_w_out,
             mix_norm=mix_norm, ffn2_norm=ffn2_norm, ffn2_w_in=ffn2_w_in, ffn2_w_out=ffn2_w_out,
             ret_w_in=ret_w_in, ret_gn_gain=ret_gn_gain, ret_w_out=ret_w_out,
             kv_norm=kv_norm, w_dkv=w_dkv, kv_lat_norm=kv_lat_norm, w_ukv=w_ukv, k_norm=k_norm,
             w_dq=w_dq, q_lat_norm=q_lat_norm, w_uq=w_uq, q_norm=q_norm, w_o=w_o)
    pos_prompt = jnp.arange(x_prompt.shape[1])
    past = cache_ckv.shape[1]
    pos_sample = past + jnp.arange(x_sample.shape[1])
    y_prompt, ret_p, ckv_p, kr_p = run_trunk(x_prompt, pos_prompt, None, None, None, True, p)
    y_sample, ret_s, ckv_s, kr_s = run_trunk(x_sample, pos_sample, state_ret, cache_ckv,
                                             cache_krope, False, p)
    return (y_prompt, y_sample, ret_p, ckv_p, kr_p, ret_s, ckv_s, kr_s)
```

```python
import functools

import jax
import jax.numpy as jnp
from jax import lax
from jax.experimental import pallas as pl
from jax.experimental.pallas import tpu as pltpu

F32 = jnp.float32
BF16 = jnp.bfloat16

D_MODEL = 2048
CHUNK = 64
D_FF = 5632
RET_HEADS = 8
RET_DK = D_MODEL // RET_HEADS
RET_DV = 2 * D_MODEL // RET_HEADS
MLA_HEADS = 16
QK_NOPE = 128
QK_ROPE = 64
QK_HEAD = QK_NOPE + QK_ROPE
V_HEAD = 128
KV_LORA = 512
Q_LORA = 512
ROPE_BASE = 10000.0
EPS = 1e-6
NEG_INF = -1e30

VMEM_LIMIT_BYTES = 56 * 1024 * 1024


def _params(*semantics):
    return pltpu.CompilerParams(dimension_semantics=semantics,
                                vmem_limit_bytes=VMEM_LIMIT_BYTES)


def _rms_scale(x):
    return lax.rsqrt(jnp.mean(x * x, axis=-1, keepdims=True) + EPS)


def _dot(a, b):
    return jnp.dot(a, b, preferred_element_type=F32)


def _dot_nt(a, b):
    return lax.dot_general(a, b, (((1,), (1,)), ((), ())), preferred_element_type=F32)


def _dot_tn(a, b):
    return lax.dot_general(a, b, (((0,), (0,)), ((), ())), preferred_element_type=F32)


def _silu(x):
    return x * jax.nn.sigmoid(x)


def _ffn_kernel(x_ref, g_ref, wa_ref, wb_ref, wo_ref, o_ref, xn_ref):
    j = pl.program_id(1)

    @pl.when(j == 0)
    def _():
        x = x_ref[...]
        xn_ref[...] = (x * _rms_scale(x) * g_ref[...]).astype(BF16)
        o_ref[...] = x

    xn = xn_ref[...]
    a = _dot(xn, wa_ref[...])
    b = _dot(xn, wb_ref[...])
    h = (_silu(a) * (0.5 * b)).astype(BF16)
    o_ref[...] += _dot(h, wo_ref[...])


def _ffn(x, g, w_in, w_out, *, tm, tf):
    m, d = x.shape
    nf = D_FF // tf
    return pl.pallas_call(
        _ffn_kernel,
        out_shape=jax.ShapeDtypeStruct((m, d), F32),
        grid=(m // tm, nf),
        in_specs=[
            pl.BlockSpec((tm, d), lambda i, j: (i, 0)),
            pl.BlockSpec((1, d), lambda i, j: (0, 0)),
            pl.BlockSpec((d, tf), lambda i, j: (0, j)),
            pl.BlockSpec((d, tf), lambda i, j: (0, nf + j)),
            pl.BlockSpec((tf, d), lambda i, j: (j, 0)),
        ],
        out_specs=pl.BlockSpec((tm, d), lambda i, j: (i, 0)),
        scratch_shapes=[pltpu.VMEM((tm, d), BF16)],
        compiler_params=_params("parallel", "arbitrary"),
        name="ffn",
    )(x, g.reshape(1, d), w_in, w_in, w_out)


def _ret_in_kernel(x_ref, g_ref, w_ref, cos_ref, sin_ref, o_ref, xn_ref, *, rope_blocks, heads_per_block):
    j = pl.program_id(1)

    @pl.when(j == 0)
    def _():
        x = x_ref[...]
        xn_ref[...] = (x * _rms_scale(x) * g_ref[...]).astype(BF16)

    y = _dot(xn_ref[...], w_ref[...])

    @pl.when(j < 2 * rope_blocks)
    def _():
        scale = jnp.where(j < rope_blocks, RET_DK ** -0.5, 1.0).astype(F32)
        cos = cos_ref[...]
        sin = sin_ref[...]
        half = RET_DK // 2
        for h in range(heads_per_block):
            lo = h * RET_DK
            x1 = y[:, lo:lo + half]
            x2 = y[:, lo + half:lo + RET_DK]
            o_ref[:, lo:lo + half] = ((x1 * cos - x2 * sin) * scale).astype(BF16)
            o_ref[:, lo + half:lo + RET_DK] = ((x1 * sin + x2 * cos) * scale).astype(BF16)

    @pl.when(j >= 2 * rope_blocks)
    def _():
        o_ref[...] = y.astype(BF16)


def _ret_in(x, g, w, cos, sin, *, tm, tn):
    m, d = x.shape
    n = w.shape[1]
    tab_blocks = cos.shape[0] // tm
    kern = functools.partial(_ret_in_kernel, rope_blocks=(RET_HEADS * RET_DK) // tn,
                             heads_per_block=tn // RET_DK)
    return pl.pallas_call(
        kern,
        out_shape=jax.ShapeDtypeStruct((m, n), BF16),
        grid=(m // tm, n // tn),
        in_specs=[
            pl.BlockSpec((tm, d), lambda i, j: (i, 0)),
            pl.BlockSpec((1, d), lambda i, j: (0, 0)),
            pl.BlockSpec((d, tn), lambda i, j: (0, j)),
            pl.BlockSpec((tm, RET_DK // 2), lambda i, j: (i % tab_blocks, 0)),
            pl.BlockSpec((tm, RET_DK // 2), lambda i, j: (i % tab_blocks, 0)),
        ],
        out_specs=pl.BlockSpec((tm, tn), lambda i, j: (i, j)),
        scratch_shapes=[pltpu.VMEM((tm, d), BF16)],
        compiler_params=_params("parallel", "arbitrary"),
        name="ret_in",
    )(x, g.reshape(1, d), w, cos, sin)


def _ret_core_kernel(lg_ref, q_ref, k_ref, v_ref, g_ref, gain_ref, *rest, chunk, has_state):
    if has_state:
        s0_ref, y_ref, s_ref, state = rest
    else:
        y_ref, s_ref, state = rest
    h = pl.program_id(1)
    c = pl.program_id(2)
    lg = lg_ref[h]

    @pl.when(c == 0)
    def _():
        if has_state:
            state[...] = s0_ref[0, 0]
        else:
            state[...] = jnp.zeros_like(state)

    q = q_ref[...]
    k = k_ref[...]
    v = v_ref[...]
    row = lax.broadcasted_iota(jnp.int32, (chunk, chunk), 0)
    col = lax.broadcasted_iota(jnp.int32, (chunk, chunk), 1)
    diff = (row - col).astype(F32)
    decay = jnp.where(diff >= 0, jnp.exp(lg * jnp.maximum(diff, 0.0)), 0.0)
    inner = (_dot_nt(q, k) * decay).astype(BF16)
    n = lax.broadcasted_iota(jnp.int32, (chunk, 1), 0).astype(F32)
    cross = jnp.exp(lg * (n + 1.0))
    s_old = state[...]
    o = _dot(inner, v) + cross * _dot(q, s_old.astype(BF16))
    kdec = jnp.exp(lg * (chunk - 1.0 - n))
    kd = (k.astype(F32) * kdec).astype(BF16)
    carry = jnp.exp(lg * jnp.full((1, RET_DV), float(chunk), F32))
    s_new = carry * s_old + _dot_tn(kd, v)
    state[...] = s_new

    mu = jnp.mean(o, axis=-1, keepdims=True)
    dev = o - mu
    var = jnp.mean(dev * dev, axis=-1, keepdims=True)
    of = dev * lax.rsqrt(var + EPS) * gain_ref[...]
    y_ref[...] = (_silu(g_ref[...].astype(F32)) * of).astype(BF16)

    @pl.when(c == pl.num_programs(2) - 1)
    def _():
        s_ref[0, 0] = s_new


def _ret_core(qkvg, log_gamma, gain, state, *, batch, seq, chunk):
    m = qkvg.shape[0]
    nc = seq // chunk
    hk = RET_HEADS * RET_DK
    hv = RET_HEADS * RET_DV
    kblk = hk // RET_DK
    vblk = 2 * hk // RET_DV
    gblk = vblk + hv // RET_DV
    has_state = state is not None
    in_specs = [
        pl.BlockSpec(memory_space=pltpu.SMEM),
        pl.BlockSpec((chunk, RET_DK), lambda b, h, c: (b * nc + c, h)),
        pl.BlockSpec((chunk, RET_DK), lambda b, h, c: (b * nc + c, kblk + h)),
        pl.BlockSpec((chunk, RET_DV), lambda b, h, c: (b * nc + c, vblk + h)),
        pl.BlockSpec((chunk, RET_DV), lambda b, h, c: (b * nc + c, gblk + h)),
        pl.BlockSpec((1, RET_DV), lambda b, h, c: (0, h)),
    ]
    args = [log_gamma, qkvg, qkvg, qkvg, qkvg, gain.reshape(1, hv)]
    if has_state:
        in_specs.append(pl.BlockSpec((1, 1, RET_DK, RET_DV), lambda b, h, c: (b, h, 0, 0)))
        args.append(state)
    kern = functools.partial(_ret_core_kernel, chunk=chunk, has_state=has_state)
    return pl.pallas_call(
        kern,
        out_shape=(jax.ShapeDtypeStruct((m, hv), BF16),
                   jax.ShapeDtypeStruct((batch, RET_HEADS, RET_DK, RET_DV), F32)),
        grid=(batch, RET_HEADS, nc),
        in_specs=in_specs,
        out_specs=(pl.BlockSpec((chunk, RET_DV), lambda b, h, c: (b * nc + c, h)),
                   pl.BlockSpec((1, 1, RET_DK, RET_DV), lambda b, h, c: (b, h, 0, 0))),
        scratch_shapes=[pltpu.VMEM((RET_DK, RET_DV), F32)],
        compiler_params=_params("parallel", "parallel", "arbitrary"),
        name="ret_core",
    )(*args)


def _mm_res_kernel(a_ref, w_ref, r_ref, o_ref):
    o_ref[...] = r_ref[...] + _dot(a_ref[...], w_ref[...])


def _mm_res(a, w, res, *, tm, tn):
    m, k = a.shape
    n = w.shape[1]
    return pl.pallas_call(
        _mm_res_kernel,
        out_shape=jax.ShapeDtypeStruct((m, n), F32),
        grid=(m // tm, n // tn),
        in_specs=[
            pl.BlockSpec((tm, k), lambda i, j: (i, 0)),
            pl.BlockSpec((k, tn), lambda i, j: (0, j)),
            pl.BlockSpec((tm, tn), lambda i, j: (i, j)),
        ],
        out_specs=pl.BlockSpec((tm, tn), lambda i, j: (i, j)),
        compiler_params=_params("parallel", "arbitrary"),
        name="mm_res",
    )(a, w, res)


def _latent_kernel(x_ref, g_ref, w_ref, gl_ref, cos_ref, sin_ref, c_ref, kr_ref):
    x = x_ref[...]
    xn = (x * _rms_scale(x) * g_ref[...]).astype(BF16)
    y = _dot(xn, w_ref[...])
    cl = y[:, :KV_LORA]
    c_ref[...] = cl * _rms_scale(cl) * gl_ref[...]
    half = QK_ROPE // 2
    x1 = y[:, KV_LORA:KV_LORA + half]
    x2 = y[:, KV_LORA + half:KV_LORA + QK_ROPE]
    cos = cos_ref[...]
    sin = sin_ref[...]
    kr_ref[...] = jnp.concatenate([x1 * cos - x2 * sin, x1 * sin + x2 * cos], axis=-1)


def _latent(x, g, w, g_lat, cos, sin, *, tm):
    m, d = x.shape
    n = w.shape[1]
    half = QK_ROPE // 2
    tab_blocks = cos.shape[0] // tm
    return pl.pallas_call(
        _latent_kernel,
        out_shape=(jax.ShapeDtypeStruct((m, KV_LORA), F32),
                   jax.ShapeDtypeStruct((m, QK_ROPE), F32)),
        grid=(m // tm,),
        in_specs=[
            pl.BlockSpec((tm, d), lambda i: (i, 0)),
            pl.BlockSpec((1, d), lambda i: (0, 0)),
            pl.BlockSpec((d, n), lambda i: (0, 0)),
            pl.BlockSpec((1, KV_LORA), lambda i: (0, 0)),
            pl.BlockSpec((tm, half), lambda i: (i % tab_blocks, 0)),
            pl.BlockSpec((tm, half), lambda i: (i % tab_blocks, 0)),
        ],
        out_specs=(pl.BlockSpec((tm, KV_LORA), lambda i: (i, 0)),
                   pl.BlockSpec((tm, QK_ROPE), lambda i: (i, 0))),
        compiler_params=_params("parallel"),
        name="latent",
    )(x, g.reshape(1, d), w, g_lat.reshape(1, KV_LORA), cos, sin)


def _kv_up_kernel(c_ref, kr_ref, wk_ref, wv_ref, gk_ref, k_ref, v_ref):
    c = c_ref[...].astype(BF16)
    kn = _dot(c, wk_ref[...])
    v_ref[...] = _dot(c, wv_ref[...]).astype(BF16)
    kr = kr_ref[...]
    ss_rope = jnp.sum(kr * kr, axis=-1, keepdims=True)
    g_nope = gk_ref[:, :QK_NOPE]
    g_rope = gk_ref[:, QK_NOPE:]
    for h in range(MLA_HEADS):
        knh = kn[:, h * QK_NOPE:(h + 1) * QK_NOPE]
        ss = jnp.sum(knh * knh, axis=-1, keepdims=True) + ss_rope
        r = lax.rsqrt(ss * (1.0 / QK_HEAD) + EPS)
        k_ref[h, :, :QK_NOPE] = (knh * r * g_nope).astype(BF16)
        k_ref[h, :, QK_NOPE:] = (kr * r * g_rope).astype(BF16)


def _kv_up(c, kr, wk, wv, gk, *, tm):
    m = c.shape[0]
    return pl.pallas_call(
        _kv_up_kernel,
        out_shape=(jax.ShapeDtypeStruct((MLA_HEADS, m, QK_HEAD), BF16),
                   jax.ShapeDtypeStruct((m, MLA_HEADS * V_HEAD), BF16)),
        grid=(m // tm,),
        in_specs=[
            pl.BlockSpec((tm, KV_LORA), lambda i: (i, 0)),
            pl.BlockSpec((tm, QK_ROPE), lambda i: (i, 0)),
            pl.BlockSpec(wk.shape, lambda i: (0, 0)),
            pl.BlockSpec(wv.shape, lambda i: (0, 0)),
            pl.BlockSpec((1, QK_HEAD), lambda i: (0, 0)),
        ],
        out_specs=(pl.BlockSpec((MLA_HEADS, tm, QK_HEAD), lambda i: (0, i, 0)),
                   pl.BlockSpec((tm, MLA_HEADS * V_HEAD), lambda i: (i, 0))),
        compiler_params=_params("parallel"),
        name="kv_up",
    )(c, kr, wk, wv, gk.reshape(1, QK_HEAD))


def _q_proj_kernel(x_ref, g_ref, wdq_ref, gl_ref, wuq_ref, cos_ref, sin_ref, gq_ref, q_ref):
    x = x_ref[...]
    hn = (x * _rms_scale(x) * g_ref[...]).astype(BF16)
    ql = _dot(hn, wdq_ref[...])
    qn = (ql * _rms_scale(ql) * gl_ref[...]).astype(BF16)
    q = _dot(qn, wuq_ref[...])
    nope_w = MLA_HEADS * QK_NOPE
    half = QK_ROPE // 2
    half_w = MLA_HEADS * half
    x1 = q[:, nope_w:nope_w + half_w]
    x2 = q[:, nope_w + half_w:]
    cos = cos_ref[...]
    sin = sin_ref[...]
    r1 = x1 * cos - x2 * sin
    r2 = x1 * sin + x2 * cos
    g_nope = gq_ref[:, :QK_NOPE]
    g_rope = gq_ref[:, QK_NOPE:]
    for h in range(MLA_HEADS):
        nh = q[:, h * QK_NOPE:(h + 1) * QK_NOPE]
        rp = jnp.concatenate([r1[:, h * half:(h + 1) * half], r2[:, h * half:(h + 1) * half]], axis=-1)
        ss = jnp.sum(nh * nh, axis=-1, keepdims=True) + jnp.sum(rp * rp, axis=-1, keepdims=True)
        r = lax.rsqrt(ss * (1.0 / QK_HEAD) + EPS) * (QK_HEAD ** -0.5)
        q_ref[h, :, :QK_NOPE] = (nh * r * g_nope).astype(BF16)
        q_ref[h, :, QK_NOPE:] = (rp * r * g_rope).astype(BF16)


def _q_proj(x, g, wdq, g_lat, wuq, cos, sin, gq, *, tm):
    m, d = x.shape
    tab_blocks = cos.shape[0] // tm
    tw = cos.shape[1]
    return pl.pallas_call(
        _q_proj_kernel,
        out_shape=jax.ShapeDtypeStruct((MLA_HEADS, m, QK_HEAD), BF16),
        grid=(m // tm,),
        in_specs=[
            pl.BlockSpec((tm, d), lambda i: (i, 0)),
            pl.BlockSpec((1, d), lambda i: (0, 0)),
            pl.BlockSpec(wdq.shape, lambda i: (0, 0)),
            pl.BlockSpec((1, Q_LORA), lambda i: (0, 0)),
            pl.BlockSpec(wuq.shape, lambda i: (0, 0)),
            pl.BlockSpec((tm, tw), lambda i: (i % tab_blocks, 0)),
            pl.BlockSpec((tm, tw), lambda i: (i % tab_blocks, 0)),
            pl.BlockSpec((1, QK_HEAD), lambda i: (0, 0)),
        ],
        out_specs=pl.BlockSpec((MLA_HEADS, tm, QK_HEAD), lambda i: (0, i, 0)),
        compiler_params=_params("parallel"),
        name="q_proj",
    )(x, g.reshape(1, d), wdq, g_lat.reshape(1, Q_LORA), wuq, cos, sin, gq.reshape(1, QK_HEAD))


def _attn_kernel(q_ref, k_ref, v_ref, o_ref, *, heads, tq, causal):
    t_q = q_ref.shape[1]
    t_kv = k_ref.shape[1]
    for h in range(heads):
        for qi in range(t_q // tq):
            q = q_ref[h, qi * tq:(qi + 1) * tq, :]
            if causal:
                lo = qi * tq
                row = lax.broadcasted_iota(jnp.int32, (tq, tq), 0) // CHUNK
                col = lax.broadcasted_iota(jnp.int32, (tq, tq), 1) // CHUNK
                s_d = jnp.where(col <= row, _dot_nt(q, k_ref[h, lo:lo + tq, :]), NEG_INF)
                v_d = v_ref[lo:lo + tq, h * V_HEAD:(h + 1) * V_HEAD]
                m = jnp.max(s_d, axis=-1, keepdims=True)
                if qi > 0:
                    s_p = _dot_nt(q, k_ref[h, :lo, :])
                    m = jnp.maximum(m, jnp.max(s_p, axis=-1, keepdims=True))
                    p_p = jnp.exp(s_p - m)
                p_d = jnp.exp(s_d - m)
                l = jnp.sum(p_d, axis=-1, keepdims=True)
                acc = _dot(p_d.astype(BF16), v_d)
                if qi > 0:
                    l = l + jnp.sum(p_p, axis=-1, keepdims=True)
                    acc = acc + _dot(p_p.astype(BF16), v_ref[:lo, h * V_HEAD:(h + 1) * V_HEAD])
            else:
                s = _dot_nt(q, k_ref[h])
                m = jnp.max(s, axis=-1, keepdims=True)
                p = jnp.exp(s - m)
                l = jnp.sum(p, axis=-1, keepdims=True)
                acc = _dot(p.astype(BF16), v_ref[:, h * V_HEAD:(h + 1) * V_HEAD])
            o_ref[qi * tq:(qi + 1) * tq, h * V_HEAD:(h + 1) * V_HEAD] = (acc / l).astype(BF16)
    del t_kv


def _attention(q, k, v, *, batch, t_q, t_kv, heads, tq, causal):
    hg = MLA_HEADS // heads
    kern = functools.partial(_attn_kernel, heads=heads, tq=tq, causal=causal)
    return pl.pallas_call(
        kern,
        out_shape=jax.ShapeDtypeStruct((batch * t_q, MLA_HEADS * V_HEAD), BF16),
        grid=(batch, hg),
        in_specs=[
            pl.BlockSpec((heads, t_q, QK_HEAD), lambda b, g: (g, b, 0)),
            pl.BlockSpec((heads, t_kv, QK_HEAD), lambda b, g: (g, b, 0)),
            pl.BlockSpec((t_kv, heads * V_HEAD), lambda b, g: (b, g)),
        ],
        out_specs=pl.BlockSpec((t_q, heads * V_HEAD), lambda b, g: (b, g)),
        compiler_params=_params("parallel", "parallel"),
        name="attention",
    )(q, k, v)


def _rope_tables(pos, half, reps, rows):
    inv = jnp.power(ROPE_BASE, -jnp.arange(half, dtype=F32) / half)
    ang = pos.astype(F32)[:, None] * inv[None, :]
    cos = jnp.tile(jnp.cos(ang), (max(rows // pos.shape[0], 1), reps))
    sin = jnp.tile(jnp.sin(ang), (max(rows // pos.shape[0], 1), reps))
    return cos, sin


def _trunk(x, pos, ret_state, ckv_past, krope_past, p, *, causal):
    batch, seq, d = x.shape
    m = batch * seq
    tm = 1024
    h = x.reshape(m, d)

    h = _ffn(h, p["ffn1_norm"][0], p["ffn1_w_in"][0], p["ffn1_w_out"][0], tm=512, tf=512)
    cos, sin = _rope_tables(pos, RET_DK // 2, 1, tm)
    qkvg = _ret_in(h, p["mix_norm"][0], p["ret_w_in"][0], cos, sin, tm=tm, tn=512)
    log_gamma = jnp.log1p(-jnp.exp2(-5.0 - jnp.arange(RET_HEADS, dtype=F32)))
    chunk = min(seq, 256)
    y, s_fin = _ret_core(qkvg, log_gamma, p["ret_gn_gain"][0],
                         None if ret_state is None else ret_state[0],
                         batch=batch, seq=seq, chunk=chunk)
    h = _mm_res(y, p["ret_w_out"][0], h, tm=tm, tn=512)
    h = _ffn(h, p["ffn2_norm"][0], p["ffn2_w_in"][0], p["ffn2_w_out"][0], tm=512, tf=512)

    cos, sin = _rope_tables(pos, QK_ROPE // 2, 1, 512)
    c_new, kr_new = _latent(h, p["kv_norm"], p["w_dkv"], p["kv_lat_norm"], cos, sin, tm=512)
    if ckv_past is None:
        c_all, kr_all, t_kv = c_new, kr_new, seq
    else:
        t_kv = ckv_past.shape[1] + seq
        c_all = jnp.concatenate([ckv_past, c_new.reshape(batch, seq, KV_LORA)], axis=1).reshape(batch * t_kv, KV_LORA)
        kr_all = jnp.concatenate([krope_past, kr_new.reshape(batch, seq, QK_ROPE)], axis=1).reshape(batch * t_kv, QK_ROPE)
    k_sh, v_sh = _kv_up(c_all, kr_all, p["w_uk"], p["w_uv"], p["k_norm"], tm=512 if t_kv % 512 == 0 else t_kv)

    h = _ffn(h, p["ffn1_norm"][1], p["ffn1_w_in"][1], p["ffn1_w_out"][1], tm=512, tf=512)
    cos, sin = _rope_tables(pos, QK_ROPE // 2, MLA_HEADS, 512)
    q = _q_proj(h, p["mix_norm"][1], p["w_dq"][0], p["q_lat_norm"][0], p["w_uq"][0], cos, sin,
                p["q_norm"][0], tm=512)
    if causal:
        o = _attention(q, k_sh, v_sh, batch=batch, t_q=seq, t_kv=t_kv, heads=1, tq=512, causal=True)
    else:
        o = _attention(q, k_sh, v_sh, batch=batch, t_q=seq, t_kv=t_kv, heads=MLA_HEADS, tq=seq, causal=False)
    h = _mm_res(o, p["w_o"][0], h, tm=tm, tn=512)
    h = _ffn(h, p["ffn2_norm"][1], p["ffn2_w_in"][1], p["ffn2_w_out"][1], tm=512, tf=512)

    return (h.reshape(batch, seq, d), s_fin[None],
            c_new.reshape(batch, seq, KV_LORA), kr_new.reshape(batch, seq, QK_ROPE))


def kernel(x_prompt, x_sample, state_ret, cache_ckv, cache_krope, ffn1_norm, ffn1_w_in, ffn1_w_out, mix_norm, ffn2_norm, ffn2_w_in, ffn2_w_out, ret_w_in, ret_gn_gain, ret_w_out, kv_norm, w_dkv, kv_lat_norm, w_ukv, k_norm, w_dq, q_lat_norm, w_uq, q_norm, w_o):
    bf = lambda w: w.astype(BF16)
    w_ukv_h = w_ukv.reshape(KV_LORA, MLA_HEADS, QK_NOPE + V_HEAD)
    w_uk = w_ukv_h[:, :, :QK_NOPE].reshape(KV_LORA, MLA_HEADS * QK_NOPE)
    w_uv = w_ukv_h[:, :, QK_NOPE:].reshape(KV_LORA, MLA_HEADS * V_HEAD)
    half = QK_ROPE // 2
    w_uq_h = w_uq.reshape(w_uq.shape[0], Q_LORA, MLA_HEADS, QK_HEAD)
    w_uq_p = jnp.concatenate([
        w_uq_h[..., :QK_NOPE].reshape(w_uq.shape[0], Q_LORA, MLA_HEADS * QK_NOPE),
        w_uq_h[..., QK_NOPE:QK_NOPE + half].reshape(w_uq.shape[0], Q_LORA, MLA_HEADS * half),
        w_uq_h[..., QK_NOPE + half:].reshape(w_uq.shape[0], Q_LORA, MLA_HEADS * half)], axis=-1)
    p = dict(ffn1_norm=ffn1_norm, ffn1_w_in=bf(ffn1_w_in), ffn1_w_out=bf(ffn1_w_out),
             mix_norm=mix_norm, ffn2_norm=ffn2_norm, ffn2_w_in=bf(ffn2_w_in), ffn2_w_out=bf(ffn2_w_out),
             ret_w_in=bf(ret_w_in), ret_gn_gain=ret_gn_gain, ret_w_out=bf(ret_w_out),
             kv_norm=kv_norm, w_dkv=bf(w_dkv), kv_lat_norm=kv_lat_norm,
             w_uk=bf(w_uk), w_uv=bf(w_uv), k_norm=k_norm,
             w_dq=bf(w_dq), q_lat_norm=q_lat_norm, w_uq=bf(w_uq_p), q_norm=q_norm, w_o=bf(w_o))
    pos_prompt = jnp.arange(x_prompt.shape[1])
    past = cache_ckv.shape[1]
    pos_sample = past + jnp.arange(x_sample.shape[1])
    y_p, ret_p, ckv_p, kr_p = _trunk(x_prompt, pos_prompt, None, None, None, p, causal=True)
    y_s, ret_s, ckv_s, kr_s = _trunk(x_sample, pos_sample, state_ret, cache_ckv, cache_krope, p, causal=False)
    return (y_p, y_s, ret_p, ckv_p, kr_p, ret_s, ckv_s, kr_s)
```

```python
import functools

import jax
import jax.numpy as jnp
from jax import lax
from jax.experimental import pallas as pl
from jax.experimental.pallas import tpu as pltpu

F32 = jnp.float32
BF16 = jnp.bfloat16

D_MODEL = 2048
CHUNK = 64
D_FF = 5632
RET_HEADS = 8
RET_DK = D_MODEL // RET_HEADS
RET_DV = 2 * D_MODEL // RET_HEADS
MLA_HEADS = 16
QK_NOPE = 128
QK_ROPE = 64
QK_HEAD = QK_NOPE + QK_ROPE
V_HEAD = 128
KV_LORA = 512
Q_LORA = 512
ROPE_BASE = 10000.0
EPS = 1e-6
NEG_INF = -1e30

VMEM_LIMIT_BYTES = 60 * 1024 * 1024

FFN_TM = 1024
FFN_TF = 512


def _params(*semantics):
    return pltpu.CompilerParams(dimension_semantics=semantics,
                                vmem_limit_bytes=VMEM_LIMIT_BYTES)


def _rms_scale(x):
    return lax.rsqrt(jnp.mean(x * x, axis=-1, keepdims=True) + EPS)


def _dot(a, b):
    return jnp.dot(a, b, preferred_element_type=F32)


def _dot_nt(a, b):
    return lax.dot_general(a, b, (((1,), (1,)), ((), ())), preferred_element_type=F32)


def _dot_tn(a, b):
    return lax.dot_general(a, b, (((0,), (0,)), ((), ())), preferred_element_type=F32)


def _silu(x):
    return x * jax.nn.sigmoid(x)


def _ffn_kernel(x_ref, g_ref, wa_ref, wb_ref, wo_ref, o_ref, xn_ref):
    j = pl.program_id(1)

    @pl.when(j == 0)
    def _():
        x = x_ref[...]
        xn_ref[...] = (x * _rms_scale(x) * g_ref[...]).astype(BF16)
        o_ref[...] = x

    xn = xn_ref[...]
    a = _dot(xn, wa_ref[...])
    b = _dot(xn, wb_ref[...])
    h = (_silu(a) * (0.5 * b)).astype(BF16)
    o_ref[...] += _dot(h, wo_ref[...])


def _ffn(x, g, w_in, w_out, layer, *, tm, tf):
    m, d = x.shape
    nf = D_FF // tf
    g = g[layer]
    return pl.pallas_call(
        _ffn_kernel,
        out_shape=jax.ShapeDtypeStruct((m, d), F32),
        grid=(m // tm, nf),
        in_specs=[
            pl.BlockSpec((tm, d), lambda i, j: (i, 0)),
            pl.BlockSpec((1, d), lambda i, j: (0, 0)),
            pl.BlockSpec((None, d, tf), lambda i, j: (layer, 0, j)),
            pl.BlockSpec((None, d, tf), lambda i, j: (layer, 0, nf + j)),
            pl.BlockSpec((None, tf, d), lambda i, j: (layer, j, 0)),
        ],
        out_specs=pl.BlockSpec((tm, d), lambda i, j: (i, 0)),
        scratch_shapes=[pltpu.VMEM((tm, d), BF16)],
        compiler_params=_params("parallel", "arbitrary"),
        name="ffn",
    )(x, g.reshape(1, d), w_in, w_in, w_out)


def _ret_in_kernel(x_ref, g_ref, w_ref, cos_ref, sin_ref, o_ref, xn_ref, *, rope_blocks, heads_per_block):
    j = pl.program_id(1)

    @pl.when(j == 0)
    def _():
        x = x_ref[...]
        xn_ref[...] = (x * _rms_scale(x) * g_ref[...]).astype(BF16)

    y = _dot(xn_ref[...], w_ref[...])

    @pl.when(j < 2 * rope_blocks)
    def _():
        scale = jnp.where(j < rope_blocks, RET_DK ** -0.5, 1.0).astype(F32)
        cos = cos_ref[...]
        sin = sin_ref[...]
        half = RET_DK // 2
        for h in range(heads_per_block):
            lo = h * RET_DK
            x1 = y[:, lo:lo + half]
            x2 = y[:, lo + half:lo + RET_DK]
            o_ref[:, lo:lo + half] = ((x1 * cos - x2 * sin) * scale).astype(BF16)
            o_ref[:, lo + half:lo + RET_DK] = ((x1 * sin + x2 * cos) * scale).astype(BF16)

    @pl.when(j >= 2 * rope_blocks)
    def _():
        o_ref[...] = y.astype(BF16)


def _ret_in(x, g, w, cos, sin, *, tm, tn):
    m, d = x.shape
    n = w.shape[1]
    tab_blocks = cos.shape[0] // tm
    kern = functools.partial(_ret_in_kernel, rope_blocks=(RET_HEADS * RET_DK) // tn,
                             heads_per_block=tn // RET_DK)
    return pl.pallas_call(
        kern,
        out_shape=jax.ShapeDtypeStruct((m, n), BF16),
        grid=(m // tm, n // tn),
        in_specs=[
            pl.BlockSpec((tm, d), lambda i, j: (i, 0)),
            pl.BlockSpec((1, d), lambda i, j: (0, 0)),
            pl.BlockSpec((d, tn), lambda i, j: (0, j)),
            pl.BlockSpec((tm, RET_DK // 2), lambda i, j: (i % tab_blocks, 0)),
            pl.BlockSpec((tm, RET_DK // 2), lambda i, j: (i % tab_blocks, 0)),
        ],
        out_specs=pl.BlockSpec((tm, tn), lambda i, j: (i, j)),
        scratch_shapes=[pltpu.VMEM((tm, d), BF16)],
        compiler_params=_params("parallel", "arbitrary"),
        name="ret_in",
    )(x, g.reshape(1, d), w, cos, sin)


def _ret_core_kernel(lg_ref, q_ref, k_ref, v_ref, g_ref, gain_ref, *rest, chunk, heads, has_state):
    if has_state:
        s0_ref, y_ref, s_ref = rest
    else:
        y_ref, s_ref = rest
    head0 = pl.program_id(1) * heads
    n_chunks = q_ref.shape[0] // chunk
    row = lax.broadcasted_iota(jnp.int32, (chunk, chunk), 0)
    col = lax.broadcasted_iota(jnp.int32, (chunk, chunk), 1)
    diff = (row - col).astype(F32)
    n = lax.broadcasted_iota(jnp.int32, (chunk, 1), 0).astype(F32)
    for hh in range(heads):
        lg = lg_ref[head0 + hh]
        decay = jnp.where(diff >= 0, jnp.exp(lg * jnp.maximum(diff, 0.0)), 0.0)
        cross = jnp.exp(lg * (n + 1.0))
        kdec = jnp.exp(lg * (chunk - 1.0 - n))
        carry = jnp.exp(lg * jnp.full((1, RET_DV), float(chunk), F32))
        gain = gain_ref[:, hh * RET_DV:(hh + 1) * RET_DV]
        if has_state:
            s_ref[0, hh] = s0_ref[0, hh]
        else:
            s_ref[0, hh] = jnp.zeros((RET_DK, RET_DV), F32)
        for c in range(n_chunks):
            rows = slice(c * chunk, (c + 1) * chunk)
            q = q_ref[rows, hh * RET_DK:(hh + 1) * RET_DK]
            k = k_ref[rows, hh * RET_DK:(hh + 1) * RET_DK]
            v = v_ref[rows, hh * RET_DV:(hh + 1) * RET_DV]
            inner = (_dot_nt(q, k) * decay).astype(BF16)
            s_old = s_ref[0, hh]
            o = _dot(inner, v) + cross * _dot(q, s_old.astype(BF16))
            kd = (k.astype(F32) * kdec).astype(BF16)
            s_ref[0, hh] = carry * s_old + _dot_tn(kd, v)

            mu = jnp.mean(o, axis=-1, keepdims=True)
            dev = o - mu
            var = jnp.mean(dev * dev, axis=-1, keepdims=True)
            of = dev * lax.rsqrt(var + EPS) * gain
            gate = g_ref[rows, hh * RET_DV:(hh + 1) * RET_DV].astype(F32)
            y_ref[rows, hh * RET_DV:(hh + 1) * RET_DV] = (_silu(gate) * of).astype(BF16)


def _ret_core(qkvg, log_gamma, gain, state, *, batch, seq, chunk, heads):
    m = qkvg.shape[0]
    hk = RET_HEADS * RET_DK
    hv = RET_HEADS * RET_DV
    wk = heads * RET_DK
    wv = heads * RET_DV
    kblk = hk // wk
    vblk = 2 * hk // wv
    gblk = vblk + hv // wv
    has_state = state is not None
    in_specs = [
        pl.BlockSpec(memory_space=pltpu.SMEM),
        pl.BlockSpec((seq, wk), lambda b, g: (b, g)),
        pl.BlockSpec((seq, wk), lambda b, g: (b, kblk + g)),
        pl.BlockSpec((seq, wv), lambda b, g: (b, vblk + g)),
        pl.BlockSpec((seq, wv), lambda b, g: (b, gblk + g)),
        pl.BlockSpec((1, wv), lambda b, g: (0, g)),
    ]
    args = [log_gamma, qkvg, qkvg, qkvg, qkvg, gain.reshape(1, hv)]
    state_spec = pl.BlockSpec((1, heads, RET_DK, RET_DV), lambda b, g: (b, g, 0, 0))
    if has_state:
        in_specs.append(state_spec)
        args.append(state)
    kern = functools.partial(_ret_core_kernel, chunk=chunk, heads=heads, has_state=has_state)
    return pl.pallas_call(
        kern,
        out_shape=(jax.ShapeDtypeStruct((m, hv), BF16),
                   jax.ShapeDtypeStruct((batch, RET_HEADS, RET_DK, RET_DV), F32)),
        grid=(batch, RET_HEADS // heads),
        in_specs=in_specs,
        out_specs=(pl.BlockSpec((seq, wv), lambda b, g: (b, g)), state_spec),
        compiler_params=_params("parallel", "parallel"),
        name="ret_core",
    )(*args)


def _mm_res_kernel(a_ref, w_ref, r_ref, o_ref):
    o_ref[...] = r_ref[...] + _dot(a_ref[...], w_ref[...])


def _mm_res(a, w, res, *, tm):
    m, k = a.shape
    n = w.shape[1]
    return pl.pallas_call(
        _mm_res_kernel,
        out_shape=jax.ShapeDtypeStruct((m, n), F32),
        grid=(m // tm,),
        in_specs=[
            pl.BlockSpec((tm, k), lambda i: (i, 0)),
            pl.BlockSpec((k, n), lambda i: (0, 0), pipeline_mode=pl.Buffered(1)),
            pl.BlockSpec((tm, n), lambda i: (i, 0)),
        ],
        out_specs=pl.BlockSpec((tm, n), lambda i: (i, 0)),
        compiler_params=_params("parallel"),
        name="mm_res",
    )(a, w, res)


def _latent_kernel(x_ref, g_ref, w_ref, gl_ref, cos_ref, sin_ref, c_ref, kr_ref):
    x = x_ref[...]
    xn = (x * _rms_scale(x) * g_ref[...]).astype(BF16)
    y = _dot(xn, w_ref[...])
    cl = y[:, :KV_LORA]
    c_ref[...] = cl * _rms_scale(cl) * gl_ref[...]
    half = QK_ROPE // 2
    x1 = y[:, KV_LORA:KV_LORA + half]
    x2 = y[:, KV_LORA + half:KV_LORA + QK_ROPE]
    cos = cos_ref[...]
    sin = sin_ref[...]
    kr_ref[...] = jnp.concatenate([x1 * cos - x2 * sin, x1 * sin + x2 * cos], axis=-1)


def _latent(x, g, w, g_lat, cos, sin, *, tm):
    m, d = x.shape
    n = w.shape[1]
    half = QK_ROPE // 2
    tab_blocks = cos.shape[0] // tm
    return pl.pallas_call(
        _latent_kernel,
        out_shape=(jax.ShapeDtypeStruct((m, KV_LORA), F32),
                   jax.ShapeDtypeStruct((m, QK_ROPE), F32)),
        grid=(m // tm,),
        in_specs=[
            pl.BlockSpec((tm, d), lambda i: (i, 0)),
            pl.BlockSpec((1, d), lambda i: (0, 0)),
            pl.BlockSpec((d, n), lambda i: (0, 0)),
            pl.BlockSpec((1, KV_LORA), lambda i: (0, 0)),
            pl.BlockSpec((tm, half), lambda i: (i % tab_blocks, 0)),
            pl.BlockSpec((tm, half), lambda i: (i % tab_blocks, 0)),
        ],
        out_specs=(pl.BlockSpec((tm, KV_LORA), lambda i: (i, 0)),
                   pl.BlockSpec((tm, QK_ROPE), lambda i: (i, 0))),
        compiler_params=_params("parallel"),
        name="latent",
    )(x, g.reshape(1, d), w, g_lat.reshape(1, KV_LORA), cos, sin)


def _kv_up_kernel(*refs, seg_rows):
    n_seg = len(seg_rows)
    wk_ref, wv_ref, gk_ref, k_ref, v_ref = refs[2 * n_seg:]
    g_nope = gk_ref[:, :QK_NOPE]
    g_rope = gk_ref[:, QK_NOPE:]
    lo = 0
    for s, rows in enumerate(seg_rows):
        c = refs[2 * s][...].astype(BF16)
        kr = refs[2 * s + 1][...]
        kn = _dot(c, wk_ref[...])
        v_ref[lo:lo + rows, :] = _dot(c, wv_ref[...]).astype(BF16)
        ss_rope = jnp.sum(kr * kr, axis=-1, keepdims=True)
        for h in range(MLA_HEADS):
            knh = kn[:, h * QK_NOPE:(h + 1) * QK_NOPE]
            ss = jnp.sum(knh * knh, axis=-1, keepdims=True) + ss_rope
            r = lax.rsqrt(ss * (1.0 / QK_HEAD) + EPS)
            k_ref[h, lo:lo + rows, :QK_NOPE] = (knh * r * g_nope).astype(BF16)
            k_ref[h, lo:lo + rows, QK_NOPE:] = (kr * r * g_rope).astype(BF16)
        lo += rows


def _kv_up(segments, wk, wv, gk, *, steps):
    seg_rows = tuple(c.shape[0] // steps for c, _ in segments)
    tm = sum(seg_rows)
    m = steps * tm
    in_specs, args = [], []
    for (c, kr), rows in zip(segments, seg_rows):
        in_specs += [pl.BlockSpec((rows, KV_LORA), lambda i: (i, 0)),
                     pl.BlockSpec((rows, QK_ROPE), lambda i: (i, 0))]
        args += [c, kr]
    in_specs += [pl.BlockSpec(wk.shape, lambda i: (0, 0)),
                 pl.BlockSpec(wv.shape, lambda i: (0, 0)),
                 pl.BlockSpec((1, QK_HEAD), lambda i: (0, 0))]
    return pl.pallas_call(
        functools.partial(_kv_up_kernel, seg_rows=seg_rows),
        out_shape=(jax.ShapeDtypeStruct((MLA_HEADS, m, QK_HEAD), BF16),
                   jax.ShapeDtypeStruct((m, MLA_HEADS * V_HEAD), BF16)),
        grid=(steps,),
        in_specs=in_specs,
        out_specs=(pl.BlockSpec((MLA_HEADS, tm, QK_HEAD), lambda i: (0, i, 0)),
                   pl.BlockSpec((tm, MLA_HEADS * V_HEAD), lambda i: (i, 0))),
        compiler_params=_params("parallel"),
        name="kv_up",
    )(*args, wk, wv, gk.reshape(1, QK_HEAD))


def _q_proj_kernel(x_ref, g_ref, wdq_ref, gl_ref, wuq_ref, cos_ref, sin_ref, gq_ref, q_ref):
    x = x_ref[...]
    hn = (x * _rms_scale(x) * g_ref[...]).astype(BF16)
    ql = _dot(hn, wdq_ref[...])
    qn = (ql * _rms_scale(ql) * gl_ref[...]).astype(BF16)
    q = _dot(qn, wuq_ref[...])
    nope_w = MLA_HEADS * QK_NOPE
    half = QK_ROPE // 2
    half_w = MLA_HEADS * half
    x1 = q[:, nope_w:nope_w + half_w]
    x2 = q[:, nope_w + half_w:]
    cos = cos_ref[...]
    sin = sin_ref[...]
    r1 = x1 * cos - x2 * sin
    r2 = x1 * sin + x2 * cos
    g_nope = gq_ref[:, :QK_NOPE]
    g_rope = gq_ref[:, QK_NOPE:]
    for h in range(MLA_HEADS):
        nh = q[:, h * QK_NOPE:(h + 1) * QK_NOPE]
        rp = jnp.concatenate([r1[:, h * half:(h + 1) * half], r2[:, h * half:(h + 1) * half]], axis=-1)
        ss = jnp.sum(nh * nh, axis=-1, keepdims=True) + jnp.sum(rp * rp, axis=-1, keepdims=True)
        r = lax.rsqrt(ss * (1.0 / QK_HEAD) + EPS) * (QK_HEAD ** -0.5)
        q_ref[h, :, :QK_NOPE] = (nh * r * g_nope).astype(BF16)
        q_ref[h, :, QK_NOPE:] = (rp * r * g_rope).astype(BF16)


def _q_proj(x, g, wdq, g_lat, wuq, cos, sin, gq, *, tm):
    m, d = x.shape
    tab_blocks = cos.shape[0] // tm
    tw = cos.shape[1]
    return pl.pallas_call(
        _q_proj_kernel,
        out_shape=jax.ShapeDtypeStruct((MLA_HEADS, m, QK_HEAD), BF16),
        grid=(m // tm,),
        in_specs=[
            pl.BlockSpec((tm, d), lambda i: (i, 0)),
            pl.BlockSpec((1, d), lambda i: (0, 0)),
            pl.BlockSpec(wdq.shape, lambda i: (0, 0)),
            pl.BlockSpec((1, Q_LORA), lambda i: (0, 0)),
            pl.BlockSpec(wuq.shape, lambda i: (0, 0)),
            pl.BlockSpec((tm, tw), lambda i: (i % tab_blocks, 0)),
            pl.BlockSpec((tm, tw), lambda i: (i % tab_blocks, 0)),
            pl.BlockSpec((1, QK_HEAD), lambda i: (0, 0)),
        ],
        out_specs=pl.BlockSpec((MLA_HEADS, tm, QK_HEAD), lambda i: (0, i, 0)),
        compiler_params=_params("parallel"),
        name="q_proj",
    )(x, g.reshape(1, d), wdq, g_lat.reshape(1, Q_LORA), wuq, cos, sin, gq.reshape(1, QK_HEAD))


def _attn_kernel(q_ref, k_ref, v_ref, o_ref, *, heads, tq, causal):
    t_q = q_ref.shape[1]
    t_kv = k_ref.shape[1]
    for h in range(heads):
        for qi in range(t_q // tq):
            q = q_ref[h, qi * tq:(qi + 1) * tq, :]
            if causal:
                lo = qi * tq
                row = lax.broadcasted_iota(jnp.int32, (tq, tq), 0) // CHUNK
                col = lax.broadcasted_iota(jnp.int32, (tq, tq), 1) // CHUNK
                s_d = jnp.where(col <= row, _dot_nt(q, k_ref[h, lo:lo + tq, :]), NEG_INF)
                v_d = v_ref[lo:lo + tq, h * V_HEAD:(h + 1) * V_HEAD]
                m = jnp.max(s_d, axis=-1, keepdims=True)
                if qi > 0:
                    s_p = _dot_nt(q, k_ref[h, :lo, :])
                    m = jnp.maximum(m, jnp.max(s_p, axis=-1, keepdims=True))
                    p_p = jnp.exp(s_p - m)
                p_d = jnp.exp(s_d - m)
                l = jnp.sum(p_d, axis=-1, keepdims=True)
                acc = _dot(p_d.astype(BF16), v_d)
                if qi > 0:
                    l = l + jnp.sum(p_p, axis=-1, keepdims=True)
                    acc = acc + _dot(p_p.astype(BF16), v_ref[:lo, h * V_HEAD:(h + 1) * V_HEAD])
            else:
                s = _dot_nt(q, k_ref[h])
                m = jnp.max(s, axis=-1, keepdims=True)
                p = jnp.exp(s - m)
                l = jnp.sum(p, axis=-1, keepdims=True)
                acc = _dot(p.astype(BF16), v_ref[:, h * V_HEAD:(h + 1) * V_HEAD])
            o_ref[qi * tq:(qi + 1) * tq, h * V_HEAD:(h + 1) * V_HEAD] = (acc / l).astype(BF16)
    del t_kv


def _attention(q, k, v, *, batch, t_q, t_kv, heads, tq, causal):
    hg = MLA_HEADS // heads
    kern = functools.partial(_attn_kernel, heads=heads, tq=tq, causal=causal)
    return pl.pallas_call(
        kern,
        out_shape=jax.ShapeDtypeStruct((batch * t_q, MLA_HEADS * V_HEAD), BF16),
        grid=(batch, hg),
        in_specs=[
            pl.BlockSpec((heads, t_q, QK_HEAD), lambda b, g: (g, b, 0)),
            pl.BlockSpec((heads, t_kv, QK_HEAD), lambda b, g: (g, b, 0)),
            pl.BlockSpec((t_kv, heads * V_HEAD), lambda b, g: (b, g)),
        ],
        out_specs=pl.BlockSpec((t_q, heads * V_HEAD), lambda b, g: (b, g)),
        compiler_params=_params("parallel", "parallel"),
        name="attention",
    )(q, k, v)


def _rope_tables(pos, half, reps, rows):
    inv = jnp.power(ROPE_BASE, -jnp.arange(half, dtype=F32) / half)
    ang = pos.astype(F32)[:, None] * inv[None, :]
    cos = jnp.tile(jnp.cos(ang), (max(rows // pos.shape[0], 1), reps))
    sin = jnp.tile(jnp.sin(ang), (max(rows // pos.shape[0], 1), reps))
    return cos, sin


def _trunk(x, pos, ret_state, ckv_past, krope_past, p, *, causal):
    batch, seq, d = x.shape
    m = batch * seq
    tm = 1024
    h = x.reshape(m, d)

    h = _ffn(h, p["ffn1_norm"], p["ffn1_w_in"], p["ffn1_w_out"], 0, tm=FFN_TM, tf=FFN_TF)
    cos, sin = _rope_tables(pos, RET_DK // 2, 1, tm)
    qkvg = _ret_in(h, p["mix_norm"][0], p["ret_w_in"][0], cos, sin, tm=tm, tn=1024)
    log_gamma = jnp.log1p(-jnp.exp2(-5.0 - jnp.arange(RET_HEADS, dtype=F32)))
    chunk = min(seq, 256)
    y, s_fin = _ret_core(qkvg, log_gamma, p["ret_gn_gain"][0],
                         None if ret_state is None else ret_state[0],
                         batch=batch, seq=seq, chunk=chunk,
                         heads=max(1, min(RET_HEADS, 512 // seq)))
    h = _mm_res(y, p["ret_w_out"][0], h, tm=512)
    h = _ffn(h, p["ffn2_norm"], p["ffn2_w_in"], p["ffn2_w_out"], 0, tm=FFN_TM, tf=FFN_TF)

    cos, sin = _rope_tables(pos, QK_ROPE // 2, 1, 512)
    c_new, kr_new = _latent(h, p["kv_norm"], p["w_dkv"], p["kv_lat_norm"], cos, sin, tm=512)
    if ckv_past is None:
        t_kv = seq
        segments, steps = [(c_new, kr_new)], m // 512
    else:
        past = ckv_past.shape[1]
        t_kv = past + seq
        segments = [(ckv_past.reshape(batch * past, KV_LORA), krope_past.reshape(batch * past, QK_ROPE)),
                    (c_new, kr_new)]
        steps = batch
    k_sh, v_sh = _kv_up(segments, p["w_uk"], p["w_uv"], p["k_norm"], steps=steps)

    h = _ffn(h, p["ffn1_norm"], p["ffn1_w_in"], p["ffn1_w_out"], 1, tm=FFN_TM, tf=FFN_TF)
    cos, sin = _rope_tables(pos, QK_ROPE // 2, MLA_HEADS, 512)
    q = _q_proj(h, p["mix_norm"][1], p["w_dq"][0], p["q_lat_norm"][0], p["w_uq"][0], cos, sin,
                p["q_norm"][0], tm=512)
    if causal:
        o = _attention(q, k_sh, v_sh, batch=batch, t_q=seq, t_kv=t_kv, heads=1, tq=512, causal=True)
    else:
        o = _attention(q, k_sh, v_sh, batch=batch, t_q=seq, t_kv=t_kv, heads=MLA_HEADS, tq=seq, causal=False)
    h = _mm_res(o, p["w_o"][0], h, tm=512)
    h = _ffn(h, p["ffn2_norm"], p["ffn2_w_in"], p["ffn2_w_out"], 1, tm=FFN_TM, tf=FFN_TF)

    return (h.reshape(batch, seq, d), s_fin[None],
            c_new.reshape(batch, seq, KV_LORA), kr_new.reshape(batch, seq, QK_ROPE))


def kernel(x_prompt, x_sample, state_ret, cache_ckv, cache_krope, ffn1_norm, ffn1_w_in, ffn1_w_out, mix_norm, ffn2_norm, ffn2_w_in, ffn2_w_out, ret_w_in, ret_gn_gain, ret_w_out, kv_norm, w_dkv, kv_lat_norm, w_ukv, k_norm, w_dq, q_lat_norm, w_uq, q_norm, w_o):
    bf = lambda w: w.astype(BF16)
    w_ukv_h = w_ukv.reshape(KV_LORA, MLA_HEADS, QK_NOPE + V_HEAD)
    w_uk = w_ukv_h[:, :, :QK_NOPE].reshape(KV_LORA, MLA_HEADS * QK_NOPE)
    w_uv = w_ukv_h[:, :, QK_NOPE:].reshape(KV_LORA, MLA_HEADS * V_HEAD)
    half = QK_ROPE // 2
    w_uq_h = w_uq.reshape(w_uq.shape[0], Q_LORA, MLA_HEADS, QK_HEAD)
    w_uq_p = jnp.concatenate([
        w_uq_h[..., :QK_NOPE].reshape(w_uq.shape[0], Q_LORA, MLA_HEADS * QK_NOPE),
        w_uq_h[..., QK_NOPE:QK_NOPE + half].reshape(w_uq.shape[0], Q_LORA, MLA_HEADS * half),
        w_uq_h[..., QK_NOPE + half:].reshape(w_uq.shape[0], Q_LORA, MLA_HEADS * half)], axis=-1)
    p = dict(ffn1_norm=ffn1_norm, ffn1_w_in=bf(ffn1_w_in), ffn1_w_out=bf(ffn1_w_out),
             mix_norm=mix_norm, ffn2_norm=ffn2_norm, ffn2_w_in=bf(ffn2_w_in), ffn2_w_out=bf(ffn2_w_out),
             ret_w_in=bf(ret_w_in), ret_gn_gain=ret_gn_gain, ret_w_out=bf(ret_w_out),
             kv_norm=kv_norm, w_dkv=bf(w_dkv), kv_lat_norm=kv_lat_norm,
             w_uk=bf(w_uk), w_uv=bf(w_uv), k_norm=k_norm,
             w_dq=bf(w_dq), q_lat_norm=q_lat_norm, w_uq=bf(w_uq_p), q_norm=q_norm, w_o=bf(w_o))
    pos_prompt = jnp.arange(x_prompt.shape[1])
    past = cache_ckv.shape[1]
    pos_sample = past + jnp.arange(x_sample.shape[1])
    y_p, ret_p, ckv_p, kr_p = _trunk(x_prompt, pos_prompt, None, None, None, p, causal=True)
    y_s, ret_s, ckv_s, kr_s = _trunk(x_sample, pos_sample, state_ret, cache_ckv, cache_krope, p, causal=False)
    return (y_p, y_s, ret_p, ckv_p, kr_p, ret_s, ckv_s, kr_s)
```

```python
import functools

import jax
import jax.numpy as jnp
from jax import lax
from jax.experimental import pallas as pl
from jax.experimental.pallas import tpu as pltpu

F32 = jnp.float32
BF16 = jnp.bfloat16

D_MODEL = 2048
CHUNK = 64
D_FF = 5632
RET_HEADS = 8
RET_DK = D_MODEL // RET_HEADS
RET_DV = 2 * D_MODEL // RET_HEADS
MLA_HEADS = 16
QK_NOPE = 128
QK_ROPE = 64
QK_HEAD = QK_NOPE + QK_ROPE
V_HEAD = 128
KV_LORA = 512
Q_LORA = 512
ROPE_BASE = 10000.0
EPS = 1e-6
NEG_INF = -1e30
LOG2E = 1.4426950408889634

VMEM_LIMIT_BYTES = 60 * 1024 * 1024

FFN_TM = 1024
FFN_TF = 512


def _params(*semantics):
    return pltpu.CompilerParams(dimension_semantics=semantics,
                                vmem_limit_bytes=VMEM_LIMIT_BYTES)


def _rms_scale(x):
    return lax.rsqrt(jnp.mean(x * x, axis=-1, keepdims=True) + EPS)


def _dot(a, b):
    return jnp.dot(a, b, preferred_element_type=F32)


def _dot_nt(a, b):
    return lax.dot_general(a, b, (((1,), (1,)), ((), ())), preferred_element_type=F32)


def _dot_tn(a, b):
    return lax.dot_general(a, b, (((0,), (0,)), ((), ())), preferred_element_type=F32)


def _silu(x):
    return x * jax.nn.sigmoid(x)


def _ffn_kernel(x_ref, g_ref, wa_ref, wb_ref, wo_ref, o_ref, xn_ref):
    j = pl.program_id(1)

    @pl.when(j == 0)
    def _():
        x = x_ref[...]
        xn_ref[...] = (x * _rms_scale(x) * g_ref[...]).astype(BF16)
        o_ref[...] = x

    xn = xn_ref[...]
    a = _dot(xn, wa_ref[...])
    b = _dot(xn, wb_ref[...])
    h = (_silu(a) * (0.5 * b)).astype(BF16)
    o_ref[...] += _dot(h, wo_ref[...])


def _ffn(x, g, w_in, w_out, layer, *, tm, tf):
    m, d = x.shape
    nf = D_FF // tf
    g = g[layer]
    return pl.pallas_call(
        _ffn_kernel,
        out_shape=jax.ShapeDtypeStruct((m, d), F32),
        grid=(m // tm, nf),
        in_specs=[
            pl.BlockSpec((tm, d), lambda i, j: (i, 0)),
            pl.BlockSpec((1, d), lambda i, j: (0, 0)),
            pl.BlockSpec((None, d, tf), lambda i, j: (layer, 0, j)),
            pl.BlockSpec((None, d, tf), lambda i, j: (layer, 0, nf + j)),
            pl.BlockSpec((None, tf, d), lambda i, j: (layer, j, 0)),
        ],
        out_specs=pl.BlockSpec((tm, d), lambda i, j: (i, 0)),
        scratch_shapes=[pltpu.VMEM((tm, d), BF16)],
        compiler_params=_params("parallel", "arbitrary"),
        name="ffn",
    )(x, g.reshape(1, d), w_in, w_in, w_out)


def _ret_in_kernel(x_ref, g_ref, w_ref, cos_ref, sin_ref, o_ref, xn_ref, *, rope_blocks, heads_per_block):
    j = pl.program_id(1)

    @pl.when(j == 0)
    def _():
        x = x_ref[...]
        xn_ref[...] = (x * _rms_scale(x) * g_ref[...]).astype(BF16)

    half = RET_DK // 2

    @pl.when(j < 2 * rope_blocks)
    def _():
        scale = jnp.where(j < rope_blocks, RET_DK ** -0.5, 1.0).astype(F32)
        cos = cos_ref[...] * scale
        sin = sin_ref[...] * scale
        xn = xn_ref[...]
        for h in range(heads_per_block):
            lo = h * RET_DK
            y = _dot(xn, w_ref[:, lo:lo + RET_DK])
            x1 = y[:, :half]
            x2 = y[:, half:]
            o_ref[:, lo:lo + half] = (x1 * cos - x2 * sin).astype(BF16)
            o_ref[:, lo + half:lo + RET_DK] = (x1 * sin + x2 * cos).astype(BF16)

    @pl.when(j >= 2 * rope_blocks)
    def _():
        o_ref[...] = _dot(xn_ref[...], w_ref[...]).astype(BF16)


def _ret_in(x, g, w, cos, sin, *, tm, tn):
    m, d = x.shape
    n = w.shape[1]
    tab_blocks = cos.shape[0] // tm
    kern = functools.partial(_ret_in_kernel, rope_blocks=(RET_HEADS * RET_DK) // tn,
                             heads_per_block=tn // RET_DK)
    return pl.pallas_call(
        kern,
        out_shape=jax.ShapeDtypeStruct((m, n), BF16),
        grid=(m // tm, n // tn),
        in_specs=[
            pl.BlockSpec((tm, d), lambda i, j: (i, 0)),
            pl.BlockSpec((1, d), lambda i, j: (0, 0)),
            pl.BlockSpec((d, tn), lambda i, j: (0, j)),
            pl.BlockSpec((tm, RET_DK // 2), lambda i, j: (i % tab_blocks, 0)),
            pl.BlockSpec((tm, RET_DK // 2), lambda i, j: (i % tab_blocks, 0)),
        ],
        out_specs=pl.BlockSpec((tm, tn), lambda i, j: (i, j)),
        scratch_shapes=[pltpu.VMEM((tm, d), BF16)],
        compiler_params=_params("parallel", "arbitrary"),
        name="ret_in",
    )(x, g.reshape(1, d), w, cos, sin)


def _ret_core_kernel(lg_ref, q_ref, k_ref, v_ref, g_ref, gain_ref, *rest, chunk, heads, has_state):
    if has_state:
        s0_ref, y_ref, s_ref = rest
    else:
        y_ref, s_ref = rest
    head0 = pl.program_id(1) * heads
    n_chunks = q_ref.shape[0] // chunk
    row = lax.broadcasted_iota(jnp.int32, (chunk, chunk), 0)
    col = lax.broadcasted_iota(jnp.int32, (chunk, chunk), 1)
    diff = (row - col).astype(F32)
    n = lax.broadcasted_iota(jnp.int32, (chunk, 1), 0).astype(F32)

    def head_consts(hh):
        lg = lg_ref[head0 + hh]
        decay = jnp.where(diff >= 0, jnp.exp(lg * jnp.maximum(diff, 0.0)), 0.0)
        cross = jnp.exp(lg * (n + 1.0))
        kdec = jnp.exp(lg * (chunk - 1.0 - n))
        carry = jnp.exp(lg * jnp.full((1, RET_DV), float(chunk), F32))
        return decay, cross, kdec, carry

    def mix(hh, c, consts):
        decay, cross, kdec, carry = consts
        rows = slice(c * chunk, (c + 1) * chunk)
        q = q_ref[rows, hh * RET_DK:(hh + 1) * RET_DK]
        k = k_ref[rows, hh * RET_DK:(hh + 1) * RET_DK]
        v = v_ref[rows, hh * RET_DV:(hh + 1) * RET_DV]
        inner = (_dot_nt(q, k) * decay).astype(BF16)
        if c > 0:
            s_old = s_ref[0, hh]
        elif has_state:
            s_old = s0_ref[0, hh]
        else:
            s_old = None
        kd = (k.astype(F32) * kdec).astype(BF16)
        o = _dot(inner, v)
        s_new = _dot_tn(kd, v)
        if s_old is not None:
            o = o + cross * _dot(q, s_old.astype(BF16))
            s_new = carry * s_old + s_new
        s_ref[0, hh] = s_new
        return o

    def norm_gate(hh, c, o):
        rows = slice(c * chunk, (c + 1) * chunk)
        cols = slice(hh * RET_DV, (hh + 1) * RET_DV)
        mu = jnp.mean(o, axis=-1, keepdims=True)
        dev = o - mu
        var = jnp.mean(dev * dev, axis=-1, keepdims=True)
        of = dev * lax.rsqrt(var + EPS) * gain_ref[:, cols]
        y_ref[rows, cols] = (_silu(g_ref[rows, cols].astype(F32)) * of).astype(BF16)

    consts = None
    pending = None
    for hh in range(heads):
        for c in range(n_chunks):
            if c == 0:
                consts = head_consts(hh)
            o = mix(hh, c, consts)
            if pending is not None:
                norm_gate(*pending)
            pending = (hh, c, o)
    norm_gate(*pending)


def _ret_core(qkvg, log_gamma, gain, state, *, batch, seq, chunk, heads):
    m = qkvg.shape[0]
    hk = RET_HEADS * RET_DK
    hv = RET_HEADS * RET_DV
    wk = heads * RET_DK
    wv = heads * RET_DV
    kblk = hk // wk
    vblk = 2 * hk // wv
    gblk = vblk + hv // wv
    has_state = state is not None
    in_specs = [
        pl.BlockSpec(memory_space=pltpu.SMEM),
        pl.BlockSpec((seq, wk), lambda b, g: (b, g)),
        pl.BlockSpec((seq, wk), lambda b, g: (b, kblk + g)),
        pl.BlockSpec((seq, wv), lambda b, g: (b, vblk + g)),
        pl.BlockSpec((seq, wv), lambda b, g: (b, gblk + g)),
        pl.BlockSpec((1, wv), lambda b, g: (0, g)),
    ]
    args = [log_gamma, qkvg, qkvg, qkvg, qkvg, gain.reshape(1, hv)]
    state_spec = pl.BlockSpec((1, heads, RET_DK, RET_DV), lambda b, g: (b, g, 0, 0))
    if has_state:
        in_specs.append(state_spec)
        args.append(state)
    kern = functools.partial(_ret_core_kernel, chunk=chunk, heads=heads, has_state=has_state)
    return pl.pallas_call(
        kern,
        out_shape=(jax.ShapeDtypeStruct((m, hv), BF16),
                   jax.ShapeDtypeStruct((batch, RET_HEADS, RET_DK, RET_DV), F32)),
        grid=(batch, RET_HEADS // heads),
        in_specs=in_specs,
        out_specs=(pl.BlockSpec((seq, wv), lambda b, g: (b, g)), state_spec),
        compiler_params=_params("parallel", "parallel"),
        name="ret_core",
    )(*args)


def _mm_res_kernel(a_ref, w_ref, r_ref, o_ref):
    o_ref[...] = r_ref[...] + _dot(a_ref[...], w_ref[...])


def _mm_res(a, w, res, *, tm):
    m, k = a.shape
    n = w.shape[1]
    return pl.pallas_call(
        _mm_res_kernel,
        out_shape=jax.ShapeDtypeStruct((m, n), F32),
        grid=(m // tm,),
        in_specs=[
            pl.BlockSpec((tm, k), lambda i: (i, 0)),
            pl.BlockSpec((k, n), lambda i: (0, 0), pipeline_mode=pl.Buffered(1)),
            pl.BlockSpec((tm, n), lambda i: (i, 0)),
        ],
        out_specs=pl.BlockSpec((tm, n), lambda i: (i, 0)),
        compiler_params=_params("parallel"),
        name="mm_res",
    )(a, w, res)


def _latent_kernel(x_ref, g_ref, w_ref, gl_ref, cos_ref, sin_ref, c_ref, kr_ref):
    x = x_ref[...]
    xn = (x * _rms_scale(x) * g_ref[...]).astype(BF16)
    y = _dot(xn, w_ref[...])
    cl = y[:, :KV_LORA]
    c_ref[...] = cl * _rms_scale(cl) * gl_ref[...]
    half = QK_ROPE // 2
    x1 = y[:, KV_LORA:KV_LORA + half]
    x2 = y[:, KV_LORA + half:KV_LORA + QK_ROPE]
    cos = cos_ref[...]
    sin = sin_ref[...]
    kr_ref[...] = jnp.concatenate([x1 * cos - x2 * sin, x1 * sin + x2 * cos], axis=-1)


def _latent(x, g, w, g_lat, cos, sin, *, tm):
    m, d = x.shape
    n = w.shape[1]
    half = QK_ROPE // 2
    tab_blocks = cos.shape[0] // tm
    return pl.pallas_call(
        _latent_kernel,
        out_shape=(jax.ShapeDtypeStruct((m, KV_LORA), F32),
                   jax.ShapeDtypeStruct((m, QK_ROPE), F32)),
        grid=(m // tm,),
        in_specs=[
            pl.BlockSpec((tm, d), lambda i: (i, 0)),
            pl.BlockSpec((1, d), lambda i: (0, 0)),
            pl.BlockSpec((d, n), lambda i: (0, 0)),
            pl.BlockSpec((1, KV_LORA), lambda i: (0, 0)),
            pl.BlockSpec((tm, half), lambda i: (i % tab_blocks, 0)),
            pl.BlockSpec((tm, half), lambda i: (i % tab_blocks, 0)),
        ],
        out_specs=(pl.BlockSpec((tm, KV_LORA), lambda i: (i, 0)),
                   pl.BlockSpec((tm, QK_ROPE), lambda i: (i, 0))),
        compiler_params=_params("parallel"),
        name="latent",
    )(x, g.reshape(1, d), w, g_lat.reshape(1, KV_LORA), cos, sin)


def _kv_up_kernel(*refs, seg_rows):
    n_seg = len(seg_rows)
    wk_ref, wv_ref, gk_ref, k_ref, v_ref = refs[2 * n_seg:]
    g_nope = gk_ref[:, :QK_NOPE]
    g_rope = gk_ref[:, QK_NOPE:]
    lo = 0
    for s, rows in enumerate(seg_rows):
        c = refs[2 * s][...].astype(BF16)
        kr = refs[2 * s + 1][...]
        kn = _dot(c, wk_ref[...])
        v_ref[lo:lo + rows, :] = _dot(c, wv_ref[...]).astype(BF16)
        ss_rope = jnp.sum(kr * kr, axis=-1, keepdims=True)
        for h in range(MLA_HEADS):
            knh = kn[:, h * QK_NOPE:(h + 1) * QK_NOPE]
            ss = jnp.sum(knh * knh, axis=-1, keepdims=True) + ss_rope
            r = lax.rsqrt(ss * (1.0 / QK_HEAD) + EPS)
            k_ref[h, lo:lo + rows, :QK_NOPE] = (knh * r * g_nope).astype(BF16)
            k_ref[h, lo:lo + rows, QK_NOPE:] = (kr * r * g_rope).astype(BF16)
        lo += rows


def _kv_up(segments, wk, wv, gk, *, steps):
    seg_rows = tuple(c.shape[0] // steps for c, _ in segments)
    tm = sum(seg_rows)
    m = steps * tm
    in_specs, args = [], []
    for (c, kr), rows in zip(segments, seg_rows):
        in_specs += [pl.BlockSpec((rows, KV_LORA), lambda i: (i, 0)),
                     pl.BlockSpec((rows, QK_ROPE), lambda i: (i, 0))]
        args += [c, kr]
    in_specs += [pl.BlockSpec(wk.shape, lambda i: (0, 0)),
                 pl.BlockSpec(wv.shape, lambda i: (0, 0)),
                 pl.BlockSpec((1, QK_HEAD), lambda i: (0, 0))]
    return pl.pallas_call(
        functools.partial(_kv_up_kernel, seg_rows=seg_rows),
        out_shape=(jax.ShapeDtypeStruct((MLA_HEADS, m, QK_HEAD), BF16),
                   jax.ShapeDtypeStruct((m, MLA_HEADS * V_HEAD), BF16)),
        grid=(steps,),
        in_specs=in_specs,
        out_specs=(pl.BlockSpec((MLA_HEADS, tm, QK_HEAD), lambda i: (0, i, 0)),
                   pl.BlockSpec((tm, MLA_HEADS * V_HEAD), lambda i: (i, 0))),
        compiler_params=_params("parallel"),
        name="kv_up",
    )(*args, wk, wv, gk.reshape(1, QK_HEAD))


def _q_proj_kernel(x_ref, g_ref, wdq_ref, gl_ref, wuq_ref, cos_ref, sin_ref, gq_ref, q_ref):
    x = x_ref[...]
    hn = (x * _rms_scale(x) * g_ref[...]).astype(BF16)
    ql = _dot(hn, wdq_ref[...])
    qn = (ql * _rms_scale(ql) * gl_ref[...]).astype(BF16)
    q = _dot(qn, wuq_ref[...])
    nope_w = MLA_HEADS * QK_NOPE
    half = QK_ROPE // 2
    half_w = MLA_HEADS * half
    x1 = q[:, nope_w:nope_w + half_w]
    x2 = q[:, nope_w + half_w:]
    cos = cos_ref[...]
    sin = sin_ref[...]
    r1 = x1 * cos - x2 * sin
    r2 = x1 * sin + x2 * cos
    g_nope = gq_ref[:, :QK_NOPE]
    g_rope = gq_ref[:, QK_NOPE:]
    for h in range(MLA_HEADS):
        nh = q[:, h * QK_NOPE:(h + 1) * QK_NOPE]
        rp = jnp.concatenate([r1[:, h * half:(h + 1) * half], r2[:, h * half:(h + 1) * half]], axis=-1)
        ss = jnp.sum(nh * nh, axis=-1, keepdims=True) + jnp.sum(rp * rp, axis=-1, keepdims=True)
        r = lax.rsqrt(ss * (1.0 / QK_HEAD) + EPS) * (QK_HEAD ** -0.5 * LOG2E)
        q_ref[h, :, :QK_NOPE] = (nh * r * g_nope).astype(BF16)
        q_ref[h, :, QK_NOPE:] = (rp * r * g_rope).astype(BF16)


def _q_proj(x, g, wdq, g_lat, wuq, cos, sin, gq, *, tm):
    m, d = x.shape
    tab_blocks = cos.shape[0] // tm
    tw = cos.shape[1]
    return pl.pallas_call(
        _q_proj_kernel,
        out_shape=jax.ShapeDtypeStruct((MLA_HEADS, m, QK_HEAD), BF16),
        grid=(m // tm,),
        in_specs=[
            pl.BlockSpec((tm, d), lambda i: (i, 0)),
            pl.BlockSpec((1, d), lambda i: (0, 0)),
            pl.BlockSpec(wdq.shape, lambda i: (0, 0)),
            pl.BlockSpec((1, Q_LORA), lambda i: (0, 0)),
            pl.BlockSpec(wuq.shape, lambda i: (0, 0)),
            pl.BlockSpec((tm, tw), lambda i: (i % tab_blocks, 0)),
            pl.BlockSpec((tm, tw), lambda i: (i % tab_blocks, 0)),
            pl.BlockSpec((1, QK_HEAD), lambda i: (0, 0)),
        ],
        out_specs=pl.BlockSpec((MLA_HEADS, tm, QK_HEAD), lambda i: (0, i, 0)),
        compiler_params=_params("parallel"),
        name="q_proj",
    )(x, g.reshape(1, d), wdq, g_lat.reshape(1, Q_LORA), wuq, cos, sin, gq.reshape(1, QK_HEAD))


def _attn_causal_kernel(q_ref, k_ref, v_ref, o_ref, *, heads, tq):
    n_q = q_ref.shape[1] // tq
    krow = lax.broadcasted_iota(jnp.int32, (tq, tq), 0) // CHUNK
    qcol = lax.broadcasted_iota(jnp.int32, (tq, tq), 1) // CHUNK
    keep = krow <= qcol

    def scores(h, qi):
        lo = qi * tq
        q = q_ref[h, lo:lo + tq, :]
        st_d = jnp.where(keep, _dot_nt(k_ref[h, lo:lo + tq, :], q), NEG_INF)
        st_p = _dot_nt(k_ref[h, :lo, :], q) if qi > 0 else None
        return st_d, st_p

    def softmax(st):
        st_d, st_p = st
        m = jnp.max(st_d, axis=0, keepdims=True)
        if st_p is not None:
            m = jnp.maximum(m, jnp.max(st_p, axis=0, keepdims=True))
        pt_d = jnp.exp2(st_d - m)
        l = jnp.sum(pt_d, axis=0, keepdims=True)
        pt_p = None
        if st_p is not None:
            pt_p = jnp.exp2(st_p - m)
            l = l + jnp.sum(pt_p, axis=0, keepdims=True)
            pt_p = pt_p.astype(BF16)
        return pt_d.astype(BF16), pt_p, l

    def values(h, qi, p):
        pt_d, pt_p, l = p
        lo = qi * tq
        cols = slice(h * V_HEAD, (h + 1) * V_HEAD)
        acc_t = _dot_tn(v_ref[lo:lo + tq, cols], pt_d)
        if pt_p is not None:
            acc_t = acc_t + _dot_tn(v_ref[:lo, cols], pt_p)
        o_ref[lo:lo + tq, cols] = (acc_t / l).T.astype(BF16)

    work = [(h, qi) for h in range(heads) for qi in range(n_q)]
    st = scores(*work[0])
    for idx, item in enumerate(work):
        st_next = scores(*work[idx + 1]) if idx + 1 < len(work) else None
        values(*item, softmax(st))
        st = st_next


def _attn_full_kernel(q_ref, k_ref, v_ref, o_ref, *, heads):
    def scores(h):
        return _dot_nt(q_ref[h], k_ref[h])

    def softmax(s):
        m = jnp.max(s, axis=-1, keepdims=True)
        p = jnp.exp2(s - m)
        return p.astype(BF16), jnp.sum(p, axis=-1, keepdims=True)

    def values(h, pl_):
        p, l = pl_
        cols = slice(h * V_HEAD, (h + 1) * V_HEAD)
        o_ref[:, cols] = (_dot(p, v_ref[:, cols]) / l).astype(BF16)

    s_q, p_q = {}, {}
    for step in range(heads + 2):
        if step < heads:
            s_q[step] = scores(step)
        if 0 <= step - 1 < heads:
            p_q[step - 1] = softmax(s_q.pop(step - 1))
        if 0 <= step - 2 < heads:
            values(step - 2, p_q.pop(step - 2))


def _attention(q, k, v, *, batch, t_q, t_kv, heads, tq, causal):
    hg = MLA_HEADS // heads
    if causal:
        kern = functools.partial(_attn_causal_kernel, heads=heads, tq=tq)
    else:
        kern = functools.partial(_attn_full_kernel, heads=heads)
    return pl.pallas_call(
        kern,
        out_shape=jax.ShapeDtypeStruct((batch * t_q, MLA_HEADS * V_HEAD), BF16),
        grid=(batch, hg),
        in_specs=[
            pl.BlockSpec((heads, t_q, QK_HEAD), lambda b, g: (g, b, 0)),
            pl.BlockSpec((heads, t_kv, QK_HEAD), lambda b, g: (g, b, 0)),
            pl.BlockSpec((t_kv, heads * V_HEAD), lambda b, g: (b, g)),
        ],
        out_specs=pl.BlockSpec((t_q, heads * V_HEAD), lambda b, g: (b, g)),
        compiler_params=_params("parallel", "parallel"),
        name="attention",
    )(q, k, v)


def _rope_tables(pos, half, reps, rows):
    inv = jnp.power(ROPE_BASE, -jnp.arange(half, dtype=F32) / half)
    ang = pos.astype(F32)[:, None] * inv[None, :]
    cos = jnp.tile(jnp.cos(ang), (max(rows // pos.shape[0], 1), reps))
    sin = jnp.tile(jnp.sin(ang), (max(rows // pos.shape[0], 1), reps))
    return cos, sin


def _trunk(x, pos, ret_state, ckv_past, krope_past, p, *, causal):
    batch, seq, d = x.shape
    m = batch * seq
    tm = 1024
    h = x.reshape(m, d)

    h = _ffn(h, p["ffn1_norm"], p["ffn1_w_in"], p["ffn1_w_out"], 0, tm=FFN_TM, tf=FFN_TF)
    cos, sin = _rope_tables(pos, RET_DK // 2, 1, tm)
    qkvg = _ret_in(h, p["mix_norm"][0], p["ret_w_in"][0], cos, sin, tm=tm, tn=1024)
    log_gamma = jnp.log1p(-jnp.exp2(-5.0 - jnp.arange(RET_HEADS, dtype=F32)))
    chunk = min(seq, 256)
    y, s_fin = _ret_core(qkvg, log_gamma, p["ret_gn_gain"][0],
                         None if ret_state is None else ret_state[0],
                         batch=batch, seq=seq, chunk=chunk,
                         heads=max(1, min(RET_HEADS, 512 // seq)))
    h = _mm_res(y, p["ret_w_out"][0], h, tm=512)
    h = _ffn(h, p["ffn2_norm"], p["ffn2_w_in"], p["ffn2_w_out"], 0, tm=FFN_TM, tf=FFN_TF)

    cos, sin = _rope_tables(pos, QK_ROPE // 2, 1, 512)
    c_new, kr_new = _latent(h, p["kv_norm"], p["w_dkv"], p["kv_lat_norm"], cos, sin, tm=512)
    if ckv_past is None:
        t_kv = seq
        segments, steps = [(c_new, kr_new)], m // 512
    else:
        past = ckv_past.shape[1]
        t_kv = past + seq
        segments = [(ckv_past.reshape(batch * past, KV_LORA), krope_past.reshape(batch * past, QK_ROPE)),
                    (c_new, kr_new)]
        steps = batch
    k_sh, v_sh = _kv_up(segments, p["w_uk"], p["w_uv"], p["k_norm"], steps=steps)

    h = _ffn(h, p["ffn1_norm"], p["ffn1_w_in"], p["ffn1_w_out"], 1, tm=FFN_TM, tf=FFN_TF)
    cos, sin = _rope_tables(pos, QK_ROPE // 2, MLA_HEADS, 512)
    q = _q_proj(h, p["mix_norm"][1], p["w_dq"][0], p["q_lat_norm"][0], p["w_uq"][0], cos, sin,
                p["q_norm"][0], tm=512)
    if causal:
        o = _attention(q, k_sh, v_sh, batch=batch, t_q=seq, t_kv=t_kv, heads=1, tq=512, causal=True)
    else:
        o = _attention(q, k_sh, v_sh, batch=batch, t_q=seq, t_kv=t_kv, heads=MLA_HEADS, tq=seq, causal=False)
    h = _mm_res(o, p["w_o"][0], h, tm=512)
    h = _ffn(h, p["ffn2_norm"], p["ffn2_w_in"], p["ffn2_w_out"], 1, tm=FFN_TM, tf=FFN_TF)

    return (h.reshape(batch, seq, d), s_fin[None],
            c_new.reshape(batch, seq, KV_LORA), kr_new.reshape(batch, seq, QK_ROPE))


def kernel(x_prompt, x_sample, state_ret, cache_ckv, cache_krope, ffn1_norm, ffn1_w_in, ffn1_w_out, mix_norm, ffn2_norm, ffn2_w_in, ffn2_w_out, ret_w_in, ret_gn_gain, ret_w_out, kv_norm, w_dkv, kv_lat_norm, w_ukv, k_norm, w_dq, q_lat_norm, w_uq, q_norm, w_o):
    bf = lambda w: w.astype(BF16)
    w_ukv_h = w_ukv.reshape(KV_LORA, MLA_HEADS, QK_NOPE + V_HEAD)
    w_uk = w_ukv_h[:, :, :QK_NOPE].reshape(KV_LORA, MLA_HEADS * QK_NOPE)
    w_uv = w_ukv_h[:, :, QK_NOPE:].reshape(KV_LORA, MLA_HEADS * V_HEAD)
    half = QK_ROPE // 2
    w_uq_h = w_uq.reshape(w_uq.shape[0], Q_LORA, MLA_HEADS, QK_HEAD)
    w_uq_p = jnp.concatenate([
        w_uq_h[..., :QK_NOPE].reshape(w_uq.shape[0], Q_LORA, MLA_HEADS * QK_NOPE),
        w_uq_h[..., QK_NOPE:QK_NOPE + half].reshape(w_uq.shape[0], Q_LORA, MLA_HEADS * half),
        w_uq_h[..., QK_NOPE + half:].reshape(w_uq.shape[0], Q_LORA, MLA_HEADS * half)], axis=-1)
    p = dict(ffn1_norm=ffn1_norm, ffn1_w_in=bf(ffn1_w_in), ffn1_w_out=bf(ffn1_w_out),
             mix_norm=mix_norm, ffn2_norm=ffn2_norm, ffn2_w_in=bf(ffn2_w_in), ffn2_w_out=bf(ffn2_w_out),
             ret_w_in=bf(ret_w_in), ret_gn_gain=ret_gn_gain, ret_w_out=bf(ret_w_out),
             kv_norm=kv_norm, w_dkv=bf(w_dkv), kv_lat_norm=kv_lat_norm,
             w_uk=bf(w_uk), w_uv=bf(w_uv), k_norm=k_norm,
             w_dq=bf(w_dq), q_lat_norm=q_lat_norm, w_uq=bf(w_uq_p), q_norm=q_norm, w_o=bf(w_o))
    pos_prompt = jnp.arange(x_prompt.shape[1])
    past = cache_ckv.shape[1]
    pos_sample = past + jnp.arange(x_sample.shape[1])
    y_p, ret_p, ckv_p, kr_p = _trunk(x_prompt, pos_prompt, None, None, None, p, causal=True)
    y_s, ret_s, ckv_s, kr_s = _trunk(x_sample, pos_sample, state_ret, cache_ckv, cache_krope, p, causal=False)
    return (y_p, y_s, ret_p, ckv_p, kr_p, ret_s, ckv_s, kr_s)
```

```python
import functools

import jax
import jax.numpy as jnp
from jax import lax
from jax.experimental import pallas as pl
from jax.experimental.pallas import tpu as pltpu

F32 = jnp.float32
BF16 = jnp.bfloat16

D_MODEL = 2048
CHUNK = 64
D_FF = 5632
RET_HEADS = 8
RET_DK = D_MODEL // RET_HEADS
RET_DV = 2 * D_MODEL // RET_HEADS
MLA_HEADS = 16
QK_NOPE = 128
QK_ROPE = 64
QK_HEAD = QK_NOPE + QK_ROPE
V_HEAD = 128
KV_LORA = 512
Q_LORA = 512
ROPE_BASE = 10000.0
EPS = 1e-6
NEG_INF = -1e30
LOG2E = 1.4426950408889634

VMEM_LIMIT_BYTES = 60 * 1024 * 1024

FFN_TM = 1024
FFN_TF = 512
FFN_NORM_CHUNKS = 4


def _params(*semantics):
    return pltpu.CompilerParams(dimension_semantics=semantics,
                                vmem_limit_bytes=VMEM_LIMIT_BYTES)


def _rms_scale(x):
    return lax.rsqrt(jnp.mean(x * x, axis=-1, keepdims=True) + EPS)


def _dot(a, b):
    return jnp.dot(a, b, preferred_element_type=F32)


def _dot_nt(a, b):
    return lax.dot_general(a, b, (((1,), (1,)), ((), ())), preferred_element_type=F32)


def _dot_tn(a, b):
    return lax.dot_general(a, b, (((0,), (0,)), ((), ())), preferred_element_type=F32)


def _silu(x):
    return x * jax.nn.sigmoid(x)


def _ffn_kernel(x_ref, g_ref, wa_ref, wb_ref, wo_ref, o_ref, xn_ref):
    j = pl.program_id(1)

    def update(xn):
        a = _dot(xn, wa_ref[...])
        b = _dot(xn, wb_ref[...])
        h = (_silu(a) * (0.5 * b)).astype(BF16)
        return _dot(h, wo_ref[...])

    @pl.when(j == 0)
    def _():
        tc = x_ref.shape[0] // FFN_NORM_CHUNKS

        def norm(c):
            x = x_ref[c * tc:(c + 1) * tc, :]
            xn = (x * _rms_scale(x) * g_ref[...]).astype(BF16)
            xn_ref[c * tc:(c + 1) * tc, :] = xn
            return x, xn

        cur = norm(0)
        for c in range(FFN_NORM_CHUNKS):
            nxt = norm(c + 1) if c + 1 < FFN_NORM_CHUNKS else None
            x, xn = cur
            o_ref[c * tc:(c + 1) * tc, :] = x + update(xn)
            cur = nxt

    @pl.when(j > 0)
    def _():
        o_ref[...] += update(xn_ref[...])


def _ffn(x, g, w_in, w_out, layer, *, tm, tf):
    m, d = x.shape
    nf = D_FF // tf
    g = g[layer]
    return pl.pallas_call(
        _ffn_kernel,
        out_shape=jax.ShapeDtypeStruct((m, d), F32),
        grid=(m // tm, nf),
        in_specs=[
            pl.BlockSpec((tm, d), lambda i, j: (i, 0)),
            pl.BlockSpec((1, d), lambda i, j: (0, 0)),
            pl.BlockSpec((None, d, tf), lambda i, j: (layer, 0, j)),
            pl.BlockSpec((None, d, tf), lambda i, j: (layer, 0, nf + j)),
            pl.BlockSpec((None, tf, d), lambda i, j: (layer, j, 0)),
        ],
        out_specs=pl.BlockSpec((tm, d), lambda i, j: (i, 0)),
        scratch_shapes=[pltpu.VMEM((tm, d), BF16)],
        compiler_params=_params("parallel", "arbitrary"),
        name="ffn",
    )(x, g.reshape(1, d), w_in, w_in, w_out)


def _ret_in_kernel(x_ref, g_ref, w_ref, cos_ref, sin_ref, o_ref, xn_ref, *, rope_blocks, heads_per_block):
    j = pl.program_id(1)

    @pl.when(j == 0)
    def _():
        x = x_ref[...]
        xn_ref[...] = (x * _rms_scale(x) * g_ref[...]).astype(BF16)

    half = RET_DK // 2

    @pl.when(j < 2 * rope_blocks)
    def _():
        scale = jnp.where(j < rope_blocks, RET_DK ** -0.5, 1.0).astype(F32)
        cos = cos_ref[...] * scale
        sin = sin_ref[...] * scale
        xn = xn_ref[...]
        for h in range(heads_per_block):
            lo = h * RET_DK
            y = _dot(xn, w_ref[:, lo:lo + RET_DK])
            x1 = y[:, :half]
            x2 = y[:, half:]
            o_ref[:, lo:lo + half] = (x1 * cos - x2 * sin).astype(BF16)
            o_ref[:, lo + half:lo + RET_DK] = (x1 * sin + x2 * cos).astype(BF16)

    @pl.when(j >= 2 * rope_blocks)
    def _():
        o_ref[...] = _dot(xn_ref[...], w_ref[...]).astype(BF16)


def _ret_in(x, g, w, cos, sin, *, tm, tn):
    m, d = x.shape
    n = w.shape[1]
    tab_blocks = cos.shape[0] // tm
    kern = functools.partial(_ret_in_kernel, rope_blocks=(RET_HEADS * RET_DK) // tn,
                             heads_per_block=tn // RET_DK)
    return pl.pallas_call(
        kern,
        out_shape=jax.ShapeDtypeStruct((m, n), BF16),
        grid=(m // tm, n // tn),
        in_specs=[
            pl.BlockSpec((tm, d), lambda i, j: (i, 0)),
            pl.BlockSpec((1, d), lambda i, j: (0, 0)),
            pl.BlockSpec((d, tn), lambda i, j: (0, j)),
            pl.BlockSpec((tm, RET_DK // 2), lambda i, j: (i % tab_blocks, 0)),
            pl.BlockSpec((tm, RET_DK // 2), lambda i, j: (i % tab_blocks, 0)),
        ],
        out_specs=pl.BlockSpec((tm, tn), lambda i, j: (i, j)),
        scratch_shapes=[pltpu.VMEM((tm, d), BF16)],
        compiler_params=_params("parallel", "arbitrary"),
        name="ret_in",
    )(x, g.reshape(1, d), w, cos, sin)


def _ret_core_kernel(lg_ref, q_ref, k_ref, v_ref, g_ref, gain_ref, *rest, chunk, heads, has_state):
    if has_state:
        s0_ref, y_ref, s_ref = rest
    else:
        y_ref, s_ref = rest
    head0 = pl.program_id(1) * heads
    n_chunks = q_ref.shape[0] // chunk
    row = lax.broadcasted_iota(jnp.int32, (chunk, chunk), 0)
    col = lax.broadcasted_iota(jnp.int32, (chunk, chunk), 1)
    diff = (row - col).astype(F32)
    n = lax.broadcasted_iota(jnp.int32, (chunk, 1), 0).astype(F32)

    def head_consts(hh):
        lg = lg_ref[head0 + hh]
        decay = jnp.where(diff >= 0, jnp.exp(lg * jnp.maximum(diff, 0.0)), 0.0)
        cross = jnp.exp(lg * (n + 1.0))
        kdec = jnp.exp(lg * (chunk - 1.0 - n))
        carry = jnp.exp(lg * jnp.full((1, RET_DV), float(chunk), F32))
        return decay, cross, kdec, carry

    def mix(hh, c, consts):
        decay, cross, kdec, carry = consts
        rows = slice(c * chunk, (c + 1) * chunk)
        q = q_ref[rows, hh * RET_DK:(hh + 1) * RET_DK]
        k = k_ref[rows, hh * RET_DK:(hh + 1) * RET_DK]
        v = v_ref[rows, hh * RET_DV:(hh + 1) * RET_DV]
        inner = (_dot_nt(q, k) * decay).astype(BF16)
        if c > 0:
            s_old = s_ref[0, hh]
        elif has_state:
            s_old = s0_ref[0, hh]
        else:
            s_old = None
        kd = (k.astype(F32) * kdec).astype(BF16)
        o = _dot(inner, v)
        s_new = _dot_tn(kd, v)
        if s_old is not None:
            o = o + cross * _dot(q, s_old.astype(BF16))
            s_new = carry * s_old + s_new
        s_ref[0, hh] = s_new
        return o

    def norm_gate(hh, c, o):
        rows = slice(c * chunk, (c + 1) * chunk)
        cols = slice(hh * RET_DV, (hh + 1) * RET_DV)
        mu = jnp.mean(o, axis=-1, keepdims=True)
        dev = o - mu
        var = jnp.mean(dev * dev, axis=-1, keepdims=True)
        of = dev * lax.rsqrt(var + EPS) * gain_ref[:, cols]
        y_ref[rows, cols] = (_silu(g_ref[rows, cols].astype(F32)) * of).astype(BF16)

    consts = None
    pending = None
    for hh in range(heads):
        for c in range(n_chunks):
            if c == 0:
                consts = head_consts(hh)
            o = mix(hh, c, consts)
            if pending is not None:
                norm_gate(*pending)
            pending = (hh, c, o)
    norm_gate(*pending)


def _ret_core(qkvg, log_gamma, gain, state, *, batch, seq, chunk, heads):
    m = qkvg.shape[0]
    hk = RET_HEADS * RET_DK
    hv = RET_HEADS * RET_DV
    wk = heads * RET_DK
    wv = heads * RET_DV
    kblk = hk // wk
    vblk = 2 * hk // wv
    gblk = vblk + hv // wv
    has_state = state is not None
    in_specs = [
        pl.BlockSpec(memory_space=pltpu.SMEM),
        pl.BlockSpec((seq, wk), lambda b, g: (b, g)),
        pl.BlockSpec((seq, wk), lambda b, g: (b, kblk + g)),
        pl.BlockSpec((seq, wv), lambda b, g: (b, vblk + g)),
        pl.BlockSpec((seq, wv), lambda b, g: (b, gblk + g)),
        pl.BlockSpec((1, wv), lambda b, g: (0, g)),
    ]
    args = [log_gamma, qkvg, qkvg, qkvg, qkvg, gain.reshape(1, hv)]
    state_spec = pl.BlockSpec((1, heads, RET_DK, RET_DV), lambda b, g: (b, g, 0, 0))
    if has_state:
        in_specs.append(state_spec)
        args.append(state)
    kern = functools.partial(_ret_core_kernel, chunk=chunk, heads=heads, has_state=has_state)
    return pl.pallas_call(
        kern,
        out_shape=(jax.ShapeDtypeStruct((m, hv), BF16),
                   jax.ShapeDtypeStruct((batch, RET_HEADS, RET_DK, RET_DV), F32)),
        grid=(batch, RET_HEADS // heads),
        in_specs=in_specs,
        out_specs=(pl.BlockSpec((seq, wv), lambda b, g: (b, g)), state_spec),
        compiler_params=_params("parallel", "parallel"),
        name="ret_core",
    )(*args)


def _mm_res_kernel(a_ref, w_ref, r_ref, o_ref):
    o_ref[...] = r_ref[...] + _dot(a_ref[...], w_ref[...])


def _mm_res(a, w, res, *, tm):
    m, k = a.shape
    n = w.shape[1]
    return pl.pallas_call(
        _mm_res_kernel,
        out_shape=jax.ShapeDtypeStruct((m, n), F32),
        grid=(m // tm,),
        in_specs=[
            pl.BlockSpec((tm, k), lambda i: (i, 0)),
            pl.BlockSpec((k, n), lambda i: (0, 0), pipeline_mode=pl.Buffered(1)),
            pl.BlockSpec((tm, n), lambda i: (i, 0)),
        ],
        out_specs=pl.BlockSpec((tm, n), lambda i: (i, 0)),
        compiler_params=_params("parallel"),
        name="mm_res",
    )(a, w, res)


def _latent_kernel(x_ref, g_ref, w_ref, gl_ref, cos_ref, sin_ref, c_ref, kr_ref, *, row_chunks):
    half = QK_ROPE // 2
    tc = x_ref.shape[0] // row_chunks

    def norm(c):
        x = x_ref[c * tc:(c + 1) * tc, :]
        return (x * _rms_scale(x) * g_ref[...]).astype(BF16)

    def project(c, xn):
        rows = slice(c * tc, (c + 1) * tc)
        y = _dot(xn, w_ref[...])
        cl = y[:, :KV_LORA]
        c_ref[rows, :] = cl * _rms_scale(cl) * gl_ref[...]
        x1 = y[:, KV_LORA:KV_LORA + half]
        x2 = y[:, KV_LORA + half:KV_LORA + QK_ROPE]
        cos = cos_ref[rows, :]
        sin = sin_ref[rows, :]
        kr_ref[rows, :] = jnp.concatenate([x1 * cos - x2 * sin, x1 * sin + x2 * cos], axis=-1)

    xn = norm(0)
    for c in range(row_chunks):
        xn_next = norm(c + 1) if c + 1 < row_chunks else None
        project(c, xn)
        xn = xn_next


def _latent(x, g, w, g_lat, cos, sin, *, tm):
    m, d = x.shape
    n = w.shape[1]
    half = QK_ROPE // 2
    tab_blocks = cos.shape[0] // tm
    return pl.pallas_call(
        functools.partial(_latent_kernel, row_chunks=4),
        out_shape=(jax.ShapeDtypeStruct((m, KV_LORA), F32),
                   jax.ShapeDtypeStruct((m, QK_ROPE), F32)),
        grid=(m // tm,),
        in_specs=[
            pl.BlockSpec((tm, d), lambda i: (i, 0)),
            pl.BlockSpec((1, d), lambda i: (0, 0)),
            pl.BlockSpec((d, n), lambda i: (0, 0)),
            pl.BlockSpec((1, KV_LORA), lambda i: (0, 0)),
            pl.BlockSpec((tm, half), lambda i: (i % tab_blocks, 0)),
            pl.BlockSpec((tm, half), lambda i: (i % tab_blocks, 0)),
        ],
        out_specs=(pl.BlockSpec((tm, KV_LORA), lambda i: (i, 0)),
                   pl.BlockSpec((tm, QK_ROPE), lambda i: (i, 0))),
        compiler_params=_params("parallel"),
        name="latent",
    )(x, g.reshape(1, d), w, g_lat.reshape(1, KV_LORA), cos, sin)


def _kv_up_kernel(*refs, seg_rows):
    n_seg = len(seg_rows)
    wk_ref, wv_ref, gk_ref, k_ref, v_ref = refs[2 * n_seg:]
    g_nope = gk_ref[:, :QK_NOPE]
    g_rope = gk_ref[:, QK_NOPE:]
    lo = 0
    for s, rows in enumerate(seg_rows):
        c = refs[2 * s][...].astype(BF16)
        kr = refs[2 * s + 1][...]
        kn = _dot(c, wk_ref[...])
        v_ref[lo:lo + rows, :] = _dot(c, wv_ref[...]).astype(BF16)
        ss_rope = jnp.sum(kr * kr, axis=-1, keepdims=True)
        for h in range(MLA_HEADS):
            knh = kn[:, h * QK_NOPE:(h + 1) * QK_NOPE]
            ss = jnp.sum(knh * knh, axis=-1, keepdims=True) + ss_rope
            r = lax.rsqrt(ss * (1.0 / QK_HEAD) + EPS)
            k_ref[h, lo:lo + rows, :QK_NOPE] = (knh * r * g_nope).astype(BF16)
            k_ref[h, lo:lo + rows, QK_NOPE:] = (kr * r * g_rope).astype(BF16)
        lo += rows


def _kv_up(segments, wk, wv, gk, *, steps):
    seg_rows = tuple(c.shape[0] // steps for c, _ in segments)
    tm = sum(seg_rows)
    m = steps * tm
    in_specs, args = [], []
    for (c, kr), rows in zip(segments, seg_rows):
        in_specs += [pl.BlockSpec((rows, KV_LORA), lambda i: (i, 0)),
                     pl.BlockSpec((rows, QK_ROPE), lambda i: (i, 0))]
        args += [c, kr]
    in_specs += [pl.BlockSpec(wk.shape, lambda i: (0, 0)),
                 pl.BlockSpec(wv.shape, lambda i: (0, 0)),
                 pl.BlockSpec((1, QK_HEAD), lambda i: (0, 0))]
    return pl.pallas_call(
        functools.partial(_kv_up_kernel, seg_rows=seg_rows),
        out_shape=(jax.ShapeDtypeStruct((MLA_HEADS, m, QK_HEAD), BF16),
                   jax.ShapeDtypeStruct((m, MLA_HEADS * V_HEAD), BF16)),
        grid=(steps,),
        in_specs=in_specs,
        out_specs=(pl.BlockSpec((MLA_HEADS, tm, QK_HEAD), lambda i: (0, i, 0)),
                   pl.BlockSpec((tm, MLA_HEADS * V_HEAD), lambda i: (i, 0))),
        compiler_params=_params("parallel"),
        name="kv_up",
    )(*args, wk, wv, gk.reshape(1, QK_HEAD))


Q_SLOT = 2 * QK_NOPE


def _q_proj_kernel(x_ref, g_ref, wdq_ref, gl_ref, wuq_ref, cos_ref, sin_ref, gq_ref, q_ref, *, row_chunks):
    g_nope = gq_ref[:, :QK_NOPE]
    g_rope = gq_ref[:, QK_NOPE:]
    tc = x_ref.shape[0] // row_chunks

    def project(c):
        rows = slice(c * tc, (c + 1) * tc)
        x = x_ref[rows, :]
        hn = (x * _rms_scale(x) * g_ref[...]).astype(BF16)
        ql = _dot(hn, wdq_ref[...])
        qn = (ql * _rms_scale(ql) * gl_ref[...]).astype(BF16)
        return _dot(qn, wuq_ref[...])

    def per_head(c, q):
        rows = slice(c * tc, (c + 1) * tc)
        cos = cos_ref[rows, :]
        sin = sin_ref[rows, :]
        for h in range(MLA_HEADS):
            nh = q[:, h * Q_SLOT:h * Q_SLOT + QK_NOPE]
            t = q[:, h * Q_SLOT + QK_NOPE:(h + 1) * Q_SLOT]
            rp = t * cos + pltpu.roll(t, QK_ROPE, 1) * sin
            ss = jnp.sum(nh * nh + rp * rp, axis=-1, keepdims=True)
            r = lax.rsqrt(ss * (1.0 / QK_HEAD) + EPS) * (QK_HEAD ** -0.5 * LOG2E)
            q_ref[h, rows, :QK_NOPE] = (nh * r * g_nope).astype(BF16)
            q_ref[h, rows, QK_NOPE:] = (rp[:, :QK_ROPE] * r * g_rope).astype(BF16)

    q = project(0)
    for c in range(row_chunks):
        q_next = project(c + 1) if c + 1 < row_chunks else None
        per_head(c, q)
        q = q_next


def _q_proj(x, g, wdq, g_lat, wuq, cos, sin, gq, *, tm):
    m, d = x.shape
    tab_blocks = cos.shape[0] // tm
    tw = cos.shape[1]
    return pl.pallas_call(
        functools.partial(_q_proj_kernel, row_chunks=2),
        out_shape=jax.ShapeDtypeStruct((MLA_HEADS, m, QK_HEAD), BF16),
        grid=(m // tm,),
        in_specs=[
            pl.BlockSpec((tm, d), lambda i: (i, 0)),
            pl.BlockSpec((1, d), lambda i: (0, 0)),
            pl.BlockSpec(wdq.shape, lambda i: (0, 0)),
            pl.BlockSpec((1, Q_LORA), lambda i: (0, 0)),
            pl.BlockSpec(wuq.shape, lambda i: (0, 0)),
            pl.BlockSpec((tm, tw), lambda i: (i % tab_blocks, 0)),
            pl.BlockSpec((tm, tw), lambda i: (i % tab_blocks, 0)),
            pl.BlockSpec((1, QK_HEAD), lambda i: (0, 0)),
        ],
        out_specs=pl.BlockSpec((MLA_HEADS, tm, QK_HEAD), lambda i: (0, i, 0)),
        compiler_params=_params("parallel"),
        name="q_proj",
    )(x, g.reshape(1, d), wdq, g_lat.reshape(1, Q_LORA), wuq, cos, sin, gq.reshape(1, QK_HEAD))


def _attn_causal_kernel(q_ref, k_ref, v_ref, o_ref, *, heads, tq):
    n_q = q_ref.shape[1] // tq
    krow = lax.broadcasted_iota(jnp.int32, (tq, tq), 0) // CHUNK
    qcol = lax.broadcasted_iota(jnp.int32, (tq, tq), 1) // CHUNK
    keep = krow <= qcol

    def scores(h, qi):
        lo = qi * tq
        q = q_ref[h, lo:lo + tq, :]
        st_d = jnp.where(keep, _dot_nt(k_ref[h, lo:lo + tq, :], q), NEG_INF)
        st_p = _dot_nt(k_ref[h, :lo, :], q) if qi > 0 else None
        return st_d, st_p

    def softmax(st):
        st_d, st_p = st
        m = jnp.max(st_d, axis=0, keepdims=True)
        if st_p is not None:
            m = jnp.maximum(m, jnp.max(st_p, axis=0, keepdims=True))
        pt_d = jnp.exp2(st_d - m)
        l = jnp.sum(pt_d, axis=0, keepdims=True)
        pt_p = None
        if st_p is not None:
            pt_p = jnp.exp2(st_p - m)
            l = l + jnp.sum(pt_p, axis=0, keepdims=True)
            pt_p = pt_p.astype(BF16)
        return pt_d.astype(BF16), pt_p, l

    def values(h, qi, p):
        pt_d, pt_p, l = p
        lo = qi * tq
        cols = slice(h * V_HEAD, (h + 1) * V_HEAD)
        acc_t = _dot_tn(v_ref[lo:lo + tq, cols], pt_d)
        if pt_p is not None:
            acc_t = acc_t + _dot_tn(v_ref[:lo, cols], pt_p)
        o_ref[lo:lo + tq, cols] = (acc_t / l).T.astype(BF16)

    work = [(h, qi) for h in range(heads) for qi in range(n_q)]
    s_q, p_q = {}, {}
    for step in range(len(work) + 2):
        if step < len(work):
            s_q[step] = scores(*work[step])
        if 0 <= step - 1 < len(work):
            p_q[step - 1] = softmax(s_q.pop(step - 1))
        if 0 <= step - 2 < len(work):
            values(*work[step - 2], p_q.pop(step - 2))


def _attn_full_kernel(q_ref, k_ref, v_ref, o_ref, *, heads):
    def scores(h):
        return _dot_nt(q_ref[h], k_ref[h])

    def softmax(s):
        m = jnp.max(s, axis=-1, keepdims=True)
        p = jnp.exp2(s - m)
        return p.astype(BF16), jnp.sum(p, axis=-1, keepdims=True)

    def values(h, pl_):
        p, l = pl_
        cols = slice(h * V_HEAD, (h + 1) * V_HEAD)
        o_ref[:, cols] = (_dot(p, v_ref[:, cols]) / l).astype(BF16)

    s_q, p_q = {}, {}
    for step in range(heads + 2):
        if step < heads:
            s_q[step] = scores(step)
        if 0 <= step - 1 < heads:
            p_q[step - 1] = softmax(s_q.pop(step - 1))
        if 0 <= step - 2 < heads:
            values(step - 2, p_q.pop(step - 2))


def _attention(q, k, v, *, batch, t_q, t_kv, heads, tq, causal):
    hg = MLA_HEADS // heads
    if causal:
        kern = functools.partial(_attn_causal_kernel, heads=heads, tq=tq)
    else:
        kern = functools.partial(_attn_full_kernel, heads=heads)
    return pl.pallas_call(
        kern,
        out_shape=jax.ShapeDtypeStruct((batch * t_q, MLA_HEADS * V_HEAD), BF16),
        grid=(batch, hg),
        in_specs=[
            pl.BlockSpec((heads, t_q, QK_HEAD), lambda b, g: (g, b, 0)),
            pl.BlockSpec((heads, t_kv, QK_HEAD), lambda b, g: (g, b, 0)),
            pl.BlockSpec((t_kv, heads * V_HEAD), lambda b, g: (b, g)),
        ],
        out_specs=pl.BlockSpec((t_q, heads * V_HEAD), lambda b, g: (b, g)),
        compiler_params=_params("parallel", "parallel"),
        name="attention",
    )(q, k, v)


def _rope_tables(pos, half, reps, rows):
    inv = jnp.power(ROPE_BASE, -jnp.arange(half, dtype=F32) / half)
    ang = pos.astype(F32)[:, None] * inv[None, :]
    cos = jnp.tile(jnp.cos(ang), (max(rows // pos.shape[0], 1), reps))
    sin = jnp.tile(jnp.sin(ang), (max(rows // pos.shape[0], 1), reps))
    return cos, sin


def _trunk(x, pos, ret_state, ckv_past, krope_past, p, *, causal):
    batch, seq, d = x.shape
    m = batch * seq
    tm = 1024
    h = x.reshape(m, d)

    h = _ffn(h, p["ffn1_norm"], p["ffn1_w_in"], p["ffn1_w_out"], 0, tm=FFN_TM, tf=FFN_TF)
    cos, sin = _rope_tables(pos, RET_DK // 2, 1, tm)
    qkvg = _ret_in(h, p["mix_norm"][0], p["ret_w_in"][0], cos, sin, tm=tm, tn=1024)
    log_gamma = jnp.log1p(-jnp.exp2(-5.0 - jnp.arange(RET_HEADS, dtype=F32)))
    chunk = min(seq, 256)
    y, s_fin = _ret_core(qkvg, log_gamma, p["ret_gn_gain"][0],
                         None if ret_state is None else ret_state[0],
                         batch=batch, seq=seq, chunk=chunk,
                         heads=max(1, min(RET_HEADS, 512 // seq)))
    h = _mm_res(y, p["ret_w_out"][0], h, tm=512)
    h = _ffn(h, p["ffn2_norm"], p["ffn2_w_in"], p["ffn2_w_out"], 0, tm=FFN_TM, tf=FFN_TF)

    cos, sin = _rope_tables(pos, QK_ROPE // 2, 1, 512)
    c_new, kr_new = _latent(h, p["kv_norm"], p["w_dkv"], p["kv_lat_norm"], cos, sin, tm=512)
    if ckv_past is None:
        t_kv = seq
        segments, steps = [(c_new, kr_new)], m // 512
    else:
        past = ckv_past.shape[1]
        t_kv = past + seq
        segments = [(ckv_past.reshape(batch * past, KV_LORA), krope_past.reshape(batch * past, QK_ROPE)),
                    (c_new, kr_new)]
        steps = batch
    k_sh, v_sh = _kv_up(segments, p["w_uk"], p["w_uv"], p["k_norm"], steps=steps)

    h = _ffn(h, p["ffn1_norm"], p["ffn1_w_in"], p["ffn1_w_out"], 1, tm=FFN_TM, tf=FFN_TF)
    cos, sin = _rope_tables(pos, QK_ROPE // 2, 1, 512)
    zeros = jnp.zeros_like(cos)
    cos = jnp.concatenate([cos, cos, zeros, zeros], axis=-1)
    sin = jnp.concatenate([-sin, sin, zeros, zeros], axis=-1)
    q = _q_proj(h, p["mix_norm"][1], p["w_dq"][0], p["q_lat_norm"][0], p["w_uq"][0], cos, sin,
                p["q_norm"][0], tm=512)
    if causal:
        o = _attention(q, k_sh, v_sh, batch=batch, t_q=seq, t_kv=t_kv, heads=1, tq=512, causal=True)
    else:
        o = _attention(q, k_sh, v_sh, batch=batch, t_q=seq, t_kv=t_kv, heads=MLA_HEADS, tq=seq, causal=False)
    h = _mm_res(o, p["w_o"][0], h, tm=512)
    h = _ffn(h, p["ffn2_norm"], p["ffn2_w_in"], p["ffn2_w_out"], 1, tm=FFN_TM, tf=FFN_TF)

    return (h.reshape(batch, seq, d), s_fin[None],
            c_new.reshape(batch, seq, KV_LORA), kr_new.reshape(batch, seq, QK_ROPE))


def kernel(x_prompt, x_sample, state_ret, cache_ckv, cache_krope, ffn1_norm, ffn1_w_in, ffn1_w_out, mix_norm, ffn2_norm, ffn2_w_in, ffn2_w_out, ret_w_in, ret_gn_gain, ret_w_out, kv_norm, w_dkv, kv_lat_norm, w_ukv, k_norm, w_dq, q_lat_norm, w_uq, q_norm, w_o):
    bf = lambda w: w.astype(BF16)
    w_ukv_h = w_ukv.reshape(KV_LORA, MLA_HEADS, QK_NOPE + V_HEAD)
    w_uk = w_ukv_h[:, :, :QK_NOPE].reshape(KV_LORA, MLA_HEADS * QK_NOPE)
    w_uv = w_ukv_h[:, :, QK_NOPE:].reshape(KV_LORA, MLA_HEADS * V_HEAD)
    half = QK_ROPE // 2
    w_uq_h = w_uq.reshape(w_uq.shape[0], Q_LORA, MLA_HEADS, QK_HEAD)
    w_lo = w_uq_h[..., QK_NOPE:QK_NOPE + half]
    w_hi = w_uq_h[..., QK_NOPE + half:]
    w_uq_p = jnp.concatenate([w_uq_h[..., :QK_NOPE], w_lo, w_hi, w_hi, w_lo], axis=-1)
    w_uq_p = w_uq_p.reshape(w_uq.shape[0], Q_LORA, MLA_HEADS * Q_SLOT)
    p = dict(ffn1_norm=ffn1_norm, ffn1_w_in=bf(ffn1_w_in), ffn1_w_out=bf(ffn1_w_out),
             mix_norm=mix_norm, ffn2_norm=ffn2_norm, ffn2_w_in=bf(ffn2_w_in), ffn2_w_out=bf(ffn2_w_out),
             ret_w_in=bf(ret_w_in), ret_gn_gain=ret_gn_gain, ret_w_out=bf(ret_w_out),
             kv_norm=kv_norm, w_dkv=bf(w_dkv), kv_lat_norm=kv_lat_norm,
             w_uk=bf(w_uk), w_uv=bf(w_uv), k_norm=k_norm,
             w_dq=bf(w_dq), q_lat_norm=q_lat_norm, w_uq=bf(w_uq_p), q_norm=q_norm, w_o=bf(w_o))
    pos_prompt = jnp.arange(x_prompt.shape[1])
    past = cache_ckv.shape[1]
    pos_sample = past + jnp.arange(x_sample.shape[1])
    y_p, ret_p, ckv_p, kr_p = _trunk(x_prompt, pos_prompt, None, None, None, p, causal=True)
    y_s, ret_s, ckv_s, kr_s = _trunk(x_sample, pos_sample, state_ret, cache_ckv, cache_krope, p, causal=False)
    return (y_p, y_s, ret_p, ckv_p, kr_p, ret_s, ckv_s, kr_s)
```

```python
import functools

import jax
import jax.numpy as jnp
from jax import lax
from jax.experimental import pallas as pl
from jax.experimental.pallas import tpu as pltpu

F32 = jnp.float32
BF16 = jnp.bfloat16

D_MODEL = 2048
CHUNK = 64
D_FF = 5632
RET_HEADS = 8
RET_DK = D_MODEL // RET_HEADS
RET_DV = 2 * D_MODEL // RET_HEADS
MLA_HEADS = 16
QK_NOPE = 128
QK_ROPE = 64
QK_HEAD = QK_NOPE + QK_ROPE
V_HEAD = 128
KV_LORA = 512
Q_LORA = 512
ROPE_BASE = 10000.0
EPS = 1e-6
NEG_INF = -1e30
LOG2E = 1.4426950408889634

VMEM_LIMIT_BYTES = 60 * 1024 * 1024

FFN_TM = 1024
FFN_TF = 512
FFN_NORM_CHUNKS = 4


def _params(*semantics):
    return pltpu.CompilerParams(dimension_semantics=semantics,
                                vmem_limit_bytes=VMEM_LIMIT_BYTES)


def _rms_scale(x):
    return lax.rsqrt(jnp.mean(x * x, axis=-1, keepdims=True) + EPS)


def _dot(a, b):
    return jnp.dot(a, b, preferred_element_type=F32)


def _dot_nt(a, b):
    return lax.dot_general(a, b, (((1,), (1,)), ((), ())), preferred_element_type=F32)


def _dot_tn(a, b):
    return lax.dot_general(a, b, (((0,), (0,)), ((), ())), preferred_element_type=F32)


def _silu(x):
    return x * jax.nn.sigmoid(x)


def _ffn_kernel(x_ref, g_ref, wa_ref, wb_ref, wo_ref, o_ref, xn_ref):
    j = pl.program_id(1)

    def update(xn):
        a = _dot(xn, wa_ref[...])
        b = _dot(xn, wb_ref[...])
        h = (_silu(a) * (0.5 * b)).astype(BF16)
        return _dot(h, wo_ref[...])

    @pl.when(j == 0)
    def _():
        tc = x_ref.shape[0] // FFN_NORM_CHUNKS

        def norm(c):
            x = x_ref[c * tc:(c + 1) * tc, :]
            xn = (x * _rms_scale(x) * g_ref[...]).astype(BF16)
            xn_ref[c * tc:(c + 1) * tc, :] = xn
            return x, xn

        cur = norm(0)
        for c in range(FFN_NORM_CHUNKS):
            nxt = norm(c + 1) if c + 1 < FFN_NORM_CHUNKS else None
            x, xn = cur
            o_ref[c * tc:(c + 1) * tc, :] = x + update(xn)
            cur = nxt

    @pl.when(j > 0)
    def _():
        o_ref[...] += update(xn_ref[...])


def _ffn(x, g, w_in, w_out, layer, *, tm, tf):
    m, d = x.shape
    nf = D_FF // tf
    g = g[layer]
    return pl.pallas_call(
        _ffn_kernel,
        out_shape=jax.ShapeDtypeStruct((m, d), F32),
        grid=(m // tm, nf),
        in_specs=[
            pl.BlockSpec((tm, d), lambda i, j: (i, 0)),
            pl.BlockSpec((1, d), lambda i, j: (0, 0)),
            pl.BlockSpec((None, d, tf), lambda i, j: (layer, 0, j)),
            pl.BlockSpec((None, d, tf), lambda i, j: (layer, 0, nf + j)),
            pl.BlockSpec((None, tf, d), lambda i, j: (layer, j, 0)),
        ],
        out_specs=pl.BlockSpec((tm, d), lambda i, j: (i, 0)),
        scratch_shapes=[pltpu.VMEM((tm, d), BF16)],
        compiler_params=_params("parallel", "arbitrary"),
        name="ffn",
    )(x, g.reshape(1, d), w_in, w_in, w_out)


def _ret_in_kernel(x_ref, g_ref, w_ref, cos_ref, sin_ref, o_ref, xn_ref, *, rope_blocks, heads_per_block):
    j = pl.program_id(1)

    @pl.when(j == 0)
    def _():
        x = x_ref[...]
        xn_ref[...] = (x * _rms_scale(x) * g_ref[...]).astype(BF16)

    half = RET_DK // 2

    @pl.when(j < 2 * rope_blocks)
    def _():
        scale = jnp.where(j < rope_blocks, RET_DK ** -0.5, 1.0).astype(F32)
        cos = cos_ref[...] * scale
        sin = sin_ref[...] * scale
        xn = xn_ref[...]
        for h in range(heads_per_block):
            lo = h * RET_DK
            y = _dot(xn, w_ref[:, lo:lo + RET_DK])
            x1 = y[:, :half]
            x2 = y[:, half:]
            o_ref[:, lo:lo + half] = (x1 * cos - x2 * sin).astype(BF16)
            o_ref[:, lo + half:lo + RET_DK] = (x1 * sin + x2 * cos).astype(BF16)

    @pl.when(j >= 2 * rope_blocks)
    def _():
        o_ref[...] = _dot(xn_ref[...], w_ref[...]).astype(BF16)


def _ret_in(x, g, w, cos, sin, *, tm, tn):
    m, d = x.shape
    n = w.shape[1]
    tab_blocks = cos.shape[0] // tm
    kern = functools.partial(_ret_in_kernel, rope_blocks=(RET_HEADS * RET_DK) // tn,
                             heads_per_block=tn // RET_DK)
    return pl.pallas_call(
        kern,
        out_shape=jax.ShapeDtypeStruct((m, n), BF16),
        grid=(m // tm, n // tn),
        in_specs=[
            pl.BlockSpec((tm, d), lambda i, j: (i, 0)),
            pl.BlockSpec((1, d), lambda i, j: (0, 0)),
            pl.BlockSpec((d, tn), lambda i, j: (0, j)),
            pl.BlockSpec((tm, RET_DK // 2), lambda i, j: (i % tab_blocks, 0)),
            pl.BlockSpec((tm, RET_DK // 2), lambda i, j: (i % tab_blocks, 0)),
        ],
        out_specs=pl.BlockSpec((tm, tn), lambda i, j: (i, j)),
        scratch_shapes=[pltpu.VMEM((tm, d), BF16)],
        compiler_params=_params("parallel", "arbitrary"),
        name="ret_in",
    )(x, g.reshape(1, d), w, cos, sin)


def _ret_core_kernel(lg_ref, q_ref, k_ref, v_ref, g_ref, gain_ref, *rest, chunk, heads, has_state):
    if has_state:
        s0_ref, y_ref, s_ref = rest
    else:
        y_ref, s_ref = rest
    head0 = pl.program_id(1) * heads
    n_chunks = q_ref.shape[0] // chunk
    row = lax.broadcasted_iota(jnp.int32, (chunk, chunk), 0)
    col = lax.broadcasted_iota(jnp.int32, (chunk, chunk), 1)
    diff = (row - col).astype(F32)
    n = lax.broadcasted_iota(jnp.int32, (chunk, 1), 0).astype(F32)

    def head_consts(hh):
        lg = lg_ref[head0 + hh]
        decay = jnp.where(diff >= 0, jnp.exp(lg * jnp.maximum(diff, 0.0)), 0.0)
        cross = jnp.exp(lg * (n + 1.0))
        kdec = jnp.exp(lg * (chunk - 1.0 - n))
        carry = jnp.exp(lg * jnp.full((1, RET_DV), float(chunk), F32))
        return decay, cross, kdec, carry

    def mix(hh, c, consts):
        decay, cross, kdec, carry = consts
        rows = slice(c * chunk, (c + 1) * chunk)
        q = q_ref[rows, hh * RET_DK:(hh + 1) * RET_DK]
        k = k_ref[rows, hh * RET_DK:(hh + 1) * RET_DK]
        v = v_ref[rows, hh * RET_DV:(hh + 1) * RET_DV]
        inner = (_dot_nt(q, k) * decay).astype(BF16)
        if c > 0:
            s_old = s_ref[0, hh]
        elif has_state:
            s_old = s0_ref[0, hh]
        else:
            s_old = None
        kd = (k.astype(F32) * kdec).astype(BF16)
        o = _dot(inner, v)
        s_new = _dot_tn(kd, v)
        if s_old is not None:
            o = o + cross * _dot(q, s_old.astype(BF16))
            s_new = carry * s_old + s_new
        s_ref[0, hh] = s_new
        return o

    def norm_gate(hh, c, o):
        rows = slice(c * chunk, (c + 1) * chunk)
        cols = slice(hh * RET_DV, (hh + 1) * RET_DV)
        mu = jnp.mean(o, axis=-1, keepdims=True)
        dev = o - mu
        var = jnp.mean(dev * dev, axis=-1, keepdims=True)
        of = dev * lax.rsqrt(var + EPS) * gain_ref[:, cols]
        y_ref[rows, cols] = (_silu(g_ref[rows, cols].astype(F32)) * of).astype(BF16)

    consts = None
    pending = None
    for hh in range(heads):
        for c in range(n_chunks):
            if c == 0:
                consts = head_consts(hh)
            o = mix(hh, c, consts)
            if pending is not None:
                norm_gate(*pending)
            pending = (hh, c, o)
    norm_gate(*pending)


def _ret_core(qkvg, log_gamma, gain, state, *, batch, seq, chunk, heads):
    m = qkvg.shape[0]
    hk = RET_HEADS * RET_DK
    hv = RET_HEADS * RET_DV
    wk = heads * RET_DK
    wv = heads * RET_DV
    kblk = hk // wk
    vblk = 2 * hk // wv
    gblk = vblk + hv // wv
    has_state = state is not None
    in_specs = [
        pl.BlockSpec(memory_space=pltpu.SMEM),
        pl.BlockSpec((seq, wk), lambda b, g: (b, g)),
        pl.BlockSpec((seq, wk), lambda b, g: (b, kblk + g)),
        pl.BlockSpec((seq, wv), lambda b, g: (b, vblk + g)),
        pl.BlockSpec((seq, wv), lambda b, g: (b, gblk + g)),
        pl.BlockSpec((1, wv), lambda b, g: (0, g)),
    ]
    args = [log_gamma, qkvg, qkvg, qkvg, qkvg, gain.reshape(1, hv)]
    state_spec = pl.BlockSpec((1, heads, RET_DK, RET_DV), lambda b, g: (b, g, 0, 0))
    if has_state:
        in_specs.append(state_spec)
        args.append(state)
    kern = functools.partial(_ret_core_kernel, chunk=chunk, heads=heads, has_state=has_state)
    return pl.pallas_call(
        kern,
        out_shape=(jax.ShapeDtypeStruct((m, hv), BF16),
                   jax.ShapeDtypeStruct((batch, RET_HEADS, RET_DK, RET_DV), F32)),
        grid=(batch, RET_HEADS // heads),
        in_specs=in_specs,
        out_specs=(pl.BlockSpec((seq, wv), lambda b, g: (b, g)), state_spec),
        compiler_params=_params("parallel", "parallel"),
        name="ret_core",
    )(*args)


def _mm_res_kernel(a_ref, w_ref, r_ref, o_ref):
    o_ref[...] = r_ref[...] + _dot(a_ref[...], w_ref[...])


def _mm_res(a, w, res, *, tm):
    m, k = a.shape
    n = w.shape[1]
    return pl.pallas_call(
        _mm_res_kernel,
        out_shape=jax.ShapeDtypeStruct((m, n), F32),
        grid=(m // tm,),
        in_specs=[
            pl.BlockSpec((tm, k), lambda i: (i, 0)),
            pl.BlockSpec((k, n), lambda i: (0, 0), pipeline_mode=pl.Buffered(1)),
            pl.BlockSpec((tm, n), lambda i: (i, 0)),
        ],
        out_specs=pl.BlockSpec((tm, n), lambda i: (i, 0)),
        compiler_params=_params("parallel"),
        name="mm_res",
    )(a, w, res)


def _latent_kernel(x_ref, g_ref, w_ref, gl_ref, cos_ref, sin_ref, c_ref, kr_ref, *, row_chunks):
    half = QK_ROPE // 2
    tc = x_ref.shape[0] // row_chunks

    def norm(c):
        x = x_ref[c * tc:(c + 1) * tc, :]
        return (x * _rms_scale(x) * g_ref[...]).astype(BF16)

    def project(c, xn):
        rows = slice(c * tc, (c + 1) * tc)
        y = _dot(xn, w_ref[...])
        cl = y[:, :KV_LORA]
        c_ref[rows, :] = cl * _rms_scale(cl) * gl_ref[...]
        x1 = y[:, KV_LORA:KV_LORA + half]
        x2 = y[:, KV_LORA + half:KV_LORA + QK_ROPE]
        cos = cos_ref[rows, :]
        sin = sin_ref[rows, :]
        kr_ref[rows, :] = jnp.concatenate([x1 * cos - x2 * sin, x1 * sin + x2 * cos], axis=-1)

    xn = norm(0)
    for c in range(row_chunks):
        xn_next = norm(c + 1) if c + 1 < row_chunks else None
        project(c, xn)
        xn = xn_next


def _latent(x, g, w, g_lat, cos, sin, *, tm):
    m, d = x.shape
    n = w.shape[1]
    half = QK_ROPE // 2
    tab_blocks = cos.shape[0] // tm
    return pl.pallas_call(
        functools.partial(_latent_kernel, row_chunks=tm // 128),
        out_shape=(jax.ShapeDtypeStruct((m, KV_LORA), F32),
                   jax.ShapeDtypeStruct((m, QK_ROPE), F32)),
        grid=(m // tm,),
        in_specs=[
            pl.BlockSpec((tm, d), lambda i: (i, 0)),
            pl.BlockSpec((1, d), lambda i: (0, 0)),
            pl.BlockSpec((d, n), lambda i: (0, 0)),
            pl.BlockSpec((1, KV_LORA), lambda i: (0, 0)),
            pl.BlockSpec((tm, half), lambda i: (i % tab_blocks, 0)),
            pl.BlockSpec((tm, half), lambda i: (i % tab_blocks, 0)),
        ],
        out_specs=(pl.BlockSpec((tm, KV_LORA), lambda i: (i, 0)),
                   pl.BlockSpec((tm, QK_ROPE), lambda i: (i, 0))),
        compiler_params=_params("parallel"),
        name="latent",
    )(x, g.reshape(1, d), w, g_lat.reshape(1, KV_LORA), cos, sin)


def _kv_up_kernel(*refs, seg_rows):
    n_seg = len(seg_rows)
    wk_ref, wv_ref, gk_ref, k_ref, v_ref = refs[2 * n_seg:]
    g_nope = gk_ref[:, :QK_NOPE]
    g_rope = gk_ref[:, QK_NOPE:]
    lo = 0
    for s, rows in enumerate(seg_rows):
        c = refs[2 * s][...].astype(BF16)
        kr = refs[2 * s + 1][...]
        kn = _dot(c, wk_ref[...])
        v_ref[lo:lo + rows, :] = _dot(c, wv_ref[...]).astype(BF16)
        ss_rope = jnp.sum(kr * kr, axis=-1, keepdims=True)
        for h in range(MLA_HEADS):
            knh = kn[:, h * QK_NOPE:(h + 1) * QK_NOPE]
            ss = jnp.sum(knh * knh, axis=-1, keepdims=True) + ss_rope
            r = lax.rsqrt(ss * (1.0 / QK_HEAD) + EPS)
            k_ref[h, lo:lo + rows, :QK_NOPE] = (knh * r * g_nope).astype(BF16)
            k_ref[h, lo:lo + rows, QK_NOPE:] = (kr * r * g_rope).astype(BF16)
        lo += rows


def _kv_up(segments, wk, wv, gk, *, steps):
    seg_rows = tuple(c.shape[0] // steps for c, _ in segments)
    tm = sum(seg_rows)
    m = steps * tm
    in_specs, args = [], []
    for (c, kr), rows in zip(segments, seg_rows):
        in_specs += [pl.BlockSpec((rows, KV_LORA), lambda i: (i, 0)),
                     pl.BlockSpec((rows, QK_ROPE), lambda i: (i, 0))]
        args += [c, kr]
    in_specs += [pl.BlockSpec(wk.shape, lambda i: (0, 0)),
                 pl.BlockSpec(wv.shape, lambda i: (0, 0)),
                 pl.BlockSpec((1, QK_HEAD), lambda i: (0, 0))]
    return pl.pallas_call(
        functools.partial(_kv_up_kernel, seg_rows=seg_rows),
        out_shape=(jax.ShapeDtypeStruct((MLA_HEADS, m, QK_HEAD), BF16),
                   jax.ShapeDtypeStruct((m, MLA_HEADS * V_HEAD), BF16)),
        grid=(steps,),
        in_specs=in_specs,
        out_specs=(pl.BlockSpec((MLA_HEADS, tm, QK_HEAD), lambda i: (0, i, 0)),
                   pl.BlockSpec((tm, MLA_HEADS * V_HEAD), lambda i: (i, 0))),
        compiler_params=_params("parallel"),
        name="kv_up",
    )(*args, wk, wv, gk.reshape(1, QK_HEAD))


Q_SLOT = 2 * QK_NOPE


def _q_proj_kernel(x_ref, g_ref, wdq_ref, gl_ref, wuq_ref, cos_ref, sin_ref, gq_ref, q_ref, *, row_chunks):
    g_nope = gq_ref[:, :QK_NOPE]
    g_rope = gq_ref[:, QK_NOPE:]
    tc = x_ref.shape[0] // row_chunks

    def project(c):
        rows = slice(c * tc, (c + 1) * tc)
        x = x_ref[rows, :]
        hn = (x * _rms_scale(x) * g_ref[...]).astype(BF16)
        ql = _dot(hn, wdq_ref[...])
        qn = (ql * _rms_scale(ql) * gl_ref[...]).astype(BF16)
        return _dot(qn, wuq_ref[...])

    def per_head(c, q):
        rows = slice(c * tc, (c + 1) * tc)
        cos = cos_ref[rows, :]
        sin = sin_ref[rows, :]
        for h in range(MLA_HEADS):
            nh = q[:, h * Q_SLOT:h * Q_SLOT + QK_NOPE]
            t = q[:, h * Q_SLOT + QK_NOPE:(h + 1) * Q_SLOT]
            rp = t * cos + pltpu.roll(t, QK_ROPE, 1) * sin
            ss = jnp.sum(nh * nh + rp * rp, axis=-1, keepdims=True)
            r = lax.rsqrt(ss * (1.0 / QK_HEAD) + EPS) * (QK_HEAD ** -0.5 * LOG2E)
            q_ref[h, rows, :QK_NOPE] = (nh * r * g_nope).astype(BF16)
            q_ref[h, rows, QK_NOPE:] = (rp[:, :QK_ROPE] * r * g_rope).astype(BF16)

    q = project(0)
    for c in range(row_chunks):
        q_next = project(c + 1) if c + 1 < row_chunks else None
        per_head(c, q)
        q = q_next


def _q_proj(x, g, wdq, g_lat, wuq, cos, sin, gq, *, tm):
    m, d = x.shape
    tab_blocks = cos.shape[0] // tm
    tw = cos.shape[1]
    return pl.pallas_call(
        functools.partial(_q_proj_kernel, row_chunks=tm // 256),
        out_shape=jax.ShapeDtypeStruct((MLA_HEADS, m, QK_HEAD), BF16),
        grid=(m // tm,),
        in_specs=[
            pl.BlockSpec((tm, d), lambda i: (i, 0)),
            pl.BlockSpec((1, d), lambda i: (0, 0)),
            pl.BlockSpec(wdq.shape, lambda i: (0, 0)),
            pl.BlockSpec((1, Q_LORA), lambda i: (0, 0)),
            pl.BlockSpec(wuq.shape, lambda i: (0, 0)),
            pl.BlockSpec((tm, tw), lambda i: (i % tab_blocks, 0)),
            pl.BlockSpec((tm, tw), lambda i: (i % tab_blocks, 0)),
            pl.BlockSpec((1, QK_HEAD), lambda i: (0, 0)),
        ],
        out_specs=pl.BlockSpec((MLA_HEADS, tm, QK_HEAD), lambda i: (0, i, 0)),
        compiler_params=_params("parallel"),
        name="q_proj",
    )(x, g.reshape(1, d), wdq, g_lat.reshape(1, Q_LORA), wuq, cos, sin, gq.reshape(1, QK_HEAD))


def _attn_causal_kernel(q_ref, k_ref, v_ref, o_ref, *, heads, tq):
    n_q = q_ref.shape[1] // tq
    krow = lax.broadcasted_iota(jnp.int32, (tq, tq), 0) // CHUNK
    qcol = lax.broadcasted_iota(jnp.int32, (tq, tq), 1) // CHUNK
    keep = krow <= qcol

    def scores(h, qi):
        lo = qi * tq
        q = q_ref[h, lo:lo + tq, :]
        st_d = jnp.where(keep, _dot_nt(k_ref[h, lo:lo + tq, :], q), NEG_INF)
        st_p = _dot_nt(k_ref[h, :lo, :], q) if qi > 0 else None
        return st_d, st_p

    def softmax(st):
        st_d, st_p = st
        m = jnp.max(st_d, axis=0, keepdims=True)
        if st_p is not None:
            m = jnp.maximum(m, jnp.max(st_p, axis=0, keepdims=True))
        pt_d = jnp.exp2(st_d - m)
        l = jnp.sum(pt_d, axis=0, keepdims=True)
        pt_p = None
        if st_p is not None:
            pt_p = jnp.exp2(st_p - m)
            l = l + jnp.sum(pt_p, axis=0, keepdims=True)
            pt_p = pt_p.astype(BF16)
        return pt_d.astype(BF16), pt_p, l

    def values(h, qi, p):
        pt_d, pt_p, l = p
        lo = qi * tq
        cols = slice(h * V_HEAD, (h + 1) * V_HEAD)
        acc_t = _dot_tn(v_ref[lo:lo + tq, cols], pt_d)
        if pt_p is not None:
            acc_t = acc_t + _dot_tn(v_ref[:lo, cols], pt_p)
        o_ref[lo:lo + tq, cols] = (acc_t / l).T.astype(BF16)

    work = [(h, qi) for h in range(heads) for qi in range(n_q)]
    s_q, p_q = {}, {}
    for step in range(len(work) + 2):
        if step < len(work):
            s_q[step] = scores(*work[step])
        if 0 <= step - 1 < len(work):
            p_q[step - 1] = softmax(s_q.pop(step - 1))
        if 0 <= step - 2 < len(work):
            values(*work[step - 2], p_q.pop(step - 2))


def _attn_full_kernel(q_ref, k_ref, v_ref, o_ref, *, heads):
    def scores(h):
        return _dot_nt(q_ref[h], k_ref[h])

    def softmax(s):
        m = jnp.max(s, axis=-1, keepdims=True)
        p = jnp.exp2(s - m)
        return p.astype(BF16), jnp.sum(p, axis=-1, keepdims=True)

    def values(h, pl_):
        p, l = pl_
        cols = slice(h * V_HEAD, (h + 1) * V_HEAD)
        o_ref[:, cols] = (_dot(p, v_ref[:, cols]) / l).astype(BF16)

    s_q, p_q = {}, {}
    for step in range(heads + 2):
        if step < heads:
            s_q[step] = scores(step)
        if 0 <= step - 1 < heads:
            p_q[step - 1] = softmax(s_q.pop(step - 1))
        if 0 <= step - 2 < heads:
            values(step - 2, p_q.pop(step - 2))


def _attention(q, k, v, *, batch, t_q, t_kv, heads, tq, causal):
    hg = MLA_HEADS // heads
    if causal:
        kern = functools.partial(_attn_causal_kernel, heads=heads, tq=tq)
    else:
        kern = functools.partial(_attn_full_kernel, heads=heads)
    return pl.pallas_call(
        kern,
        out_shape=jax.ShapeDtypeStruct((batch * t_q, MLA_HEADS * V_HEAD), BF16),
        grid=(batch, hg),
        in_specs=[
            pl.BlockSpec((heads, t_q, QK_HEAD), lambda b, g: (g, b, 0)),
            pl.BlockSpec((heads, t_kv, QK_HEAD), lambda b, g: (g, b, 0)),
            pl.BlockSpec((t_kv, heads * V_HEAD), lambda b, g: (b, g)),
        ],
        out_specs=pl.BlockSpec((t_q, heads * V_HEAD), lambda b, g: (b, g)),
        compiler_params=_params("parallel", "parallel"),
        name="attention",
    )(q, k, v)


def _rope_tables(pos, half, reps, rows):
    inv = jnp.power(ROPE_BASE, -jnp.arange(half, dtype=F32) / half)
    ang = pos.astype(F32)[:, None] * inv[None, :]
    cos = jnp.tile(jnp.cos(ang), (max(rows // pos.shape[0], 1), reps))
    sin = jnp.tile(jnp.sin(ang), (max(rows // pos.shape[0], 1), reps))
    return cos, sin


def _trunk(x, pos, ret_state, ckv_past, krope_past, p, *, causal):
    batch, seq, d = x.shape
    m = batch * seq
    tm = 1024
    h = x.reshape(m, d)

    h = _ffn(h, p["ffn1_norm"], p["ffn1_w_in"], p["ffn1_w_out"], 0, tm=FFN_TM, tf=FFN_TF)
    cos, sin = _rope_tables(pos, RET_DK // 2, 1, tm)
    qkvg = _ret_in(h, p["mix_norm"][0], p["ret_w_in"][0], cos, sin, tm=tm, tn=2048)
    log_gamma = jnp.log1p(-jnp.exp2(-5.0 - jnp.arange(RET_HEADS, dtype=F32)))
    chunk = min(seq, 256)
    y, s_fin = _ret_core(qkvg, log_gamma, p["ret_gn_gain"][0],
                         None if ret_state is None else ret_state[0],
                         batch=batch, seq=seq, chunk=chunk,
                         heads=max(2, min(RET_HEADS, 512 // seq)))
    h = _mm_res(y, p["ret_w_out"][0], h, tm=512)
    h = _ffn(h, p["ffn2_norm"], p["ffn2_w_in"], p["ffn2_w_out"], 0, tm=FFN_TM, tf=FFN_TF)

    cos, sin = _rope_tables(pos, QK_ROPE // 2, 1, tm)
    c_new, kr_new = _latent(h, p["kv_norm"], p["w_dkv"], p["kv_lat_norm"], cos, sin, tm=tm)
    if ckv_past is None:
        t_kv = seq
        segments, steps = [(c_new, kr_new)], m // tm
    else:
        past = ckv_past.shape[1]
        t_kv = past + seq
        segments = [(ckv_past.reshape(batch * past, KV_LORA), krope_past.reshape(batch * past, QK_ROPE)),
                    (c_new, kr_new)]
        steps = batch
    k_sh, v_sh = _kv_up(segments, p["w_uk"], p["w_uv"], p["k_norm"], steps=steps)

    h = _ffn(h, p["ffn1_norm"], p["ffn1_w_in"], p["ffn1_w_out"], 1, tm=FFN_TM, tf=FFN_TF)
    cos, sin = _rope_tables(pos, QK_ROPE // 2, 1, tm)
    zeros = jnp.zeros_like(cos)
    cos = jnp.concatenate([cos, cos, zeros, zeros], axis=-1)
    sin = jnp.concatenate([-sin, sin, zeros, zeros], axis=-1)
    q = _q_proj(h, p["mix_norm"][1], p["w_dq"][0], p["q_lat_norm"][0], p["w_uq"][0], cos, sin,
                p["q_norm"][0], tm=tm)
    if causal:
        o = _attention(q, k_sh, v_sh, batch=batch, t_q=seq, t_kv=t_kv, heads=2, tq=512, causal=True)
    else:
        o = _attention(q, k_sh, v_sh, batch=batch, t_q=seq, t_kv=t_kv, heads=MLA_HEADS, tq=seq, causal=False)
    h = _mm_res(o, p["w_o"][0], h, tm=512)
    h = _ffn(h, p["ffn2_norm"], p["ffn2_w_in"], p["ffn2_w_out"], 1, tm=FFN_TM, tf=FFN_TF)

    return (h.reshape(batch, seq, d), s_fin[None],
            c_new.reshape(batch, seq, KV_LORA), kr_new.reshape(batch, seq, QK_ROPE))


def kernel(x_prompt, x_sample, state_ret, cache_ckv, cache_krope, ffn1_norm, ffn1_w_in, ffn1_w_out, mix_norm, ffn2_norm, ffn2_w_in, ffn2_w_out, ret_w_in, ret_gn_gain, ret_w_out, kv_norm, w_dkv, kv_lat_norm, w_ukv, k_norm, w_dq, q_lat_norm, w_uq, q_norm, w_o):
    bf = lambda w: w.astype(BF16)
    w_ukv_h = w_ukv.reshape(KV_LORA, MLA_HEADS, QK_NOPE + V_HEAD)
    w_uk = w_ukv_h[:, :, :QK_NOPE].reshape(KV_LORA, MLA_HEADS * QK_NOPE)
    w_uv = w_ukv_h[:, :, QK_NOPE:].reshape(KV_LORA, MLA_HEADS * V_HEAD)
    half = QK_ROPE // 2
    w_uq_h = w_uq.reshape(w_uq.shape[0], Q_LORA, MLA_HEADS, QK_HEAD)
    w_lo = w_uq_h[..., QK_NOPE:QK_NOPE + half]
    w_hi = w_uq_h[..., QK_NOPE + half:]
    w_uq_p = jnp.concatenate([w_uq_h[..., :QK_NOPE], w_lo, w_hi, w_hi, w_lo], axis=-1)
    w_uq_p = w_uq_p.reshape(w_uq.shape[0], Q_LORA, MLA_HEADS * Q_SLOT)
    p = dict(ffn1_norm=ffn1_norm, ffn1_w_in=bf(ffn1_w_in), ffn1_w_out=bf(ffn1_w_out),
             mix_norm=mix_norm, ffn2_norm=ffn2_norm, ffn2_w_in=bf(ffn2_w_in), ffn2_w_out=bf(ffn2_w_out),
             ret_w_in=bf(ret_w_in), ret_gn_gain=ret_gn_gain, ret_w_out=bf(ret_w_out),
             kv_norm=kv_norm, w_dkv=bf(w_dkv), kv_lat_norm=kv_lat_norm,
             w_uk=bf(w_uk), w_uv=bf(w_uv), k_norm=k_norm,
             w_dq=bf(w_dq), q_lat_norm=q_lat_norm, w_uq=bf(w_uq_p), q_norm=q_norm, w_o=bf(w_o))
    pos_prompt = jnp.arange(x_prompt.shape[1])
    past = cache_ckv.shape[1]
    pos_sample = past + jnp.arange(x_sample.shape[1])
    y_p, ret_p, ckv_p, kr_p = _trunk(x_prompt, pos_prompt, None, None, None, p, causal=True)
    y_s, ret_s, ckv_s, kr_s = _trunk(x_sample, pos_sample, state_ret, cache_ckv, cache_krope, p, causal=False)
    return (y_p, y_s, ret_p, ckv_p, kr_p, ret_s, ckv_s, kr_s)
```

```python
import functools

import jax
import jax.numpy as jnp
from jax import lax
from jax.experimental import pallas as pl
from jax.experimental.pallas import tpu as pltpu

F32 = jnp.float32
BF16 = jnp.bfloat16

D_MODEL = 2048
CHUNK = 64
D_FF = 5632
RET_HEADS = 8
RET_DK = D_MODEL // RET_HEADS
RET_DV = 2 * D_MODEL // RET_HEADS
MLA_HEADS = 16
QK_NOPE = 128
QK_ROPE = 64
QK_HEAD = QK_NOPE + QK_ROPE
V_HEAD = 128
KV_LORA = 512
Q_LORA = 512
ROPE_BASE = 10000.0
EPS = 1e-6
NEG_INF = -1e30
LOG2E = 1.4426950408889634

VMEM_LIMIT_BYTES = 60 * 1024 * 1024

FFN_TM = 1024
FFN_TF = 512
FFN_NORM_CHUNKS = 4


def _params(*semantics):
    return pltpu.CompilerParams(dimension_semantics=semantics,
                                vmem_limit_bytes=VMEM_LIMIT_BYTES)


def _rms_scale(x):
    return lax.rsqrt(jnp.mean(x * x, axis=-1, keepdims=True) + EPS)


def _dot(a, b):
    return jnp.dot(a, b, preferred_element_type=F32)


def _dot_nt(a, b):
    return lax.dot_general(a, b, (((1,), (1,)), ((), ())), preferred_element_type=F32)


def _dot_tn(a, b):
    return lax.dot_general(a, b, (((0,), (0,)), ((), ())), preferred_element_type=F32)


def _silu(x):
    return x * jax.nn.sigmoid(x)


def _ffn_kernel(x_ref, g_ref, wa_ref, wb_ref, wo_ref, o_ref, xn_ref):
    j = pl.program_id(1)

    def update(xn):
        a = _dot(xn, wa_ref[...])
        b = _dot(xn, wb_ref[...])
        h = (_silu(a) * (0.5 * b)).astype(BF16)
        return _dot(h, wo_ref[...])

    @pl.when(j == 0)
    def _():
        tc = x_ref.shape[0] // FFN_NORM_CHUNKS

        def norm(c):
            x = x_ref[c * tc:(c + 1) * tc, :]
            xn = (x * _rms_scale(x) * g_ref[...]).astype(BF16)
            xn_ref[c * tc:(c + 1) * tc, :] = xn
            return x, xn

        cur = norm(0)
        for c in range(FFN_NORM_CHUNKS):
            nxt = norm(c + 1) if c + 1 < FFN_NORM_CHUNKS else None
            x, xn = cur
            o_ref[c * tc:(c + 1) * tc, :] = x + update(xn)
            cur = nxt

    @pl.when(j > 0)
    def _():
        o_ref[...] += update(xn_ref[...])


def _ffn(x, g, w_in, w_out, layer, *, tm, tf):
    m, d = x.shape
    nf = D_FF // tf
    g = g[layer]
    return pl.pallas_call(
        _ffn_kernel,
        out_shape=jax.ShapeDtypeStruct((m, d), F32),
        grid=(m // tm, nf),
        in_specs=[
            pl.BlockSpec((tm, d), lambda i, j: (i, 0)),
            pl.BlockSpec((1, d), lambda i, j: (0, 0)),
            pl.BlockSpec((None, None, None, d, tf), lambda i, j: (layer, 0, j, 0, 0)),
            pl.BlockSpec((None, None, None, d, tf), lambda i, j: (layer, 1, j, 0, 0)),
            pl.BlockSpec((None, tf, d), lambda i, j: (layer, j, 0)),
        ],
        out_specs=pl.BlockSpec((tm, d), lambda i, j: (i, 0)),
        scratch_shapes=[pltpu.VMEM((tm, d), BF16)],
        compiler_params=_params("parallel", "arbitrary"),
        name="ffn",
    )(x, g.reshape(1, d), w_in, w_in, w_out)


def _ret_in_kernel(x_ref, g_ref, w_ref, cos_ref, sin_ref, o_ref, xn_ref, *, rope_blocks, heads_per_block):
    j = pl.program_id(1)

    @pl.when(j == 0)
    def _():
        x = x_ref[...]
        xn_ref[...] = (x * _rms_scale(x) * g_ref[...]).astype(BF16)

    half = RET_DK // 2

    @pl.when(j < 2 * rope_blocks)
    def _():
        scale = jnp.where(j < rope_blocks, RET_DK ** -0.5, 1.0).astype(F32)
        cos = cos_ref[...] * scale
        sin = sin_ref[...] * scale
        xn = xn_ref[...]
        for h in range(heads_per_block):
            lo = h * RET_DK
            y = _dot(xn, w_ref[:, lo:lo + RET_DK])
            x1 = y[:, :half]
            x2 = y[:, half:]
            o_ref[:, lo:lo + half] = (x1 * cos - x2 * sin).astype(BF16)
            o_ref[:, lo + half:lo + RET_DK] = (x1 * sin + x2 * cos).astype(BF16)

    @pl.when(j >= 2 * rope_blocks)
    def _():
        o_ref[...] = _dot(xn_ref[...], w_ref[...]).astype(BF16)


def _ret_in(x, g, w, cos, sin, *, tm, tn):
    m, d = x.shape
    n = w.shape[1]
    tab_blocks = cos.shape[0] // tm
    kern = functools.partial(_ret_in_kernel, rope_blocks=(RET_HEADS * RET_DK) // tn,
                             heads_per_block=tn // RET_DK)
    return pl.pallas_call(
        kern,
        out_shape=jax.ShapeDtypeStruct((m, n), BF16),
        grid=(m // tm, n // tn),
        in_specs=[
            pl.BlockSpec((tm, d), lambda i, j: (i, 0)),
            pl.BlockSpec((1, d), lambda i, j: (0, 0)),
            pl.BlockSpec((d, tn), lambda i, j: (0, j)),
            pl.BlockSpec((tm, RET_DK // 2), lambda i, j: (i % tab_blocks, 0)),
            pl.BlockSpec((tm, RET_DK // 2), lambda i, j: (i % tab_blocks, 0)),
        ],
        out_specs=pl.BlockSpec((tm, tn), lambda i, j: (i, j)),
        scratch_shapes=[pltpu.VMEM((tm, d), BF16)],
        compiler_params=_params("parallel", "arbitrary"),
        name="ret_in",
    )(x, g.reshape(1, d), w, cos, sin)


def _ret_core_kernel(lg_ref, q_ref, k_ref, v_ref, g_ref, gain_ref, *rest, chunk, heads, has_state):
    if has_state:
        s0_ref, y_ref, s_ref = rest
    else:
        y_ref, s_ref = rest
    head0 = pl.program_id(1) * heads
    n_chunks = q_ref.shape[0] // chunk
    row = lax.broadcasted_iota(jnp.int32, (chunk, chunk), 0)
    col = lax.broadcasted_iota(jnp.int32, (chunk, chunk), 1)
    diff = (row - col).astype(F32)
    n = lax.broadcasted_iota(jnp.int32, (chunk, 1), 0).astype(F32)

    def head_consts(hh):
        lg = lg_ref[head0 + hh]
        decay = jnp.where(diff >= 0, jnp.exp(lg * jnp.maximum(diff, 0.0)), 0.0)
        cross = jnp.exp(lg * (n + 1.0))
        kdec = jnp.exp(lg * (chunk - 1.0 - n))
        carry = jnp.exp(lg * jnp.full((1, RET_DV), float(chunk), F32))
        return decay, cross, kdec, carry

    def mix(hh, c, consts):
        decay, cross, kdec, carry = consts
        rows = slice(c * chunk, (c + 1) * chunk)
        q = q_ref[rows, hh * RET_DK:(hh + 1) * RET_DK]
        k = k_ref[rows, hh * RET_DK:(hh + 1) * RET_DK]
        v = v_ref[rows, hh * RET_DV:(hh + 1) * RET_DV]
        inner = (_dot_nt(q, k) * decay).astype(BF16)
        if c > 0:
            s_old = s_ref[0, hh]
        elif has_state:
            s_old = s0_ref[0, hh]
        else:
            s_old = None
        kd = (k.astype(F32) * kdec).astype(BF16)
        o = _dot(inner, v)
        s_new = _dot_tn(kd, v)
        if s_old is not None:
            o = o + cross * _dot(q, s_old.astype(BF16))
            s_new = carry * s_old + s_new
        s_ref[0, hh] = s_new
        return o

    def norm_gate(hh, c, o):
        rows = slice(c * chunk, (c + 1) * chunk)
        cols = slice(hh * RET_DV, (hh + 1) * RET_DV)
        mu = jnp.mean(o, axis=-1, keepdims=True)
        dev = o - mu
        var = jnp.mean(dev * dev, axis=-1, keepdims=True)
        of = dev * lax.rsqrt(var + EPS) * gain_ref[:, cols]
        y_ref[rows, cols] = (_silu(g_ref[rows, cols].astype(F32)) * of).astype(BF16)

    consts = None
    pending = None
    for hh in range(heads):
        for c in range(n_chunks):
            if c == 0:
                consts = head_consts(hh)
            o = mix(hh, c, consts)
            if pending is not None:
                norm_gate(*pending)
            pending = (hh, c, o)
    norm_gate(*pending)


def _ret_core(qkvg, log_gamma, gain, state, *, batch, seq, chunk, heads):
    m = qkvg.shape[0]
    hk = RET_HEADS * RET_DK
    hv = RET_HEADS * RET_DV
    wk = heads * RET_DK
    wv = heads * RET_DV
    kblk = hk // wk
    vblk = 2 * hk // wv
    gblk = vblk + hv // wv
    has_state = state is not None
    in_specs = [
        pl.BlockSpec(memory_space=pltpu.SMEM),
        pl.BlockSpec((seq, wk), lambda b, g: (b, g)),
        pl.BlockSpec((seq, wk), lambda b, g: (b, kblk + g)),
        pl.BlockSpec((seq, wv), lambda b, g: (b, vblk + g)),
        pl.BlockSpec((seq, wv), lambda b, g: (b, gblk + g)),
        pl.BlockSpec((1, wv), lambda b, g: (0, g)),
    ]
    args = [log_gamma, qkvg, qkvg, qkvg, qkvg, gain.reshape(1, hv)]
    state_spec = pl.BlockSpec((1, heads, RET_DK, RET_DV), lambda b, g: (b, g, 0, 0))
    if has_state:
        in_specs.append(state_spec)
        args.append(state)
    kern = functools.partial(_ret_core_kernel, chunk=chunk, heads=heads, has_state=has_state)
    return pl.pallas_call(
        kern,
        out_shape=(jax.ShapeDtypeStruct((m, hv), BF16),
                   jax.ShapeDtypeStruct((batch, RET_HEADS, RET_DK, RET_DV), F32)),
        grid=(batch, RET_HEADS // heads),
        in_specs=in_specs,
        out_specs=(pl.BlockSpec((seq, wv), lambda b, g: (b, g)), state_spec),
        compiler_params=_params("parallel", "parallel"),
        name="ret_core",
    )(*args)


def _mm_res_kernel(a_ref, w_ref, r_ref, o_ref):
    o_ref[...] = r_ref[...] + _dot(a_ref[...], w_ref[...])


def _mm_res(a, w, res, *, tm):
    m, k = a.shape
    n = w.shape[1]
    return pl.pallas_call(
        _mm_res_kernel,
        out_shape=jax.ShapeDtypeStruct((m, n), F32),
        grid=(m // tm,),
        in_specs=[
            pl.BlockSpec((tm, k), lambda i: (i, 0)),
            pl.BlockSpec((k, n), lambda i: (0, 0), pipeline_mode=pl.Buffered(1)),
            pl.BlockSpec((tm, n), lambda i: (i, 0)),
        ],
        out_specs=pl.BlockSpec((tm, n), lambda i: (i, 0)),
        compiler_params=_params("parallel"),
        name="mm_res",
    )(a, w, res)


def _latent_kernel(x_ref, g_ref, w_ref, gl_ref, cos_ref, sin_ref, c_ref, kr_ref, *, row_chunks):
    half = QK_ROPE // 2
    tc = x_ref.shape[0] // row_chunks

    def norm(c):
        x = x_ref[c * tc:(c + 1) * tc, :]
        return (x * _rms_scale(x) * g_ref[...]).astype(BF16)

    def project(c, xn):
        rows = slice(c * tc, (c + 1) * tc)
        y = _dot(xn, w_ref[...])
        cl = y[:, :KV_LORA]
        c_ref[rows, :] = cl * _rms_scale(cl) * gl_ref[...]
        x1 = y[:, KV_LORA:KV_LORA + half]
        x2 = y[:, KV_LORA + half:KV_LORA + QK_ROPE]
        cos = cos_ref[rows, :]
        sin = sin_ref[rows, :]
        kr_ref[rows, :] = jnp.concatenate([x1 * cos - x2 * sin, x1 * sin + x2 * cos], axis=-1)

    xn = norm(0)
    for c in range(row_chunks):
        xn_next = norm(c + 1) if c + 1 < row_chunks else None
        project(c, xn)
        xn = xn_next


def _latent(x, g, w, g_lat, cos, sin, *, tm):
    m, d = x.shape
    n = w.shape[1]
    half = QK_ROPE // 2
    tab_blocks = cos.shape[0] // tm
    return pl.pallas_call(
        functools.partial(_latent_kernel, row_chunks=tm // 128),
        out_shape=(jax.ShapeDtypeStruct((m, KV_LORA), F32),
                   jax.ShapeDtypeStruct((m, QK_ROPE), F32)),
        grid=(m // tm,),
        in_specs=[
            pl.BlockSpec((tm, d), lambda i: (i, 0)),
            pl.BlockSpec((1, d), lambda i: (0, 0)),
            pl.BlockSpec((d, n), lambda i: (0, 0)),
            pl.BlockSpec((1, KV_LORA), lambda i: (0, 0)),
            pl.BlockSpec((tm, half), lambda i: (i % tab_blocks, 0)),
            pl.BlockSpec((tm, half), lambda i: (i % tab_blocks, 0)),
        ],
        out_specs=(pl.BlockSpec((tm, KV_LORA), lambda i: (i, 0)),
                   pl.BlockSpec((tm, QK_ROPE), lambda i: (i, 0))),
        compiler_params=_params("parallel"),
        name="latent",
    )(x, g.reshape(1, d), w, g_lat.reshape(1, KV_LORA), cos, sin)


def _kv_up_kernel(*refs, seg_rows):
    n_seg = len(seg_rows)
    wk_ref, wv_ref, gk_ref, k_ref, v_ref = refs[2 * n_seg:]
    g_nope = gk_ref[:, :QK_NOPE]
    g_rope = gk_ref[:, QK_NOPE:]
    lo = 0
    for s, rows in enumerate(seg_rows):
        c = refs[2 * s][...].astype(BF16)
        kr = refs[2 * s + 1][...]
        kn = _dot(c, wk_ref[...])
        v_ref[lo:lo + rows, :] = _dot(c, wv_ref[...]).astype(BF16)
        ss_rope = jnp.sum(kr * kr, axis=-1, keepdims=True)
        for h in range(MLA_HEADS):
            knh = kn[:, h * QK_NOPE:(h + 1) * QK_NOPE]
            ss = jnp.sum(knh * knh, axis=-1, keepdims=True) + ss_rope
            r = lax.rsqrt(ss * (1.0 / QK_HEAD) + EPS)
            k_ref[h, lo:lo + rows, :QK_NOPE] = (knh * r * g_nope).astype(BF16)
            k_ref[h, lo:lo + rows, QK_NOPE:] = (kr * r * g_rope).astype(BF16)
        lo += rows


def _kv_up(segments, wk, wv, gk, *, steps):
    seg_rows = tuple(c.shape[0] // steps for c, _ in segments)
    tm = sum(seg_rows)
    m = steps * tm
    in_specs, args = [], []
    for (c, kr), rows in zip(segments, seg_rows):
        in_specs += [pl.BlockSpec((rows, KV_LORA), lambda i: (i, 0)),
                     pl.BlockSpec((rows, QK_ROPE), lambda i: (i, 0))]
        args += [c, kr]
    in_specs += [pl.BlockSpec(wk.shape, lambda i: (0, 0)),
                 pl.BlockSpec(wv.shape, lambda i: (0, 0)),
                 pl.BlockSpec((1, QK_HEAD), lambda i: (0, 0))]
    return pl.pallas_call(
        functools.partial(_kv_up_kernel, seg_rows=seg_rows),
        out_shape=(jax.ShapeDtypeStruct((MLA_HEADS, m, QK_HEAD), BF16),
                   jax.ShapeDtypeStruct((m, MLA_HEADS * V_HEAD), BF16)),
        grid=(steps,),
        in_specs=in_specs,
        out_specs=(pl.BlockSpec((MLA_HEADS, tm, QK_HEAD), lambda i: (0, i, 0)),
                   pl.BlockSpec((tm, MLA_HEADS * V_HEAD), lambda i: (i, 0))),
        compiler_params=_params("parallel"),
        name="kv_up",
    )(*args, wk, wv, gk.reshape(1, QK_HEAD))


Q_SLOT = 2 * QK_NOPE


def _q_proj_kernel(x_ref, g_ref, wdq_ref, gl_ref, wuq_ref, cos_ref, sin_ref, gq_ref, q_ref, *, row_chunks):
    g_nope = gq_ref[:, :QK_NOPE]
    g_rope = gq_ref[:, QK_NOPE:]
    tc = x_ref.shape[0] // row_chunks

    def project(c):
        rows = slice(c * tc, (c + 1) * tc)
        x = x_ref[rows, :]
        hn = (x * _rms_scale(x) * g_ref[...]).astype(BF16)
        ql = _dot(hn, wdq_ref[...])
        qn = (ql * _rms_scale(ql) * gl_ref[...]).astype(BF16)
        return _dot(qn, wuq_ref[...])

    def per_head(c, q):
        rows = slice(c * tc, (c + 1) * tc)
        cos = cos_ref[rows, :]
        sin = sin_ref[rows, :]
        for h in range(MLA_HEADS):
            nh = q[:, h * Q_SLOT:h * Q_SLOT + QK_NOPE]
            t = q[:, h * Q_SLOT + QK_NOPE:(h + 1) * Q_SLOT]
            rp = t * cos + pltpu.roll(t, QK_ROPE, 1) * sin
            ss = jnp.sum(nh * nh + rp * rp, axis=-1, keepdims=True)
            r = lax.rsqrt(ss * (1.0 / QK_HEAD) + EPS) * (QK_HEAD ** -0.5 * LOG2E)
            q_ref[h, rows, :QK_NOPE] = (nh * r * g_nope).astype(BF16)
            q_ref[h, rows, QK_NOPE:] = (rp[:, :QK_ROPE] * r * g_rope).astype(BF16)

    q = project(0)
    for c in range(row_chunks):
        q_next = project(c + 1) if c + 1 < row_chunks else None
        per_head(c, q)
        q = q_next


def _q_proj(x, g, wdq, g_lat, wuq, cos, sin, gq, *, tm):
    m, d = x.shape
    tab_blocks = cos.shape[0] // tm
    tw = cos.shape[1]
    return pl.pallas_call(
        functools.partial(_q_proj_kernel, row_chunks=tm // 256),
        out_shape=jax.ShapeDtypeStruct((MLA_HEADS, m, QK_HEAD), BF16),
        grid=(m // tm,),
        in_specs=[
            pl.BlockSpec((tm, d), lambda i: (i, 0)),
            pl.BlockSpec((1, d), lambda i: (0, 0)),
            pl.BlockSpec(wdq.shape, lambda i: (0, 0)),
            pl.BlockSpec((1, Q_LORA), lambda i: (0, 0)),
            pl.BlockSpec(wuq.shape, lambda i: (0, 0)),
            pl.BlockSpec((tm, tw), lambda i: (i % tab_blocks, 0)),
            pl.BlockSpec((tm, tw), lambda i: (i % tab_blocks, 0)),
            pl.BlockSpec((1, QK_HEAD), lambda i: (0, 0)),
        ],
        out_specs=pl.BlockSpec((MLA_HEADS, tm, QK_HEAD), lambda i: (0, i, 0)),
        compiler_params=_params("parallel"),
        name="q_proj",
    )(x, g.reshape(1, d), wdq, g_lat.reshape(1, Q_LORA), wuq, cos, sin, gq.reshape(1, QK_HEAD))


def _attn_causal_kernel(q_ref, k_ref, v_ref, o_ref, *, heads, tq):
    n_q = q_ref.shape[1] // tq
    krow = lax.broadcasted_iota(jnp.int32, (tq, tq), 0) // CHUNK
    qcol = lax.broadcasted_iota(jnp.int32, (tq, tq), 1) // CHUNK
    keep = krow <= qcol

    def scores(h, qi):
        lo = qi * tq
        q = q_ref[h, lo:lo + tq, :]
        st_d = jnp.where(keep, _dot_nt(k_ref[h, lo:lo + tq, :], q), NEG_INF)
        st_p = _dot_nt(k_ref[h, :lo, :], q) if qi > 0 else None
        return st_d, st_p

    def softmax(st):
        st_d, st_p = st
        m = jnp.max(st_d, axis=0, keepdims=True)
        if st_p is not None:
            m = jnp.maximum(m, jnp.max(st_p, axis=0, keepdims=True))
        pt_d = jnp.exp2(st_d - m)
        l = jnp.sum(pt_d, axis=0, keepdims=True)
        pt_p = None
        if st_p is not None:
            pt_p = jnp.exp2(st_p - m)
            l = l + jnp.sum(pt_p, axis=0, keepdims=True)
            pt_p = pt_p.astype(BF16)
        return pt_d.astype(BF16), pt_p, l

    def values(h, qi, p):
        pt_d, pt_p, l = p
        lo = qi * tq
        cols = slice(h * V_HEAD, (h + 1) * V_HEAD)
        acc_t = _dot_tn(v_ref[lo:lo + tq, cols], pt_d)
        if pt_p is not None:
            acc_t = acc_t + _dot_tn(v_ref[:lo, cols], pt_p)
        o_ref[lo:lo + tq, cols] = (acc_t / l).T.astype(BF16)

    work = [(h, qi) for h in range(heads) for qi in range(n_q)]
    s_q, p_q = {}, {}
    for step in range(len(work) + 2):
        if step < len(work):
            s_q[step] = scores(*work[step])
        if 0 <= step - 1 < len(work):
            p_q[step - 1] = softmax(s_q.pop(step - 1))
        if 0 <= step - 2 < len(work):
            values(*work[step - 2], p_q.pop(step - 2))


def _attn_full_kernel(q_ref, k_ref, v_ref, o_ref, *, heads):
    def scores(h):
        return _dot_nt(q_ref[h], k_ref[h])

    def softmax(s):
        m = jnp.max(s, axis=-1, keepdims=True)
        p = jnp.exp2(s - m)
        return p.astype(BF16), jnp.sum(p, axis=-1, keepdims=True)

    def values(h, pl_):
        p, l = pl_
        cols = slice(h * V_HEAD, (h + 1) * V_HEAD)
        o_ref[:, cols] = (_dot(p, v_ref[:, cols]) / l).astype(BF16)

    s_q, p_q = {}, {}
    for step in range(heads + 2):
        if step < heads:
            s_q[step] = scores(step)
        if 0 <= step - 1 < heads:
            p_q[step - 1] = softmax(s_q.pop(step - 1))
        if 0 <= step - 2 < heads:
            values(step - 2, p_q.pop(step - 2))


def _attention(q, k, v, *, batch, t_q, t_kv, heads, tq, causal):
    hg = MLA_HEADS // heads
    if causal:
        kern = functools.partial(_attn_causal_kernel, heads=heads, tq=tq)
    else:
        kern = functools.partial(_attn_full_kernel, heads=heads)
    return pl.pallas_call(
        kern,
        out_shape=jax.ShapeDtypeStruct((batch * t_q, MLA_HEADS * V_HEAD), BF16),
        grid=(batch, hg),
        in_specs=[
            pl.BlockSpec((heads, t_q, QK_HEAD), lambda b, g: (g, b, 0)),
            pl.BlockSpec((heads, t_kv, QK_HEAD), lambda b, g: (g, b, 0)),
            pl.BlockSpec((t_kv, heads * V_HEAD), lambda b, g: (b, g)),
        ],
        out_specs=pl.BlockSpec((t_q, heads * V_HEAD), lambda b, g: (b, g)),
        compiler_params=_params("parallel", "parallel"),
        name="attention",
    )(q, k, v)


def _rope_tables(pos, half, reps, rows):
    inv = jnp.power(ROPE_BASE, -jnp.arange(half, dtype=F32) / half)
    ang = pos.astype(F32)[:, None] * inv[None, :]
    cos = jnp.tile(jnp.cos(ang), (max(rows // pos.shape[0], 1), reps))
    sin = jnp.tile(jnp.sin(ang), (max(rows // pos.shape[0], 1), reps))
    return cos, sin


def _trunk(x, pos, ret_state, ckv_past, krope_past, p, *, causal):
    batch, seq, d = x.shape
    m = batch * seq
    tm = 1024
    h = x.reshape(m, d)

    h = _ffn(h, p["ffn1_norm"], p["ffn1_w_in"], p["ffn1_w_out"], 0, tm=FFN_TM, tf=FFN_TF)
    cos, sin = _rope_tables(pos, RET_DK // 2, 1, tm)
    qkvg = _ret_in(h, p["mix_norm"][0], p["ret_w_in"][0], cos, sin, tm=tm, tn=2048)
    log_gamma = jnp.log1p(-jnp.exp2(-5.0 - jnp.arange(RET_HEADS, dtype=F32)))
    chunk = min(seq, 256)
    y, s_fin = _ret_core(qkvg, log_gamma, p["ret_gn_gain"][0],
                         None if ret_state is None else ret_state[0],
                         batch=batch, seq=seq, chunk=chunk,
                         heads=max(1, min(RET_HEADS, 512 // seq)))
    h = _mm_res(y, p["ret_w_out"][0], h, tm=512)
    h = _ffn(h, p["ffn2_norm"], p["ffn2_w_in"], p["ffn2_w_out"], 0, tm=FFN_TM, tf=FFN_TF)

    cos, sin = _rope_tables(pos, QK_ROPE // 2, 1, tm)
    c_new, kr_new = _latent(h, p["kv_norm"], p["w_dkv"], p["kv_lat_norm"], cos, sin, tm=tm)
    if ckv_past is None:
        t_kv = seq
        segments, steps = [(c_new, kr_new)], m // tm
    else:
        past = ckv_past.shape[1]
        t_kv = past + seq
        segments = [(ckv_past.reshape(batch * past, KV_LORA), krope_past.reshape(batch * past, QK_ROPE)),
                    (c_new, kr_new)]
        steps = batch
    k_sh, v_sh = _kv_up(segments, p["w_uk"], p["w_uv"], p["k_norm"], steps=steps)

    h = _ffn(h, p["ffn1_norm"], p["ffn1_w_in"], p["ffn1_w_out"], 1, tm=FFN_TM, tf=FFN_TF)
    cos, sin = _rope_tables(pos, QK_ROPE // 2, 1, tm)
    zeros = jnp.zeros_like(cos)
    cos = jnp.concatenate([cos, cos, zeros, zeros], axis=-1)
    sin = jnp.concatenate([-sin, sin, zeros, zeros], axis=-1)
    q = _q_proj(h, p["mix_norm"][1], p["w_dq"][0], p["q_lat_norm"][0], p["w_uq"][0], cos, sin,
                p["q_norm"][0], tm=tm)
    if causal:
        o = _attention(q, k_sh, v_sh, batch=batch, t_q=seq, t_kv=t_kv, heads=2, tq=512, causal=True)
    else:
        o = _attention(q, k_sh, v_sh, batch=batch, t_q=seq, t_kv=t_kv, heads=MLA_HEADS, tq=seq, causal=False)
    h = _mm_res(o, p["w_o"][0], h, tm=512)
    h = _ffn(h, p["ffn2_norm"], p["ffn2_w_in"], p["ffn2_w_out"], 1, tm=FFN_TM, tf=FFN_TF)

    return (h.reshape(batch, seq, d), s_fin[None],
            c_new.reshape(batch, seq, KV_LORA), kr_new.reshape(batch, seq, QK_ROPE))


def kernel(x_prompt, x_sample, state_ret, cache_ckv, cache_krope, ffn1_norm, ffn1_w_in, ffn1_w_out, mix_norm, ffn2_norm, ffn2_w_in, ffn2_w_out, ret_w_in, ret_gn_gain, ret_w_out, kv_norm, w_dkv, kv_lat_norm, w_ukv, k_norm, w_dq, q_lat_norm, w_uq, q_norm, w_o):
    bf = lambda w: w.astype(BF16)
    w_ukv_h = w_ukv.reshape(KV_LORA, MLA_HEADS, QK_NOPE + V_HEAD)
    w_uk = w_ukv_h[:, :, :QK_NOPE].reshape(KV_LORA, MLA_HEADS * QK_NOPE)
    w_uv = w_ukv_h[:, :, QK_NOPE:].reshape(KV_LORA, MLA_HEADS * V_HEAD)
    half = QK_ROPE // 2
    w_uq_h = w_uq.reshape(w_uq.shape[0], Q_LORA, MLA_HEADS, QK_HEAD)
    w_lo = w_uq_h[..., QK_NOPE:QK_NOPE + half]
    w_hi = w_uq_h[..., QK_NOPE + half:]
    w_uq_p = jnp.concatenate([w_uq_h[..., :QK_NOPE], w_lo, w_hi, w_hi, w_lo], axis=-1)
    w_uq_p = w_uq_p.reshape(w_uq.shape[0], Q_LORA, MLA_HEADS * Q_SLOT)

    def ffn_in_blocks(w):
        layers, d, _ = w.shape
        w = bf(w).reshape(layers, d, 2, D_FF // FFN_TF, FFN_TF)
        return w.transpose(0, 2, 3, 1, 4)

    p = dict(ffn1_norm=ffn1_norm, ffn1_w_in=ffn_in_blocks(ffn1_w_in), ffn1_w_out=bf(ffn1_w_out),
             mix_norm=mix_norm, ffn2_norm=ffn2_norm, ffn2_w_in=ffn_in_blocks(ffn2_w_in), ffn2_w_out=bf(ffn2_w_out),
             ret_w_in=bf(ret_w_in), ret_gn_gain=ret_gn_gain, ret_w_out=bf(ret_w_out),
             kv_norm=kv_norm, w_dkv=bf(w_dkv), kv_lat_norm=kv_lat_norm,
             w_uk=bf(w_uk), w_uv=bf(w_uv), k_norm=k_norm,
             w_dq=bf(w_dq), q_lat_norm=q_lat_norm, w_uq=bf(w_uq_p), q_norm=q_norm, w_o=bf(w_o))
    pos_prompt = jnp.arange(x_prompt.shape[1])
    past = cache_ckv.shape[1]
    pos_sample = past + jnp.arange(x_sample.shape[1])
    y_p, ret_p, ckv_p, kr_p = _trunk(x_prompt, pos_prompt, None, None, None, p, causal=True)
    y_s, ret_s, ckv_s, kr_s = _trunk(x_sample, pos_sample, state_ret, cache_ckv, cache_krope, p, causal=False)
    return (y_p, y_s, ret_p, ckv_p, kr_p, ret_s, ckv_s, kr_s)
```

```python
import functools

import jax
import jax.numpy as jnp
from jax import lax
from jax.experimental import pallas as pl
from jax.experimental.pallas import tpu as pltpu

F32 = jnp.float32
BF16 = jnp.bfloat16

D_MODEL = 2048
CHUNK = 64
D_FF = 5632
RET_HEADS = 8
RET_DK = D_MODEL // RET_HEADS
RET_DV = 2 * D_MODEL // RET_HEADS
MLA_HEADS = 16
QK_NOPE = 128
QK_ROPE = 64
QK_HEAD = QK_NOPE + QK_ROPE
V_HEAD = 128
KV_LORA = 512
Q_LORA = 512
ROPE_BASE = 10000.0
EPS = 1e-6
NEG_INF = -1e30
LOG2E = 1.4426950408889634

VMEM_LIMIT_BYTES = 60 * 1024 * 1024

FFN_TM = 1024
FFN_TF = 512
FFN_NORM_CHUNKS = 4


def _params(*semantics):
    return pltpu.CompilerParams(dimension_semantics=semantics,
                                vmem_limit_bytes=VMEM_LIMIT_BYTES)


def _rms_scale(x):
    return lax.rsqrt(jnp.mean(x * x, axis=-1, keepdims=True) + EPS)


def _dot(a, b):
    return jnp.dot(a, b, preferred_element_type=F32)


def _dot_nt(a, b):
    return lax.dot_general(a, b, (((1,), (1,)), ((), ())), preferred_element_type=F32)


def _dot_tn(a, b):
    return lax.dot_general(a, b, (((0,), (0,)), ((), ())), preferred_element_type=F32)


def _silu(x):
    return x * jax.nn.sigmoid(x)


def _ffn_kernel(x_ref, g_ref, win_hbm, wout_hbm, o_ref, xn_ref, wa_buf, wb_buf, wo_buf, sem,
                *, layer, tf):
    i = pl.program_id(0)
    nf = D_FF // tf
    last = nf - 1
    forward = i % 2 == 0

    def block_of(pos):
        return jnp.where(forward, pos, last - pos)

    def copies(pos, slot):
        col = pl.multiple_of(block_of(pos) * tf, tf)
        return (
            pltpu.make_async_copy(win_hbm.at[layer, :, pl.ds(col, tf)], wa_buf.at[slot], sem.at[slot, 0]),
            pltpu.make_async_copy(win_hbm.at[layer, :, pl.ds(D_FF + col, tf)], wb_buf.at[slot], sem.at[slot, 1]),
            pltpu.make_async_copy(wout_hbm.at[layer, pl.ds(col, tf), :], wo_buf.at[slot], sem.at[slot, 2]),
        )

    def start(pos, slot):
        for c in copies(pos, slot):
            c.start()

    def wait(pos, slot):
        for c in copies(pos, slot):
            c.wait()

    def update(xn, slot):
        a = _dot(xn, wa_buf[slot])
        b = _dot(xn, wb_buf[slot])
        h = (_silu(a) * (0.5 * b)).astype(BF16)
        return _dot(h, wo_buf[slot])

    @pl.when(i == 0)
    def _():
        start(0, 0)
        start(1, 1)
        wait(0, 0)

    tc = x_ref.shape[0] // FFN_NORM_CHUNKS

    def norm(c):
        x = x_ref[c * tc:(c + 1) * tc, :]
        xn = (x * _rms_scale(x) * g_ref[...]).astype(BF16)
        xn_ref[c * tc:(c + 1) * tc, :] = xn
        return x, xn

    cur = norm(0)
    for c in range(FFN_NORM_CHUNKS):
        nxt = norm(c + 1) if c + 1 < FFN_NORM_CHUNKS else None
        x, xn = cur
        o_ref[c * tc:(c + 1) * tc, :] = x + update(xn, 0)
        cur = nxt
    start(2, 0)

    def pair(q, carry):
        p1 = 2 * q + 1

        @pl.when(jnp.logical_or(q > 0, i == 0))
        def _():
            wait(p1, 1)

        o_ref[...] += update(xn_ref[...], 1)

        @pl.when(p1 + 2 <= last)
        def _():
            start(p1 + 2, 1)

        wait(p1 + 1, 0)
        o_ref[...] += update(xn_ref[...], 0)

        @pl.when(p1 + 3 <= last)
        def _():
            start(p1 + 3, 0)

        return carry

    lax.fori_loop(0, (nf - 1) // 2, pair, 0)


def _ffn(x, g, w_in, w_out, layer, *, tm, tf):
    m, d = x.shape
    nf = D_FF // tf
    assert nf % 2 == 1 and nf >= 3, "the block walk assumes an odd number of hidden blocks"
    return pl.pallas_call(
        functools.partial(_ffn_kernel, layer=layer, tf=tf),
        out_shape=jax.ShapeDtypeStruct((m, d), F32),
        grid=(m // tm,),
        in_specs=[
            pl.BlockSpec((tm, d), lambda i: (i, 0)),
            pl.BlockSpec((1, d), lambda i: (0, 0)),
            pl.BlockSpec(memory_space=pl.ANY),
            pl.BlockSpec(memory_space=pl.ANY),
        ],
        out_specs=pl.BlockSpec((tm, d), lambda i: (i, 0)),
        scratch_shapes=[
            pltpu.VMEM((tm, d), BF16),
            pltpu.VMEM((2, d, tf), BF16),
            pltpu.VMEM((2, d, tf), BF16),
            pltpu.VMEM((2, tf, d), BF16),
            pltpu.SemaphoreType.DMA((2, 3)),
        ],
        compiler_params=_params("arbitrary"),
        name="ffn",
    )(x, g[layer].reshape(1, d), w_in, w_out)


def _ret_in_kernel(x_ref, g_ref, w_ref, cos_ref, sin_ref, o_ref, xn_ref, *, rope_blocks, heads_per_block):
    j = pl.program_id(1)

    @pl.when(j == 0)
    def _():
        x = x_ref[...]
        xn_ref[...] = (x * _rms_scale(x) * g_ref[...]).astype(BF16)

    half = RET_DK // 2

    @pl.when(j < 2 * rope_blocks)
    def _():
        scale = jnp.where(j < rope_blocks, RET_DK ** -0.5, 1.0).astype(F32)
        cos = cos_ref[...] * scale
        sin = sin_ref[...] * scale
        xn = xn_ref[...]
        for h in range(heads_per_block):
            lo = h * RET_DK
            y = _dot(xn, w_ref[:, lo:lo + RET_DK])
            x1 = y[:, :half]
            x2 = y[:, half:]
            o_ref[:, lo:lo + half] = (x1 * cos - x2 * sin).astype(BF16)
            o_ref[:, lo + half:lo + RET_DK] = (x1 * sin + x2 * cos).astype(BF16)

    @pl.when(j >= 2 * rope_blocks)
    def _():
        o_ref[...] = _dot(xn_ref[...], w_ref[...]).astype(BF16)


def _ret_in(x, g, w, cos, sin, *, tm, tn):
    m, d = x.shape
    n = w.shape[1]
    tab_blocks = cos.shape[0] // tm
    kern = functools.partial(_ret_in_kernel, rope_blocks=(RET_HEADS * RET_DK) // tn,
                             heads_per_block=tn // RET_DK)
    return pl.pallas_call(
        kern,
        out_shape=jax.ShapeDtypeStruct((m, n), BF16),
        grid=(m // tm, n // tn),
        in_specs=[
            pl.BlockSpec((tm, d), lambda i, j: (i, 0)),
            pl.BlockSpec((1, d), lambda i, j: (0, 0)),
            pl.BlockSpec((d, tn), lambda i, j: (0, j)),
            pl.BlockSpec((tm, RET_DK // 2), lambda i, j: (i % tab_blocks, 0)),
            pl.BlockSpec((tm, RET_DK // 2), lambda i, j: (i % tab_blocks, 0)),
        ],
        out_specs=pl.BlockSpec((tm, tn), lambda i, j: (i, j)),
        scratch_shapes=[pltpu.VMEM((tm, d), BF16)],
        compiler_params=_params("parallel", "arbitrary"),
        name="ret_in",
    )(x, g.reshape(1, d), w, cos, sin)


def _ret_core_kernel(lg_ref, q_ref, k_ref, v_ref, g_ref, gain_ref, *rest, chunk, heads, has_state):
    if has_state:
        s0_ref, y_ref, s_ref = rest
    else:
        y_ref, s_ref = rest
    head0 = pl.program_id(1) * heads
    n_chunks = q_ref.shape[0] // chunk
    row = lax.broadcasted_iota(jnp.int32, (chunk, chunk), 0)
    col = lax.broadcasted_iota(jnp.int32, (chunk, chunk), 1)
    diff = (row - col).astype(F32)
    n = lax.broadcasted_iota(jnp.int32, (chunk, 1), 0).astype(F32)

    def head_consts(hh):
        lg = lg_ref[head0 + hh]
        decay = jnp.where(diff >= 0, jnp.exp(lg * jnp.maximum(diff, 0.0)), 0.0)
        cross = jnp.exp(lg * (n + 1.0))
        kdec = jnp.exp(lg * (chunk - 1.0 - n))
        carry = jnp.exp(lg * jnp.full((1, RET_DV), float(chunk), F32))
        return decay, cross, kdec, carry

    def mix(hh, c, consts):
        decay, cross, kdec, carry = consts
        rows = slice(c * chunk, (c + 1) * chunk)
        q = q_ref[rows, hh * RET_DK:(hh + 1) * RET_DK]
        k = k_ref[rows, hh * RET_DK:(hh + 1) * RET_DK]
        v = v_ref[rows, hh * RET_DV:(hh + 1) * RET_DV]
        inner = (_dot_nt(q, k) * decay).astype(BF16)
        if c > 0:
            s_old = s_ref[0, hh]
        elif has_state:
            s_old = s0_ref[0, hh]
        else:
            s_old = None
        kd = (k.astype(F32) * kdec).astype(BF16)
        o = _dot(inner, v)
        s_new = _dot_tn(kd, v)
        if s_old is not None:
            o = o + cross * _dot(q, s_old.astype(BF16))
            s_new = carry * s_old + s_new
        s_ref[0, hh] = s_new
        return o

    def norm_gate(hh, c, o):
        rows = slice(c * chunk, (c + 1) * chunk)
        cols = slice(hh * RET_DV, (hh + 1) * RET_DV)
        mu = jnp.mean(o, axis=-1, keepdims=True)
        dev = o - mu
        var = jnp.mean(dev * dev, axis=-1, keepdims=True)
        of = dev * lax.rsqrt(var + EPS) * gain_ref[:, cols]
        y_ref[rows, cols] = (_silu(g_ref[rows, cols].astype(F32)) * of).astype(BF16)

    consts = None
    pending = None
    for hh in range(heads):
        for c in range(n_chunks):
            if c == 0:
                consts = head_consts(hh)
            o = mix(hh, c, consts)
            if pending is not None:
                norm_gate(*pending)
            pending = (hh, c, o)
    norm_gate(*pending)


def _ret_core(qkvg, log_gamma, gain, state, *, batch, seq, chunk, heads):
    m = qkvg.shape[0]
    hk = RET_HEADS * RET_DK
    hv = RET_HEADS * RET_DV
    wk = heads * RET_DK
    wv = heads * RET_DV
    kblk = hk // wk
    vblk = 2 * hk // wv
    gblk = vblk + hv // wv
    has_state = state is not None
    in_specs = [
        pl.BlockSpec(memory_space=pltpu.SMEM),
        pl.BlockSpec((seq, wk), lambda b, g: (b, g)),
        pl.BlockSpec((seq, wk), lambda b, g: (b, kblk + g)),
        pl.BlockSpec((seq, wv), lambda b, g: (b, vblk + g)),
        pl.BlockSpec((seq, wv), lambda b, g: (b, gblk + g)),
        pl.BlockSpec((1, wv), lambda b, g: (0, g)),
    ]
    args = [log_gamma, qkvg, qkvg, qkvg, qkvg, gain.reshape(1, hv)]
    state_spec = pl.BlockSpec((1, heads, RET_DK, RET_DV), lambda b, g: (b, g, 0, 0))
    if has_state:
        in_specs.append(state_spec)
        args.append(state)
    kern = functools.partial(_ret_core_kernel, chunk=chunk, heads=heads, has_state=has_state)
    return pl.pallas_call(
        kern,
        out_shape=(jax.ShapeDtypeStruct((m, hv), BF16),
                   jax.ShapeDtypeStruct((batch, RET_HEADS, RET_DK, RET_DV), F32)),
        grid=(batch, RET_HEADS // heads),
        in_specs=in_specs,
        out_specs=(pl.BlockSpec((seq, wv), lambda b, g: (b, g)), state_spec),
        compiler_params=_params("parallel", "parallel"),
        name="ret_core",
    )(*args)


def _mm_res_kernel(a_ref, w_ref, r_ref, o_ref):
    o_ref[...] = r_ref[...] + _dot(a_ref[...], w_ref[...])


def _mm_res(a, w, res, *, tm):
    m, k = a.shape
    n = w.shape[1]
    return pl.pallas_call(
        _mm_res_kernel,
        out_shape=jax.ShapeDtypeStruct((m, n), F32),
        grid=(m // tm,),
        in_specs=[
            pl.BlockSpec((tm, k), lambda i: (i, 0)),
            pl.BlockSpec((k, n), lambda i: (0, 0), pipeline_mode=pl.Buffered(1)),
            pl.BlockSpec((tm, n), lambda i: (i, 0)),
        ],
        out_specs=pl.BlockSpec((tm, n), lambda i: (i, 0)),
        compiler_params=_params("parallel"),
        name="mm_res",
    )(a, w, res)


def _latent_kernel(x_ref, g_ref, w_ref, gl_ref, cos_ref, sin_ref, c_ref, kr_ref, *, row_chunks):
    half = QK_ROPE // 2
    tc = x_ref.shape[0] // row_chunks

    def norm(c):
        x = x_ref[c * tc:(c + 1) * tc, :]
        return (x * _rms_scale(x) * g_ref[...]).astype(BF16)

    def project(c, xn):
        rows = slice(c * tc, (c + 1) * tc)
        y = _dot(xn, w_ref[...])
        cl = y[:, :KV_LORA]
        c_ref[rows, :] = cl * _rms_scale(cl) * gl_ref[...]
        x1 = y[:, KV_LORA:KV_LORA + half]
        x2 = y[:, KV_LORA + half:KV_LORA + QK_ROPE]
        cos = cos_ref[rows, :]
        sin = sin_ref[rows, :]
        kr_ref[rows, :] = jnp.concatenate([x1 * cos - x2 * sin, x1 * sin + x2 * cos], axis=-1)

    xn = norm(0)
    for c in range(row_chunks):
        xn_next = norm(c + 1) if c + 1 < row_chunks else None
        project(c, xn)
        xn = xn_next


def _latent(x, g, w, g_lat, cos, sin, *, tm):
    m, d = x.shape
    n = w.shape[1]
    half = QK_ROPE // 2
    tab_blocks = cos.shape[0] // tm
    return pl.pallas_call(
        functools.partial(_latent_kernel, row_chunks=tm // 128),
        out_shape=(jax.ShapeDtypeStruct((m, KV_LORA), F32),
                   jax.ShapeDtypeStruct((m, QK_ROPE), F32)),
        grid=(m // tm,),
        in_specs=[
            pl.BlockSpec((tm, d), lambda i: (i, 0)),
            pl.BlockSpec((1, d), lambda i: (0, 0)),
            pl.BlockSpec((d, n), lambda i: (0, 0)),
            pl.BlockSpec((1, KV_LORA), lambda i: (0, 0)),
            pl.BlockSpec((tm, half), lambda i: (i % tab_blocks, 0)),
            pl.BlockSpec((tm, half), lambda i: (i % tab_blocks, 0)),
        ],
        out_specs=(pl.BlockSpec((tm, KV_LORA), lambda i: (i, 0)),
                   pl.BlockSpec((tm, QK_ROPE), lambda i: (i, 0))),
        compiler_params=_params("parallel"),
        name="latent",
    )(x, g.reshape(1, d), w, g_lat.reshape(1, KV_LORA), cos, sin)


def _kv_up_kernel(*refs, seg_rows):
    n_seg = len(seg_rows)
    wk_ref, wv_ref, gk_ref, k_ref, v_ref = refs[2 * n_seg:]
    g_nope = gk_ref[:, :QK_NOPE]
    g_rope = gk_ref[:, QK_NOPE:]
    lo = 0
    for s, rows in enumerate(seg_rows):
        c = refs[2 * s][...].astype(BF16)
        kr = refs[2 * s + 1][...]
        kn = _dot(c, wk_ref[...])
        v_ref[lo:lo + rows, :] = _dot(c, wv_ref[...]).astype(BF16)
        ss_rope = jnp.sum(kr * kr, axis=-1, keepdims=True)
        for h in range(MLA_HEADS):
            knh = kn[:, h * QK_NOPE:(h + 1) * QK_NOPE]
            ss = jnp.sum(knh * knh, axis=-1, keepdims=True) + ss_rope
            r = lax.rsqrt(ss * (1.0 / QK_HEAD) + EPS)
            k_ref[h, lo:lo + rows, :QK_NOPE] = (knh * r * g_nope).astype(BF16)
            k_ref[h, lo:lo + rows, QK_NOPE:] = (kr * r * g_rope).astype(BF16)
        lo += rows


def _kv_up(segments, wk, wv, gk, *, steps):
    seg_rows = tuple(c.shape[0] // steps for c, _ in segments)
    tm = sum(seg_rows)
    m = steps * tm
    in_specs, args = [], []
    for (c, kr), rows in zip(segments, seg_rows):
        in_specs += [pl.BlockSpec((rows, KV_LORA), lambda i: (i, 0)),
                     pl.BlockSpec((rows, QK_ROPE), lambda i: (i, 0))]
        args += [c, kr]
    in_specs += [pl.BlockSpec(wk.shape, lambda i: (0, 0)),
                 pl.BlockSpec(wv.shape, lambda i: (0, 0)),
                 pl.BlockSpec((1, QK_HEAD), lambda i: (0, 0))]
    return pl.pallas_call(
        functools.partial(_kv_up_kernel, seg_rows=seg_rows),
        out_shape=(jax.ShapeDtypeStruct((MLA_HEADS, m, QK_HEAD), BF16),
                   jax.ShapeDtypeStruct((m, MLA_HEADS * V_HEAD), BF16)),
        grid=(steps,),
        in_specs=in_specs,
        out_specs=(pl.BlockSpec((MLA_HEADS, tm, QK_HEAD), lambda i: (0, i, 0)),
                   pl.BlockSpec((tm, MLA_HEADS * V_HEAD), lambda i: (i, 0))),
        compiler_params=_params("parallel"),
        name="kv_up",
    )(*args, wk, wv, gk.reshape(1, QK_HEAD))


Q_SLOT = 2 * QK_NOPE


def _q_proj_kernel(x_ref, g_ref, wdq_ref, gl_ref, wuq_ref, cos_ref, sin_ref, gq_ref, q_ref, *, row_chunks):
    g_nope = gq_ref[:, :QK_NOPE]
    g_rope = gq_ref[:, QK_NOPE:]
    tc = x_ref.shape[0] // row_chunks

    def project(c):
        rows = slice(c * tc, (c + 1) * tc)
        x = x_ref[rows, :]
        hn = (x * _rms_scale(x) * g_ref[...]).astype(BF16)
        ql = _dot(hn, wdq_ref[...])
        qn = (ql * _rms_scale(ql) * gl_ref[...]).astype(BF16)
        return _dot(qn, wuq_ref[...])

    def per_head(c, q):
        rows = slice(c * tc, (c + 1) * tc)
        cos = cos_ref[rows, :]
        sin = sin_ref[rows, :]
        for h in range(MLA_HEADS):
            nh = q[:, h * Q_SLOT:h * Q_SLOT + QK_NOPE]
            t = q[:, h * Q_SLOT + QK_NOPE:(h + 1) * Q_SLOT]
            rp = t * cos + pltpu.roll(t, QK_ROPE, 1) * sin
            ss = jnp.sum(nh * nh + rp * rp, axis=-1, keepdims=True)
            r = lax.rsqrt(ss * (1.0 / QK_HEAD) + EPS) * (QK_HEAD ** -0.5 * LOG2E)
            q_ref[h, rows, :QK_NOPE] = (nh * r * g_nope).astype(BF16)
            q_ref[h, rows, QK_NOPE:] = (rp[:, :QK_ROPE] * r * g_rope).astype(BF16)

    q = project(0)
    for c in range(row_chunks):
        q_next = project(c + 1) if c + 1 < row_chunks else None
        per_head(c, q)
        q = q_next


def _q_proj(x, g, wdq, g_lat, wuq, cos, sin, gq, *, tm):
    m, d = x.shape
    tab_blocks = cos.shape[0] // tm
    tw = cos.shape[1]
    return pl.pallas_call(
        functools.partial(_q_proj_kernel, row_chunks=tm // 256),
        out_shape=jax.ShapeDtypeStruct((MLA_HEADS, m, QK_HEAD), BF16),
        grid=(m // tm,),
        in_specs=[
            pl.BlockSpec((tm, d), lambda i: (i, 0)),
            pl.BlockSpec((1, d), lambda i: (0, 0)),
            pl.BlockSpec(wdq.shape, lambda i: (0, 0)),
            pl.BlockSpec((1, Q_LORA), lambda i: (0, 0)),
            pl.BlockSpec(wuq.shape, lambda i: (0, 0)),
            pl.BlockSpec((tm, tw), lambda i: (i % tab_blocks, 0)),
            pl.BlockSpec((tm, tw), lambda i: (i % tab_blocks, 0)),
            pl.BlockSpec((1, QK_HEAD), lambda i: (0, 0)),
        ],
        out_specs=pl.BlockSpec((MLA_HEADS, tm, QK_HEAD), lambda i: (0, i, 0)),
        compiler_params=_params("parallel"),
        name="q_proj",
    )(x, g.reshape(1, d), wdq, g_lat.reshape(1, Q_LORA), wuq, cos, sin, gq.reshape(1, QK_HEAD))


def _attn_causal_kernel(q_ref, k_ref, v_ref, o_ref, *, heads, tq):
    n_q = q_ref.shape[1] // tq
    krow = lax.broadcasted_iota(jnp.int32, (tq, tq), 0) // CHUNK
    qcol = lax.broadcasted_iota(jnp.int32, (tq, tq), 1) // CHUNK
    keep = krow <= qcol

    def scores(h, qi):
        lo = qi * tq
        q = q_ref[h, lo:lo + tq, :]
        st_d = jnp.where(keep, _dot_nt(k_ref[h, lo:lo + tq, :], q), NEG_INF)
        st_p = _dot_nt(k_ref[h, :lo, :], q) if qi > 0 else None
        return st_d, st_p

    def softmax(st):
        st_d, st_p = st
        m = jnp.max(st_d, axis=0, keepdims=True)
        if st_p is not None:
            m = jnp.maximum(m, jnp.max(st_p, axis=0, keepdims=True))
        pt_d = jnp.exp2(st_d - m)
        l = jnp.sum(pt_d, axis=0, keepdims=True)
        pt_p = None
        if st_p is not None:
            pt_p = jnp.exp2(st_p - m)
            l = l + jnp.sum(pt_p, axis=0, keepdims=True)
            pt_p = pt_p.astype(BF16)
        return pt_d.astype(BF16), pt_p, l

    def values(h, qi, p):
        pt_d, pt_p, l = p
        lo = qi * tq
        cols = slice(h * V_HEAD, (h + 1) * V_HEAD)
        acc_t = _dot_tn(v_ref[lo:lo + tq, cols], pt_d)
        if pt_p is not None:
            acc_t = acc_t + _dot_tn(v_ref[:lo, cols], pt_p)
        o_ref[lo:lo + tq, cols] = (acc_t / l).T.astype(BF16)

    work = [(h, qi) for h in range(heads) for qi in range(n_q)]
    s_q, p_q = {}, {}
    for step in range(len(work) + 2):
        if step < len(work):
            s_q[step] = scores(*work[step])
        if 0 <= step - 1 < len(work):
            p_q[step - 1] = softmax(s_q.pop(step - 1))
        if 0 <= step - 2 < len(work):
            values(*work[step - 2], p_q.pop(step - 2))


def _attn_full_kernel(q_ref, k_ref, v_ref, o_ref, *, heads):
    def scores(h):
        return _dot_nt(q_ref[h], k_ref[h])

    def softmax(s):
        m = jnp.max(s, axis=-1, keepdims=True)
        p = jnp.exp2(s - m)
        return p.astype(BF16), jnp.sum(p, axis=-1, keepdims=True)

    def values(h, pl_):
        p, l = pl_
        cols = slice(h * V_HEAD, (h + 1) * V_HEAD)
        o_ref[:, cols] = (_dot(p, v_ref[:, cols]) / l).astype(BF16)

    s_q, p_q = {}, {}
    for step in range(heads + 2):
        if step < heads:
            s_q[step] = scores(step)
        if 0 <= step - 1 < heads:
            p_q[step - 1] = softmax(s_q.pop(step - 1))
        if 0 <= step - 2 < heads:
            values(step - 2, p_q.pop(step - 2))


def _attention(q, k, v, *, batch, t_q, t_kv, heads, tq, causal):
    hg = MLA_HEADS // heads
    if causal:
        kern = functools.partial(_attn_causal_kernel, heads=heads, tq=tq)
    else:
        kern = functools.partial(_attn_full_kernel, heads=heads)
    return pl.pallas_call(
        kern,
        out_shape=jax.ShapeDtypeStruct((batch * t_q, MLA_HEADS * V_HEAD), BF16),
        grid=(batch, hg),
        in_specs=[
            pl.BlockSpec((heads, t_q, QK_HEAD), lambda b, g: (g, b, 0)),
            pl.BlockSpec((heads, t_kv, QK_HEAD), lambda b, g: (g, b, 0)),
            pl.BlockSpec((t_kv, heads * V_HEAD), lambda b, g: (b, g)),
        ],
        out_specs=pl.BlockSpec((t_q, heads * V_HEAD), lambda b, g: (b, g)),
        compiler_params=_params("parallel", "parallel"),
        name="attention",
    )(q, k, v)


def _rope_tables(pos, half, reps, rows):
    inv = jnp.power(ROPE_BASE, -jnp.arange(half, dtype=F32) / half)
    ang = pos.astype(F32)[:, None] * inv[None, :]
    cos = jnp.tile(jnp.cos(ang), (max(rows // pos.shape[0], 1), reps))
    sin = jnp.tile(jnp.sin(ang), (max(rows // pos.shape[0], 1), reps))
    return cos, sin


def _trunk(x, pos, ret_state, ckv_past, krope_past, p, *, causal):
    batch, seq, d = x.shape
    m = batch * seq
    tm = 1024
    h = x.reshape(m, d)

    h = _ffn(h, p["ffn1_norm"], p["ffn1_w_in"], p["ffn1_w_out"], 0, tm=FFN_TM, tf=FFN_TF)
    cos, sin = _rope_tables(pos, RET_DK // 2, 1, tm)
    qkvg = _ret_in(h, p["mix_norm"][0], p["ret_w_in"][0], cos, sin, tm=tm, tn=2048)
    log_gamma = jnp.log1p(-jnp.exp2(-5.0 - jnp.arange(RET_HEADS, dtype=F32)))
    chunk = min(seq, 256)
    y, s_fin = _ret_core(qkvg, log_gamma, p["ret_gn_gain"][0],
                         None if ret_state is None else ret_state[0],
                         batch=batch, seq=seq, chunk=chunk,
                         heads=max(1, min(RET_HEADS, 512 // seq)))
    h = _mm_res(y, p["ret_w_out"][0], h, tm=512)
    h = _ffn(h, p["ffn2_norm"], p["ffn2_w_in"], p["ffn2_w_out"], 0, tm=FFN_TM, tf=FFN_TF)

    cos, sin = _rope_tables(pos, QK_ROPE // 2, 1, tm)
    c_new, kr_new = _latent(h, p["kv_norm"], p["w_dkv"], p["kv_lat_norm"], cos, sin, tm=tm)
    if ckv_past is None:
        t_kv = seq
        segments, steps = [(c_new, kr_new)], m // tm
    else:
        past = ckv_past.shape[1]
        t_kv = past + seq
        segments = [(ckv_past.reshape(batch * past, KV_LORA), krope_past.reshape(batch * past, QK_ROPE)),
                    (c_new, kr_new)]
        steps = batch
    k_sh, v_sh = _kv_up(segments, p["w_uk"], p["w_uv"], p["k_norm"], steps=steps)

    h = _ffn(h, p["ffn1_norm"], p["ffn1_w_in"], p["ffn1_w_out"], 1, tm=FFN_TM, tf=FFN_TF)
    cos, sin = _rope_tables(pos, QK_ROPE // 2, 1, tm)
    zeros = jnp.zeros_like(cos)
    cos = jnp.concatenate([cos, cos, zeros, zeros], axis=-1)
    sin = jnp.concatenate([-sin, sin, zeros, zeros], axis=-1)
    q = _q_proj(h, p["mix_norm"][1], p["w_dq"][0], p["q_lat_norm"][0], p["w_uq"][0], cos, sin,
                p["q_norm"][0], tm=tm)
    if causal:
        o = _attention(q, k_sh, v_sh, batch=batch, t_q=seq, t_kv=t_kv, heads=2, tq=512, causal=True)
    else:
        o = _attention(q, k_sh, v_sh, batch=batch, t_q=seq, t_kv=t_kv, heads=MLA_HEADS, tq=seq, causal=False)
    h = _mm_res(o, p["w_o"][0], h, tm=512)
    h = _ffn(h, p["ffn2_norm"], p["ffn2_w_in"], p["ffn2_w_out"], 1, tm=FFN_TM, tf=FFN_TF)

    return (h.reshape(batch, seq, d), s_fin[None],
            c_new.reshape(batch, seq, KV_LORA), kr_new.reshape(batch, seq, QK_ROPE))


def kernel(x_prompt, x_sample, state_ret, cache_ckv, cache_krope, ffn1_norm, ffn1_w_in, ffn1_w_out, mix_norm, ffn2_norm, ffn2_w_in, ffn2_w_out, ret_w_in, ret_gn_gain, ret_w_out, kv_norm, w_dkv, kv_lat_norm, w_ukv, k_norm, w_dq, q_lat_norm, w_uq, q_norm, w_o):
    bf = lambda w: w.astype(BF16)
    w_ukv_h = w_ukv.reshape(KV_LORA, MLA_HEADS, QK_NOPE + V_HEAD)
    w_uk = w_ukv_h[:, :, :QK_NOPE].reshape(KV_LORA, MLA_HEADS * QK_NOPE)
    w_uv = w_ukv_h[:, :, QK_NOPE:].reshape(KV_LORA, MLA_HEADS * V_HEAD)
    half = QK_ROPE // 2
    w_uq_h = w_uq.reshape(w_uq.shape[0], Q_LORA, MLA_HEADS, QK_HEAD)
    w_lo = w_uq_h[..., QK_NOPE:QK_NOPE + half]
    w_hi = w_uq_h[..., QK_NOPE + half:]
    w_uq_p = jnp.concatenate([w_uq_h[..., :QK_NOPE], w_lo, w_hi, w_hi, w_lo], axis=-1)
    w_uq_p = w_uq_p.reshape(w_uq.shape[0], Q_LORA, MLA_HEADS * Q_SLOT)
    p = dict(ffn1_norm=ffn1_norm, ffn1_w_in=bf(ffn1_w_in), ffn1_w_out=bf(ffn1_w_out),
             mix_norm=mix_norm, ffn2_norm=ffn2_norm, ffn2_w_in=bf(ffn2_w_in), ffn2_w_out=bf(ffn2_w_out),
             ret_w_in=bf(ret_w_in), ret_gn_gain=ret_gn_gain, ret_w_out=bf(ret_w_out),
             kv_norm=kv_norm, w_dkv=bf(w_dkv), kv_lat_norm=kv_lat_norm,
             w_uk=bf(w_uk), w_uv=bf(w_uv), k_norm=k_norm,
             w_dq=bf(w_dq), q_lat_norm=q_lat_norm, w_uq=bf(w_uq_p), q_norm=q_norm, w_o=bf(w_o))
    pos_prompt = jnp.arange(x_prompt.shape[1])
    past = cache_ckv.shape[1]
    pos_sample = past + jnp.arange(x_sample.shape[1])
    y_p, ret_p, ckv_p, kr_p = _trunk(x_prompt, pos_prompt, None, None, None, p, causal=True)
    y_s, ret_s, ckv_s, kr_s = _trunk(x_sample, pos_sample, state_ret, cache_ckv, cache_krope, p, causal=False)
    return (y_p, y_s, ret_p, ckv_p, kr_p, ret_s, ckv_s, kr_s)
```

```python
import functools

import jax
import jax.numpy as jnp
from jax import lax
from jax.experimental import pallas as pl
from jax.experimental.pallas import tpu as pltpu

F32 = jnp.float32
BF16 = jnp.bfloat16

D_MODEL = 2048
CHUNK = 64
D_FF = 5632
RET_HEADS = 8
RET_DK = D_MODEL // RET_HEADS
RET_DV = 2 * D_MODEL // RET_HEADS
MLA_HEADS = 16
QK_NOPE = 128
QK_ROPE = 64
QK_HEAD = QK_NOPE + QK_ROPE
V_HEAD = 128
KV_LORA = 512
Q_LORA = 512
ROPE_BASE = 10000.0
EPS = 1e-6
NEG_INF = -1e30
LOG2E = 1.4426950408889634

VMEM_LIMIT_BYTES = 60 * 1024 * 1024

FFN_TM = 1024
FFN_TF = 512
FFN_NORM_CHUNKS = 4


def _params(*semantics):
    return pltpu.CompilerParams(dimension_semantics=semantics,
                                vmem_limit_bytes=VMEM_LIMIT_BYTES)


def _rms_scale(x):
    return lax.rsqrt(jnp.mean(x * x, axis=-1, keepdims=True) + EPS)


def _dot(a, b):
    return jnp.dot(a, b, preferred_element_type=F32)


def _dot_nt(a, b):
    return lax.dot_general(a, b, (((1,), (1,)), ((), ())), preferred_element_type=F32)


def _dot_tn(a, b):
    return lax.dot_general(a, b, (((0,), (0,)), ((), ())), preferred_element_type=F32)


def _silu(x):
    return x * jax.nn.sigmoid(x)


def _ffn_kernel(x_ref, g_ref, win_hbm, wout_hbm, o_ref, xn_ref, wa_buf, wb_buf, wo_buf, sem,
                *, layer, tf):
    i = pl.program_id(0)
    nf = D_FF // tf
    last = nf - 1
    forward = i % 2 == 0

    def block_of(pos):
        return jnp.where(forward, pos, last - pos)

    def copies(pos, slot):
        col = pl.multiple_of(block_of(pos) * tf, tf)
        return (
            pltpu.make_async_copy(win_hbm.at[layer, :, pl.ds(col, tf)], wa_buf.at[slot], sem.at[slot, 0]),
            pltpu.make_async_copy(win_hbm.at[layer, :, pl.ds(D_FF + col, tf)], wb_buf.at[slot], sem.at[slot, 1]),
            pltpu.make_async_copy(wout_hbm.at[layer, pl.ds(col, tf), :], wo_buf.at[slot], sem.at[slot, 2]),
        )

    def start(pos, slot):
        for c in copies(pos, slot):
            c.start()

    def wait(pos, slot):
        for c in copies(pos, slot):
            c.wait()

    def update(xn, slot):
        a = _dot(xn, wa_buf[slot])
        b = _dot(xn, wb_buf[slot])
        h = (_silu(a) * (0.5 * b)).astype(BF16)
        return _dot(h, wo_buf[slot])

    @pl.when(i == 0)
    def _():
        start(0, 0)
        start(1, 1)
        wait(0, 0)

    tc = x_ref.shape[0] // FFN_NORM_CHUNKS

    def norm(c):
        x = x_ref[c * tc:(c + 1) * tc, :]
        xn = (x * _rms_scale(x) * g_ref[...]).astype(BF16)
        xn_ref[c * tc:(c + 1) * tc, :] = xn
        return x, xn

    cur = norm(0)
    for c in range(FFN_NORM_CHUNKS):
        nxt = norm(c + 1) if c + 1 < FFN_NORM_CHUNKS else None
        x, xn = cur
        o_ref[c * tc:(c + 1) * tc, :] = x + update(xn, 0)
        cur = nxt
    start(2, 0)

    def pair(q, carry):
        p1 = 2 * q + 1

        @pl.when(jnp.logical_or(q > 0, i == 0))
        def _():
            wait(p1, 1)

        o_ref[...] += update(xn_ref[...], 1)

        @pl.when(p1 + 2 <= last)
        def _():
            start(p1 + 2, 1)

        wait(p1 + 1, 0)
        o_ref[...] += update(xn_ref[...], 0)

        @pl.when(p1 + 3 <= last)
        def _():
            start(p1 + 3, 0)

        return carry

    lax.fori_loop(0, (nf - 1) // 2, pair, 0)


def _ffn(x, g, w_in, w_out, layer, *, tm, tf):
    m, d = x.shape
    nf = D_FF // tf
    assert nf % 2 == 1 and nf >= 3, "the block walk assumes an odd number of hidden blocks"
    return pl.pallas_call(
        functools.partial(_ffn_kernel, layer=layer, tf=tf),
        out_shape=jax.ShapeDtypeStruct((m, d), F32),
        grid=(m // tm,),
        in_specs=[
            pl.BlockSpec((tm, d), lambda i: (i, 0)),
            pl.BlockSpec((1, d), lambda i: (0, 0)),
            pl.BlockSpec(memory_space=pl.ANY),
            pl.BlockSpec(memory_space=pl.ANY),
        ],
        out_specs=pl.BlockSpec((tm, d), lambda i: (i, 0)),
        scratch_shapes=[
            pltpu.VMEM((tm, d), BF16),
            pltpu.VMEM((2, d, tf), BF16),
            pltpu.VMEM((2, d, tf), BF16),
            pltpu.VMEM((2, tf, d), BF16),
            pltpu.SemaphoreType.DMA((2, 3)),
        ],
        compiler_params=_params("arbitrary"),
        name="ffn",
    )(x, g.reshape(1, d), w_in, w_out)


def _ret_in_kernel(x_ref, g_ref, w_ref, cos_ref, sin_ref, o_ref, xn_ref, *, rope_blocks, heads_per_block):
    j = pl.program_id(1)

    @pl.when(j == 0)
    def _():
        x = x_ref[...]
        xn_ref[...] = (x * _rms_scale(x) * g_ref[...]).astype(BF16)

    half = RET_DK // 2

    @pl.when(j < 2 * rope_blocks)
    def _():
        scale = jnp.where(j < rope_blocks, RET_DK ** -0.5, 1.0).astype(F32)
        cos = cos_ref[...] * scale
        sin = sin_ref[...] * scale
        xn = xn_ref[...]
        for h in range(heads_per_block):
            lo = h * RET_DK
            y = _dot(xn, w_ref[:, lo:lo + RET_DK])
            x1 = y[:, :half]
            x2 = y[:, half:]
            o_ref[:, lo:lo + half] = (x1 * cos - x2 * sin).astype(BF16)
            o_ref[:, lo + half:lo + RET_DK] = (x1 * sin + x2 * cos).astype(BF16)

    @pl.when(j >= 2 * rope_blocks)
    def _():
        o_ref[...] = _dot(xn_ref[...], w_ref[...]).astype(BF16)


def _ret_in(x, g, w, cos, sin, *, tm, tn):
    m, d = x.shape
    n = w.shape[1]
    tab_blocks = cos.shape[0] // tm
    kern = functools.partial(_ret_in_kernel, rope_blocks=(RET_HEADS * RET_DK) // tn,
                             heads_per_block=tn // RET_DK)
    return pl.pallas_call(
        kern,
        out_shape=jax.ShapeDtypeStruct((m, n), BF16),
        grid=(m // tm, n // tn),
        in_specs=[
            pl.BlockSpec((tm, d), lambda i, j: (i, 0)),
            pl.BlockSpec((1, d), lambda i, j: (0, 0)),
            pl.BlockSpec((d, tn), lambda i, j: (0, j)),
            pl.BlockSpec((tm, RET_DK // 2), lambda i, j: (i % tab_blocks, 0)),
            pl.BlockSpec((tm, RET_DK // 2), lambda i, j: (i % tab_blocks, 0)),
        ],
        out_specs=pl.BlockSpec((tm, tn), lambda i, j: (i, j)),
        scratch_shapes=[pltpu.VMEM((tm, d), BF16)],
        compiler_params=_params("parallel", "arbitrary"),
        name="ret_in",
    )(x, g.reshape(1, d), w, cos, sin)


def _ret_core_kernel(lg_ref, q_ref, k_ref, v_ref, g_ref, gain_ref, *rest, chunk, heads, has_state):
    if has_state:
        s0_ref, y_ref, s_ref = rest
    else:
        y_ref, s_ref = rest
    head0 = pl.program_id(1) * heads
    n_chunks = q_ref.shape[0] // chunk
    row = lax.broadcasted_iota(jnp.int32, (chunk, chunk), 0)
    col = lax.broadcasted_iota(jnp.int32, (chunk, chunk), 1)
    diff = (row - col).astype(F32)
    n = lax.broadcasted_iota(jnp.int32, (chunk, 1), 0).astype(F32)

    def head_consts(hh):
        lg = lg_ref[head0 + hh]
        decay = jnp.where(diff >= 0, jnp.exp(lg * jnp.maximum(diff, 0.0)), 0.0)
        cross = jnp.exp(lg * (n + 1.0))
        kdec = jnp.exp(lg * (chunk - 1.0 - n))
        carry = jnp.exp(lg * jnp.full((1, RET_DV), float(chunk), F32))
        return decay, cross, kdec, carry

    def mix(hh, c, consts):
        decay, cross, kdec, carry = consts
        rows = slice(c * chunk, (c + 1) * chunk)
        q = q_ref[rows, hh * RET_DK:(hh + 1) * RET_DK]
        k = k_ref[rows, hh * RET_DK:(hh + 1) * RET_DK]
        v = v_ref[rows, hh * RET_DV:(hh + 1) * RET_DV]
        inner = (_dot_nt(q, k) * decay).astype(BF16)
        if c > 0:
            s_old = s_ref[0, hh]
        elif has_state:
            s_old = s0_ref[0, hh]
        else:
            s_old = None
        kd = (k.astype(F32) * kdec).astype(BF16)
        o = _dot(inner, v)
        s_new = _dot_tn(kd, v)
        if s_old is not None:
            o = o + cross * _dot(q, s_old.astype(BF16))
            s_new = carry * s_old + s_new
        s_ref[0, hh] = s_new
        return o

    def norm_gate(hh, c, o):
        rows = slice(c * chunk, (c + 1) * chunk)
        cols = slice(hh * RET_DV, (hh + 1) * RET_DV)
        mu = jnp.mean(o, axis=-1, keepdims=True)
        dev = o - mu
        var = jnp.mean(dev * dev, axis=-1, keepdims=True)
        of = dev * lax.rsqrt(var + EPS) * gain_ref[:, cols]
        y_ref[rows, cols] = (_silu(g_ref[rows, cols].astype(F32)) * of).astype(BF16)

    consts = None
    pending = None
    for hh in range(heads):
        for c in range(n_chunks):
            if c == 0:
                consts = head_consts(hh)
            o = mix(hh, c, consts)
            if pending is not None:
                norm_gate(*pending)
            pending = (hh, c, o)
    norm_gate(*pending)


def _ret_core(qkvg, log_gamma, gain, state, *, batch, seq, chunk, heads):
    m = qkvg.shape[0]
    hk = RET_HEADS * RET_DK
    hv = RET_HEADS * RET_DV
    wk = heads * RET_DK
    wv = heads * RET_DV
    kblk = hk // wk
    vblk = 2 * hk // wv
    gblk = vblk + hv // wv
    has_state = state is not None
    in_specs = [
        pl.BlockSpec(memory_space=pltpu.SMEM),
        pl.BlockSpec((seq, wk), lambda b, g: (b, g)),
        pl.BlockSpec((seq, wk), lambda b, g: (b, kblk + g)),
        pl.BlockSpec((seq, wv), lambda b, g: (b, vblk + g)),
        pl.BlockSpec((seq, wv), lambda b, g: (b, gblk + g)),
        pl.BlockSpec((1, wv), lambda b, g: (0, g)),
    ]
    args = [log_gamma, qkvg, qkvg, qkvg, qkvg, gain.reshape(1, hv)]
    state_spec = pl.BlockSpec((1, heads, RET_DK, RET_DV), lambda b, g: (b, g, 0, 0))
    if has_state:
        in_specs.append(state_spec)
        args.append(state)
    kern = functools.partial(_ret_core_kernel, chunk=chunk, heads=heads, has_state=has_state)
    return pl.pallas_call(
        kern,
        out_shape=(jax.ShapeDtypeStruct((m, hv), BF16),
                   jax.ShapeDtypeStruct((batch, RET_HEADS, RET_DK, RET_DV), F32)),
        grid=(batch, RET_HEADS // heads),
        in_specs=in_specs,
        out_specs=(pl.BlockSpec((seq, wv), lambda b, g: (b, g)), state_spec),
        compiler_params=_params("parallel", "parallel"),
        name="ret_core",
    )(*args)


def _mm_res_kernel(a_ref, w_ref, r_ref, *rest):
    n_cast = (len(rest) - 1) // 2
    o_ref = rest[n_cast]
    o_ref[...] = r_ref[...] + _dot(a_ref[...], w_ref[...])
    for src, dst in zip(rest[:n_cast], rest[n_cast + 1:]):
        dst[...] = src[...].astype(BF16)


def _mm_res(a, w, res, *, tm, cast=()):
    m, k = a.shape
    n = w.shape[1]
    steps = m // tm
    in_specs = [
        pl.BlockSpec((tm, k), lambda i: (i, 0)),
        pl.BlockSpec((k, n), lambda i: (0, 0), pipeline_mode=pl.Buffered(1)),
        pl.BlockSpec((tm, n), lambda i: (i, 0)),
    ]
    out_shape = [jax.ShapeDtypeStruct((m, n), F32)]
    out_specs = [pl.BlockSpec((tm, n), lambda i: (i, 0))]
    for wf, layer in cast:
        _, rows, cols = wf.shape
        slab = rows // steps
        assert slab * steps == rows and slab % 16 == 0
        in_specs.append(pl.BlockSpec((None, slab, cols), lambda i, layer=layer: (layer, i, 0)))
        out_shape.append(jax.ShapeDtypeStruct((1, rows, cols), BF16))
        out_specs.append(pl.BlockSpec((None, slab, cols), lambda i: (0, i, 0)))
    outs = pl.pallas_call(
        _mm_res_kernel,
        out_shape=out_shape,
        grid=(steps,),
        in_specs=in_specs,
        out_specs=out_specs,
        compiler_params=_params("parallel"),
        name="mm_res",
    )(a, w, res, *[wf for wf, _ in cast])
    return outs[0], tuple(outs[1:])


def _latent_kernel(x_ref, g_ref, w_ref, gl_ref, cos_ref, sin_ref, c_ref, kr_ref, *, row_chunks):
    half = QK_ROPE // 2
    tc = x_ref.shape[0] // row_chunks

    def norm(c):
        x = x_ref[c * tc:(c + 1) * tc, :]
        return (x * _rms_scale(x) * g_ref[...]).astype(BF16)

    def project(c, xn):
        rows = slice(c * tc, (c + 1) * tc)
        y = _dot(xn, w_ref[...])
        cl = y[:, :KV_LORA]
        c_ref[rows, :] = cl * _rms_scale(cl) * gl_ref[...]
        x1 = y[:, KV_LORA:KV_LORA + half]
        x2 = y[:, KV_LORA + half:KV_LORA + QK_ROPE]
        cos = cos_ref[rows, :]
        sin = sin_ref[rows, :]
        kr_ref[rows, :] = jnp.concatenate([x1 * cos - x2 * sin, x1 * sin + x2 * cos], axis=-1)

    xn = norm(0)
    for c in range(row_chunks):
        xn_next = norm(c + 1) if c + 1 < row_chunks else None
        project(c, xn)
        xn = xn_next


def _latent(x, g, w, g_lat, cos, sin, *, tm):
    m, d = x.shape
    n = w.shape[1]
    half = QK_ROPE // 2
    tab_blocks = cos.shape[0] // tm
    return pl.pallas_call(
        functools.partial(_latent_kernel, row_chunks=tm // 128),
        out_shape=(jax.ShapeDtypeStruct((m, KV_LORA), F32),
                   jax.ShapeDtypeStruct((m, QK_ROPE), F32)),
        grid=(m // tm,),
        in_specs=[
            pl.BlockSpec((tm, d), lambda i: (i, 0)),
            pl.BlockSpec((1, d), lambda i: (0, 0)),
            pl.BlockSpec((d, n), lambda i: (0, 0)),
            pl.BlockSpec((1, KV_LORA), lambda i: (0, 0)),
            pl.BlockSpec((tm, half), lambda i: (i % tab_blocks, 0)),
            pl.BlockSpec((tm, half), lambda i: (i % tab_blocks, 0)),
        ],
        out_specs=(pl.BlockSpec((tm, KV_LORA), lambda i: (i, 0)),
                   pl.BlockSpec((tm, QK_ROPE), lambda i: (i, 0))),
        compiler_params=_params("parallel"),
        name="latent",
    )(x, g.reshape(1, d), w, g_lat.reshape(1, KV_LORA), cos, sin)


def _kv_up_kernel(*refs, seg_rows):
    n_seg = len(seg_rows)
    wk_ref, wv_ref, gk_ref, k_ref, v_ref = refs[2 * n_seg:]
    g_nope = gk_ref[:, :QK_NOPE]
    g_rope = gk_ref[:, QK_NOPE:]
    lo = 0
    for s, rows in enumerate(seg_rows):
        c = refs[2 * s][...].astype(BF16)
        kr = refs[2 * s + 1][...]
        kn = _dot(c, wk_ref[...])
        v_ref[lo:lo + rows, :] = _dot(c, wv_ref[...]).astype(BF16)
        ss_rope = jnp.sum(kr * kr, axis=-1, keepdims=True)
        for h in range(MLA_HEADS):
            knh = kn[:, h * QK_NOPE:(h + 1) * QK_NOPE]
            ss = jnp.sum(knh * knh, axis=-1, keepdims=True) + ss_rope
            r = lax.rsqrt(ss * (1.0 / QK_HEAD) + EPS)
            k_ref[h, lo:lo + rows, :QK_NOPE] = (knh * r * g_nope).astype(BF16)
            k_ref[h, lo:lo + rows, QK_NOPE:] = (kr * r * g_rope).astype(BF16)
        lo += rows


def _kv_up(segments, wk, wv, gk, *, steps):
    seg_rows = tuple(c.shape[0] // steps for c, _ in segments)
    tm = sum(seg_rows)
    m = steps * tm
    in_specs, args = [], []
    for (c, kr), rows in zip(segments, seg_rows):
        in_specs += [pl.BlockSpec((rows, KV_LORA), lambda i: (i, 0)),
                     pl.BlockSpec((rows, QK_ROPE), lambda i: (i, 0))]
        args += [c, kr]
    in_specs += [pl.BlockSpec(wk.shape, lambda i: (0, 0)),
                 pl.BlockSpec(wv.shape, lambda i: (0, 0)),
                 pl.BlockSpec((1, QK_HEAD), lambda i: (0, 0))]
    return pl.pallas_call(
        functools.partial(_kv_up_kernel, seg_rows=seg_rows),
        out_shape=(jax.ShapeDtypeStruct((MLA_HEADS, m, QK_HEAD), BF16),
                   jax.ShapeDtypeStruct((m, MLA_HEADS * V_HEAD), BF16)),
        grid=(steps,),
        in_specs=in_specs,
        out_specs=(pl.BlockSpec((MLA_HEADS, tm, QK_HEAD), lambda i: (0, i, 0)),
                   pl.BlockSpec((tm, MLA_HEADS * V_HEAD), lambda i: (i, 0))),
        compiler_params=_params("parallel"),
        name="kv_up",
    )(*args, wk, wv, gk.reshape(1, QK_HEAD))


Q_SLOT = 2 * QK_NOPE


def _q_proj_kernel(x_ref, g_ref, wdq_ref, gl_ref, wuq_ref, cos_ref, sin_ref, gq_ref, q_ref, *, row_chunks):
    g_nope = gq_ref[:, :QK_NOPE]
    g_rope = gq_ref[:, QK_NOPE:]
    tc = x_ref.shape[0] // row_chunks

    def project(c):
        rows = slice(c * tc, (c + 1) * tc)
        x = x_ref[rows, :]
        hn = (x * _rms_scale(x) * g_ref[...]).astype(BF16)
        ql = _dot(hn, wdq_ref[...])
        qn = (ql * _rms_scale(ql) * gl_ref[...]).astype(BF16)
        return _dot(qn, wuq_ref[...])

    def per_head(c, q):
        rows = slice(c * tc, (c + 1) * tc)
        cos = cos_ref[rows, :]
        sin = sin_ref[rows, :]
        for h in range(MLA_HEADS):
            nh = q[:, h * Q_SLOT:h * Q_SLOT + QK_NOPE]
            t = q[:, h * Q_SLOT + QK_NOPE:(h + 1) * Q_SLOT]
            rp = t * cos + pltpu.roll(t, QK_ROPE, 1) * sin
            ss = jnp.sum(nh * nh + rp * rp, axis=-1, keepdims=True)
            r = lax.rsqrt(ss * (1.0 / QK_HEAD) + EPS) * (QK_HEAD ** -0.5 * LOG2E)
            q_ref[h, rows, :QK_NOPE] = (nh * r * g_nope).astype(BF16)
            q_ref[h, rows, QK_NOPE:] = (rp[:, :QK_ROPE] * r * g_rope).astype(BF16)

    q = project(0)
    for c in range(row_chunks):
        q_next = project(c + 1) if c + 1 < row_chunks else None
        per_head(c, q)
        q = q_next


def _q_proj(x, g, wdq, g_lat, wuq, cos, sin, gq, *, tm):
    m, d = x.shape
    tab_blocks = cos.shape[0] // tm
    tw = cos.shape[1]
    return pl.pallas_call(
        functools.partial(_q_proj_kernel, row_chunks=tm // 256),
        out_shape=jax.ShapeDtypeStruct((MLA_HEADS, m, QK_HEAD), BF16),
        grid=(m // tm,),
        in_specs=[
            pl.BlockSpec((tm, d), lambda i: (i, 0)),
            pl.BlockSpec((1, d), lambda i: (0, 0)),
            pl.BlockSpec(wdq.shape, lambda i: (0, 0)),
            pl.BlockSpec((1, Q_LORA), lambda i: (0, 0)),
            pl.BlockSpec(wuq.shape, lambda i: (0, 0)),
            pl.BlockSpec((tm, tw), lambda i: (i % tab_blocks, 0)),
            pl.BlockSpec((tm, tw), lambda i: (i % tab_blocks, 0)),
            pl.BlockSpec((1, QK_HEAD), lambda i: (0, 0)),
        ],
        out_specs=pl.BlockSpec((MLA_HEADS, tm, QK_HEAD), lambda i: (0, i, 0)),
        compiler_params=_params("parallel"),
        name="q_proj",
    )(x, g.reshape(1, d), wdq, g_lat.reshape(1, Q_LORA), wuq, cos, sin, gq.reshape(1, QK_HEAD))


def _attn_causal_kernel(q_ref, k_ref, v_ref, o_ref, *, heads, tq):
    n_q = q_ref.shape[1] // tq
    krow = lax.broadcasted_iota(jnp.int32, (tq, tq), 0) // CHUNK
    qcol = lax.broadcasted_iota(jnp.int32, (tq, tq), 1) // CHUNK
    keep = krow <= qcol

    def scores(h, qi):
        lo = qi * tq
        q = q_ref[h, lo:lo + tq, :]
        st_d = jnp.where(keep, _dot_nt(k_ref[h, lo:lo + tq, :], q), NEG_INF)
        st_p = _dot_nt(k_ref[h, :lo, :], q) if qi > 0 else None
        return st_d, st_p

    def softmax(st):
        st_d, st_p = st
        m = jnp.max(st_d, axis=0, keepdims=True)
        if st_p is not None:
            m = jnp.maximum(m, jnp.max(st_p, axis=0, keepdims=True))
        pt_d = jnp.exp2(st_d - m)
        l = jnp.sum(pt_d, axis=0, keepdims=True)
        pt_p = None
        if st_p is not None:
            pt_p = jnp.exp2(st_p - m)
            l = l + jnp.sum(pt_p, axis=0, keepdims=True)
            pt_p = pt_p.astype(BF16)
        return pt_d.astype(BF16), pt_p, l

    def values(h, qi, p):
        pt_d, pt_p, l = p
        lo = qi * tq
        cols = slice(h * V_HEAD, (h + 1) * V_HEAD)
        acc_t = _dot_tn(v_ref[lo:lo + tq, cols], pt_d)
        if pt_p is not None:
            acc_t = acc_t + _dot_tn(v_ref[:lo, cols], pt_p)
        o_ref[lo:lo + tq, cols] = (acc_t / l).T.astype(BF16)

    work = [(h, qi) for h in range(heads) for qi in range(n_q)]
    s_q, p_q = {}, {}
    for step in range(len(work) + 2):
        if step < len(work):
            s_q[step] = scores(*work[step])
        if 0 <= step - 1 < len(work):
            p_q[step - 1] = softmax(s_q.pop(step - 1))
        if 0 <= step - 2 < len(work):
            values(*work[step - 2], p_q.pop(step - 2))


def _attn_full_kernel(q_ref, k_ref, v_ref, o_ref, *, heads):
    def scores(h):
        return _dot_nt(q_ref[h], k_ref[h])

    def softmax(s):
        m = jnp.max(s, axis=-1, keepdims=True)
        p = jnp.exp2(s - m)
        return p.astype(BF16), jnp.sum(p, axis=-1, keepdims=True)

    def values(h, pl_):
        p, l = pl_
        cols = slice(h * V_HEAD, (h + 1) * V_HEAD)
        o_ref[:, cols] = (_dot(p, v_ref[:, cols]) / l).astype(BF16)

    s_q, p_q = {}, {}
    for step in range(heads + 2):
        if step < heads:
            s_q[step] = scores(step)
        if 0 <= step - 1 < heads:
            p_q[step - 1] = softmax(s_q.pop(step - 1))
        if 0 <= step - 2 < heads:
            values(step - 2, p_q.pop(step - 2))


def _attention(q, k, v, *, batch, t_q, t_kv, heads, tq, causal):
    hg = MLA_HEADS // heads
    if causal:
        kern = functools.partial(_attn_causal_kernel, heads=heads, tq=tq)
    else:
        kern = functools.partial(_attn_full_kernel, heads=heads)
    return pl.pallas_call(
        kern,
        out_shape=jax.ShapeDtypeStruct((batch * t_q, MLA_HEADS * V_HEAD), BF16),
        grid=(batch, hg),
        in_specs=[
            pl.BlockSpec((heads, t_q, QK_HEAD), lambda b, g: (g, b, 0)),
            pl.BlockSpec((heads, t_kv, QK_HEAD), lambda b, g: (g, b, 0)),
            pl.BlockSpec((t_kv, heads * V_HEAD), lambda b, g: (b, g)),
        ],
        out_specs=pl.BlockSpec((t_q, heads * V_HEAD), lambda b, g: (b, g)),
        compiler_params=_params("parallel", "parallel"),
        name="attention",
    )(q, k, v)


def _rope_tables(pos, half, reps, rows):
    inv = jnp.power(ROPE_BASE, -jnp.arange(half, dtype=F32) / half)
    ang = pos.astype(F32)[:, None] * inv[None, :]
    cos = jnp.tile(jnp.cos(ang), (max(rows // pos.shape[0], 1), reps))
    sin = jnp.tile(jnp.sin(ang), (max(rows // pos.shape[0], 1), reps))
    return cos, sin


def _mm_res_and_ffn2(a, w, h, layer, p):
    key = ("ffn2_bf16", layer)
    if key not in p:
        h, p[key] = _mm_res(a, w, h, tm=512, cast=((p["ffn2_w_in"], layer), (p["ffn2_w_out"], layer)))
    else:
        h, _ = _mm_res(a, w, h, tm=512)
    w_in, w_out = p[key]
    return _ffn(h, p["ffn2_norm"][layer], w_in, w_out, 0, tm=FFN_TM, tf=FFN_TF)


def _trunk(x, pos, ret_state, ckv_past, krope_past, p, *, causal):
    batch, seq, d = x.shape
    m = batch * seq
    tm = 1024
    h = x.reshape(m, d)

    h = _ffn(h, p["ffn1_norm"][0], p["ffn1_w_in"], p["ffn1_w_out"], 0, tm=FFN_TM, tf=FFN_TF)
    cos, sin = _rope_tables(pos, RET_DK // 2, 1, tm)
    qkvg = _ret_in(h, p["mix_norm"][0], p["ret_w_in"][0], cos, sin, tm=tm, tn=2048)
    log_gamma = jnp.log1p(-jnp.exp2(-5.0 - jnp.arange(RET_HEADS, dtype=F32)))
    chunk = min(seq, 256)
    y, s_fin = _ret_core(qkvg, log_gamma, p["ret_gn_gain"][0],
                         None if ret_state is None else ret_state[0],
                         batch=batch, seq=seq, chunk=chunk,
                         heads=max(1, min(RET_HEADS, 512 // seq)))
    h = _mm_res_and_ffn2(y, p["ret_w_out"][0], h, 0, p)

    cos, sin = _rope_tables(pos, QK_ROPE // 2, 1, tm)
    c_new, kr_new = _latent(h, p["kv_norm"], p["w_dkv"], p["kv_lat_norm"], cos, sin, tm=tm)
    if ckv_past is None:
        t_kv = seq
        segments, steps = [(c_new, kr_new)], m // tm
    else:
        past = ckv_past.shape[1]
        t_kv = past + seq
        segments = [(ckv_past.reshape(batch * past, KV_LORA), krope_past.reshape(batch * past, QK_ROPE)),
                    (c_new, kr_new)]
        steps = batch
    k_sh, v_sh = _kv_up(segments, p["w_uk"], p["w_uv"], p["k_norm"], steps=steps)

    h = _ffn(h, p["ffn1_norm"][1], p["ffn1_w_in"], p["ffn1_w_out"], 1, tm=FFN_TM, tf=FFN_TF)
    cos, sin = _rope_tables(pos, QK_ROPE // 2, 1, tm)
    zeros = jnp.zeros_like(cos)
    cos = jnp.concatenate([cos, cos, zeros, zeros], axis=-1)
    sin = jnp.concatenate([-sin, sin, zeros, zeros], axis=-1)
    q = _q_proj(h, p["mix_norm"][1], p["w_dq"][0], p["q_lat_norm"][0], p["w_uq"][0], cos, sin,
                p["q_norm"][0], tm=tm)
    if causal:
        o = _attention(q, k_sh, v_sh, batch=batch, t_q=seq, t_kv=t_kv, heads=2, tq=512, causal=True)
    else:
        o = _attention(q, k_sh, v_sh, batch=batch, t_q=seq, t_kv=t_kv, heads=MLA_HEADS, tq=seq, causal=False)
    h = _mm_res_and_ffn2(o, p["w_o"][0], h, 1, p)

    return (h.reshape(batch, seq, d), s_fin[None],
            c_new.reshape(batch, seq, KV_LORA), kr_new.reshape(batch, seq, QK_ROPE))


def kernel(x_prompt, x_sample, state_ret, cache_ckv, cache_krope, ffn1_norm, ffn1_w_in, ffn1_w_out, mix_norm, ffn2_norm, ffn2_w_in, ffn2_w_out, ret_w_in, ret_gn_gain, ret_w_out, kv_norm, w_dkv, kv_lat_norm, w_ukv, k_norm, w_dq, q_lat_norm, w_uq, q_norm, w_o):
    bf = lambda w: w.astype(BF16)
    w_ukv_h = w_ukv.reshape(KV_LORA, MLA_HEADS, QK_NOPE + V_HEAD)
    w_uk = w_ukv_h[:, :, :QK_NOPE].reshape(KV_LORA, MLA_HEADS * QK_NOPE)
    w_uv = w_ukv_h[:, :, QK_NOPE:].reshape(KV_LORA, MLA_HEADS * V_HEAD)
    half = QK_ROPE // 2
    w_uq_h = w_uq.reshape(w_uq.shape[0], Q_LORA, MLA_HEADS, QK_HEAD)
    w_lo = w_uq_h[..., QK_NOPE:QK_NOPE + half]
    w_hi = w_uq_h[..., QK_NOPE + half:]
    w_uq_p = jnp.concatenate([w_uq_h[..., :QK_NOPE], w_lo, w_hi, w_hi, w_lo], axis=-1)
    w_uq_p = w_uq_p.reshape(w_uq.shape[0], Q_LORA, MLA_HEADS * Q_SLOT)
    p = dict(ffn1_norm=ffn1_norm, ffn1_w_in=bf(ffn1_w_in), ffn1_w_out=bf(ffn1_w_out),
             mix_norm=mix_norm, ffn2_norm=ffn2_norm, ffn2_w_in=ffn2_w_in, ffn2_w_out=ffn2_w_out,
             ret_w_in=bf(ret_w_in), ret_gn_gain=ret_gn_gain, ret_w_out=bf(ret_w_out),
             kv_norm=kv_norm, w_dkv=bf(w_dkv), kv_lat_norm=kv_lat_norm,
             w_uk=bf(w_uk), w_uv=bf(w_uv), k_norm=k_norm,
             w_dq=bf(w_dq), q_lat_norm=q_lat_norm, w_uq=bf(w_uq_p), q_norm=q_norm, w_o=bf(w_o))
    pos_prompt = jnp.arange(x_prompt.shape[1])
    past = cache_ckv.shape[1]
    pos_sample = past + jnp.arange(x_sample.shape[1])
    y_p, ret_p, ckv_p, kr_p = _trunk(x_prompt, pos_prompt, None, None, None, p, causal=True)
    y_s, ret_s, ckv_s, kr_s = _trunk(x_sample, pos_sample, state_ret, cache_ckv, cache_krope, p, causal=False)
    return (y_p, y_s, ret_p, ckv_p, kr_p, ret_s, ckv_s, kr_s)
```

```python
import functools

import jax
import jax.numpy as jnp
from jax import lax
from jax.experimental import pallas as pl
from jax.experimental.pallas import tpu as pltpu

F32 = jnp.float32
BF16 = jnp.bfloat16

D_MODEL = 2048
CHUNK = 64
D_FF = 5632
RET_HEADS = 8
RET_DK = D_MODEL // RET_HEADS
RET_DV = 2 * D_MODEL // RET_HEADS
MLA_HEADS = 16
QK_NOPE = 128
QK_ROPE = 64
QK_HEAD = QK_NOPE + QK_ROPE
V_HEAD = 128
KV_LORA = 512
Q_LORA = 512
ROPE_BASE = 10000.0
EPS = 1e-6
NEG_INF = -1e30
LOG2E = 1.4426950408889634

VMEM_LIMIT_BYTES = 60 * 1024 * 1024

FFN_TM = 1024
FFN_TF = 512
FFN_NORM_CHUNKS = 4


def _params(*semantics):
    return pltpu.CompilerParams(dimension_semantics=semantics,
                                vmem_limit_bytes=VMEM_LIMIT_BYTES)


def _rms_scale(x):
    return lax.rsqrt(jnp.mean(x * x, axis=-1, keepdims=True) + EPS)


def _dot(a, b):
    return jnp.dot(a, b, preferred_element_type=F32)


def _dot_nt(a, b):
    return lax.dot_general(a, b, (((1,), (1,)), ((), ())), preferred_element_type=F32)


def _dot_tn(a, b):
    return lax.dot_general(a, b, (((0,), (0,)), ((), ())), preferred_element_type=F32)


def _silu(x):
    return x * jax.nn.sigmoid(x)


def _ffn_kernel(x_ref, g_ref, win_hbm, wout_hbm, o_ref, xn_ref, wa_buf, wb_buf, wo_buf, sem,
                *, layer, tf):
    i = pl.program_id(0)
    nf = D_FF // tf
    last = nf - 1
    forward = i % 2 == 0

    def block_of(pos):
        return jnp.where(forward, pos, last - pos)

    def copies(pos, slot):
        col = pl.multiple_of(block_of(pos) * tf, tf)
        return (
            pltpu.make_async_copy(win_hbm.at[layer, :, pl.ds(col, tf)], wa_buf.at[slot], sem.at[slot, 0]),
            pltpu.make_async_copy(win_hbm.at[layer, :, pl.ds(D_FF + col, tf)], wb_buf.at[slot], sem.at[slot, 1]),
            pltpu.make_async_copy(wout_hbm.at[layer, pl.ds(col, tf), :], wo_buf.at[slot], sem.at[slot, 2]),
        )

    def start(pos, slot):
        for c in copies(pos, slot):
            c.start()

    def wait(pos, slot):
        for c in copies(pos, slot):
            c.wait()

    def update(xn, slot):
        a = _dot(xn, wa_buf[slot])
        b = _dot(xn, wb_buf[slot])
        h = (_silu(a) * (0.5 * b)).astype(BF16)
        return _dot(h, wo_buf[slot])

    @pl.when(i == 0)
    def _():
        start(0, 0)
        start(1, 1)
        wait(0, 0)

    tc = x_ref.shape[0] // FFN_NORM_CHUNKS

    def norm(c):
        x = x_ref[c * tc:(c + 1) * tc, :]
        xn = (x * _rms_scale(x) * g_ref[...]).astype(BF16)
        xn_ref[c * tc:(c + 1) * tc, :] = xn
        return x, xn

    cur = norm(0)
    for c in range(FFN_NORM_CHUNKS):
        nxt = norm(c + 1) if c + 1 < FFN_NORM_CHUNKS else None
        x, xn = cur
        o_ref[c * tc:(c + 1) * tc, :] = x + update(xn, 0)
        cur = nxt
    start(2, 0)

    def pair(q, carry):
        p1 = 2 * q + 1

        @pl.when(jnp.logical_or(q > 0, i == 0))
        def _():
            wait(p1, 1)

        o_ref[...] += update(xn_ref[...], 1)

        @pl.when(p1 + 2 <= last)
        def _():
            start(p1 + 2, 1)

        wait(p1 + 1, 0)
        o_ref[...] += update(xn_ref[...], 0)

        @pl.when(p1 + 3 <= last)
        def _():
            start(p1 + 3, 0)

        return carry

    lax.fori_loop(0, (nf - 1) // 2, pair, 0)


def _ffn(x, g, w_in, w_out, layer, *, tm, tf):
    m, d = x.shape
    nf = D_FF // tf
    assert nf % 2 == 1 and nf >= 3, "the block walk assumes an odd number of hidden blocks"
    return pl.pallas_call(
        functools.partial(_ffn_kernel, layer=layer, tf=tf),
        out_shape=jax.ShapeDtypeStruct((m, d), F32),
        grid=(m // tm,),
        in_specs=[
            pl.BlockSpec((tm, d), lambda i: (i, 0)),
            pl.BlockSpec((1, d), lambda i: (0, 0)),
            pl.BlockSpec(memory_space=pl.ANY),
            pl.BlockSpec(memory_space=pl.ANY),
        ],
        out_specs=pl.BlockSpec((tm, d), lambda i: (i, 0)),
        scratch_shapes=[
            pltpu.VMEM((tm, d), BF16),
            pltpu.VMEM((2, d, tf), BF16),
            pltpu.VMEM((2, d, tf), BF16),
            pltpu.VMEM((2, tf, d), BF16),
            pltpu.SemaphoreType.DMA((2, 3)),
        ],
        compiler_params=_params("arbitrary"),
        name="ffn",
    )(x, g.reshape(1, d), w_in, w_out)


def _ret_in_kernel(x_ref, g_ref, w_ref, cos_ref, sin_ref, *rest, rope_blocks, heads_per_block, cast_steps):
    n_cast = len(cast_steps)
    o_ref = rest[n_cast]
    xn_ref = rest[-1]
    j = pl.program_id(1)

    for src, dst, steps in zip(rest[:n_cast], rest[n_cast + 1:-1], cast_steps):
        @pl.when(j < steps)
        def _(src=src, dst=dst):
            dst[...] = src[...].astype(BF16)

    @pl.when(j == 0)
    def _():
        x = x_ref[...]
        xn_ref[...] = (x * _rms_scale(x) * g_ref[...]).astype(BF16)

    half = RET_DK // 2

    @pl.when(j < 2 * rope_blocks)
    def _():
        scale = jnp.where(j < rope_blocks, RET_DK ** -0.5, 1.0).astype(F32)
        cos = cos_ref[...] * scale
        sin = sin_ref[...] * scale
        xn = xn_ref[...]
        for h in range(heads_per_block):
            lo = h * RET_DK
            y = _dot(xn, w_ref[:, lo:lo + RET_DK])
            x1 = y[:, :half]
            x2 = y[:, half:]
            o_ref[:, lo:lo + half] = (x1 * cos - x2 * sin).astype(BF16)
            o_ref[:, lo + half:lo + RET_DK] = (x1 * sin + x2 * cos).astype(BF16)

    @pl.when(j >= 2 * rope_blocks)
    def _():
        o_ref[...] = _dot(xn_ref[...], w_ref[...]).astype(BF16)


def _ret_in(x, g, w, cos, sin, *, tm, tn, cast=()):
    m, d = x.shape
    n = w.shape[1]
    tab_blocks = cos.shape[0] // tm
    n_i, n_j = m // tm, n // tn
    in_specs = [
        pl.BlockSpec((tm, d), lambda i, j: (i, 0)),
        pl.BlockSpec((1, d), lambda i, j: (0, 0)),
        pl.BlockSpec((d, tn), lambda i, j: (0, j)),
        pl.BlockSpec((tm, RET_DK // 2), lambda i, j: (i % tab_blocks, 0)),
        pl.BlockSpec((tm, RET_DK // 2), lambda i, j: (i % tab_blocks, 0)),
    ]
    out_shape = [jax.ShapeDtypeStruct((m, n), BF16)]
    out_specs = [pl.BlockSpec((tm, tn), lambda i, j: (i, j))]
    cast_steps = []
    for wf, layer in cast:
        _, rows, cols = wf.shape
        slab = rows // n_i
        assert slab * n_i == rows and slab % 16 == 0
        steps = max(s for s in range(1, n_j + 1) if cols % (128 * s) == 0)
        cast_steps.append(steps)
        in_specs.append(pl.BlockSpec((None, slab, cols // steps),
                                     lambda i, j, layer=layer, steps=steps: (layer, i, jnp.minimum(j, steps - 1))))
        out_shape.append(jax.ShapeDtypeStruct((1, rows, cols), BF16))
        out_specs.append(pl.BlockSpec((None, slab, cols // steps),
                                      lambda i, j, steps=steps: (0, i, jnp.minimum(j, steps - 1))))
    kern = functools.partial(_ret_in_kernel, rope_blocks=(RET_HEADS * RET_DK) // tn,
                             heads_per_block=tn // RET_DK, cast_steps=tuple(cast_steps))
    outs = pl.pallas_call(
        kern,
        out_shape=out_shape,
        grid=(n_i, n_j),
        in_specs=in_specs,
        out_specs=out_specs,
        scratch_shapes=[pltpu.VMEM((tm, d), BF16)],
        compiler_params=_params("parallel", "arbitrary"),
        name="ret_in",
    )(x, g.reshape(1, d), w, cos, sin, *[wf for wf, _ in cast])
    return outs[0], tuple(outs[1:])


def _ret_core_kernel(lg_ref, q_ref, k_ref, v_ref, g_ref, gain_ref, *rest, chunk, heads, has_state):
    if has_state:
        s0_ref, y_ref, s_ref = rest
    else:
        y_ref, s_ref = rest
    head0 = pl.program_id(1) * heads
    n_chunks = q_ref.shape[0] // chunk
    row = lax.broadcasted_iota(jnp.int32, (chunk, chunk), 0)
    col = lax.broadcasted_iota(jnp.int32, (chunk, chunk), 1)
    diff = (row - col).astype(F32)
    n = lax.broadcasted_iota(jnp.int32, (chunk, 1), 0).astype(F32)

    def head_consts(hh):
        lg = lg_ref[head0 + hh]
        decay = jnp.where(diff >= 0, jnp.exp(lg * jnp.maximum(diff, 0.0)), 0.0)
        cross = jnp.exp(lg * (n + 1.0))
        kdec = jnp.exp(lg * (chunk - 1.0 - n))
        carry = jnp.exp(lg * jnp.full((1, RET_DV), float(chunk), F32))
        return decay, cross, kdec, carry

    def mix(hh, c, consts):
        decay, cross, kdec, carry = consts
        rows = slice(c * chunk, (c + 1) * chunk)
        q = q_ref[rows, hh * RET_DK:(hh + 1) * RET_DK]
        k = k_ref[rows, hh * RET_DK:(hh + 1) * RET_DK]
        v = v_ref[rows, hh * RET_DV:(hh + 1) * RET_DV]
        inner = (_dot_nt(q, k) * decay).astype(BF16)
        if c > 0:
            s_old = s_ref[0, hh]
        elif has_state:
            s_old = s0_ref[0, hh]
        else:
            s_old = None
        kd = (k.astype(F32) * kdec).astype(BF16)
        o = _dot(inner, v)
        s_new = _dot_tn(kd, v)
        if s_old is not None:
            o = o + cross * _dot(q, s_old.astype(BF16))
            s_new = carry * s_old + s_new
        s_ref[0, hh] = s_new
        return o

    def norm_gate(hh, c, o):
        rows = slice(c * chunk, (c + 1) * chunk)
        cols = slice(hh * RET_DV, (hh + 1) * RET_DV)
        mu = jnp.mean(o, axis=-1, keepdims=True)
        dev = o - mu
        var = jnp.mean(dev * dev, axis=-1, keepdims=True)
        of = dev * lax.rsqrt(var + EPS) * gain_ref[:, cols]
        y_ref[rows, cols] = (_silu(g_ref[rows, cols].astype(F32)) * of).astype(BF16)

    consts = None
    pending = None
    for hh in range(heads):
        for c in range(n_chunks):
            if c == 0:
                consts = head_consts(hh)
            o = mix(hh, c, consts)
            if pending is not None:
                norm_gate(*pending)
            pending = (hh, c, o)
    norm_gate(*pending)


def _ret_core(qkvg, log_gamma, gain, state, *, batch, seq, chunk, heads):
    m = qkvg.shape[0]
    hk = RET_HEADS * RET_DK
    hv = RET_HEADS * RET_DV
    wk = heads * RET_DK
    wv = heads * RET_DV
    kblk = hk // wk
    vblk = 2 * hk // wv
    gblk = vblk + hv // wv
    has_state = state is not None
    in_specs = [
        pl.BlockSpec(memory_space=pltpu.SMEM),
        pl.BlockSpec((seq, wk), lambda b, g: (b, g)),
        pl.BlockSpec((seq, wk), lambda b, g: (b, kblk + g)),
        pl.BlockSpec((seq, wv), lambda b, g: (b, vblk + g)),
        pl.BlockSpec((seq, wv), lambda b, g: (b, gblk + g)),
        pl.BlockSpec((1, wv), lambda b, g: (0, g)),
    ]
    args = [log_gamma, qkvg, qkvg, qkvg, qkvg, gain.reshape(1, hv)]
    state_spec = pl.BlockSpec((1, heads, RET_DK, RET_DV), lambda b, g: (b, g, 0, 0))
    if has_state:
        in_specs.append(state_spec)
        args.append(state)
    kern = functools.partial(_ret_core_kernel, chunk=chunk, heads=heads, has_state=has_state)
    return pl.pallas_call(
        kern,
        out_shape=(jax.ShapeDtypeStruct((m, hv), BF16),
                   jax.ShapeDtypeStruct((batch, RET_HEADS, RET_DK, RET_DV), F32)),
        grid=(batch, RET_HEADS // heads),
        in_specs=in_specs,
        out_specs=(pl.BlockSpec((seq, wv), lambda b, g: (b, g)), state_spec),
        compiler_params=_params("parallel", "parallel"),
        name="ret_core",
    )(*args)


def _mm_res_kernel(a_ref, w_ref, r_ref, *rest):
    n_cast = (len(rest) - 1) // 2
    o_ref = rest[n_cast]
    o_ref[...] = r_ref[...] + _dot(a_ref[...], w_ref[...])
    for src, dst in zip(rest[:n_cast], rest[n_cast + 1:]):
        dst[...] = src[...].astype(BF16)


def _mm_res(a, w, res, *, tm, cast=()):
    m, k = a.shape
    n = w.shape[1]
    steps = m // tm
    in_specs = [
        pl.BlockSpec((tm, k), lambda i: (i, 0)),
        pl.BlockSpec((k, n), lambda i: (0, 0), pipeline_mode=pl.Buffered(1)),
        pl.BlockSpec((tm, n), lambda i: (i, 0)),
    ]
    out_shape = [jax.ShapeDtypeStruct((m, n), F32)]
    out_specs = [pl.BlockSpec((tm, n), lambda i: (i, 0))]
    for wf, layer in cast:
        _, rows, cols = wf.shape
        slab = rows // steps
        assert slab * steps == rows and slab % 16 == 0
        in_specs.append(pl.BlockSpec((None, slab, cols), lambda i, layer=layer: (layer, i, 0)))
        out_shape.append(jax.ShapeDtypeStruct((1, rows, cols), BF16))
        out_specs.append(pl.BlockSpec((None, slab, cols), lambda i: (0, i, 0)))
    outs = pl.pallas_call(
        _mm_res_kernel,
        out_shape=out_shape,
        grid=(steps,),
        in_specs=in_specs,
        out_specs=out_specs,
        compiler_params=_params("parallel"),
        name="mm_res",
    )(a, w, res, *[wf for wf, _ in cast])
    return outs[0], tuple(outs[1:])


def _latent_kernel(x_ref, g_ref, w_ref, gl_ref, cos_ref, sin_ref, c_ref, kr_ref, *, row_chunks):
    half = QK_ROPE // 2
    tc = x_ref.shape[0] // row_chunks

    def norm(c):
        x = x_ref[c * tc:(c + 1) * tc, :]
        return (x * _rms_scale(x) * g_ref[...]).astype(BF16)

    def project(c, xn):
        rows = slice(c * tc, (c + 1) * tc)
        y = _dot(xn, w_ref[...])
        cl = y[:, :KV_LORA]
        c_ref[rows, :] = cl * _rms_scale(cl) * gl_ref[...]
        x1 = y[:, KV_LORA:KV_LORA + half]
        x2 = y[:, KV_LORA + half:KV_LORA + QK_ROPE]
        cos = cos_ref[rows, :]
        sin = sin_ref[rows, :]
        kr_ref[rows, :] = jnp.concatenate([x1 * cos - x2 * sin, x1 * sin + x2 * cos], axis=-1)

    xn = norm(0)
    for c in range(row_chunks):
        xn_next = norm(c + 1) if c + 1 < row_chunks else None
        project(c, xn)
        xn = xn_next


def _latent(x, g, w, g_lat, cos, sin, *, tm):
    m, d = x.shape
    n = w.shape[1]
    half = QK_ROPE // 2
    tab_blocks = cos.shape[0] // tm
    return pl.pallas_call(
        functools.partial(_latent_kernel, row_chunks=tm // 128),
        out_shape=(jax.ShapeDtypeStruct((m, KV_LORA), F32),
                   jax.ShapeDtypeStruct((m, QK_ROPE), F32)),
        grid=(m // tm,),
        in_specs=[
            pl.BlockSpec((tm, d), lambda i: (i, 0)),
            pl.BlockSpec((1, d), lambda i: (0, 0)),
            pl.BlockSpec((d, n), lambda i: (0, 0)),
            pl.BlockSpec((1, KV_LORA), lambda i: (0, 0)),
            pl.BlockSpec((tm, half), lambda i: (i % tab_blocks, 0)),
            pl.BlockSpec((tm, half), lambda i: (i % tab_blocks, 0)),
        ],
        out_specs=(pl.BlockSpec((tm, KV_LORA), lambda i: (i, 0)),
                   pl.BlockSpec((tm, QK_ROPE), lambda i: (i, 0))),
        compiler_params=_params("parallel"),
        name="latent",
    )(x, g.reshape(1, d), w, g_lat.reshape(1, KV_LORA), cos, sin)


def _kv_up_kernel(*refs, seg_rows):
    n_seg = len(seg_rows)
    wk_ref, wv_ref, gk_ref, k_ref, v_ref = refs[2 * n_seg:]
    g_nope = gk_ref[:, :QK_NOPE]
    g_rope = gk_ref[:, QK_NOPE:]
    lo = 0
    for s, rows in enumerate(seg_rows):
        c = refs[2 * s][...].astype(BF16)
        kr = refs[2 * s + 1][...]
        kn = _dot(c, wk_ref[...])
        v_ref[lo:lo + rows, :] = _dot(c, wv_ref[...]).astype(BF16)
        ss_rope = jnp.sum(kr * kr, axis=-1, keepdims=True)
        for h in range(MLA_HEADS):
            knh = kn[:, h * QK_NOPE:(h + 1) * QK_NOPE]
            ss = jnp.sum(knh * knh, axis=-1, keepdims=True) + ss_rope
            r = lax.rsqrt(ss * (1.0 / QK_HEAD) + EPS)
            k_ref[h, lo:lo + rows, :QK_NOPE] = (knh * r * g_nope).astype(BF16)
            k_ref[h, lo:lo + rows, QK_NOPE:] = (kr * r * g_rope).astype(BF16)
        lo += rows


def _kv_up(segments, wk, wv, gk, *, steps):
    seg_rows = tuple(c.shape[0] // steps for c, _ in segments)
    tm = sum(seg_rows)
    m = steps * tm
    in_specs, args = [], []
    for (c, kr), rows in zip(segments, seg_rows):
        in_specs += [pl.BlockSpec((rows, KV_LORA), lambda i: (i, 0)),
                     pl.BlockSpec((rows, QK_ROPE), lambda i: (i, 0))]
        args += [c, kr]
    in_specs += [pl.BlockSpec(wk.shape, lambda i: (0, 0)),
                 pl.BlockSpec(wv.shape, lambda i: (0, 0)),
                 pl.BlockSpec((1, QK_HEAD), lambda i: (0, 0))]
    return pl.pallas_call(
        functools.partial(_kv_up_kernel, seg_rows=seg_rows),
        out_shape=(jax.ShapeDtypeStruct((MLA_HEADS, m, QK_HEAD), BF16),
                   jax.ShapeDtypeStruct((m, MLA_HEADS * V_HEAD), BF16)),
        grid=(steps,),
        in_specs=in_specs,
        out_specs=(pl.BlockSpec((MLA_HEADS, tm, QK_HEAD), lambda i: (0, i, 0)),
                   pl.BlockSpec((tm, MLA_HEADS * V_HEAD), lambda i: (i, 0))),
        compiler_params=_params("parallel"),
        name="kv_up",
    )(*args, wk, wv, gk.reshape(1, QK_HEAD))


Q_SLOT = 2 * QK_NOPE


def _q_proj_kernel(x_ref, g_ref, wdq_ref, gl_ref, wuq_ref, cos_ref, sin_ref, gq_ref, q_ref, *, row_chunks):
    g_nope = gq_ref[:, :QK_NOPE]
    g_rope = gq_ref[:, QK_NOPE:]
    tc = x_ref.shape[0] // row_chunks

    def project(c):
        rows = slice(c * tc, (c + 1) * tc)
        x = x_ref[rows, :]
        hn = (x * _rms_scale(x) * g_ref[...]).astype(BF16)
        ql = _dot(hn, wdq_ref[...])
        qn = (ql * _rms_scale(ql) * gl_ref[...]).astype(BF16)
        return _dot(qn, wuq_ref[...])

    def per_head(c, q):
        rows = slice(c * tc, (c + 1) * tc)
        cos = cos_ref[rows, :]
        sin = sin_ref[rows, :]
        for h in range(MLA_HEADS):
            nh = q[:, h * Q_SLOT:h * Q_SLOT + QK_NOPE]
            t = q[:, h * Q_SLOT + QK_NOPE:(h + 1) * Q_SLOT]
            rp = t * cos + pltpu.roll(t, QK_ROPE, 1) * sin
            ss = jnp.sum(nh * nh + rp * rp, axis=-1, keepdims=True)
            r = lax.rsqrt(ss * (1.0 / QK_HEAD) + EPS) * (QK_HEAD ** -0.5 * LOG2E)
            q_ref[h, rows, :QK_NOPE] = (nh * r * g_nope).astype(BF16)
            q_ref[h, rows, QK_NOPE:] = (rp[:, :QK_ROPE] * r * g_rope).astype(BF16)

    q = project(0)
    for c in range(row_chunks):
        q_next = project(c + 1) if c + 1 < row_chunks else None
        per_head(c, q)
        q = q_next


def _q_proj(x, g, wdq, g_lat, wuq, cos, sin, gq, *, tm):
    m, d = x.shape
    tab_blocks = cos.shape[0] // tm
    tw = cos.shape[1]
    return pl.pallas_call(
        functools.partial(_q_proj_kernel, row_chunks=tm // 256),
        out_shape=jax.ShapeDtypeStruct((MLA_HEADS, m, QK_HEAD), BF16),
        grid=(m // tm,),
        in_specs=[
            pl.BlockSpec((tm, d), lambda i: (i, 0)),
            pl.BlockSpec((1, d), lambda i: (0, 0)),
            pl.BlockSpec(wdq.shape, lambda i: (0, 0)),
            pl.BlockSpec((1, Q_LORA), lambda i: (0, 0)),
            pl.BlockSpec(wuq.shape, lambda i: (0, 0)),
            pl.BlockSpec((tm, tw), lambda i: (i % tab_blocks, 0)),
            pl.BlockSpec((tm, tw), lambda i: (i % tab_blocks, 0)),
            pl.BlockSpec((1, QK_HEAD), lambda i: (0, 0)),
        ],
        out_specs=pl.BlockSpec((MLA_HEADS, tm, QK_HEAD), lambda i: (0, i, 0)),
        compiler_params=_params("parallel"),
        name="q_proj",
    )(x, g.reshape(1, d), wdq, g_lat.reshape(1, Q_LORA), wuq, cos, sin, gq.reshape(1, QK_HEAD))


def _attn_causal_kernel(q_ref, k_ref, v_ref, o_ref, *, heads, tq):
    n_q = q_ref.shape[1] // tq
    krow = lax.broadcasted_iota(jnp.int32, (tq, tq), 0) // CHUNK
    qcol = lax.broadcasted_iota(jnp.int32, (tq, tq), 1) // CHUNK
    keep = krow <= qcol

    def scores(h, qi):
        lo = qi * tq
        q = q_ref[h, lo:lo + tq, :]
        st_d = jnp.where(keep, _dot_nt(k_ref[h, lo:lo + tq, :], q), NEG_INF)
        st_p = _dot_nt(k_ref[h, :lo, :], q) if qi > 0 else None
        return st_d, st_p

    def softmax(st):
        st_d, st_p = st
        m = jnp.max(st_d, axis=0, keepdims=True)
        if st_p is not None:
            m = jnp.maximum(m, jnp.max(st_p, axis=0, keepdims=True))
        pt_d = jnp.exp2(st_d - m)
        l = jnp.sum(pt_d, axis=0, keepdims=True)
        pt_p = None
        if st_p is not None:
            pt_p = jnp.exp2(st_p - m)
            l = l + jnp.sum(pt_p, axis=0, keepdims=True)
            pt_p = pt_p.astype(BF16)
        return pt_d.astype(BF16), pt_p, l

    def values(h, qi, p):
        pt_d, pt_p, l = p
        lo = qi * tq
        cols = slice(h * V_HEAD, (h + 1) * V_HEAD)
        acc_t = _dot_tn(v_ref[lo:lo + tq, cols], pt_d)
        if pt_p is not None:
            acc_t = acc_t + _dot_tn(v_ref[:lo, cols], pt_p)
        o_ref[lo:lo + tq, cols] = (acc_t / l).T.astype(BF16)

    work = [(h, qi) for h in range(heads) for qi in range(n_q)]
    s_q, p_q = {}, {}
    for step in range(len(work) + 2):
        if step < len(work):
            s_q[step] = scores(*work[step])
        if 0 <= step - 1 < len(work):
            p_q[step - 1] = softmax(s_q.pop(step - 1))
        if 0 <= step - 2 < len(work):
            values(*work[step - 2], p_q.pop(step - 2))


def _attn_full_kernel(q_ref, k_ref, v_ref, o_ref, *, heads):
    def scores(h):
        return _dot_nt(q_ref[h], k_ref[h])

    def softmax(s):
        m = jnp.max(s, axis=-1, keepdims=True)
        p = jnp.exp2(s - m)
        return p.astype(BF16), jnp.sum(p, axis=-1, keepdims=True)

    def values(h, pl_):
        p, l = pl_
        cols = slice(h * V_HEAD, (h + 1) * V_HEAD)
        o_ref[:, cols] = (_dot(p, v_ref[:, cols]) / l).astype(BF16)

    s_q, p_q = {}, {}
    for step in range(heads + 2):
        if step < heads:
            s_q[step] = scores(step)
        if 0 <= step - 1 < heads:
            p_q[step - 1] = softmax(s_q.pop(step - 1))
        if 0 <= step - 2 < heads:
            values(step - 2, p_q.pop(step - 2))


def _attention(q, k, v, *, batch, t_q, t_kv, heads, tq, causal):
    hg = MLA_HEADS // heads
    if causal:
        kern = functools.partial(_attn_causal_kernel, heads=heads, tq=tq)
    else:
        kern = functools.partial(_attn_full_kernel, heads=heads)
    return pl.pallas_call(
        kern,
        out_shape=jax.ShapeDtypeStruct((batch * t_q, MLA_HEADS * V_HEAD), BF16),
        grid=(batch, hg),
        in_specs=[
            pl.BlockSpec((heads, t_q, QK_HEAD), lambda b, g: (g, b, 0)),
            pl.BlockSpec((heads, t_kv, QK_HEAD), lambda b, g: (g, b, 0)),
            pl.BlockSpec((t_kv, heads * V_HEAD), lambda b, g: (b, g)),
        ],
        out_specs=pl.BlockSpec((t_q, heads * V_HEAD), lambda b, g: (b, g)),
        compiler_params=_params("parallel", "parallel"),
        name="attention",
    )(q, k, v)


def _rope_tables(pos, half, reps, rows):
    inv = jnp.power(ROPE_BASE, -jnp.arange(half, dtype=F32) / half)
    ang = pos.astype(F32)[:, None] * inv[None, :]
    cos = jnp.tile(jnp.cos(ang), (max(rows // pos.shape[0], 1), reps))
    sin = jnp.tile(jnp.sin(ang), (max(rows // pos.shape[0], 1), reps))
    return cos, sin


def _mm_res_and_ffn2(a, w, h, layer, p):
    key = ("ffn2_bf16", layer)
    if key not in p:
        h, p[key] = _mm_res(a, w, h, tm=512, cast=((p["ffn2_w_in"], layer), (p["ffn2_w_out"], layer)))
    else:
        h, _ = _mm_res(a, w, h, tm=512)
    w_in, w_out = p[key]
    return _ffn(h, p["ffn2_norm"][layer], w_in, w_out, 0, tm=FFN_TM, tf=FFN_TF)


def _trunk(x, pos, ret_state, ckv_past, krope_past, p, *, causal):
    batch, seq, d = x.shape
    m = batch * seq
    tm = 1024
    h = x.reshape(m, d)

    h = _ffn(h, p["ffn1_norm"][0], p["ffn1_w_in_l0"], p["ffn1_w_out_l0"], 0, tm=FFN_TM, tf=FFN_TF)
    cos, sin = _rope_tables(pos, RET_DK // 2, 1, tm)
    key = "cast_by_ret_in"
    cast = () if key in p else ((p["ffn1_w_in"], 1), (p["ffn1_w_out"], 1), (p["ret_w_out"], 0), (p["w_o"], 0))
    qkvg, cast_out = _ret_in(h, p["mix_norm"][0], p["ret_w_in"][0], cos, sin, tm=tm, tn=2048, cast=cast)
    if cast:
        p[key] = cast_out
    ffn1_l1_w_in, ffn1_l1_w_out, ret_w_out, w_o = p[key]
    log_gamma = jnp.log1p(-jnp.exp2(-5.0 - jnp.arange(RET_HEADS, dtype=F32)))
    chunk = min(seq, 256)
    y, s_fin = _ret_core(qkvg, log_gamma, p["ret_gn_gain"][0],
                         None if ret_state is None else ret_state[0],
                         batch=batch, seq=seq, chunk=chunk,
                         heads=max(1, min(RET_HEADS, 512 // seq)))
    h = _mm_res_and_ffn2(y, ret_w_out[0], h, 0, p)

    cos, sin = _rope_tables(pos, QK_ROPE // 2, 1, tm)
    c_new, kr_new = _latent(h, p["kv_norm"], p["w_dkv"], p["kv_lat_norm"], cos, sin, tm=tm)
    if ckv_past is None:
        t_kv = seq
        segments, steps = [(c_new, kr_new)], m // tm
    else:
        past = ckv_past.shape[1]
        t_kv = past + seq
        segments = [(ckv_past.reshape(batch * past, KV_LORA), krope_past.reshape(batch * past, QK_ROPE)),
                    (c_new, kr_new)]
        steps = batch
    k_sh, v_sh = _kv_up(segments, p["w_uk"], p["w_uv"], p["k_norm"], steps=steps)

    h = _ffn(h, p["ffn1_norm"][1], ffn1_l1_w_in, ffn1_l1_w_out, 0, tm=FFN_TM, tf=FFN_TF)
    cos, sin = _rope_tables(pos, QK_ROPE // 2, 1, tm)
    zeros = jnp.zeros_like(cos)
    cos = jnp.concatenate([cos, cos, zeros, zeros], axis=-1)
    sin = jnp.concatenate([-sin, sin, zeros, zeros], axis=-1)
    q = _q_proj(h, p["mix_norm"][1], p["w_dq"][0], p["q_lat_norm"][0], p["w_uq"][0], cos, sin,
                p["q_norm"][0], tm=tm)
    if causal:
        o = _attention(q, k_sh, v_sh, batch=batch, t_q=seq, t_kv=t_kv, heads=2, tq=512, causal=True)
    else:
        o = _attention(q, k_sh, v_sh, batch=batch, t_q=seq, t_kv=t_kv, heads=MLA_HEADS, tq=seq, causal=False)
    h = _mm_res_and_ffn2(o, w_o[0], h, 1, p)

    return (h.reshape(batch, seq, d), s_fin[None],
            c_new.reshape(batch, seq, KV_LORA), kr_new.reshape(batch, seq, QK_ROPE))


def kernel(x_prompt, x_sample, state_ret, cache_ckv, cache_krope, ffn1_norm, ffn1_w_in, ffn1_w_out, mix_norm, ffn2_norm, ffn2_w_in, ffn2_w_out, ret_w_in, ret_gn_gain, ret_w_out, kv_norm, w_dkv, kv_lat_norm, w_ukv, k_norm, w_dq, q_lat_norm, w_uq, q_norm, w_o):
    bf = lambda w: w.astype(BF16)
    w_ukv_h = w_ukv.reshape(KV_LORA, MLA_HEADS, QK_NOPE + V_HEAD)
    w_uk = w_ukv_h[:, :, :QK_NOPE].reshape(KV_LORA, MLA_HEADS * QK_NOPE)
    w_uv = w_ukv_h[:, :, QK_NOPE:].reshape(KV_LORA, MLA_HEADS * V_HEAD)
    half = QK_ROPE // 2
    w_uq_h = w_uq.reshape(w_uq.shape[0], Q_LORA, MLA_HEADS, QK_HEAD)
    w_lo = w_uq_h[..., QK_NOPE:QK_NOPE + half]
    w_hi = w_uq_h[..., QK_NOPE + half:]
    w_uq_p = jnp.concatenate([w_uq_h[..., :QK_NOPE], w_lo, w_hi, w_hi, w_lo], axis=-1)
    w_uq_p = w_uq_p.reshape(w_uq.shape[0], Q_LORA, MLA_HEADS * Q_SLOT)
    p = dict(ffn1_norm=ffn1_norm, ffn1_w_in=ffn1_w_in, ffn1_w_out=ffn1_w_out,
             ffn1_w_in_l0=bf(ffn1_w_in[:1]), ffn1_w_out_l0=bf(ffn1_w_out[:1]),
             mix_norm=mix_norm, ffn2_norm=ffn2_norm, ffn2_w_in=ffn2_w_in, ffn2_w_out=ffn2_w_out,
             ret_w_in=bf(ret_w_in), ret_gn_gain=ret_gn_gain, ret_w_out=ret_w_out,
             kv_norm=kv_norm, w_dkv=bf(w_dkv), kv_lat_norm=kv_lat_norm,
             w_uk=bf(w_uk), w_uv=bf(w_uv), k_norm=k_norm,
             w_dq=bf(w_dq), q_lat_norm=q_lat_norm, w_uq=bf(w_uq_p), q_norm=q_norm, w_o=w_o)
    pos_prompt = jnp.arange(x_prompt.shape[1])
    past = cache_ckv.shape[1]
    pos_sample = past + jnp.arange(x_sample.shape[1])
    y_p, ret_p, ckv_p, kr_p = _trunk(x_prompt, pos_prompt, None, None, None, p, causal=True)
    y_s, ret_s, ckv_s, kr_s = _trunk(x_sample, pos_sample, state_ret, cache_ckv, cache_krope, p, causal=False)
    return (y_p, y_s, ret_p, ckv_p, kr_p, ret_s, ckv_s, kr_s)
```

```python
import functools

import jax
import jax.numpy as jnp
from jax import lax
from jax.experimental import pallas as pl
from jax.experimental.pallas import tpu as pltpu

F32 = jnp.float32
BF16 = jnp.bfloat16

D_MODEL = 2048
CHUNK = 64
D_FF = 5632
RET_HEADS = 8
RET_DK = D_MODEL // RET_HEADS
RET_DV = 2 * D_MODEL // RET_HEADS
MLA_HEADS = 16
QK_NOPE = 128
QK_ROPE = 64
QK_HEAD = QK_NOPE + QK_ROPE
V_HEAD = 128
KV_LORA = 512
Q_LORA = 512
ROPE_BASE = 10000.0
EPS = 1e-6
NEG_INF = -1e30
LOG2E = 1.4426950408889634

VMEM_LIMIT_BYTES = 60 * 1024 * 1024

FFN_TM = 1024
FFN_TF = 512
FFN_NORM_CHUNKS = 4


def _params(*semantics):
    return pltpu.CompilerParams(dimension_semantics=semantics,
                                vmem_limit_bytes=VMEM_LIMIT_BYTES)


def _rms_scale(x):
    return lax.rsqrt(jnp.mean(x * x, axis=-1, keepdims=True) + EPS)


def _dot(a, b):
    return jnp.dot(a, b, preferred_element_type=F32)


def _dot_nt(a, b):
    return lax.dot_general(a, b, (((1,), (1,)), ((), ())), preferred_element_type=F32)


def _dot_tn(a, b):
    return lax.dot_general(a, b, (((0,), (0,)), ((), ())), preferred_element_type=F32)


def _silu(x):
    return x * jax.nn.sigmoid(x)


def _ffn_kernel(x_ref, g_ref, win_hbm, wout_hbm, o_ref, xn_ref, wa_buf, wb_buf, wo_buf, sem,
                *, layer, tf):
    i = pl.program_id(0)
    nf = D_FF // tf
    last = nf - 1
    forward = i % 2 == 0

    def block_of(pos):
        return jnp.where(forward, pos, last - pos)

    def copies(pos, slot):
        col = pl.multiple_of(block_of(pos) * tf, tf)
        return (
            pltpu.make_async_copy(win_hbm.at[layer, :, pl.ds(col, tf)], wa_buf.at[slot], sem.at[slot, 0]),
            pltpu.make_async_copy(win_hbm.at[layer, :, pl.ds(D_FF + col, tf)], wb_buf.at[slot], sem.at[slot, 1]),
            pltpu.make_async_copy(wout_hbm.at[layer, pl.ds(col, tf), :], wo_buf.at[slot], sem.at[slot, 2]),
        )

    def start(pos, slot):
        for c in copies(pos, slot):
            c.start()

    def wait(pos, slot):
        for c in copies(pos, slot):
            c.wait()

    def update(xn, slot):
        a = _dot(xn, wa_buf[slot])
        b = _dot(xn, wb_buf[slot])
        h = (_silu(a) * (0.5 * b)).astype(BF16)
        return _dot(h, wo_buf[slot])

    @pl.when(i == 0)
    def _():
        start(0, 0)
        start(1, 1)
        wait(0, 0)

    tc = x_ref.shape[0] // FFN_NORM_CHUNKS

    def norm(c):
        x = x_ref[c * tc:(c + 1) * tc, :]
        xn = (x * _rms_scale(x) * g_ref[...]).astype(BF16)
        xn_ref[c * tc:(c + 1) * tc, :] = xn
        return x, xn

    cur = norm(0)
    for c in range(FFN_NORM_CHUNKS):
        nxt = norm(c + 1) if c + 1 < FFN_NORM_CHUNKS else None
        x, xn = cur
        o_ref[c * tc:(c + 1) * tc, :] = x + update(xn, 0)
        cur = nxt
    start(2, 0)

    def pair(q, carry):
        p1 = 2 * q + 1

        @pl.when(jnp.logical_or(q > 0, i == 0))
        def _():
            wait(p1, 1)

        o_ref[...] += update(xn_ref[...], 1)

        @pl.when(p1 + 2 <= last)
        def _():
            start(p1 + 2, 1)

        wait(p1 + 1, 0)
        o_ref[...] += update(xn_ref[...], 0)

        @pl.when(p1 + 3 <= last)
        def _():
            start(p1 + 3, 0)

        return carry

    lax.fori_loop(0, (nf - 1) // 2, pair, 0)


def _ffn(x, g, w_in, w_out, layer, *, tm, tf):
    m, d = x.shape
    nf = D_FF // tf
    assert nf % 2 == 1 and nf >= 3, "the block walk assumes an odd number of hidden blocks"
    return pl.pallas_call(
        functools.partial(_ffn_kernel, layer=layer, tf=tf),
        out_shape=jax.ShapeDtypeStruct((m, d), F32),
        grid=(m // tm,),
        in_specs=[
            pl.BlockSpec((tm, d), lambda i: (i, 0)),
            pl.BlockSpec((1, d), lambda i: (0, 0)),
            pl.BlockSpec(memory_space=pl.ANY),
            pl.BlockSpec(memory_space=pl.ANY),
        ],
        out_specs=pl.BlockSpec((tm, d), lambda i: (i, 0)),
        scratch_shapes=[
            pltpu.VMEM((tm, d), BF16),
            pltpu.VMEM((2, d, tf), BF16),
            pltpu.VMEM((2, d, tf), BF16),
            pltpu.VMEM((2, tf, d), BF16),
            pltpu.SemaphoreType.DMA((2, 3)),
        ],
        compiler_params=_params("arbitrary"),
        name="ffn",
    )(x, g.reshape(1, d), w_in, w_out)


def _ret_in_kernel(x_ref, g_ref, w_ref, cos_ref, sin_ref, *rest, rope_blocks, gate_from, heads_per_block,
                   cast_steps):
    n_cast = len(cast_steps)
    o_ref = rest[n_cast]
    xn_ref = rest[-1]
    j = pl.program_id(1)

    for src, dst, steps in zip(rest[:n_cast], rest[n_cast + 1:-1], cast_steps):
        @pl.when(j < steps)
        def _(src=src, dst=dst):
            dst[...] = src[...].astype(BF16)

    @pl.when(j == 0)
    def _():
        x = x_ref[...]
        xn_ref[...] = (x * _rms_scale(x) * g_ref[...]).astype(BF16)

    half = RET_DK // 2

    @pl.when(j < 2 * rope_blocks)
    def _():
        scale = jnp.where(j < rope_blocks, RET_DK ** -0.5, 1.0).astype(F32)
        cos = cos_ref[...] * scale
        sin = sin_ref[...] * scale
        xn = xn_ref[...]
        for h in range(heads_per_block):
            lo = h * RET_DK
            y = _dot(xn, w_ref[:, lo:lo + RET_DK])
            x1 = y[:, :half]
            x2 = y[:, half:]
            o_ref[:, lo:lo + half] = (x1 * cos - x2 * sin).astype(BF16)
            o_ref[:, lo + half:lo + RET_DK] = (x1 * sin + x2 * cos).astype(BF16)

    @pl.when(jnp.logical_and(j >= 2 * rope_blocks, j < gate_from))
    def _():
        o_ref[...] = _dot(xn_ref[...], w_ref[...]).astype(BF16)

    @pl.when(j >= gate_from)
    def _():
        o_ref[...] = _silu(_dot(xn_ref[...], w_ref[...])).astype(BF16)


def _ret_in(x, g, w, cos, sin, *, tm, tn, cast=()):
    m, d = x.shape
    n = w.shape[1]
    tab_blocks = cos.shape[0] // tm
    n_i, n_j = m // tm, n // tn
    in_specs = [
        pl.BlockSpec((tm, d), lambda i, j: (i, 0)),
        pl.BlockSpec((1, d), lambda i, j: (0, 0)),
        pl.BlockSpec((d, tn), lambda i, j: (0, j)),
        pl.BlockSpec((tm, RET_DK // 2), lambda i, j: (i % tab_blocks, 0)),
        pl.BlockSpec((tm, RET_DK // 2), lambda i, j: (i % tab_blocks, 0)),
    ]
    out_shape = [jax.ShapeDtypeStruct((m, n), BF16)]
    out_specs = [pl.BlockSpec((tm, tn), lambda i, j: (i, j))]
    cast_steps = []
    for wf, layer in cast:
        _, rows, cols = wf.shape
        slab = rows // n_i
        assert slab * n_i == rows and slab % 16 == 0
        steps = max(s for s in range(1, n_j + 1) if cols % (128 * s) == 0)
        cast_steps.append(steps)
        in_specs.append(pl.BlockSpec((None, slab, cols // steps),
                                     lambda i, j, layer=layer, steps=steps: (layer, i, jnp.minimum(j, steps - 1))))
        out_shape.append(jax.ShapeDtypeStruct((1, rows, cols), BF16))
        out_specs.append(pl.BlockSpec((None, slab, cols // steps),
                                      lambda i, j, steps=steps: (0, i, jnp.minimum(j, steps - 1))))
    rope_blocks = (RET_HEADS * RET_DK) // tn
    kern = functools.partial(_ret_in_kernel, rope_blocks=rope_blocks,
                             gate_from=n_j - (n_j - 2 * rope_blocks) // 2,
                             heads_per_block=tn // RET_DK, cast_steps=tuple(cast_steps))
    outs = pl.pallas_call(
        kern,
        out_shape=out_shape,
        grid=(n_i, n_j),
        in_specs=in_specs,
        out_specs=out_specs,
        scratch_shapes=[pltpu.VMEM((tm, d), BF16)],
        compiler_params=_params("parallel", "arbitrary"),
        name="ret_in",
    )(x, g.reshape(1, d), w, cos, sin, *[wf for wf, _ in cast])
    return outs[0], tuple(outs[1:])


def _ret_core_kernel(lg_ref, q_ref, k_ref, v_ref, g_ref, gain_ref, *rest, chunk, heads, has_state):
    if has_state:
        s0_ref, y_ref, s_ref = rest
    else:
        y_ref, s_ref = rest
    head0 = pl.program_id(1) * heads
    n_chunks = q_ref.shape[0] // chunk
    row = lax.broadcasted_iota(jnp.int32, (chunk, chunk), 0)
    col = lax.broadcasted_iota(jnp.int32, (chunk, chunk), 1)
    diff = (row - col).astype(F32)
    n = lax.broadcasted_iota(jnp.int32, (chunk, 1), 0).astype(F32)

    def head_consts(hh):
        lg = lg_ref[head0 + hh]
        decay = jnp.where(diff >= 0, jnp.exp(lg * jnp.maximum(diff, 0.0)), 0.0)
        cross = jnp.exp(lg * (n + 1.0))
        kdec = jnp.exp(lg * (chunk - 1.0 - n))
        carry = jnp.exp(lg * jnp.full((1, RET_DV), float(chunk), F32))
        return decay, cross, kdec, carry

    def mix(hh, c, consts):
        decay, cross, kdec, carry = consts
        rows = slice(c * chunk, (c + 1) * chunk)
        q = q_ref[rows, hh * RET_DK:(hh + 1) * RET_DK]
        k = k_ref[rows, hh * RET_DK:(hh + 1) * RET_DK]
        v = v_ref[rows, hh * RET_DV:(hh + 1) * RET_DV]
        inner = (_dot_nt(q, k) * decay).astype(BF16)
        if c > 0:
            s_old = s_ref[0, hh]
        elif has_state:
            s_old = s0_ref[0, hh]
        else:
            s_old = None
        kd = (k.astype(F32) * kdec).astype(BF16)
        o = _dot(inner, v)
        s_new = _dot_tn(kd, v)
        if s_old is not None:
            o = o + cross * _dot(q, s_old.astype(BF16))
            s_new = carry * s_old + s_new
        s_ref[0, hh] = s_new
        return o

    def norm_gate(hh, c, o):
        rows = slice(c * chunk, (c + 1) * chunk)
        cols = slice(hh * RET_DV, (hh + 1) * RET_DV)
        mu = jnp.mean(o, axis=-1, keepdims=True)
        dev = o - mu
        var = jnp.mean(dev * dev, axis=-1, keepdims=True)
        of = dev * lax.rsqrt(var + EPS) * gain_ref[:, cols]
        y_ref[rows, cols] = (g_ref[rows, cols].astype(F32) * of).astype(BF16)

    consts = None
    pending = None
    for hh in range(heads):
        for c in range(n_chunks):
            if c == 0:
                consts = head_consts(hh)
            o = mix(hh, c, consts)
            if pending is not None:
                norm_gate(*pending)
            pending = (hh, c, o)
    norm_gate(*pending)


def _ret_core(qkvg, log_gamma, gain, state, *, batch, seq, chunk, heads):
    m = qkvg.shape[0]
    hk = RET_HEADS * RET_DK
    hv = RET_HEADS * RET_DV
    wk = heads * RET_DK
    wv = heads * RET_DV
    kblk = hk // wk
    vblk = 2 * hk // wv
    gblk = vblk + hv // wv
    has_state = state is not None
    in_specs = [
        pl.BlockSpec(memory_space=pltpu.SMEM),
        pl.BlockSpec((seq, wk), lambda b, g: (b, g)),
        pl.BlockSpec((seq, wk), lambda b, g: (b, kblk + g)),
        pl.BlockSpec((seq, wv), lambda b, g: (b, vblk + g)),
        pl.BlockSpec((seq, wv), lambda b, g: (b, gblk + g)),
        pl.BlockSpec((1, wv), lambda b, g: (0, g)),
    ]
    args = [log_gamma, qkvg, qkvg, qkvg, qkvg, gain.reshape(1, hv)]
    state_spec = pl.BlockSpec((1, heads, RET_DK, RET_DV), lambda b, g: (b, g, 0, 0))
    if has_state:
        in_specs.append(state_spec)
        args.append(state)
    kern = functools.partial(_ret_core_kernel, chunk=chunk, heads=heads, has_state=has_state)
    return pl.pallas_call(
        kern,
        out_shape=(jax.ShapeDtypeStruct((m, hv), BF16),
                   jax.ShapeDtypeStruct((batch, RET_HEADS, RET_DK, RET_DV), F32)),
        grid=(batch, RET_HEADS // heads),
        in_specs=in_specs,
        out_specs=(pl.BlockSpec((seq, wv), lambda b, g: (b, g)), state_spec),
        compiler_params=_params("parallel", "parallel"),
        name="ret_core",
    )(*args)


def _mm_res_kernel(a_ref, w_ref, r_ref, *rest):
    n_cast = (len(rest) - 1) // 2
    o_ref = rest[n_cast]
    o_ref[...] = r_ref[...] + _dot(a_ref[...], w_ref[...])
    for src, dst in zip(rest[:n_cast], rest[n_cast + 1:]):
        dst[...] = src[...].astype(BF16)


def _mm_res(a, w, res, *, tm, cast=()):
    m, k = a.shape
    n = w.shape[1]
    steps = m // tm
    in_specs = [
        pl.BlockSpec((tm, k), lambda i: (i, 0)),
        pl.BlockSpec((k, n), lambda i: (0, 0), pipeline_mode=pl.Buffered(1)),
        pl.BlockSpec((tm, n), lambda i: (i, 0)),
    ]
    out_shape = [jax.ShapeDtypeStruct((m, n), F32)]
    out_specs = [pl.BlockSpec((tm, n), lambda i: (i, 0))]
    for wf, layer in cast:
        _, rows, cols = wf.shape
        slab = rows // steps
        assert slab * steps == rows and slab % 16 == 0
        in_specs.append(pl.BlockSpec((None, slab, cols), lambda i, layer=layer: (layer, i, 0)))
        out_shape.append(jax.ShapeDtypeStruct((1, rows, cols), BF16))
        out_specs.append(pl.BlockSpec((None, slab, cols), lambda i: (0, i, 0)))
    outs = pl.pallas_call(
        _mm_res_kernel,
        out_shape=out_shape,
        grid=(steps,),
        in_specs=in_specs,
        out_specs=out_specs,
        compiler_params=_params("parallel"),
        name="mm_res",
    )(a, w, res, *[wf for wf, _ in cast])
    return outs[0], tuple(outs[1:])


def _latent_kernel(x_ref, g_ref, w_ref, gl_ref, cos_ref, sin_ref, c_ref, kr_ref, *, row_chunks):
    half = QK_ROPE // 2
    tc = x_ref.shape[0] // row_chunks

    def norm(c):
        x = x_ref[c * tc:(c + 1) * tc, :]
        return (x * _rms_scale(x) * g_ref[...]).astype(BF16)

    def project(c, xn):
        rows = slice(c * tc, (c + 1) * tc)
        y = _dot(xn, w_ref[...])
        cl = y[:, :KV_LORA]
        c_ref[rows, :] = cl * _rms_scale(cl) * gl_ref[...]
        x1 = y[:, KV_LORA:KV_LORA + half]
        x2 = y[:, KV_LORA + half:KV_LORA + QK_ROPE]
        cos = cos_ref[rows, :]
        sin = sin_ref[rows, :]
        kr_ref[rows, :] = jnp.concatenate([x1 * cos - x2 * sin, x1 * sin + x2 * cos], axis=-1)

    xn = norm(0)
    for c in range(row_chunks):
        xn_next = norm(c + 1) if c + 1 < row_chunks else None
        project(c, xn)
        xn = xn_next


def _latent(x, g, w, g_lat, cos, sin, *, tm):
    m, d = x.shape
    n = w.shape[1]
    half = QK_ROPE // 2
    tab_blocks = cos.shape[0] // tm
    return pl.pallas_call(
        functools.partial(_latent_kernel, row_chunks=tm // 128),
        out_shape=(jax.ShapeDtypeStruct((m, KV_LORA), F32),
                   jax.ShapeDtypeStruct((m, QK_ROPE), F32)),
        grid=(m // tm,),
        in_specs=[
            pl.BlockSpec((tm, d), lambda i: (i, 0)),
            pl.BlockSpec((1, d), lambda i: (0, 0)),
            pl.BlockSpec((d, n), lambda i: (0, 0)),
            pl.BlockSpec((1, KV_LORA), lambda i: (0, 0)),
            pl.BlockSpec((tm, half), lambda i: (i % tab_blocks, 0)),
            pl.BlockSpec((tm, half), lambda i: (i % tab_blocks, 0)),
        ],
        out_specs=(pl.BlockSpec((tm, KV_LORA), lambda i: (i, 0)),
                   pl.BlockSpec((tm, QK_ROPE), lambda i: (i, 0))),
        compiler_params=_params("parallel"),
        name="latent",
    )(x, g.reshape(1, d), w, g_lat.reshape(1, KV_LORA), cos, sin)


def _kv_up_kernel(*refs, seg_rows):
    n_seg = len(seg_rows)
    wk_ref, wv_ref, gk_ref, k_ref, v_ref = refs[2 * n_seg:]
    g_nope = gk_ref[:, :QK_NOPE]
    g_rope = gk_ref[:, QK_NOPE:]
    lo = 0
    for s, rows in enumerate(seg_rows):
        c = refs[2 * s][...].astype(BF16)
        kr = refs[2 * s + 1][...]
        kn = _dot(c, wk_ref[...])
        v_ref[lo:lo + rows, :] = _dot(c, wv_ref[...]).astype(BF16)
        ss_rope = jnp.sum(kr * kr, axis=-1, keepdims=True)
        for h in range(MLA_HEADS):
            knh = kn[:, h * QK_NOPE:(h + 1) * QK_NOPE]
            ss = jnp.sum(knh * knh, axis=-1, keepdims=True) + ss_rope
            r = lax.rsqrt(ss * (1.0 / QK_HEAD) + EPS)
            k_ref[h, lo:lo + rows, :QK_NOPE] = (knh * r * g_nope).astype(BF16)
            k_ref[h, lo:lo + rows, QK_NOPE:] = (kr * r * g_rope).astype(BF16)
        lo += rows


def _kv_up(segments, wk, wv, gk, *, steps):
    seg_rows = tuple(c.shape[0] // steps for c, _ in segments)
    tm = sum(seg_rows)
    m = steps * tm
    in_specs, args = [], []
    for (c, kr), rows in zip(segments, seg_rows):
        in_specs += [pl.BlockSpec((rows, KV_LORA), lambda i: (i, 0)),
                     pl.BlockSpec((rows, QK_ROPE), lambda i: (i, 0))]
        args += [c, kr]
    in_specs += [pl.BlockSpec(wk.shape, lambda i: (0, 0)),
                 pl.BlockSpec(wv.shape, lambda i: (0, 0)),
                 pl.BlockSpec((1, QK_HEAD), lambda i: (0, 0))]
    return pl.pallas_call(
        functools.partial(_kv_up_kernel, seg_rows=seg_rows),
        out_shape=(jax.ShapeDtypeStruct((MLA_HEADS, m, QK_HEAD), BF16),
                   jax.ShapeDtypeStruct((m, MLA_HEADS * V_HEAD), BF16)),
        grid=(steps,),
        in_specs=in_specs,
        out_specs=(pl.BlockSpec((MLA_HEADS, tm, QK_HEAD), lambda i: (0, i, 0)),
                   pl.BlockSpec((tm, MLA_HEADS * V_HEAD), lambda i: (i, 0))),
        compiler_params=_params("parallel"),
        name="kv_up",
    )(*args, wk, wv, gk.reshape(1, QK_HEAD))


Q_SLOT = 2 * QK_NOPE


def _q_proj_kernel(x_ref, g_ref, wdq_ref, gl_ref, wuq_ref, cos_ref, sin_ref, gq_ref, q_ref, *, row_chunks):
    g_nope = gq_ref[:, :QK_NOPE]
    g_rope = gq_ref[:, QK_NOPE:]
    tc = x_ref.shape[0] // row_chunks

    def project(c):
        rows = slice(c * tc, (c + 1) * tc)
        x = x_ref[rows, :]
        hn = (x * _rms_scale(x) * g_ref[...]).astype(BF16)
        ql = _dot(hn, wdq_ref[...])
        qn = (ql * _rms_scale(ql) * gl_ref[...]).astype(BF16)
        return _dot(qn, wuq_ref[...])

    def per_head(c, q):
        rows = slice(c * tc, (c + 1) * tc)
        cos = cos_ref[rows, :]
        sin = sin_ref[rows, :]
        for h in range(MLA_HEADS):
            nh = q[:, h * Q_SLOT:h * Q_SLOT + QK_NOPE]
            t = q[:, h * Q_SLOT + QK_NOPE:(h + 1) * Q_SLOT]
            rp = t * cos + pltpu.roll(t, QK_ROPE, 1) * sin
            ss = jnp.sum(nh * nh + rp * rp, axis=-1, keepdims=True)
            r = lax.rsqrt(ss * (1.0 / QK_HEAD) + EPS) * (QK_HEAD ** -0.5 * LOG2E)
            q_ref[h, rows, :QK_NOPE] = (nh * r * g_nope).astype(BF16)
            q_ref[h, rows, QK_NOPE:] = (rp[:, :QK_ROPE] * r * g_rope).astype(BF16)

    q = project(0)
    for c in range(row_chunks):
        q_next = project(c + 1) if c + 1 < row_chunks else None
        per_head(c, q)
        q = q_next


def _q_proj(x, g, wdq, g_lat, wuq, cos, sin, gq, *, tm):
    m, d = x.shape
    tab_blocks = cos.shape[0] // tm
    tw = cos.shape[1]
    return pl.pallas_call(
        functools.partial(_q_proj_kernel, row_chunks=tm // 256),
        out_shape=jax.ShapeDtypeStruct((MLA_HEADS, m, QK_HEAD), BF16),
        grid=(m // tm,),
        in_specs=[
            pl.BlockSpec((tm, d), lambda i: (i, 0)),
            pl.BlockSpec((1, d), lambda i: (0, 0)),
            pl.BlockSpec(wdq.shape, lambda i: (0, 0)),
            pl.BlockSpec((1, Q_LORA), lambda i: (0, 0)),
            pl.BlockSpec(wuq.shape, lambda i: (0, 0)),
            pl.BlockSpec((tm, tw), lambda i: (i % tab_blocks, 0)),
            pl.BlockSpec((tm, tw), lambda i: (i % tab_blocks, 0)),
            pl.BlockSpec((1, QK_HEAD), lambda i: (0, 0)),
        ],
        out_specs=pl.BlockSpec((MLA_HEADS, tm, QK_HEAD), lambda i: (0, i, 0)),
        compiler_params=_params("parallel"),
        name="q_proj",
    )(x, g.reshape(1, d), wdq, g_lat.reshape(1, Q_LORA), wuq, cos, sin, gq.reshape(1, QK_HEAD))


def _attn_causal_kernel(q_ref, k_ref, v_ref, o_ref, *, heads, tq):
    n_q = q_ref.shape[1] // tq
    krow = lax.broadcasted_iota(jnp.int32, (tq, tq), 0) // CHUNK
    qcol = lax.broadcasted_iota(jnp.int32, (tq, tq), 1) // CHUNK
    keep = krow <= qcol

    def scores(h, qi):
        lo = qi * tq
        q = q_ref[h, lo:lo + tq, :]
        st_d = jnp.where(keep, _dot_nt(k_ref[h, lo:lo + tq, :], q), NEG_INF)
        st_p = _dot_nt(k_ref[h, :lo, :], q) if qi > 0 else None
        return st_d, st_p

    def softmax(st):
        st_d, st_p = st
        m = jnp.max(st_d, axis=0, keepdims=True)
        if st_p is not None:
            m = jnp.maximum(m, jnp.max(st_p, axis=0, keepdims=True))
        pt_d = jnp.exp2(st_d - m)
        l = jnp.sum(pt_d, axis=0, keepdims=True)
        pt_p = None
        if st_p is not None:
            pt_p = jnp.exp2(st_p - m)
            l = l + jnp.sum(pt_p, axis=0, keepdims=True)
            pt_p = pt_p.astype(BF16)
        return pt_d.astype(BF16), pt_p, l

    def values(h, qi, p):
        pt_d, pt_p, l = p
        lo = qi * tq
        cols = slice(h * V_HEAD, (h + 1) * V_HEAD)
        acc_t = _dot_tn(v_ref[lo:lo + tq, cols], pt_d)
        if pt_p is not None:
            acc_t = acc_t + _dot_tn(v_ref[:lo, cols], pt_p)
        o_ref[lo:lo + tq, cols] = (acc_t / l).T.astype(BF16)

    work = [(h, qi) for h in range(heads) for qi in range(n_q)]
    s_q, p_q = {}, {}
    for step in range(len(work) + 2):
        if step < len(work):
            s_q[step] = scores(*work[step])
        if 0 <= step - 1 < len(work):
            p_q[step - 1] = softmax(s_q.pop(step - 1))
        if 0 <= step - 2 < len(work):
            values(*work[step - 2], p_q.pop(step - 2))


def _attn_full_kernel(q_ref, k_ref, v_ref, o_ref, *, heads):
    def scores(h):
        return _dot_nt(q_ref[h], k_ref[h])

    def softmax(s):
        m = jnp.max(s, axis=-1, keepdims=True)
        p = jnp.exp2(s - m)
        return p.astype(BF16), jnp.sum(p, axis=-1, keepdims=True)

    def values(h, pl_):
        p, l = pl_
        cols = slice(h * V_HEAD, (h + 1) * V_HEAD)
        o_ref[:, cols] = (_dot(p, v_ref[:, cols]) / l).astype(BF16)

    s_q, p_q = {}, {}
    for step in range(heads + 2):
        if step < heads:
            s_q[step] = scores(step)
        if 0 <= step - 1 < heads:
            p_q[step - 1] = softmax(s_q.pop(step - 1))
        if 0 <= step - 2 < heads:
            values(step - 2, p_q.pop(step - 2))


def _attention(q, k, v, *, batch, t_q, t_kv, heads, tq, causal):
    hg = MLA_HEADS // heads
    if causal:
        kern = functools.partial(_attn_causal_kernel, heads=heads, tq=tq)
    else:
        kern = functools.partial(_attn_full_kernel, heads=heads)
    return pl.pallas_call(
        kern,
        out_shape=jax.ShapeDtypeStruct((batch * t_q, MLA_HEADS * V_HEAD), BF16),
        grid=(batch, hg),
        in_specs=[
            pl.BlockSpec((heads, t_q, QK_HEAD), lambda b, g: (g, b, 0)),
            pl.BlockSpec((heads, t_kv, QK_HEAD), lambda b, g: (g, b, 0)),
            pl.BlockSpec((t_kv, heads * V_HEAD), lambda b, g: (b, g)),
        ],
        out_specs=pl.BlockSpec((t_q, heads * V_HEAD), lambda b, g: (b, g)),
        compiler_params=_params("parallel", "parallel"),
        name="attention",
    )(q, k, v)


def _rope_tables(pos, half, reps, rows):
    inv = jnp.power(ROPE_BASE, -jnp.arange(half, dtype=F32) / half)
    ang = pos.astype(F32)[:, None] * inv[None, :]
    cos = jnp.tile(jnp.cos(ang), (max(rows // pos.shape[0], 1), reps))
    sin = jnp.tile(jnp.sin(ang), (max(rows // pos.shape[0], 1), reps))
    return cos, sin


def _mm_res_and_ffn2(a, w, h, layer, p):
    key = ("ffn2_bf16", layer)
    if key not in p:
        h, p[key] = _mm_res(a, w, h, tm=512, cast=((p["ffn2_w_in"], layer), (p["ffn2_w_out"], layer)))
    else:
        h, _ = _mm_res(a, w, h, tm=512)
    w_in, w_out = p[key]
    return _ffn(h, p["ffn2_norm"][layer], w_in, w_out, 0, tm=FFN_TM, tf=FFN_TF)


def _trunk(x, pos, ret_state, ckv_past, krope_past, p, *, causal):
    batch, seq, d = x.shape
    m = batch * seq
    tm = 1024
    h = x.reshape(m, d)

    h = _ffn(h, p["ffn1_norm"][0], p["ffn1_w_in_l0"], p["ffn1_w_out_l0"], 0, tm=FFN_TM, tf=FFN_TF)
    cos, sin = _rope_tables(pos, RET_DK // 2, 1, tm)
    key = "cast_by_ret_in"
    cast = () if key in p else ((p["ffn1_w_in"], 1), (p["ffn1_w_out"], 1), (p["ret_w_out"], 0), (p["w_o"], 0))
    qkvg, cast_out = _ret_in(h, p["mix_norm"][0], p["ret_w_in"][0], cos, sin, tm=tm, tn=2048, cast=cast)
    if cast:
        p[key] = cast_out
    ffn1_l1_w_in, ffn1_l1_w_out, ret_w_out, w_o = p[key]
    log_gamma = jnp.log1p(-jnp.exp2(-5.0 - jnp.arange(RET_HEADS, dtype=F32)))
    chunk = min(seq, 256)
    y, s_fin = _ret_core(qkvg, log_gamma, p["ret_gn_gain"][0],
                         None if ret_state is None else ret_state[0],
                         batch=batch, seq=seq, chunk=chunk,
                         heads=max(1, min(RET_HEADS, 512 // seq)))
    h = _mm_res_and_ffn2(y, ret_w_out[0], h, 0, p)

    cos, sin = _rope_tables(pos, QK_ROPE // 2, 1, tm)
    c_new, kr_new = _latent(h, p["kv_norm"], p["w_dkv"], p["kv_lat_norm"], cos, sin, tm=tm)
    if ckv_past is None:
        t_kv = seq
        segments, steps = [(c_new, kr_new)], m // tm
    else:
        past = ckv_past.shape[1]
        t_kv = past + seq
        segments = [(ckv_past.reshape(batch * past, KV_LORA), krope_past.reshape(batch * past, QK_ROPE)),
                    (c_new, kr_new)]
        steps = batch
    k_sh, v_sh = _kv_up(segments, p["w_uk"], p["w_uv"], p["k_norm"], steps=steps)

    h = _ffn(h, p["ffn1_norm"][1], ffn1_l1_w_in, ffn1_l1_w_out, 0, tm=FFN_TM, tf=FFN_TF)
    cos, sin = _rope_tables(pos, QK_ROPE // 2, 1, tm)
    zeros = jnp.zeros_like(cos)
    cos = jnp.concatenate([cos, cos, zeros, zeros], axis=-1)
    sin = jnp.concatenate([-sin, sin, zeros, zeros], axis=-1)
    q = _q_proj(h, p["mix_norm"][1], p["w_dq"][0], p["q_lat_norm"][0], p["w_uq"][0], cos, sin,
                p["q_norm"][0], tm=tm)
    if causal:
        o = _attention(q, k_sh, v_sh, batch=batch, t_q=seq, t_kv=t_kv, heads=2, tq=512, causal=True)
    else:
        o = _attention(q, k_sh, v_sh, batch=batch, t_q=seq, t_kv=t_kv, heads=MLA_HEADS, tq=seq, causal=False)
    h = _mm_res_and_ffn2(o, w_o[0], h, 1, p)

    return (h.reshape(batch, seq, d), s_fin[None],
            c_new.reshape(batch, seq, KV_LORA), kr_new.reshape(batch, seq, QK_ROPE))


def kernel(x_prompt, x_sample, state_ret, cache_ckv, cache_krope, ffn1_norm, ffn1_w_in, ffn1_w_out, mix_norm, ffn2_norm, ffn2_w_in, ffn2_w_out, ret_w_in, ret_gn_gain, ret_w_out, kv_norm, w_dkv, kv_lat_norm, w_ukv, k_norm, w_dq, q_lat_norm, w_uq, q_norm, w_o):
    bf = lambda w: w.astype(BF16)
    w_ukv_h = w_ukv.reshape(KV_LORA, MLA_HEADS, QK_NOPE + V_HEAD)
    w_uk = w_ukv_h[:, :, :QK_NOPE].reshape(KV_LORA, MLA_HEADS * QK_NOPE)
    w_uv = w_ukv_h[:, :, QK_NOPE:].reshape(KV_LORA, MLA_HEADS * V_HEAD)
    half = QK_ROPE // 2
    w_uq_h = w_uq.reshape(w_uq.shape[0], Q_LORA, MLA_HEADS, QK_HEAD)
    w_lo = w_uq_h[..., QK_NOPE:QK_NOPE + half]
    w_hi = w_uq_h[..., QK_NOPE + half:]
    w_uq_p = jnp.concatenate([w_uq_h[..., :QK_NOPE], w_lo, w_hi, w_hi, w_lo], axis=-1)
    w_uq_p = w_uq_p.reshape(w_uq.shape[0], Q_LORA, MLA_HEADS * Q_SLOT)
    p = dict(ffn1_norm=ffn1_norm, ffn1_w_in=ffn1_w_in, ffn1_w_out=ffn1_w_out,
             ffn1_w_in_l0=bf(ffn1_w_in[:1]), ffn1_w_out_l0=bf(ffn1_w_out[:1]),
             mix_norm=mix_norm, ffn2_norm=ffn2_norm, ffn2_w_in=ffn2_w_in, ffn2_w_out=ffn2_w_out,
             ret_w_in=bf(ret_w_in), ret_gn_gain=ret_gn_gain, ret_w_out=ret_w_out,
             kv_norm=kv_norm, w_dkv=bf(w_dkv), kv_lat_norm=kv_lat_norm,
             w_uk=bf(w_uk), w_uv=bf(w_uv), k_norm=k_norm,
             w_dq=bf(w_dq), q_lat_norm=q_lat_norm, w_uq=bf(w_uq_p), q_norm=q_norm, w_o=w_o)
    pos_prompt = jnp.arange(x_prompt.shape[1])
    past = cache_ckv.shape[1]
    pos_sample = past + jnp.arange(x_sample.shape[1])
    y_p, ret_p, ckv_p, kr_p = _trunk(x_prompt, pos_prompt, None, None, None, p, causal=True)
    y_s, ret_s, ckv_s, kr_s = _trunk(x_sample, pos_sample, state_ret, cache_ckv, cache_krope, p, causal=False)
    return (y_p, y_s, ret_p, ckv_p, kr_p, ret_s, ckv_s, kr_s)
```

```python
import functools

import jax
import jax.numpy as jnp
from jax import lax
from jax.experimental import pallas as pl
from jax.experimental.pallas import tpu as pltpu

F32 = jnp.float32
BF16 = jnp.bfloat16

D_MODEL = 2048
CHUNK = 64
D_FF = 5632
RET_HEADS = 8
RET_DK = D_MODEL // RET_HEADS
RET_DV = 2 * D_MODEL // RET_HEADS
MLA_HEADS = 16
QK_NOPE = 128
QK_ROPE = 64
QK_HEAD = QK_NOPE + QK_ROPE
V_HEAD = 128
KV_LORA = 512
Q_LORA = 512
ROPE_BASE = 10000.0
EPS = 1e-6
NEG_INF = -1e30
LOG2E = 1.4426950408889634

VMEM_LIMIT_BYTES = 60 * 1024 * 1024

FFN_TM = 1024
FFN_TF = 512
FFN_NORM_CHUNKS = 4


def _params(*semantics):
    return pltpu.CompilerParams(dimension_semantics=semantics,
                                vmem_limit_bytes=VMEM_LIMIT_BYTES)


def _rms_scale(x):
    return lax.rsqrt(jnp.mean(x * x, axis=-1, keepdims=True) + EPS)


def _dot(a, b):
    return jnp.dot(a, b, preferred_element_type=F32)


def _dot_nt(a, b):
    return lax.dot_general(a, b, (((1,), (1,)), ((), ())), preferred_element_type=F32)


def _dot_tn(a, b):
    return lax.dot_general(a, b, (((0,), (0,)), ((), ())), preferred_element_type=F32)


def _silu(x):
    return x * jax.nn.sigmoid(x)


def _ffn_kernel(x_ref, g_ref, win_hbm, wout_hbm, o_ref, xn_ref, wa_buf, wb_buf, wo_buf, sem,
                *, layer, tf):
    i = pl.program_id(0)
    nf = D_FF // tf
    last = nf - 1
    forward = i % 2 == 0

    def block_of(pos):
        return jnp.where(forward, pos, last - pos)

    def copies(pos, slot):
        col = pl.multiple_of(block_of(pos) * tf, tf)
        return (
            pltpu.make_async_copy(win_hbm.at[layer, :, pl.ds(col, tf)], wa_buf.at[slot], sem.at[slot, 0]),
            pltpu.make_async_copy(win_hbm.at[layer, :, pl.ds(D_FF + col, tf)], wb_buf.at[slot], sem.at[slot, 1]),
            pltpu.make_async_copy(wout_hbm.at[layer, pl.ds(col, tf), :], wo_buf.at[slot], sem.at[slot, 2]),
        )

    def start(pos, slot):
        for c in copies(pos, slot):
            c.start()

    def wait(pos, slot):
        for c in copies(pos, slot):
            c.wait()

    def update(xn, slot):
        a = _dot(xn, wa_buf[slot])
        b = _dot(xn, wb_buf[slot])
        h = (_silu(a) * (0.5 * b)).astype(BF16)
        return _dot(h, wo_buf[slot])

    @pl.when(i == 0)
    def _():
        start(0, 0)
        start(1, 1)
        wait(0, 0)

    tc = x_ref.shape[0] // FFN_NORM_CHUNKS

    def norm(c):
        x = x_ref[c * tc:(c + 1) * tc, :]
        xn = (x * _rms_scale(x) * g_ref[...]).astype(BF16)
        xn_ref[c * tc:(c + 1) * tc, :] = xn
        return x, xn

    cur = norm(0)
    for c in range(FFN_NORM_CHUNKS):
        nxt = norm(c + 1) if c + 1 < FFN_NORM_CHUNKS else None
        x, xn = cur
        o_ref[c * tc:(c + 1) * tc, :] = x + update(xn, 0)
        cur = nxt
    start(2, 0)

    def pair(q, carry):
        p1 = 2 * q + 1

        @pl.when(jnp.logical_or(q > 0, i == 0))
        def _():
            wait(p1, 1)

        o_ref[...] += update(xn_ref[...], 1)

        @pl.when(p1 + 2 <= last)
        def _():
            start(p1 + 2, 1)

        wait(p1 + 1, 0)
        o_ref[...] += update(xn_ref[...], 0)

        @pl.when(p1 + 3 <= last)
        def _():
            start(p1 + 3, 0)

        return carry

    lax.fori_loop(0, (nf - 1) // 2, pair, 0)


def _ffn(x, g, w_in, w_out, layer, *, tm, tf):
    m, d = x.shape
    nf = D_FF // tf
    assert nf % 2 == 1 and nf >= 3, "the block walk assumes an odd number of hidden blocks"
    return pl.pallas_call(
        functools.partial(_ffn_kernel, layer=layer, tf=tf),
        out_shape=jax.ShapeDtypeStruct((m, d), F32),
        grid=(m // tm,),
        in_specs=[
            pl.BlockSpec((tm, d), lambda i: (i, 0)),
            pl.BlockSpec((1, d), lambda i: (0, 0)),
            pl.BlockSpec(memory_space=pl.ANY),
            pl.BlockSpec(memory_space=pl.ANY),
        ],
        out_specs=pl.BlockSpec((tm, d), lambda i: (i, 0)),
        scratch_shapes=[
            pltpu.VMEM((tm, d), BF16),
            pltpu.VMEM((2, d, tf), BF16),
            pltpu.VMEM((2, d, tf), BF16),
            pltpu.VMEM((2, tf, d), BF16),
            pltpu.SemaphoreType.DMA((2, 3)),
        ],
        compiler_params=_params("arbitrary"),
        name="ffn",
    )(x, g.reshape(1, d), w_in, w_out)


def _ret_in_kernel(x_ref, g_ref, w_ref, cos_ref, sin_ref, *rest, rope_blocks, heads_per_block, cast_steps):
    n_cast = len(cast_steps)
    o_ref = rest[n_cast]
    xn_ref = rest[-1]
    j = pl.program_id(1)

    for src, dst, steps in zip(rest[:n_cast], rest[n_cast + 1:-1], cast_steps):
        @pl.when(j < steps)
        def _(src=src, dst=dst):
            dst[...] = src[...].astype(BF16)

    @pl.when(j == 0)
    def _():
        x = x_ref[...]
        xn_ref[...] = (x * _rms_scale(x) * g_ref[...]).astype(BF16)

    half = RET_DK // 2

    @pl.when(j < 2 * rope_blocks)
    def _():
        scale = jnp.where(j < rope_blocks, RET_DK ** -0.5, 1.0).astype(F32)
        cos = cos_ref[...] * scale
        sin = sin_ref[...] * scale
        xn = xn_ref[...]
        for h in range(heads_per_block):
            lo = h * RET_DK
            y = _dot(xn, w_ref[:, lo:lo + RET_DK])
            x1 = y[:, :half]
            x2 = y[:, half:]
            o_ref[:, lo:lo + half] = (x1 * cos - x2 * sin).astype(BF16)
            o_ref[:, lo + half:lo + RET_DK] = (x1 * sin + x2 * cos).astype(BF16)

    @pl.when(j >= 2 * rope_blocks)
    def _():
        o_ref[...] = _dot(xn_ref[...], w_ref[...]).astype(BF16)


def _ret_in(x, g, w, cos, sin, *, tm, tn, cast=()):
    m, d = x.shape
    n = w.shape[1]
    tab_blocks = cos.shape[0] // tm
    n_i, n_j = m // tm, n // tn
    in_specs = [
        pl.BlockSpec((tm, d), lambda i, j: (i, 0)),
        pl.BlockSpec((1, d), lambda i, j: (0, 0)),
        pl.BlockSpec((d, tn), lambda i, j: (0, j)),
        pl.BlockSpec((tm, RET_DK // 2), lambda i, j: (i % tab_blocks, 0)),
        pl.BlockSpec((tm, RET_DK // 2), lambda i, j: (i % tab_blocks, 0)),
    ]
    out_shape = [jax.ShapeDtypeStruct((m, n), BF16)]
    out_specs = [pl.BlockSpec((tm, tn), lambda i, j: (i, j))]
    cast_steps = []
    for wf, layer in cast:
        _, rows, cols = wf.shape
        slab = rows // n_i
        assert slab * n_i == rows and slab % 16 == 0
        steps = max(s for s in range(1, n_j + 1) if cols % (128 * s) == 0)
        cast_steps.append(steps)
        in_specs.append(pl.BlockSpec((None, slab, cols // steps),
                                     lambda i, j, layer=layer, steps=steps: (layer, i, jnp.minimum(j, steps - 1))))
        out_shape.append(jax.ShapeDtypeStruct((1, rows, cols), BF16))
        out_specs.append(pl.BlockSpec((None, slab, cols // steps),
                                      lambda i, j, steps=steps: (0, i, jnp.minimum(j, steps - 1))))
    kern = functools.partial(_ret_in_kernel, rope_blocks=(RET_HEADS * RET_DK) // tn,
                             heads_per_block=tn // RET_DK, cast_steps=tuple(cast_steps))
    outs = pl.pallas_call(
        kern,
        out_shape=out_shape,
        grid=(n_i, n_j),
        in_specs=in_specs,
        out_specs=out_specs,
        scratch_shapes=[pltpu.VMEM((tm, d), BF16)],
        compiler_params=_params("parallel", "arbitrary"),
        name="ret_in",
    )(x, g.reshape(1, d), w, cos, sin, *[wf for wf, _ in cast])
    return outs[0], tuple(outs[1:])


def _ret_core_kernel(lg_ref, q_ref, k_ref, v_ref, g_ref, gain_ref, *rest, chunk, heads, has_state):
    if has_state:
        s0_ref, y_ref, s_ref = rest
    else:
        y_ref, s_ref = rest
    head0 = pl.program_id(1) * heads
    n_chunks = q_ref.shape[0] // chunk
    row = lax.broadcasted_iota(jnp.int32, (chunk, chunk), 0)
    col = lax.broadcasted_iota(jnp.int32, (chunk, chunk), 1)
    diff = (row - col).astype(F32)
    n = lax.broadcasted_iota(jnp.int32, (chunk, 1), 0).astype(F32)

    def head_consts(hh):
        lg = lg_ref[head0 + hh]
        decay = jnp.where(diff >= 0, jnp.exp(lg * jnp.maximum(diff, 0.0)), 0.0)
        cross = jnp.exp(lg * (n + 1.0))
        kdec = jnp.exp(lg * (chunk - 1.0 - n))
        carry = jnp.exp(lg * jnp.full((1, RET_DV), float(chunk), F32))
        return decay, cross, kdec, carry

    def mix(hh, c, consts):
        decay, cross, kdec, carry = consts
        rows = slice(c * chunk, (c + 1) * chunk)
        q = q_ref[rows, hh * RET_DK:(hh + 1) * RET_DK]
        k = k_ref[rows, hh * RET_DK:(hh + 1) * RET_DK]
        v = v_ref[rows, hh * RET_DV:(hh + 1) * RET_DV]
        inner = (_dot_nt(q, k) * decay).astype(BF16)
        if c > 0:
            s_old = s_ref[0, hh]
        elif has_state:
            s_old = s0_ref[0, hh]
        else:
            s_old = None
        kd = (k.astype(F32) * kdec).astype(BF16)
        o = _dot(inner, v)
        s_new = _dot_tn(kd, v)
        if s_old is not None:
            o = o + cross * _dot(q, s_old.astype(BF16))
            s_new = carry * s_old + s_new
        s_ref[0, hh] = s_new
        return o

    def norm_gate(hh, c, o):
        rows = slice(c * chunk, (c + 1) * chunk)
        cols = slice(hh * RET_DV, (hh + 1) * RET_DV)
        mu = jnp.mean(o, axis=-1, keepdims=True)
        dev = o - mu
        var = jnp.mean(dev * dev, axis=-1, keepdims=True)
        of = dev * lax.rsqrt(var + EPS) * gain_ref[:, cols]
        y_ref[rows, cols] = (_silu(g_ref[rows, cols].astype(F32)) * of).astype(BF16)

    consts = None
    pending = None
    for hh in range(heads):
        for c in range(n_chunks):
            if c == 0:
                consts = head_consts(hh)
            o = mix(hh, c, consts)
            if pending is not None:
                norm_gate(*pending)
            pending = (hh, c, o)
    norm_gate(*pending)


def _ret_core(qkvg, log_gamma, gain, state, *, batch, seq, chunk, heads):
    m = qkvg.shape[0]
    hk = RET_HEADS * RET_DK
    hv = RET_HEADS * RET_DV
    wk = heads * RET_DK
    wv = heads * RET_DV
    kblk = hk // wk
    vblk = 2 * hk // wv
    gblk = vblk + hv // wv
    has_state = state is not None
    in_specs = [
        pl.BlockSpec(memory_space=pltpu.SMEM),
        pl.BlockSpec((seq, wk), lambda b, g: (b, g)),
        pl.BlockSpec((seq, wk), lambda b, g: (b, kblk + g)),
        pl.BlockSpec((seq, wv), lambda b, g: (b, vblk + g)),
        pl.BlockSpec((seq, wv), lambda b, g: (b, gblk + g)),
        pl.BlockSpec((1, wv), lambda b, g: (0, g)),
    ]
    args = [log_gamma, qkvg, qkvg, qkvg, qkvg, gain.reshape(1, hv)]
    state_spec = pl.BlockSpec((1, heads, RET_DK, RET_DV), lambda b, g: (b, g, 0, 0))
    if has_state:
        in_specs.append(state_spec)
        args.append(state)
    kern = functools.partial(_ret_core_kernel, chunk=chunk, heads=heads, has_state=has_state)
    return pl.pallas_call(
        kern,
        out_shape=(jax.ShapeDtypeStruct((m, hv), BF16),
                   jax.ShapeDtypeStruct((batch, RET_HEADS, RET_DK, RET_DV), F32)),
        grid=(batch, RET_HEADS // heads),
        in_specs=in_specs,
        out_specs=(pl.BlockSpec((seq, wv), lambda b, g: (b, g)), state_spec),
        compiler_params=_params("parallel", "parallel"),
        name="ret_core",
    )(*args)


def _mm_res_kernel(a_ref, w_ref, r_ref, *rest):
    n_cast = (len(rest) - 1) // 2
    o_ref = rest[n_cast]
    o_ref[...] = r_ref[...] + _dot(a_ref[...], w_ref[...])
    for src, dst in zip(rest[:n_cast], rest[n_cast + 1:]):
        dst[...] = src[...].astype(BF16)


def _mm_res(a, w, res, *, tm, cast=()):
    m, k = a.shape
    n = w.shape[1]
    steps = m // tm
    in_specs = [
        pl.BlockSpec((tm, k), lambda i: (i, 0)),
        pl.BlockSpec((k, n), lambda i: (0, 0), pipeline_mode=pl.Buffered(1)),
        pl.BlockSpec((tm, n), lambda i: (i, 0)),
    ]
    out_shape = [jax.ShapeDtypeStruct((m, n), F32)]
    out_specs = [pl.BlockSpec((tm, n), lambda i: (i, 0))]
    for wf, layer in cast:
        _, rows, cols = wf.shape
        slab = rows // steps
        assert slab * steps == rows and slab % 16 == 0
        in_specs.append(pl.BlockSpec((None, slab, cols), lambda i, layer=layer: (layer, i, 0)))
        out_shape.append(jax.ShapeDtypeStruct((1, rows, cols), BF16))
        out_specs.append(pl.BlockSpec((None, slab, cols), lambda i: (0, i, 0)))
    outs = pl.pallas_call(
        _mm_res_kernel,
        out_shape=out_shape,
        grid=(steps,),
        in_specs=in_specs,
        out_specs=out_specs,
        compiler_params=_params("parallel"),
        name="mm_res",
    )(a, w, res, *[wf for wf, _ in cast])
    return outs[0], tuple(outs[1:])


def _latent_kernel(x_ref, g_ref, w_ref, gl_ref, cos_ref, sin_ref, c_ref, kr_ref, *, row_chunks):
    half = QK_ROPE // 2
    tc = x_ref.shape[0] // row_chunks

    def norm(c):
        x = x_ref[c * tc:(c + 1) * tc, :]
        return (x * _rms_scale(x) * g_ref[...]).astype(BF16)

    def project(c, xn):
        rows = slice(c * tc, (c + 1) * tc)
        y = _dot(xn, w_ref[...])
        cl = y[:, :KV_LORA]
        c_ref[rows, :] = cl * _rms_scale(cl) * gl_ref[...]
        x1 = y[:, KV_LORA:KV_LORA + half]
        x2 = y[:, KV_LORA + half:KV_LORA + QK_ROPE]
        cos = cos_ref[rows, :]
        sin = sin_ref[rows, :]
        kr_ref[rows, :] = jnp.concatenate([x1 * cos - x2 * sin, x1 * sin + x2 * cos], axis=-1)

    xn = norm(0)
    for c in range(row_chunks):
        xn_next = norm(c + 1) if c + 1 < row_chunks else None
        project(c, xn)
        xn = xn_next


def _latent(x, g, w, g_lat, cos, sin, *, tm):
    m, d = x.shape
    n = w.shape[1]
    half = QK_ROPE // 2
    tab_blocks = cos.shape[0] // tm
    return pl.pallas_call(
        functools.partial(_latent_kernel, row_chunks=tm // 128),
        out_shape=(jax.ShapeDtypeStruct((m, KV_LORA), F32),
                   jax.ShapeDtypeStruct((m, QK_ROPE), F32)),
        grid=(m // tm,),
        in_specs=[
            pl.BlockSpec((tm, d), lambda i: (i, 0)),
            pl.BlockSpec((1, d), lambda i: (0, 0)),
            pl.BlockSpec((d, n), lambda i: (0, 0)),
            pl.BlockSpec((1, KV_LORA), lambda i: (0, 0)),
            pl.BlockSpec((tm, half), lambda i: (i % tab_blocks, 0)),
            pl.BlockSpec((tm, half), lambda i: (i % tab_blocks, 0)),
        ],
        out_specs=(pl.BlockSpec((tm, KV_LORA), lambda i: (i, 0)),
                   pl.BlockSpec((tm, QK_ROPE), lambda i: (i, 0))),
        compiler_params=_params("parallel"),
        name="latent",
    )(x, g.reshape(1, d), w, g_lat.reshape(1, KV_LORA), cos, sin)


def _kv_up_body(seg_refs, seg_rows, wk_ref, wv_ref, gk_ref, k_ref, v_ref):
    refs = seg_refs
    g_nope = gk_ref[:, :QK_NOPE]
    g_rope = gk_ref[:, QK_NOPE:]
    lo = 0
    for s, rows in enumerate(seg_rows):
        c = refs[2 * s][...].astype(BF16)
        kr = refs[2 * s + 1][...]
        kn = _dot(c, wk_ref[...])
        v_ref[lo:lo + rows, :] = _dot(c, wv_ref[...]).astype(BF16)
        ss_rope = jnp.sum(kr * kr, axis=-1, keepdims=True)
        for h in range(MLA_HEADS):
            knh = kn[:, h * QK_NOPE:(h + 1) * QK_NOPE]
            ss = jnp.sum(knh * knh, axis=-1, keepdims=True) + ss_rope
            r = lax.rsqrt(ss * (1.0 / QK_HEAD) + EPS)
            k_ref[h, lo:lo + rows, :QK_NOPE] = (knh * r * g_nope).astype(BF16)
            k_ref[h, lo:lo + rows, QK_NOPE:] = (kr * r * g_rope).astype(BF16)
        lo += rows


def _kv_up_kernel(*refs, seg_rows):
    n_seg = len(seg_rows)
    wk_ref, wv_ref, gk_ref, k_ref, v_ref = refs[2 * n_seg:]
    _kv_up_body(refs[:2 * n_seg], seg_rows, wk_ref, wv_ref, gk_ref, k_ref, v_ref)


def _kv_up(segments, wk, wv, gk, *, steps):
    seg_rows = tuple(c.shape[0] // steps for c, _ in segments)
    tm = sum(seg_rows)
    m = steps * tm
    in_specs, args = [], []
    for (c, kr), rows in zip(segments, seg_rows):
        in_specs += [pl.BlockSpec((rows, KV_LORA), lambda i: (i, 0)),
                     pl.BlockSpec((rows, QK_ROPE), lambda i: (i, 0))]
        args += [c, kr]
    in_specs += [pl.BlockSpec(wk.shape, lambda i: (0, 0)),
                 pl.BlockSpec(wv.shape, lambda i: (0, 0)),
                 pl.BlockSpec((1, QK_HEAD), lambda i: (0, 0))]
    return pl.pallas_call(
        functools.partial(_kv_up_kernel, seg_rows=seg_rows),
        out_shape=(jax.ShapeDtypeStruct((MLA_HEADS, m, QK_HEAD), BF16),
                   jax.ShapeDtypeStruct((m, MLA_HEADS * V_HEAD), BF16)),
        grid=(steps,),
        in_specs=in_specs,
        out_specs=(pl.BlockSpec((MLA_HEADS, tm, QK_HEAD), lambda i: (0, i, 0)),
                   pl.BlockSpec((tm, MLA_HEADS * V_HEAD), lambda i: (i, 0))),
        compiler_params=_params("parallel"),
        name="kv_up",
    )(*args, wk, wv, gk.reshape(1, QK_HEAD))


Q_SLOT = 2 * QK_NOPE


def _q_proj_kernel(x_ref, g_ref, wdq_ref, gl_ref, wuq_ref, cos_ref, sin_ref, gq_ref, q_ref, *, row_chunks):
    g_nope = gq_ref[:, :QK_NOPE]
    g_rope = gq_ref[:, QK_NOPE:]
    tc = x_ref.shape[0] // row_chunks

    def project(c):
        rows = slice(c * tc, (c + 1) * tc)
        x = x_ref[rows, :]
        hn = (x * _rms_scale(x) * g_ref[...]).astype(BF16)
        ql = _dot(hn, wdq_ref[...])
        qn = (ql * _rms_scale(ql) * gl_ref[...]).astype(BF16)
        return _dot(qn, wuq_ref[...])

    def per_head(c, q):
        rows = slice(c * tc, (c + 1) * tc)
        cos = cos_ref[rows, :]
        sin = sin_ref[rows, :]
        for h in range(MLA_HEADS):
            nh = q[:, h * Q_SLOT:h * Q_SLOT + QK_NOPE]
            t = q[:, h * Q_SLOT + QK_NOPE:(h + 1) * Q_SLOT]
            rp = t * cos + pltpu.roll(t, QK_ROPE, 1) * sin
            ss = jnp.sum(nh * nh + rp * rp, axis=-1, keepdims=True)
            r = lax.rsqrt(ss * (1.0 / QK_HEAD) + EPS) * (QK_HEAD ** -0.5 * LOG2E)
            q_ref[h, rows, :QK_NOPE] = (nh * r * g_nope).astype(BF16)
            q_ref[h, rows, QK_NOPE:] = (rp[:, :QK_ROPE] * r * g_rope).astype(BF16)

    q = project(0)
    for c in range(row_chunks):
        q_next = project(c + 1) if c + 1 < row_chunks else None
        per_head(c, q)
        q = q_next


def _q_proj(x, g, wdq, g_lat, wuq, cos, sin, gq, *, tm):
    m, d = x.shape
    tab_blocks = cos.shape[0] // tm
    tw = cos.shape[1]
    return pl.pallas_call(
        functools.partial(_q_proj_kernel, row_chunks=tm // 256),
        out_shape=jax.ShapeDtypeStruct((MLA_HEADS, m, QK_HEAD), BF16),
        grid=(m // tm,),
        in_specs=[
            pl.BlockSpec((tm, d), lambda i: (i, 0)),
            pl.BlockSpec((1, d), lambda i: (0, 0)),
            pl.BlockSpec(wdq.shape, lambda i: (0, 0)),
            pl.BlockSpec((1, Q_LORA), lambda i: (0, 0)),
            pl.BlockSpec(wuq.shape, lambda i: (0, 0)),
            pl.BlockSpec((tm, tw), lambda i: (i % tab_blocks, 0)),
            pl.BlockSpec((tm, tw), lambda i: (i % tab_blocks, 0)),
            pl.BlockSpec((1, QK_HEAD), lambda i: (0, 0)),
        ],
        out_specs=pl.BlockSpec((MLA_HEADS, tm, QK_HEAD), lambda i: (0, i, 0)),
        compiler_params=_params("parallel"),
        name="q_proj",
    )(x, g.reshape(1, d), wdq, g_lat.reshape(1, Q_LORA), wuq, cos, sin, gq.reshape(1, QK_HEAD))


def _attn_causal_kernel(q_ref, k_ref, v_ref, o_ref, *, heads, tq):
    n_q = q_ref.shape[1] // tq
    krow = lax.broadcasted_iota(jnp.int32, (tq, tq), 0) // CHUNK
    qcol = lax.broadcasted_iota(jnp.int32, (tq, tq), 1) // CHUNK
    keep = krow <= qcol

    def scores(h, qi):
        lo = qi * tq
        q = q_ref[h, lo:lo + tq, :]
        st_d = jnp.where(keep, _dot_nt(k_ref[h, lo:lo + tq, :], q), NEG_INF)
        st_p = _dot_nt(k_ref[h, :lo, :], q) if qi > 0 else None
        return st_d, st_p

    def softmax(st):
        st_d, st_p = st
        m = jnp.max(st_d, axis=0, keepdims=True)
        if st_p is not None:
            m = jnp.maximum(m, jnp.max(st_p, axis=0, keepdims=True))
        pt_d = jnp.exp2(st_d - m)
        l = jnp.sum(pt_d, axis=0, keepdims=True)
        pt_p = None
        if st_p is not None:
            pt_p = jnp.exp2(st_p - m)
            l = l + jnp.sum(pt_p, axis=0, keepdims=True)
            pt_p = pt_p.astype(BF16)
        return pt_d.astype(BF16), pt_p, l

    def values(h, qi, p):
        pt_d, pt_p, l = p
        lo = qi * tq
        cols = slice(h * V_HEAD, (h + 1) * V_HEAD)
        acc_t = _dot_tn(v_ref[lo:lo + tq, cols], pt_d)
        if pt_p is not None:
            acc_t = acc_t + _dot_tn(v_ref[:lo, cols], pt_p)
        o_ref[lo:lo + tq, cols] = (acc_t / l).T.astype(BF16)

    work = [(h, qi) for h in range(heads) for qi in range(n_q)]
    s_q, p_q = {}, {}
    for step in range(len(work) + 2):
        if step < len(work):
            s_q[step] = scores(*work[step])
        if 0 <= step - 1 < len(work):
            p_q[step - 1] = softmax(s_q.pop(step - 1))
        if 0 <= step - 2 < len(work):
            values(*work[step - 2], p_q.pop(step - 2))


def _attn_full_body(q_ref, k_ref, v_ref, o_ref, heads):
    def scores(h):
        return _dot_nt(q_ref[h], k_ref[h])

    def softmax(s):
        m = jnp.max(s, axis=-1, keepdims=True)
        p = jnp.exp2(s - m)
        return p.astype(BF16), jnp.sum(p, axis=-1, keepdims=True)

    def values(h, pl_):
        p, l = pl_
        cols = slice(h * V_HEAD, (h + 1) * V_HEAD)
        o_ref[:, cols] = (_dot(p, v_ref[:, cols]) / l).astype(BF16)

    s_q, p_q = {}, {}
    for step in range(heads + 2):
        if step < heads:
            s_q[step] = scores(step)
        if 0 <= step - 1 < heads:
            p_q[step - 1] = softmax(s_q.pop(step - 1))
        if 0 <= step - 2 < heads:
            values(step - 2, p_q.pop(step - 2))


def _kv_attn_kernel(*refs, seg_rows):
    n_seg = len(seg_rows)
    wk_ref, wv_ref, gk_ref, q_ref, o_ref, k_scr, v_scr = refs[2 * n_seg:]
    _kv_up_body(refs[:2 * n_seg], seg_rows, wk_ref, wv_ref, gk_ref, k_scr, v_scr)
    _attn_full_body(q_ref, k_scr, v_scr, o_ref, MLA_HEADS)


def _kv_attention(segments, wk, wv, gk, q, *, batch):
    seg_rows = tuple(c.shape[0] // batch for c, _ in segments)
    t_kv = sum(seg_rows)
    t_q = q.shape[1] // batch
    in_specs, args = [], []
    for (c, kr), rows in zip(segments, seg_rows):
        in_specs += [pl.BlockSpec((rows, KV_LORA), lambda b: (b, 0)),
                     pl.BlockSpec((rows, QK_ROPE), lambda b: (b, 0))]
        args += [c, kr]
    in_specs += [pl.BlockSpec(wk.shape, lambda b: (0, 0)),
                 pl.BlockSpec(wv.shape, lambda b: (0, 0)),
                 pl.BlockSpec((1, QK_HEAD), lambda b: (0, 0)),
                 pl.BlockSpec((MLA_HEADS, t_q, QK_HEAD), lambda b: (0, b, 0))]
    return pl.pallas_call(
        functools.partial(_kv_attn_kernel, seg_rows=seg_rows),
        out_shape=jax.ShapeDtypeStruct((batch * t_q, MLA_HEADS * V_HEAD), BF16),
        grid=(batch,),
        in_specs=in_specs,
        out_specs=pl.BlockSpec((t_q, MLA_HEADS * V_HEAD), lambda b: (b, 0)),
        scratch_shapes=[pltpu.VMEM((MLA_HEADS, t_kv, QK_HEAD), BF16),
                        pltpu.VMEM((t_kv, MLA_HEADS * V_HEAD), BF16)],
        compiler_params=_params("parallel"),
        name="kv_attention",
    )(*args, wk, wv, gk.reshape(1, QK_HEAD), q)


def _attention(q, k, v, *, batch, t_q, t_kv, heads, tq):
    hg = MLA_HEADS // heads
    kern = functools.partial(_attn_causal_kernel, heads=heads, tq=tq)
    return pl.pallas_call(
        kern,
        out_shape=jax.ShapeDtypeStruct((batch * t_q, MLA_HEADS * V_HEAD), BF16),
        grid=(batch, hg),
        in_specs=[
            pl.BlockSpec((heads, t_q, QK_HEAD), lambda b, g: (g, b, 0)),
            pl.BlockSpec((heads, t_kv, QK_HEAD), lambda b, g: (g, b, 0)),
            pl.BlockSpec((t_kv, heads * V_HEAD), lambda b, g: (b, g)),
        ],
        out_specs=pl.BlockSpec((t_q, heads * V_HEAD), lambda b, g: (b, g)),
        compiler_params=_params("parallel", "parallel"),
        name="attention",
    )(q, k, v)


def _rope_tables(pos, half, reps, rows):
    inv = jnp.power(ROPE_BASE, -jnp.arange(half, dtype=F32) / half)
    ang = pos.astype(F32)[:, None] * inv[None, :]
    cos = jnp.tile(jnp.cos(ang), (max(rows // pos.shape[0], 1), reps))
    sin = jnp.tile(jnp.sin(ang), (max(rows // pos.shape[0], 1), reps))
    return cos, sin


def _mm_res_and_ffn2(a, w, h, layer, p):
    key = ("ffn2_bf16", layer)
    if key not in p:
        h, p[key] = _mm_res(a, w, h, tm=512, cast=((p["ffn2_w_in"], layer), (p["ffn2_w_out"], layer)))
    else:
        h, _ = _mm_res(a, w, h, tm=512)
    w_in, w_out = p[key]
    return _ffn(h, p["ffn2_norm"][layer], w_in, w_out, 0, tm=FFN_TM, tf=FFN_TF)


def _trunk(x, pos, ret_state, ckv_past, krope_past, p):
    batch, seq, d = x.shape
    m = batch * seq
    tm = 1024
    h = x.reshape(m, d)

    h = _ffn(h, p["ffn1_norm"][0], p["ffn1_w_in_l0"], p["ffn1_w_out_l0"], 0, tm=FFN_TM, tf=FFN_TF)
    cos, sin = _rope_tables(pos, RET_DK // 2, 1, tm)
    key = "cast_by_ret_in"
    cast = () if key in p else ((p["ffn1_w_in"], 1), (p["ffn1_w_out"], 1), (p["ret_w_out"], 0), (p["w_o"], 0))
    qkvg, cast_out = _ret_in(h, p["mix_norm"][0], p["ret_w_in"][0], cos, sin, tm=tm, tn=2048, cast=cast)
    if cast:
        p[key] = cast_out
    ffn1_l1_w_in, ffn1_l1_w_out, ret_w_out, w_o = p[key]
    log_gamma = jnp.log1p(-jnp.exp2(-5.0 - jnp.arange(RET_HEADS, dtype=F32)))
    chunk = min(seq, 256)
    y, s_fin = _ret_core(qkvg, log_gamma, p["ret_gn_gain"][0],
                         None if ret_state is None else ret_state[0],
                         batch=batch, seq=seq, chunk=chunk,
                         heads=max(1, min(RET_HEADS, 512 // seq)))
    h = _mm_res_and_ffn2(y, ret_w_out[0], h, 0, p)

    cos, sin = _rope_tables(pos, QK_ROPE // 2, 1, tm)
    c_new, kr_new = _latent(h, p["kv_norm"], p["w_dkv"], p["kv_lat_norm"], cos, sin, tm=tm)
    if ckv_past is None:
        k_sh, v_sh = _kv_up([(c_new, kr_new)], p["w_uk"], p["w_uv"], p["k_norm"], steps=m // tm)
    else:
        past = ckv_past.shape[1]
        segments = [(ckv_past.reshape(batch * past, KV_LORA), krope_past.reshape(batch * past, QK_ROPE)),
                    (c_new, kr_new)]

    h = _ffn(h, p["ffn1_norm"][1], ffn1_l1_w_in, ffn1_l1_w_out, 0, tm=FFN_TM, tf=FFN_TF)
    cos, sin = _rope_tables(pos, QK_ROPE // 2, 1, tm)
    zeros = jnp.zeros_like(cos)
    cos = jnp.concatenate([cos, cos, zeros, zeros], axis=-1)
    sin = jnp.concatenate([-sin, sin, zeros, zeros], axis=-1)
    q = _q_proj(h, p["mix_norm"][1], p["w_dq"][0], p["q_lat_norm"][0], p["w_uq"][0], cos, sin,
                p["q_norm"][0], tm=tm)
    if ckv_past is None:
        o = _attention(q, k_sh, v_sh, batch=batch, t_q=seq, t_kv=seq, heads=2, tq=512)
    else:
        o = _kv_attention(segments, p["w_uk"], p["w_uv"], p["k_norm"], q, batch=batch)
    h = _mm_res_and_ffn2(o, w_o[0], h, 1, p)

    return (h.reshape(batch, seq, d), s_fin[None],
            c_new.reshape(batch, seq, KV_LORA), kr_new.reshape(batch, seq, QK_ROPE))


def kernel(x_prompt, x_sample, state_ret, cache_ckv, cache_krope, ffn1_norm, ffn1_w_in, ffn1_w_out, mix_norm, ffn2_norm, ffn2_w_in, ffn2_w_out, ret_w_in, ret_gn_gain, ret_w_out, kv_norm, w_dkv, kv_lat_norm, w_ukv, k_norm, w_dq, q_lat_norm, w_uq, q_norm, w_o):
    bf = lambda w: w.astype(BF16)
    w_ukv_h = w_ukv.reshape(KV_LORA, MLA_HEADS, QK_NOPE + V_HEAD)
    w_uk = w_ukv_h[:, :, :QK_NOPE].reshape(KV_LORA, MLA_HEADS * QK_NOPE)
    w_uv = w_ukv_h[:, :, QK_NOPE:].reshape(KV_LORA, MLA_HEADS * V_HEAD)
    half = QK_ROPE // 2
    w_uq_h = w_uq.reshape(w_uq.shape[0], Q_LORA, MLA_HEADS, QK_HEAD)
    w_lo = w_uq_h[..., QK_NOPE:QK_NOPE + half]
    w_hi = w_uq_h[..., QK_NOPE + half:]
    w_uq_p = jnp.concatenate([w_uq_h[..., :QK_NOPE], w_lo, w_hi, w_hi, w_lo], axis=-1)
    w_uq_p = w_uq_p.reshape(w_uq.shape[0], Q_LORA, MLA_HEADS * Q_SLOT)
    p = dict(ffn1_norm=ffn1_norm, ffn1_w_in=ffn1_w_in, ffn1_w_out=ffn1_w_out,
             ffn1_w_in_l0=bf(ffn1_w_in[:1]), ffn1_w_out_l0=bf(ffn1_w_out[:1]),
             mix_norm=mix_norm, ffn2_norm=ffn2_norm, ffn2_w_in=ffn2_w_in, ffn2_w_out=ffn2_w_out,
             ret_w_in=bf(ret_w_in), ret_gn_gain=ret_gn_gain, ret_w_out=ret_w_out,
             kv_norm=kv_norm, w_dkv=bf(w_dkv), kv_lat_norm=kv_lat_norm,
             w_uk=bf(w_uk), w_uv=bf(w_uv), k_norm=k_norm,
             w_dq=bf(w_dq), q_lat_norm=q_lat_norm, w_uq=bf(w_uq_p), q_norm=q_norm, w_o=w_o)
    pos_prompt = jnp.arange(x_prompt.shape[1])
    past = cache_ckv.shape[1]
    pos_sample = past + jnp.arange(x_sample.shape[1])
    y_p, ret_p, ckv_p, kr_p = _trunk(x_prompt, pos_prompt, None, None, None, p)
    y_s, ret_s, ckv_s, kr_s = _trunk(x_sample, pos_sample, state_ret, cache_ckv, cache_krope, p)
    return (y_p, y_s, ret_p, ckv_p, kr_p, ret_s, ckv_s, kr_s)
```

```python
import functools

import jax
import jax.numpy as jnp
from jax import lax
from jax.experimental import pallas as pl
from jax.experimental.pallas import tpu as pltpu

F32 = jnp.float32
BF16 = jnp.bfloat16

D_MODEL = 2048
CHUNK = 64
D_FF = 5632
RET_HEADS = 8
RET_DK = D_MODEL // RET_HEADS
RET_DV = 2 * D_MODEL // RET_HEADS
MLA_HEADS = 16
QK_NOPE = 128
QK_ROPE = 64
QK_HEAD = QK_NOPE + QK_ROPE
V_HEAD = 128
KV_LORA = 512
Q_LORA = 512
ROPE_BASE = 10000.0
EPS = 1e-6
NEG_INF = -1e30
LOG2E = 1.4426950408889634

VMEM_LIMIT_BYTES = 60 * 1024 * 1024

FFN_TM = 1024
FFN_TF = 512
FFN_NORM_CHUNKS = 4


def _params(*semantics):
    return pltpu.CompilerParams(dimension_semantics=semantics,
                                vmem_limit_bytes=VMEM_LIMIT_BYTES)


def _rms_scale(x):
    return lax.rsqrt(jnp.mean(x * x, axis=-1, keepdims=True) + EPS)


def _dot(a, b):
    return jnp.dot(a, b, preferred_element_type=F32)


def _dot_nt(a, b):
    return lax.dot_general(a, b, (((1,), (1,)), ((), ())), preferred_element_type=F32)


def _dot_tn(a, b):
    return lax.dot_general(a, b, (((0,), (0,)), ((), ())), preferred_element_type=F32)


def _silu(x):
    return x * jax.nn.sigmoid(x)


def _ffn_kernel(x_ref, g_ref, win_hbm, wout_hbm, o_ref, xn_ref, wa_buf, wb_buf, wo_buf, sem,
                *, layer, tf):
    i = pl.program_id(0)
    nf = D_FF // tf
    last = nf - 1
    forward = i % 2 == 0

    def block_of(pos):
        return jnp.where(forward, pos, last - pos)

    def copies(pos, slot):
        col = pl.multiple_of(block_of(pos) * tf, tf)
        return (
            pltpu.make_async_copy(win_hbm.at[layer, :, pl.ds(col, tf)], wa_buf.at[slot], sem.at[slot, 0]),
            pltpu.make_async_copy(win_hbm.at[layer, :, pl.ds(D_FF + col, tf)], wb_buf.at[slot], sem.at[slot, 1]),
            pltpu.make_async_copy(wout_hbm.at[layer, pl.ds(col, tf), :], wo_buf.at[slot], sem.at[slot, 2]),
        )

    def start(pos, slot):
        for c in copies(pos, slot):
            c.start()

    def wait(pos, slot):
        for c in copies(pos, slot):
            c.wait()

    def update(xn, slot):
        a = _dot(xn, wa_buf[slot])
        b = _dot(xn, wb_buf[slot])
        h = (_silu(a) * (0.5 * b)).astype(BF16)
        return _dot(h, wo_buf[slot])

    @pl.when(i == 0)
    def _():
        start(0, 0)
        start(1, 1)
        wait(0, 0)

    tc = x_ref.shape[0] // FFN_NORM_CHUNKS

    def norm(c):
        x = x_ref[c * tc:(c + 1) * tc, :]
        xn = (x * _rms_scale(x) * g_ref[...]).astype(BF16)
        xn_ref[c * tc:(c + 1) * tc, :] = xn
        return x, xn

    cur = norm(0)
    for c in range(FFN_NORM_CHUNKS):
        nxt = norm(c + 1) if c + 1 < FFN_NORM_CHUNKS else None
        x, xn = cur
        o_ref[c * tc:(c + 1) * tc, :] = x + update(xn, 0)
        cur = nxt
    start(2, 0)

    def pair(q, carry):
        p1 = 2 * q + 1

        @pl.when(jnp.logical_or(q > 0, i == 0))
        def _():
            wait(p1, 1)

        o_ref[...] += update(xn_ref[...], 1)

        @pl.when(p1 + 2 <= last)
        def _():
            start(p1 + 2, 1)

        wait(p1 + 1, 0)
        o_ref[...] += update(xn_ref[...], 0)

        @pl.when(p1 + 3 <= last)
        def _():
            start(p1 + 3, 0)

        return carry

    lax.fori_loop(0, (nf - 1) // 2, pair, 0)


def _ffn(x, g, w_in, w_out, layer, *, tm, tf):
    m, d = x.shape
    nf = D_FF // tf
    assert nf % 2 == 1 and nf >= 3, "the block walk assumes an odd number of hidden blocks"
    return pl.pallas_call(
        functools.partial(_ffn_kernel, layer=layer, tf=tf),
        out_shape=jax.ShapeDtypeStruct((m, d), F32),
        grid=(m // tm,),
        in_specs=[
            pl.BlockSpec((tm, d), lambda i: (i, 0)),
            pl.BlockSpec((1, d), lambda i: (0, 0)),
            pl.BlockSpec(memory_space=pl.ANY),
            pl.BlockSpec(memory_space=pl.ANY),
        ],
        out_specs=pl.BlockSpec((tm, d), lambda i: (i, 0)),
        scratch_shapes=[
            pltpu.VMEM((tm, d), BF16),
            pltpu.VMEM((2, d, tf), BF16),
            pltpu.VMEM((2, d, tf), BF16),
            pltpu.VMEM((2, tf, d), BF16),
            pltpu.SemaphoreType.DMA((2, 3)),
        ],
        compiler_params=_params("arbitrary"),
        name="ffn",
    )(x, g.reshape(1, d), w_in, w_out)


def _ret_in_kernel(x_ref, g_ref, w_ref, cos_ref, sin_ref, *rest, rope_blocks, heads_per_block, cast_steps):
    n_cast = len(cast_steps)
    o_ref = rest[n_cast]
    xn_ref = rest[-1]
    j = pl.program_id(1)

    for src, dst, steps in zip(rest[:n_cast], rest[n_cast + 1:-1], cast_steps):
        @pl.when(j < steps)
        def _(src=src, dst=dst):
            dst[...] = src[...].astype(BF16)

    @pl.when(j == 0)
    def _():
        x = x_ref[...]
        xn_ref[...] = (x * _rms_scale(x) * g_ref[...]).astype(BF16)

    half = RET_DK // 2

    @pl.when(j < 2 * rope_blocks)
    def _():
        scale = jnp.where(j < rope_blocks, RET_DK ** -0.5, 1.0).astype(F32)
        cos = cos_ref[...] * scale
        sin = sin_ref[...] * scale
        xn = xn_ref[...]
        for h in range(heads_per_block):
            lo = h * RET_DK
            y = _dot(xn, w_ref[:, lo:lo + RET_DK])
            x1 = y[:, :half]
            x2 = y[:, half:]
            o_ref[:, lo:lo + half] = (x1 * cos - x2 * sin).astype(BF16)
            o_ref[:, lo + half:lo + RET_DK] = (x1 * sin + x2 * cos).astype(BF16)

    @pl.when(j >= 2 * rope_blocks)
    def _():
        o_ref[...] = _dot(xn_ref[...], w_ref[...]).astype(BF16)


def _ret_in(x, g, w, cos, sin, *, tm, tn, cast=()):
    m, d = x.shape
    n = w.shape[1]
    tab_blocks = cos.shape[0] // tm
    n_i, n_j = m // tm, n // tn
    in_specs = [
        pl.BlockSpec((tm, d), lambda i, j: (i, 0)),
        pl.BlockSpec((1, d), lambda i, j: (0, 0)),
        pl.BlockSpec((d, tn), lambda i, j: (0, j)),
        pl.BlockSpec((tm, RET_DK // 2), lambda i, j: (i % tab_blocks, 0)),
        pl.BlockSpec((tm, RET_DK // 2), lambda i, j: (i % tab_blocks, 0)),
    ]
    out_shape = [jax.ShapeDtypeStruct((m, n), BF16)]
    out_specs = [pl.BlockSpec((tm, tn), lambda i, j: (i, j))]
    cast_steps = []
    for wf, layer in cast:
        _, rows, cols = wf.shape
        slab = rows // n_i
        assert slab * n_i == rows and slab % 16 == 0
        steps = max(s for s in range(1, n_j + 1) if cols % (128 * s) == 0)
        cast_steps.append(steps)
        in_specs.append(pl.BlockSpec((None, slab, cols // steps),
                                     lambda i, j, layer=layer, steps=steps: (layer, i, jnp.minimum(j, steps - 1))))
        out_shape.append(jax.ShapeDtypeStruct((1, rows, cols), BF16))
        out_specs.append(pl.BlockSpec((None, slab, cols // steps),
                                      lambda i, j, steps=steps: (0, i, jnp.minimum(j, steps - 1))))
    kern = functools.partial(_ret_in_kernel, rope_blocks=(RET_HEADS * RET_DK) // tn,
                             heads_per_block=tn // RET_DK, cast_steps=tuple(cast_steps))
    outs = pl.pallas_call(
        kern,
        out_shape=out_shape,
        grid=(n_i, n_j),
        in_specs=in_specs,
        out_specs=out_specs,
        scratch_shapes=[pltpu.VMEM((tm, d), BF16)],
        compiler_params=_params("parallel", "arbitrary"),
        name="ret_in",
    )(x, g.reshape(1, d), w, cos, sin, *[wf for wf, _ in cast])
    return outs[0], tuple(outs[1:])


def _ret_core_kernel(lg_ref, q_ref, k_ref, v_ref, g_ref, gain_ref, *rest, chunk, heads, has_state):
    if has_state:
        s0_ref, y_ref, s_ref = rest
    else:
        y_ref, s_ref = rest
    head0 = pl.program_id(1) * heads
    n_chunks = q_ref.shape[0] // chunk
    row = lax.broadcasted_iota(jnp.int32, (chunk, chunk), 0)
    col = lax.broadcasted_iota(jnp.int32, (chunk, chunk), 1)
    diff = (row - col).astype(F32)
    n = lax.broadcasted_iota(jnp.int32, (chunk, 1), 0).astype(F32)

    def head_consts(hh):
        lg = lg_ref[head0 + hh]
        decay = jnp.where(diff >= 0, jnp.exp(lg * jnp.maximum(diff, 0.0)), 0.0)
        cross = jnp.exp(lg * (n + 1.0))
        kdec = jnp.exp(lg * (chunk - 1.0 - n))
        carry = jnp.exp(lg * jnp.full((1, RET_DV), float(chunk), F32))
        return decay, cross, kdec, carry

    def mix(hh, c, consts):
        decay, cross, kdec, carry = consts
        rows = slice(c * chunk, (c + 1) * chunk)
        q = q_ref[rows, hh * RET_DK:(hh + 1) * RET_DK]
        k = k_ref[rows, hh * RET_DK:(hh + 1) * RET_DK]
        v = v_ref[rows, hh * RET_DV:(hh + 1) * RET_DV]
        inner = (_dot_nt(q, k) * decay).astype(BF16)
        if c > 0:
            s_old = s_ref[0, hh]
        elif has_state:
            s_old = s0_ref[0, hh]
        else:
            s_old = None
        kd = (k.astype(F32) * kdec).astype(BF16)
        o = _dot(inner, v)
        s_new = _dot_tn(kd, v)
        if s_old is not None:
            o = o + cross * _dot(q, s_old.astype(BF16))
            s_new = carry * s_old + s_new
        s_ref[0, hh] = s_new
        return o

    def norm_gate(hh, c, o):
        rows = slice(c * chunk, (c + 1) * chunk)
        cols = slice(hh * RET_DV, (hh + 1) * RET_DV)
        mu = jnp.mean(o, axis=-1, keepdims=True)
        dev = o - mu
        var = jnp.mean(dev * dev, axis=-1, keepdims=True)
        of = dev * lax.rsqrt(var + EPS) * gain_ref[:, cols]
        y_ref[rows, cols] = (_silu(g_ref[rows, cols].astype(F32)) * of).astype(BF16)

    consts = None
    pending = None
    for hh in range(heads):
        for c in range(n_chunks):
            if c == 0:
                consts = head_consts(hh)
            o = mix(hh, c, consts)
            if pending is not None:
                norm_gate(*pending)
            pending = (hh, c, o)
    norm_gate(*pending)


def _ret_core(qkvg, log_gamma, gain, state, *, batch, seq, chunk, heads):
    m = qkvg.shape[0]
    hk = RET_HEADS * RET_DK
    hv = RET_HEADS * RET_DV
    wk = heads * RET_DK
    wv = heads * RET_DV
    kblk = hk // wk
    vblk = 2 * hk // wv
    gblk = vblk + hv // wv
    has_state = state is not None
    in_specs = [
        pl.BlockSpec(memory_space=pltpu.SMEM),
        pl.BlockSpec((seq, wk), lambda b, g: (b, g)),
        pl.BlockSpec((seq, wk), lambda b, g: (b, kblk + g)),
        pl.BlockSpec((seq, wv), lambda b, g: (b, vblk + g)),
        pl.BlockSpec((seq, wv), lambda b, g: (b, gblk + g)),
        pl.BlockSpec((1, wv), lambda b, g: (0, g)),
    ]
    args = [log_gamma, qkvg, qkvg, qkvg, qkvg, gain.reshape(1, hv)]
    state_spec = pl.BlockSpec((1, heads, RET_DK, RET_DV), lambda b, g: (b, g, 0, 0))
    if has_state:
        in_specs.append(state_spec)
        args.append(state)
    kern = functools.partial(_ret_core_kernel, chunk=chunk, heads=heads, has_state=has_state)
    return pl.pallas_call(
        kern,
        out_shape=(jax.ShapeDtypeStruct((m, hv), BF16),
                   jax.ShapeDtypeStruct((batch, RET_HEADS, RET_DK, RET_DV), F32)),
        grid=(batch, RET_HEADS // heads),
        in_specs=in_specs,
        out_specs=(pl.BlockSpec((seq, wv), lambda b, g: (b, g)), state_spec),
        compiler_params=_params("parallel", "parallel"),
        name="ret_core",
    )(*args)


def _mm_res_kernel(a_ref, w_ref, r_ref, *rest):
    n_cast = (len(rest) - 1) // 2
    o_ref = rest[n_cast]
    o_ref[...] = r_ref[...] + _dot(a_ref[...], w_ref[...])
    for src, dst in zip(rest[:n_cast], rest[n_cast + 1:]):
        dst[...] = src[...].astype(BF16)


def _mm_res(a, w, res, *, tm, cast=()):
    m, k = a.shape
    n = w.shape[1]
    steps = m // tm
    in_specs = [
        pl.BlockSpec((tm, k), lambda i: (i, 0)),
        pl.BlockSpec((k, n), lambda i: (0, 0), pipeline_mode=pl.Buffered(1)),
        pl.BlockSpec((tm, n), lambda i: (i, 0)),
    ]
    out_shape = [jax.ShapeDtypeStruct((m, n), F32)]
    out_specs = [pl.BlockSpec((tm, n), lambda i: (i, 0))]
    for wf, layer in cast:
        _, rows, cols = wf.shape
        slab = rows // steps
        assert slab * steps == rows and slab % 16 == 0
        in_specs.append(pl.BlockSpec((None, slab, cols), lambda i, layer=layer: (layer, i, 0)))
        out_shape.append(jax.ShapeDtypeStruct((1, rows, cols), BF16))
        out_specs.append(pl.BlockSpec((None, slab, cols), lambda i: (0, i, 0)))
    outs = pl.pallas_call(
        _mm_res_kernel,
        out_shape=out_shape,
        grid=(steps,),
        in_specs=in_specs,
        out_specs=out_specs,
        compiler_params=_params("parallel"),
        name="mm_res",
    )(a, w, res, *[wf for wf, _ in cast])
    return outs[0], tuple(outs[1:])


def _latent_kernel(x_ref, g_ref, w_ref, gl_ref, cos_ref, sin_ref, c_ref, kr_ref, *, row_chunks):
    half = QK_ROPE // 2
    tc = x_ref.shape[0] // row_chunks

    def norm(c):
        x = x_ref[c * tc:(c + 1) * tc, :]
        return (x * _rms_scale(x) * g_ref[...]).astype(BF16)

    def project(c, xn):
        rows = slice(c * tc, (c + 1) * tc)
        y = _dot(xn, w_ref[...])
        cl = y[:, :KV_LORA]
        c_ref[rows, :] = cl * _rms_scale(cl) * gl_ref[...]
        x1 = y[:, KV_LORA:KV_LORA + half]
        x2 = y[:, KV_LORA + half:KV_LORA + QK_ROPE]
        cos = cos_ref[rows, :]
        sin = sin_ref[rows, :]
        kr_ref[rows, :] = jnp.concatenate([x1 * cos - x2 * sin, x1 * sin + x2 * cos], axis=-1)

    xn = norm(0)
    for c in range(row_chunks):
        xn_next = norm(c + 1) if c + 1 < row_chunks else None
        project(c, xn)
        xn = xn_next


def _latent(x, g, w, g_lat, cos, sin, *, tm):
    m, d = x.shape
    n = w.shape[1]
    half = QK_ROPE // 2
    tab_blocks = cos.shape[0] // tm
    return pl.pallas_call(
        functools.partial(_latent_kernel, row_chunks=tm // 128),
        out_shape=(jax.ShapeDtypeStruct((m, KV_LORA), F32),
                   jax.ShapeDtypeStruct((m, QK_ROPE), F32)),
        grid=(m // tm,),
        in_specs=[
            pl.BlockSpec((tm, d), lambda i: (i, 0)),
            pl.BlockSpec((1, d), lambda i: (0, 0)),
            pl.BlockSpec((d, n), lambda i: (0, 0)),
            pl.BlockSpec((1, KV_LORA), lambda i: (0, 0)),
            pl.BlockSpec((tm, half), lambda i: (i % tab_blocks, 0)),
            pl.BlockSpec((tm, half), lambda i: (i % tab_blocks, 0)),
        ],
        out_specs=(pl.BlockSpec((tm, KV_LORA), lambda i: (i, 0)),
                   pl.BlockSpec((tm, QK_ROPE), lambda i: (i, 0))),
        compiler_params=_params("parallel"),
        name="latent",
    )(x, g.reshape(1, d), w, g_lat.reshape(1, KV_LORA), cos, sin)


def _kv_up_body(seg_refs, seg_rows, wk_ref, wv_ref, gk_ref, k_ref, v_ref):
    refs = seg_refs
    g_nope = gk_ref[:, :QK_NOPE]
    g_rope = gk_ref[:, QK_NOPE:]
    lo = 0
    for s, rows in enumerate(seg_rows):
        c = refs[2 * s][...].astype(BF16)
        kr = refs[2 * s + 1][...]
        kn = _dot(c, wk_ref[...])
        v_ref[lo:lo + rows, :] = _dot(c, wv_ref[...]).astype(BF16)
        ss_rope = jnp.sum(kr * kr, axis=-1, keepdims=True)
        for h in range(MLA_HEADS):
            knh = kn[:, h * QK_NOPE:(h + 1) * QK_NOPE]
            ss = jnp.sum(knh * knh, axis=-1, keepdims=True) + ss_rope
            r = lax.rsqrt(ss * (1.0 / QK_HEAD) + EPS)
            k_ref[h, lo:lo + rows, :QK_NOPE] = (knh * r * g_nope).astype(BF16)
            k_ref[h, lo:lo + rows, QK_NOPE:] = (kr * r * g_rope).astype(BF16)
        lo += rows


def _kv_up_kernel(*refs, seg_rows):
    n_seg = len(seg_rows)
    wk_ref, wv_ref, gk_ref, k_ref, v_ref = refs[2 * n_seg:]
    _kv_up_body(refs[:2 * n_seg], seg_rows, wk_ref, wv_ref, gk_ref, k_ref, v_ref)


def _kv_up(segments, wk, wv, gk, *, steps):
    seg_rows = tuple(c.shape[0] // steps for c, _ in segments)
    tm = sum(seg_rows)
    m = steps * tm
    in_specs, args = [], []
    for (c, kr), rows in zip(segments, seg_rows):
        in_specs += [pl.BlockSpec((rows, KV_LORA), lambda i: (i, 0)),
                     pl.BlockSpec((rows, QK_ROPE), lambda i: (i, 0))]
        args += [c, kr]
    in_specs += [pl.BlockSpec(wk.shape, lambda i: (0, 0)),
                 pl.BlockSpec(wv.shape, lambda i: (0, 0)),
                 pl.BlockSpec((1, QK_HEAD), lambda i: (0, 0))]
    return pl.pallas_call(
        functools.partial(_kv_up_kernel, seg_rows=seg_rows),
        out_shape=(jax.ShapeDtypeStruct((MLA_HEADS, m, QK_HEAD), BF16),
                   jax.ShapeDtypeStruct((m, MLA_HEADS * V_HEAD), BF16)),
        grid=(steps,),
        in_specs=in_specs,
        out_specs=(pl.BlockSpec((MLA_HEADS, tm, QK_HEAD), lambda i: (0, i, 0)),
                   pl.BlockSpec((tm, MLA_HEADS * V_HEAD), lambda i: (i, 0))),
        compiler_params=_params("parallel"),
        name="kv_up",
    )(*args, wk, wv, gk.reshape(1, QK_HEAD))


Q_SLOT = 2 * QK_NOPE


def _q_proj_kernel(x_ref, g_ref, wdq_ref, gl_ref, wuq_ref, cos_ref, sin_ref, gq_ref, q_ref, *, row_chunks):
    g_nope = gq_ref[:, :QK_NOPE]
    g_rope = gq_ref[:, QK_NOPE:]
    tc = x_ref.shape[0] // row_chunks

    def project(c):
        rows = slice(c * tc, (c + 1) * tc)
        x = x_ref[rows, :]
        hn = (x * _rms_scale(x) * g_ref[...]).astype(BF16)
        ql = _dot(hn, wdq_ref[...])
        qn = (ql * _rms_scale(ql) * gl_ref[...]).astype(BF16)
        return _dot(qn, wuq_ref[...])

    def per_head(c, q):
        rows = slice(c * tc, (c + 1) * tc)
        cos = cos_ref[rows, :]
        sin = sin_ref[rows, :]
        for h in range(MLA_HEADS):
            nh = q[:, h * Q_SLOT:h * Q_SLOT + QK_NOPE]
            t = q[:, h * Q_SLOT + QK_NOPE:(h + 1) * Q_SLOT]
            rp = t * cos + pltpu.roll(t, QK_ROPE, 1) * sin
            ss = jnp.sum(nh * nh + rp * rp, axis=-1, keepdims=True)
            r = lax.rsqrt(ss * (1.0 / QK_HEAD) + EPS) * (QK_HEAD ** -0.5 * LOG2E)
            q_ref[h, rows, :QK_NOPE] = (nh * r * g_nope).astype(BF16)
            q_ref[h, rows, QK_NOPE:] = (rp[:, :QK_ROPE] * r * g_rope).astype(BF16)

    q = project(0)
    for c in range(row_chunks):
        q_next = project(c + 1) if c + 1 < row_chunks else None
        per_head(c, q)
        q = q_next


def _q_proj(x, g, wdq, g_lat, wuq, cos, sin, gq, *, tm):
    m, d = x.shape
    tab_blocks = cos.shape[0] // tm
    tw = cos.shape[1]
    return pl.pallas_call(
        functools.partial(_q_proj_kernel, row_chunks=tm // 256),
        out_shape=jax.ShapeDtypeStruct((MLA_HEADS, m, QK_HEAD), BF16),
        grid=(m // tm,),
        in_specs=[
            pl.BlockSpec((tm, d), lambda i: (i, 0)),
            pl.BlockSpec((1, d), lambda i: (0, 0)),
            pl.BlockSpec(wdq.shape, lambda i: (0, 0)),
            pl.BlockSpec((1, Q_LORA), lambda i: (0, 0)),
            pl.BlockSpec(wuq.shape, lambda i: (0, 0)),
            pl.BlockSpec((tm, tw), lambda i: (i % tab_blocks, 0)),
            pl.BlockSpec((tm, tw), lambda i: (i % tab_blocks, 0)),
            pl.BlockSpec((1, QK_HEAD), lambda i: (0, 0)),
        ],
        out_specs=pl.BlockSpec((MLA_HEADS, tm, QK_HEAD), lambda i: (0, i, 0)),
        compiler_params=_params("parallel"),
        name="q_proj",
    )(x, g.reshape(1, d), wdq, g_lat.reshape(1, Q_LORA), wuq, cos, sin, gq.reshape(1, QK_HEAD))


def _attn_causal_kernel(q_ref, k_ref, v_ref, o_ref, *, heads, tq):
    n_q = q_ref.shape[1] // tq
    hq = tq // 2
    krow = lax.broadcasted_iota(jnp.int32, (hq, tq), 0) // CHUNK
    qcol = lax.broadcasted_iota(jnp.int32, (hq, tq), 1) // CHUNK
    keep_a = krow <= qcol
    keep_b = keep_a[:, :hq]

    def right_half(full, fn, part):
        return jnp.concatenate([full[:, :hq], fn(full[:, hq:], part)], axis=1)

    def scores(h, qi):
        lo = qi * tq
        q = q_ref[h, lo:lo + tq, :]
        st_a = jnp.where(keep_a, _dot_nt(k_ref[h, lo:lo + hq, :], q), NEG_INF)
        st_b = jnp.where(keep_b, _dot_nt(k_ref[h, lo + hq:lo + tq, :], q[hq:, :]), NEG_INF)
        st_p = _dot_nt(k_ref[h, :lo, :], q) if qi > 0 else None
        return st_a, st_b, st_p

    def softmax(st):
        st_a, st_b, st_p = st
        m = jnp.max(st_a, axis=0, keepdims=True)
        if st_p is not None:
            m = jnp.maximum(m, jnp.max(st_p, axis=0, keepdims=True))
        m = right_half(m, jnp.maximum, jnp.max(st_b, axis=0, keepdims=True))
        pt_a = jnp.exp2(st_a - m)
        pt_b = jnp.exp2(st_b - m[:, hq:])
        l = jnp.sum(pt_a, axis=0, keepdims=True)
        pt_p = None
        if st_p is not None:
            pt_p = jnp.exp2(st_p - m)
            l = l + jnp.sum(pt_p, axis=0, keepdims=True)
            pt_p = pt_p.astype(BF16)
        l = right_half(l, jnp.add, jnp.sum(pt_b, axis=0, keepdims=True))
        return pt_a.astype(BF16), pt_b.astype(BF16), pt_p, l

    def values(h, qi, p):
        pt_a, pt_b, pt_p, l = p
        lo = qi * tq
        cols = slice(h * V_HEAD, (h + 1) * V_HEAD)
        acc_t = _dot_tn(v_ref[lo:lo + hq, cols], pt_a)
        if pt_p is not None:
            acc_t = acc_t + _dot_tn(v_ref[:lo, cols], pt_p)
        acc_t = right_half(acc_t, jnp.add, _dot_tn(v_ref[lo + hq:lo + tq, cols], pt_b))
        o_ref[lo:lo + tq, cols] = (acc_t / l).T.astype(BF16)

    work = [(h, qi) for h in range(heads) for qi in range(n_q)]
    s_q, p_q = {}, {}
    for step in range(len(work) + 2):
        if step < len(work):
            s_q[step] = scores(*work[step])
        if 0 <= step - 1 < len(work):
            p_q[step - 1] = softmax(s_q.pop(step - 1))
        if 0 <= step - 2 < len(work):
            values(*work[step - 2], p_q.pop(step - 2))


def _kv_attn_kernel(*refs, seg_rows):
    n_seg = len(seg_rows)
    wk_ref, wv_ref, gk_ref, q_ref, o_ref, k_scr, v_scr = refs[2 * n_seg:]
    g_nope = gk_ref[:, :QK_NOPE]
    g_rope = gk_ref[:, QK_NOPE:]

    group = 4
    width = group * QK_NOPE
    segs, lo = [], 0
    for s, rows in enumerate(seg_rows):
        c = refs[2 * s][...].astype(BF16)
        kr = refs[2 * s + 1][...]
        segs.append((lo, rows, c, kr, jnp.sum(kr * kr, axis=-1, keepdims=True)))
        lo += rows

    def project(g):
        cols = slice(g * width, (g + 1) * width)
        kn = []
        for lo, rows, c, _, _ in segs:
            v_scr[lo:lo + rows, cols] = _dot(c, wv_ref[:, cols]).astype(BF16)
            kn.append(_dot(c, wk_ref[:, cols]))
        return kn

    kn_groups = {}

    def key_norm(h):
        g, hh = divmod(h, group)
        for (lo, rows, _, kr, ss_rope), kn in zip(segs, kn_groups[g]):
            knh = kn[:, hh * QK_NOPE:(hh + 1) * QK_NOPE]
            ss = jnp.sum(knh * knh, axis=-1, keepdims=True) + ss_rope
            r = lax.rsqrt(ss * (1.0 / QK_HEAD) + EPS)
            k_scr[h, lo:lo + rows, :QK_NOPE] = (knh * r * g_nope).astype(BF16)
            k_scr[h, lo:lo + rows, QK_NOPE:] = (kr * r * g_rope).astype(BF16)

    def scores(h):
        return _dot_nt(q_ref[h], k_scr[h])

    def softmax(s):
        m = jnp.max(s, axis=-1, keepdims=True)
        p = jnp.exp2(s - m)
        return p.astype(BF16), jnp.sum(p, axis=-1, keepdims=True)

    def values(h, pl_):
        p, l = pl_
        cols = slice(h * V_HEAD, (h + 1) * V_HEAD)
        o_ref[:, cols] = (_dot(p, v_scr[:, cols]) / l).astype(BF16)

    s_q, p_q = {}, {}
    kn_groups[0] = project(0)
    for step in range(MLA_HEADS + 3):
        if step % group == 0 and step // group + 1 < MLA_HEADS // group:
            kn_groups[step // group + 1] = project(step // group + 1)
        if step < MLA_HEADS:
            key_norm(step)
        if 0 <= step - 1 < MLA_HEADS:
            s_q[step - 1] = scores(step - 1)
        if 0 <= step - 2 < MLA_HEADS:
            p_q[step - 2] = softmax(s_q.pop(step - 2))
        if 0 <= step - 3 < MLA_HEADS:
            values(step - 3, p_q.pop(step - 3))


def _kv_attention(segments, wk, wv, gk, q, *, batch):
    seg_rows = tuple(c.shape[0] // batch for c, _ in segments)
    t_kv = sum(seg_rows)
    t_q = q.shape[1] // batch
    in_specs, args = [], []
    for (c, kr), rows in zip(segments, seg_rows):
        in_specs += [pl.BlockSpec((rows, KV_LORA), lambda b: (b, 0)),
                     pl.BlockSpec((rows, QK_ROPE), lambda b: (b, 0))]
        args += [c, kr]
    in_specs += [pl.BlockSpec(wk.shape, lambda b: (0, 0)),
                 pl.BlockSpec(wv.shape, lambda b: (0, 0)),
                 pl.BlockSpec((1, QK_HEAD), lambda b: (0, 0)),
                 pl.BlockSpec((MLA_HEADS, t_q, QK_HEAD), lambda b: (0, b, 0))]
    return pl.pallas_call(
        functools.partial(_kv_attn_kernel, seg_rows=seg_rows),
        out_shape=jax.ShapeDtypeStruct((batch * t_q, MLA_HEADS * V_HEAD), BF16),
        grid=(batch,),
        in_specs=in_specs,
        out_specs=pl.BlockSpec((t_q, MLA_HEADS * V_HEAD), lambda b: (b, 0)),
        scratch_shapes=[pltpu.VMEM((MLA_HEADS, t_kv, QK_HEAD), BF16),
                        pltpu.VMEM((t_kv, MLA_HEADS * V_HEAD), BF16)],
        compiler_params=_params("parallel"),
        name="kv_attention",
    )(*args, wk, wv, gk.reshape(1, QK_HEAD), q)


def _attention(q, k, v, *, batch, t_q, t_kv, heads, tq):
    hg = MLA_HEADS // heads
    kern = functools.partial(_attn_causal_kernel, heads=heads, tq=tq)
    return pl.pallas_call(
        kern,
        out_shape=jax.ShapeDtypeStruct((batch * t_q, MLA_HEADS * V_HEAD), BF16),
        grid=(batch, hg),
        in_specs=[
            pl.BlockSpec((heads, t_q, QK_HEAD), lambda b, g: (g, b, 0)),
            pl.BlockSpec((heads, t_kv, QK_HEAD), lambda b, g: (g, b, 0)),
            pl.BlockSpec((t_kv, heads * V_HEAD), lambda b, g: (b, g)),
        ],
        out_specs=pl.BlockSpec((t_q, heads * V_HEAD), lambda b, g: (b, g)),
        compiler_params=_params("parallel", "parallel"),
        name="attention",
    )(q, k, v)


def _rope_tables(pos, half, reps, rows):
    inv = jnp.power(ROPE_BASE, -jnp.arange(half, dtype=F32) / half)
    ang = pos.astype(F32)[:, None] * inv[None, :]
    cos = jnp.tile(jnp.cos(ang), (max(rows // pos.shape[0], 1), reps))
    sin = jnp.tile(jnp.sin(ang), (max(rows // pos.shape[0], 1), reps))
    return cos, sin


def _mm_res_and_ffn2(a, w, h, layer, p):
    key = ("ffn2_bf16", layer)
    if key not in p:
        h, p[key] = _mm_res(a, w, h, tm=512, cast=((p["ffn2_w_in"], layer), (p["ffn2_w_out"], layer)))
    else:
        h, _ = _mm_res(a, w, h, tm=512)
    w_in, w_out = p[key]
    return _ffn(h, p["ffn2_norm"][layer], w_in, w_out, 0, tm=FFN_TM, tf=FFN_TF)


def _trunk(x, pos, ret_state, ckv_past, krope_past, p):
    batch, seq, d = x.shape
    m = batch * seq
    tm = 1024
    h = x.reshape(m, d)

    h = _ffn(h, p["ffn1_norm"][0], p["ffn1_w_in_l0"], p["ffn1_w_out_l0"], 0, tm=FFN_TM, tf=FFN_TF)
    cos, sin = _rope_tables(pos, RET_DK // 2, 1, tm)
    key = "cast_by_ret_in"
    cast = () if key in p else ((p["ffn1_w_in"], 1), (p["ffn1_w_out"], 1), (p["ret_w_out"], 0), (p["w_o"], 0))
    qkvg, cast_out = _ret_in(h, p["mix_norm"][0], p["ret_w_in"][0], cos, sin, tm=tm, tn=2048, cast=cast)
    if cast:
        p[key] = cast_out
    ffn1_l1_w_in, ffn1_l1_w_out, ret_w_out, w_o = p[key]
    log_gamma = jnp.log1p(-jnp.exp2(-5.0 - jnp.arange(RET_HEADS, dtype=F32)))
    chunk = min(seq, 256)
    y, s_fin = _ret_core(qkvg, log_gamma, p["ret_gn_gain"][0],
                         None if ret_state is None else ret_state[0],
                         batch=batch, seq=seq, chunk=chunk,
                         heads=max(1, min(RET_HEADS, 512 // seq)))
    h = _mm_res_and_ffn2(y, ret_w_out[0], h, 0, p)

    cos, sin = _rope_tables(pos, QK_ROPE // 2, 1, tm)
    c_new, kr_new = _latent(h, p["kv_norm"], p["w_dkv"], p["kv_lat_norm"], cos, sin, tm=tm)
    if ckv_past is None:
        k_sh, v_sh = _kv_up([(c_new, kr_new)], p["w_uk"], p["w_uv"], p["k_norm"], steps=m // tm)
    else:
        past = ckv_past.shape[1]
        segments = [(ckv_past.reshape(batch * past, KV_LORA), krope_past.reshape(batch * past, QK_ROPE)),
                    (c_new, kr_new)]

    h = _ffn(h, p["ffn1_norm"][1], ffn1_l1_w_in, ffn1_l1_w_out, 0, tm=FFN_TM, tf=FFN_TF)
    cos, sin = _rope_tables(pos, QK_ROPE // 2, 1, tm)
    zeros = jnp.zeros_like(cos)
    cos = jnp.concatenate([cos, cos, zeros, zeros], axis=-1)
    sin = jnp.concatenate([-sin, sin, zeros, zeros], axis=-1)
    q = _q_proj(h, p["mix_norm"][1], p["w_dq"][0], p["q_lat_norm"][0], p["w_uq"][0], cos, sin,
                p["q_norm"][0], tm=tm)
    if ckv_past is None:
        o = _attention(q, k_sh, v_sh, batch=batch, t_q=seq, t_kv=seq, heads=2, tq=512)
    else:
        o = _kv_attention(segments, p["w_uk"], p["w_uv"], p["k_norm"], q, batch=batch)
    h = _mm_res_and_ffn2(o, w_o[0], h, 1, p)

    return (h.reshape(batch, seq, d), s_fin[None],
            c_new.reshape(batch, seq, KV_LORA), kr_new.reshape(batch, seq, QK_ROPE))


def kernel(x_prompt, x_sample, state_ret, cache_ckv, cache_krope, ffn1_norm, ffn1_w_in, ffn1_w_out, mix_norm, ffn2_norm, ffn2_w_in, ffn2_w_out, ret_w_in, ret_gn_gain, ret_w_out, kv_norm, w_dkv, kv_lat_norm, w_ukv, k_norm, w_dq, q_lat_norm, w_uq, q_norm, w_o):
    bf = lambda w: w.astype(BF16)
    w_ukv_h = w_ukv.reshape(KV_LORA, MLA_HEADS, QK_NOPE + V_HEAD)
    w_uk = w_ukv_h[:, :, :QK_NOPE].reshape(KV_LORA, MLA_HEADS * QK_NOPE)
    w_uv = w_ukv_h[:, :, QK_NOPE:].reshape(KV_LORA, MLA_HEADS * V_HEAD)
    half = QK_ROPE // 2
    w_uq_h = w_uq.reshape(w_uq.shape[0], Q_LORA, MLA_HEADS, QK_HEAD)
    w_lo = w_uq_h[..., QK_NOPE:QK_NOPE + half]
    w_hi = w_uq_h[..., QK_NOPE + half:]
    w_uq_p = jnp.concatenate([w_uq_h[..., :QK_NOPE], w_lo, w_hi, w_hi, w_lo], axis=-1)
    w_uq_p = w_uq_p.reshape(w_uq.shape[0], Q_LORA, MLA_HEADS * Q_SLOT)
    p = dict(ffn1_norm=ffn1_norm, ffn1_w_in=ffn1_w_in, ffn1_w_out=ffn1_w_out,
             ffn1_w_in_l0=bf(ffn1_w_in[:1]), ffn1_w_out_l0=bf(ffn1_w_out[:1]),
             mix_norm=mix_norm, ffn2_norm=ffn2_norm, ffn2_w_in=ffn2_w_in, ffn2_w_out=ffn2_w_out,
             ret_w_in=bf(ret_w_in), ret_gn_gain=ret_gn_gain, ret_w_out=ret_w_out,
             kv_norm=kv_norm, w_dkv=bf(w_dkv), kv_lat_norm=kv_lat_norm,
             w_uk=bf(w_uk), w_uv=bf(w_uv), k_norm=k_norm,
             w_dq=bf(w_dq), q_lat_norm=q_lat_norm, w_uq=bf(w_uq_p), q_norm=q_norm, w_o=w_o)
    pos_prompt = jnp.arange(x_prompt.shape[1])
    past = cache_ckv.shape[1]
    pos_sample = past + jnp.arange(x_sample.shape[1])
    y_p, ret_p, ckv_p, kr_p = _trunk(x_prompt, pos_prompt, None, None, None, p)
    y_s, ret_s, ckv_s, kr_s = _trunk(x_sample, pos_sample, state_ret, cache_ckv, cache_krope, p)
    return (y_p, y_s, ret_p, ckv_p, kr_p, ret_s, ckv_s, kr_s)
```

```python
import functools

import jax
import jax.numpy as jnp
from jax import lax
from jax.experimental import pallas as pl
from jax.experimental.pallas import tpu as pltpu

F32 = jnp.float32
BF16 = jnp.bfloat16

D_MODEL = 2048
CHUNK = 64
D_FF = 5632
RET_HEADS = 8
RET_DK = D_MODEL // RET_HEADS
RET_DV = 2 * D_MODEL // RET_HEADS
MLA_HEADS = 16
QK_NOPE = 128
QK_ROPE = 64
QK_HEAD = QK_NOPE + QK_ROPE
V_HEAD = 128
KV_LORA = 512
Q_LORA = 512
ROPE_BASE = 10000.0
EPS = 1e-6
NEG_INF = -1e30
LOG2E = 1.4426950408889634

VMEM_LIMIT_BYTES = 60 * 1024 * 1024

FFN_TM = 1024
FFN_TF = 512
FFN_NORM_CHUNKS = 4
FFN_DMA_BLOCKS = 2


def _params(*semantics):
    return pltpu.CompilerParams(dimension_semantics=semantics,
                                vmem_limit_bytes=VMEM_LIMIT_BYTES)


def _rms_scale(x):
    return lax.rsqrt(jnp.mean(x * x, axis=-1, keepdims=True) + EPS)


def _dot(a, b):
    return jnp.dot(a, b, preferred_element_type=F32)


def _dot_nt(a, b):
    return lax.dot_general(a, b, (((1,), (1,)), ((), ())), preferred_element_type=F32)


def _dot_tn(a, b):
    return lax.dot_general(a, b, (((0,), (0,)), ((), ())), preferred_element_type=F32)


def _silu(x):
    return x * jax.nn.sigmoid(x)


def _ffn_kernel(g_ref, x_hbm, win_hbm, wout_hbm, o_hbm, acc, xn_ref, wa_buf, wb_buf, wo_buf,
                wsem, xsem, osem, *, layer, tf):
    i = pl.program_id(0)
    n_tiles = pl.num_programs(0)
    tm = acc.shape[1]
    dma_cols = FFN_DMA_BLOCKS * tf
    n_pos = 1 + (D_FF - tf) // dma_cols
    last = n_pos - 1
    a_slot = i % 2

    def weight_copies(pos, slot):
        col = pl.multiple_of(jnp.maximum(pos * dma_cols - (dma_cols - tf), 0), tf)
        return (
            pltpu.make_async_copy(win_hbm.at[layer, :, pl.ds(col, dma_cols)], wa_buf.at[slot], wsem.at[slot, 0]),
            pltpu.make_async_copy(win_hbm.at[layer, :, pl.ds(D_FF + col, dma_cols)], wb_buf.at[slot], wsem.at[slot, 1]),
            pltpu.make_async_copy(wout_hbm.at[layer, pl.ds(col, dma_cols), :], wo_buf.at[slot], wsem.at[slot, 2]),
        )

    def x_copy(tile, slot):
        src = x_hbm.at[pl.ds(pl.multiple_of(tile * tm, tm), tm), :]
        return pltpu.make_async_copy(src, acc.at[slot], xsem.at[slot])

    def out_copy(tile, slot):
        dst = o_hbm.at[pl.ds(pl.multiple_of(tile * tm, tm), tm), :]
        return pltpu.make_async_copy(acc.at[slot], dst, osem.at[slot])

    def update(xn, slot, cols):
        a = _dot(xn, wa_buf[slot, :, :cols])
        b = _dot(xn, wb_buf[slot, :, :cols])
        h = (_silu(a) * (0.5 * b)).astype(BF16)
        return _dot(h, wo_buf[slot, :cols, :])

    @pl.when(i == 0)
    def _():
        x_copy(0, 0).start()
        for cp in weight_copies(0, 0):
            cp.start()

    x_copy(i, a_slot).wait()

    def sync(pos, w_slot):
        for cp in weight_copies(pos, w_slot):
            cp.wait()

        nxt = jnp.where(pos == last, 0, pos + 1)

        @pl.when(jnp.logical_or(pos < last, i + 1 < n_tiles))
        def _():
            for cp in weight_copies(nxt, 1 - w_slot):
                cp.start()

        @pl.when(pos == 1)
        def _():
            @pl.when(i > 0)
            def _():
                out_copy(i - 1, 1 - a_slot).wait()

            @pl.when(i + 1 < n_tiles)
            def _():
                x_copy(i + 1, 1 - a_slot).start()

    sync(0, 0)
    tc = tm // FFN_NORM_CHUNKS

    def norm(c):
        x = acc[a_slot, c * tc:(c + 1) * tc, :]
        xn = (x * _rms_scale(x) * g_ref[...]).astype(BF16)
        xn_ref[c * tc:(c + 1) * tc, :] = xn
        return x, xn

    cur = norm(0)
    for c in range(FFN_NORM_CHUNKS):
        nxt_c = norm(c + 1) if c + 1 < FFN_NORM_CHUNKS else None
        x, xn = cur
        acc[a_slot, c * tc:(c + 1) * tc, :] = x + update(xn, 0, tf)
        cur = nxt_c

    def position(pos, carry):
        w_slot = pos % 2
        sync(pos, w_slot)
        acc[a_slot] += update(xn_ref[...], w_slot, dma_cols)
        return carry

    lax.fori_loop(1, n_pos, position, 0)

    out_copy(i, a_slot).start()

    @pl.when(i + 1 == n_tiles)
    def _():
        out_copy(i, a_slot).wait()


def _ffn(x, g, w_in, w_out, layer, *, tm, tf):
    m, d = x.shape
    dma_cols = FFN_DMA_BLOCKS * tf
    assert (D_FF - tf) % dma_cols == 0 and ((D_FF - tf) // dma_cols) % 2 == 1
    return pl.pallas_call(
        functools.partial(_ffn_kernel, layer=layer, tf=tf),
        out_shape=jax.ShapeDtypeStruct((m, d), F32),
        grid=(m // tm,),
        in_specs=[
            pl.BlockSpec((1, d), lambda i: (0, 0)),
            pl.BlockSpec(memory_space=pl.ANY),
            pl.BlockSpec(memory_space=pl.ANY),
            pl.BlockSpec(memory_space=pl.ANY),
        ],
        out_specs=pl.BlockSpec(memory_space=pl.ANY),
        scratch_shapes=[
            pltpu.VMEM((2, tm, d), F32),
            pltpu.VMEM((tm, d), BF16),
            pltpu.VMEM((2, d, dma_cols), BF16),
            pltpu.VMEM((2, d, dma_cols), BF16),
            pltpu.VMEM((2, dma_cols, d), BF16),
            pltpu.SemaphoreType.DMA((2, 3)),
            pltpu.SemaphoreType.DMA((2,)),
            pltpu.SemaphoreType.DMA((2,)),
        ],
        compiler_params=_params("arbitrary"),
        name="ffn",
    )(g.reshape(1, d), x, w_in, w_out)


def _ret_in_kernel(x_ref, g_ref, w_ref, cos_ref, sin_ref, *rest, rope_blocks, heads_per_block, cast_steps):
    n_cast = len(cast_steps)
    o_ref = rest[n_cast]
    xn_ref = rest[-1]
    j = pl.program_id(1)

    for src, dst, steps in zip(rest[:n_cast], rest[n_cast + 1:-1], cast_steps):
        @pl.when(j < steps)
        def _(src=src, dst=dst):
            dst[...] = src[...].astype(BF16)

    @pl.when(j == 0)
    def _():
        x = x_ref[...]
        xn_ref[...] = (x * _rms_scale(x) * g_ref[...]).astype(BF16)

    half = RET_DK // 2

    @pl.when(j < 2 * rope_blocks)
    def _():
        scale = jnp.where(j < rope_blocks, RET_DK ** -0.5, 1.0).astype(F32)
        cos = cos_ref[...] * scale
        sin = sin_ref[...] * scale
        xn = xn_ref[...]
        for h in range(heads_per_block):
            lo = h * RET_DK
            y = _dot(xn, w_ref[:, lo:lo + RET_DK])
            x1 = y[:, :half]
            x2 = y[:, half:]
            o_ref[:, lo:lo + half] = (x1 * cos - x2 * sin).astype(BF16)
            o_ref[:, lo + half:lo + RET_DK] = (x1 * sin + x2 * cos).astype(BF16)

    @pl.when(j >= 2 * rope_blocks)
    def _():
        o_ref[...] = _dot(xn_ref[...], w_ref[...]).astype(BF16)


def _ret_in(x, g, w, cos, sin, *, tm, tn, cast=()):
    m, d = x.shape
    n = w.shape[1]
    tab_blocks = cos.shape[0] // tm
    n_i, n_j = m // tm, n // tn
    in_specs = [
        pl.BlockSpec((tm, d), lambda i, j: (i, 0)),
        pl.BlockSpec((1, d), lambda i, j: (0, 0)),
        pl.BlockSpec((d, tn), lambda i, j: (0, j)),
        pl.BlockSpec((tm, RET_DK // 2), lambda i, j: (i % tab_blocks, 0)),
        pl.BlockSpec((tm, RET_DK // 2), lambda i, j: (i % tab_blocks, 0)),
    ]
    out_shape = [jax.ShapeDtypeStruct((m, n), BF16)]
    out_specs = [pl.BlockSpec((tm, tn), lambda i, j: (i, j))]
    cast_steps = []
    for wf, layer in cast:
        _, rows, cols = wf.shape
        slab = rows // n_i
        assert slab * n_i == rows and slab % 16 == 0
        steps = max(s for s in range(1, n_j + 1) if cols % (128 * s) == 0)
        cast_steps.append(steps)
        in_specs.append(pl.BlockSpec((None, slab, cols // steps),
                                     lambda i, j, layer=layer, steps=steps: (layer, i, jnp.minimum(j, steps - 1))))
        out_shape.append(jax.ShapeDtypeStruct((1, rows, cols), BF16))
        out_specs.append(pl.BlockSpec((None, slab, cols // steps),
                                      lambda i, j, steps=steps: (0, i, jnp.minimum(j, steps - 1))))
    kern = functools.partial(_ret_in_kernel, rope_blocks=(RET_HEADS * RET_DK) // tn,
                             heads_per_block=tn // RET_DK, cast_steps=tuple(cast_steps))
    outs = pl.pallas_call(
        kern,
        out_shape=out_shape,
        grid=(n_i, n_j),
        in_specs=in_specs,
        out_specs=out_specs,
        scratch_shapes=[pltpu.VMEM((tm, d), BF16)],
        compiler_params=_params("parallel", "arbitrary"),
        name="ret_in",
    )(x, g.reshape(1, d), w, cos, sin, *[wf for wf, _ in cast])
    return outs[0], tuple(outs[1:])


def _ret_core_kernel(lg_ref, q_ref, k_ref, v_ref, g_ref, gain_ref, *rest, chunk, heads, has_state):
    if has_state:
        s0_ref, y_ref, s_ref = rest
    else:
        y_ref, s_ref = rest
    head0 = pl.program_id(1) * heads
    n_chunks = q_ref.shape[0] // chunk
    row = lax.broadcasted_iota(jnp.int32, (chunk, chunk), 0)
    col = lax.broadcasted_iota(jnp.int32, (chunk, chunk), 1)
    diff = (row - col).astype(F32)
    n = lax.broadcasted_iota(jnp.int32, (chunk, 1), 0).astype(F32)

    def head_consts(hh):
        lg = lg_ref[head0 + hh]
        decay = jnp.where(diff >= 0, jnp.exp(lg * jnp.maximum(diff, 0.0)), 0.0)
        cross = jnp.exp(lg * (n + 1.0))
        kdec = jnp.exp(lg * (chunk - 1.0 - n))
        carry = jnp.exp(lg * jnp.full((1, RET_DV), float(chunk), F32))
        return decay, cross, kdec, carry

    def mix(hh, c, consts):
        decay, cross, kdec, carry = consts
        rows = slice(c * chunk, (c + 1) * chunk)
        q = q_ref[rows, hh * RET_DK:(hh + 1) * RET_DK]
        k = k_ref[rows, hh * RET_DK:(hh + 1) * RET_DK]
        v = v_ref[rows, hh * RET_DV:(hh + 1) * RET_DV]
        inner = (_dot_nt(q, k) * decay).astype(BF16)
        if c > 0:
            s_old = s_ref[0, hh]
        elif has_state:
            s_old = s0_ref[0, hh]
        else:
            s_old = None
        kd = (k.astype(F32) * kdec).astype(BF16)
        o = _dot(inner, v)
        s_new = _dot_tn(kd, v)
        if s_old is not None:
            o = o + cross * _dot(q, s_old.astype(BF16))
            s_new = carry * s_old + s_new
        s_ref[0, hh] = s_new
        return o

    def norm_gate(hh, c, o):
        rows = slice(c * chunk, (c + 1) * chunk)
        cols = slice(hh * RET_DV, (hh + 1) * RET_DV)
        mu = jnp.mean(o, axis=-1, keepdims=True)
        dev = o - mu
        var = jnp.mean(dev * dev, axis=-1, keepdims=True)
        of = dev * lax.rsqrt(var + EPS) * gain_ref[:, cols]
        y_ref[rows, cols] = (_silu(g_ref[rows, cols].astype(F32)) * of).astype(BF16)

    consts = None
    pending = None
    for hh in range(heads):
        for c in range(n_chunks):
            if c == 0:
                consts = head_consts(hh)
            o = mix(hh, c, consts)
            if pending is not None:
                norm_gate(*pending)
            pending = (hh, c, o)
    norm_gate(*pending)


def _ret_core(qkvg, log_gamma, gain, state, *, batch, seq, chunk, heads):
    m = qkvg.shape[0]
    hk = RET_HEADS * RET_DK
    hv = RET_HEADS * RET_DV
    wk = heads * RET_DK
    wv = heads * RET_DV
    kblk = hk // wk
    vblk = 2 * hk // wv
    gblk = vblk + hv // wv
    has_state = state is not None
    in_specs = [
        pl.BlockSpec(memory_space=pltpu.SMEM),
        pl.BlockSpec((seq, wk), lambda b, g: (b, g)),
        pl.BlockSpec((seq, wk), lambda b, g: (b, kblk + g)),
        pl.BlockSpec((seq, wv), lambda b, g: (b, vblk + g)),
        pl.BlockSpec((seq, wv), lambda b, g: (b, gblk + g)),
        pl.BlockSpec((1, wv), lambda b, g: (0, g)),
    ]
    args = [log_gamma, qkvg, qkvg, qkvg, qkvg, gain.reshape(1, hv)]
    state_spec = pl.BlockSpec((1, heads, RET_DK, RET_DV), lambda b, g: (b, g, 0, 0))
    if has_state:
        in_specs.append(state_spec)
        args.append(state)
    kern = functools.partial(_ret_core_kernel, chunk=chunk, heads=heads, has_state=has_state)
    return pl.pallas_call(
        kern,
        out_shape=(jax.ShapeDtypeStruct((m, hv), BF16),
                   jax.ShapeDtypeStruct((batch, RET_HEADS, RET_DK, RET_DV), F32)),
        grid=(batch, RET_HEADS // heads),
        in_specs=in_specs,
        out_specs=(pl.BlockSpec((seq, wv), lambda b, g: (b, g)), state_spec),
        compiler_params=_params("parallel", "parallel"),
        name="ret_core",
    )(*args)


def _mm_res_kernel(a_ref, w_ref, r_ref, *rest):
    n_cast = (len(rest) - 1) // 2
    o_ref = rest[n_cast]
    o_ref[...] = r_ref[...] + _dot(a_ref[...], w_ref[...])
    for src, dst in zip(rest[:n_cast], rest[n_cast + 1:]):
        dst[...] = src[...].astype(BF16)


def _mm_res(a, w, res, *, tm, cast=()):
    m, k = a.shape
    n = w.shape[1]
    steps = m // tm
    in_specs = [
        pl.BlockSpec((tm, k), lambda i: (i, 0)),
        pl.BlockSpec((k, n), lambda i: (0, 0), pipeline_mode=pl.Buffered(1)),
        pl.BlockSpec((tm, n), lambda i: (i, 0)),
    ]
    out_shape = [jax.ShapeDtypeStruct((m, n), F32)]
    out_specs = [pl.BlockSpec((tm, n), lambda i: (i, 0))]
    for wf, layer in cast:
        _, rows, cols = wf.shape
        slab = rows // steps
        assert slab * steps == rows and slab % 16 == 0
        in_specs.append(pl.BlockSpec((None, slab, cols), lambda i, layer=layer: (layer, i, 0)))
        out_shape.append(jax.ShapeDtypeStruct((1, rows, cols), BF16))
        out_specs.append(pl.BlockSpec((None, slab, cols), lambda i: (0, i, 0)))
    outs = pl.pallas_call(
        _mm_res_kernel,
        out_shape=out_shape,
        grid=(steps,),
        in_specs=in_specs,
        out_specs=out_specs,
        compiler_params=_params("parallel"),
        name="mm_res",
    )(a, w, res, *[wf for wf, _ in cast])
    return outs[0], tuple(outs[1:])


def _latent_kernel(x_ref, g_ref, w_ref, gl_ref, cos_ref, sin_ref, c_ref, kr_ref, *, row_chunks):
    half = QK_ROPE // 2
    tc = x_ref.shape[0] // row_chunks

    def norm(c):
        x = x_ref[c * tc:(c + 1) * tc, :]
        return (x * _rms_scale(x) * g_ref[...]).astype(BF16)

    def project(c, xn):
        rows = slice(c * tc, (c + 1) * tc)
        y = _dot(xn, w_ref[...])
        cl = y[:, :KV_LORA]
        c_ref[rows, :] = cl * _rms_scale(cl) * gl_ref[...]
        x1 = y[:, KV_LORA:KV_LORA + half]
        x2 = y[:, KV_LORA + half:KV_LORA + QK_ROPE]
        cos = cos_ref[rows, :]
        sin = sin_ref[rows, :]
        kr_ref[rows, :] = jnp.concatenate([x1 * cos - x2 * sin, x1 * sin + x2 * cos], axis=-1)

    xn = norm(0)
    for c in range(row_chunks):
        xn_next = norm(c + 1) if c + 1 < row_chunks else None
        project(c, xn)
        xn = xn_next


def _latent(x, g, w, g_lat, cos, sin, *, tm):
    m, d = x.shape
    n = w.shape[1]
    half = QK_ROPE // 2
    tab_blocks = cos.shape[0] // tm
    return pl.pallas_call(
        functools.partial(_latent_kernel, row_chunks=tm // 128),
        out_shape=(jax.ShapeDtypeStruct((m, KV_LORA), F32),
                   jax.ShapeDtypeStruct((m, QK_ROPE), F32)),
        grid=(m // tm,),
        in_specs=[
            pl.BlockSpec((tm, d), lambda i: (i, 0)),
            pl.BlockSpec((1, d), lambda i: (0, 0)),
            pl.BlockSpec((d, n), lambda i: (0, 0)),
            pl.BlockSpec((1, KV_LORA), lambda i: (0, 0)),
            pl.BlockSpec((tm, half), lambda i: (i % tab_blocks, 0)),
            pl.BlockSpec((tm, half), lambda i: (i % tab_blocks, 0)),
        ],
        out_specs=(pl.BlockSpec((tm, KV_LORA), lambda i: (i, 0)),
                   pl.BlockSpec((tm, QK_ROPE), lambda i: (i, 0))),
        compiler_params=_params("parallel"),
        name="latent",
    )(x, g.reshape(1, d), w, g_lat.reshape(1, KV_LORA), cos, sin)


def _kv_up_body(seg_refs, seg_rows, wk_ref, wv_ref, gk_ref, k_ref, v_ref):
    refs = seg_refs
    g_nope = gk_ref[:, :QK_NOPE]
    g_rope = gk_ref[:, QK_NOPE:]
    lo = 0
    for s, rows in enumerate(seg_rows):
        c = refs[2 * s][...].astype(BF16)
        kr = refs[2 * s + 1][...]
        kn = _dot(c, wk_ref[...])
        v_ref[lo:lo + rows, :] = _dot(c, wv_ref[...]).astype(BF16)
        ss_rope = jnp.sum(kr * kr, axis=-1, keepdims=True)
        for h in range(MLA_HEADS):
            knh = kn[:, h * QK_NOPE:(h + 1) * QK_NOPE]
            ss = jnp.sum(knh * knh, axis=-1, keepdims=True) + ss_rope
            r = lax.rsqrt(ss * (1.0 / QK_HEAD) + EPS)
            k_ref[h, lo:lo + rows, :QK_NOPE] = (knh * r * g_nope).astype(BF16)
            k_ref[h, lo:lo + rows, QK_NOPE:] = (kr * r * g_rope).astype(BF16)
        lo += rows


def _kv_up_kernel(*refs, seg_rows):
    n_seg = len(seg_rows)
    wk_ref, wv_ref, gk_ref, k_ref, v_ref = refs[2 * n_seg:]
    _kv_up_body(refs[:2 * n_seg], seg_rows, wk_ref, wv_ref, gk_ref, k_ref, v_ref)


def _kv_up(segments, wk, wv, gk, *, steps):
    seg_rows = tuple(c.shape[0] // steps for c, _ in segments)
    tm = sum(seg_rows)
    m = steps * tm
    in_specs, args = [], []
    for (c, kr), rows in zip(segments, seg_rows):
        in_specs += [pl.BlockSpec((rows, KV_LORA), lambda i: (i, 0)),
                     pl.BlockSpec((rows, QK_ROPE), lambda i: (i, 0))]
        args += [c, kr]
    in_specs += [pl.BlockSpec(wk.shape, lambda i: (0, 0)),
                 pl.BlockSpec(wv.shape, lambda i: (0, 0)),
                 pl.BlockSpec((1, QK_HEAD), lambda i: (0, 0))]
    return pl.pallas_call(
        functools.partial(_kv_up_kernel, seg_rows=seg_rows),
        out_shape=(jax.ShapeDtypeStruct((MLA_HEADS, m, QK_HEAD), BF16),
                   jax.ShapeDtypeStruct((m, MLA_HEADS * V_HEAD), BF16)),
        grid=(steps,),
        in_specs=in_specs,
        out_specs=(pl.BlockSpec((MLA_HEADS, tm, QK_HEAD), lambda i: (0, i, 0)),
                   pl.BlockSpec((tm, MLA_HEADS * V_HEAD), lambda i: (i, 0))),
        compiler_params=_params("parallel"),
        name="kv_up",
    )(*args, wk, wv, gk.reshape(1, QK_HEAD))


Q_SLOT = 2 * QK_NOPE


def _q_proj_kernel(x_ref, g_ref, wdq_ref, gl_ref, wuq_ref, cos_ref, sin_ref, gq_ref, q_ref, *, row_chunks):
    g_nope = gq_ref[:, :QK_NOPE]
    g_rope = gq_ref[:, QK_NOPE:]
    tc = x_ref.shape[0] // row_chunks

    def project(c):
        rows = slice(c * tc, (c + 1) * tc)
        x = x_ref[rows, :]
        hn = (x * _rms_scale(x) * g_ref[...]).astype(BF16)
        ql = _dot(hn, wdq_ref[...])
        qn = (ql * _rms_scale(ql) * gl_ref[...]).astype(BF16)
        return _dot(qn, wuq_ref[...])

    def per_head(c, q):
        rows = slice(c * tc, (c + 1) * tc)
        cos = cos_ref[rows, :]
        sin = sin_ref[rows, :]
        for h in range(MLA_HEADS):
            nh = q[:, h * Q_SLOT:h * Q_SLOT + QK_NOPE]
            t = q[:, h * Q_SLOT + QK_NOPE:(h + 1) * Q_SLOT]
            rp = t * cos + pltpu.roll(t, QK_ROPE, 1) * sin
            ss = jnp.sum(nh * nh + rp * rp, axis=-1, keepdims=True)
            r = lax.rsqrt(ss * (1.0 / QK_HEAD) + EPS) * (QK_HEAD ** -0.5 * LOG2E)
            q_ref[h, rows, :QK_NOPE] = (nh * r * g_nope).astype(BF16)
            q_ref[h, rows, QK_NOPE:] = (rp[:, :QK_ROPE] * r * g_rope).astype(BF16)

    q = project(0)
    for c in range(row_chunks):
        q_next = project(c + 1) if c + 1 < row_chunks else None
        per_head(c, q)
        q = q_next


def _q_proj(x, g, wdq, g_lat, wuq, cos, sin, gq, *, tm):
    m, d = x.shape
    tab_blocks = cos.shape[0] // tm
    tw = cos.shape[1]
    return pl.pallas_call(
        functools.partial(_q_proj_kernel, row_chunks=tm // 256),
        out_shape=jax.ShapeDtypeStruct((MLA_HEADS, m, QK_HEAD), BF16),
        grid=(m // tm,),
        in_specs=[
            pl.BlockSpec((tm, d), lambda i: (i, 0)),
            pl.BlockSpec((1, d), lambda i: (0, 0)),
            pl.BlockSpec(wdq.shape, lambda i: (0, 0)),
            pl.BlockSpec((1, Q_LORA), lambda i: (0, 0)),
            pl.BlockSpec(wuq.shape, lambda i: (0, 0)),
            pl.BlockSpec((tm, tw), lambda i: (i % tab_blocks, 0)),
            pl.BlockSpec((tm, tw), lambda i: (i % tab_blocks, 0)),
            pl.BlockSpec((1, QK_HEAD), lambda i: (0, 0)),
        ],
        out_specs=pl.BlockSpec((MLA_HEADS, tm, QK_HEAD), lambda i: (0, i, 0)),
        compiler_params=_params("parallel"),
        name="q_proj",
    )(x, g.reshape(1, d), wdq, g_lat.reshape(1, Q_LORA), wuq, cos, sin, gq.reshape(1, QK_HEAD))


def _attn_causal_kernel(q_ref, k_ref, v_ref, o_ref, *, heads, tq):
    n_q = q_ref.shape[1] // tq
    hq = tq // 2
    krow = lax.broadcasted_iota(jnp.int32, (hq, tq), 0) // CHUNK
    qcol = lax.broadcasted_iota(jnp.int32, (hq, tq), 1) // CHUNK
    keep_a = krow <= qcol
    keep_b = keep_a[:, :hq]

    def right_half(full, fn, part):
        return jnp.concatenate([full[:, :hq], fn(full[:, hq:], part)], axis=1)

    def scores(h, qi):
        lo = qi * tq
        q = q_ref[h, lo:lo + tq, :]
        st_a = jnp.where(keep_a, _dot_nt(k_ref[h, lo:lo + hq, :], q), NEG_INF)
        st_b = jnp.where(keep_b, _dot_nt(k_ref[h, lo + hq:lo + tq, :], q[hq:, :]), NEG_INF)
        st_p = _dot_nt(k_ref[h, :lo, :], q) if qi > 0 else None
        return st_a, st_b, st_p

    def softmax(st):
        st_a, st_b, st_p = st
        m = jnp.max(st_a, axis=0, keepdims=True)
        if st_p is not None:
            m = jnp.maximum(m, jnp.max(st_p, axis=0, keepdims=True))
        m = right_half(m, jnp.maximum, jnp.max(st_b, axis=0, keepdims=True))
        pt_a = jnp.exp2(st_a - m)
        pt_b = jnp.exp2(st_b - m[:, hq:])
        l = jnp.sum(pt_a, axis=0, keepdims=True)
        pt_p = None
        if st_p is not None:
            pt_p = jnp.exp2(st_p - m)
            l = l + jnp.sum(pt_p, axis=0, keepdims=True)
            pt_p = pt_p.astype(BF16)
        l = right_half(l, jnp.add, jnp.sum(pt_b, axis=0, keepdims=True))
        return pt_a.astype(BF16), pt_b.astype(BF16), pt_p, l

    def values(h, qi, p):
        pt_a, pt_b, pt_p, l = p
        lo = qi * tq
        cols = slice(h * V_HEAD, (h + 1) * V_HEAD)
        acc_t = _dot_tn(v_ref[lo:lo + hq, cols], pt_a)
        if pt_p is not None:
            acc_t = acc_t + _dot_tn(v_ref[:lo, cols], pt_p)
        acc_t = right_half(acc_t, jnp.add, _dot_tn(v_ref[lo + hq:lo + tq, cols], pt_b))
        o_ref[lo:lo + tq, cols] = (acc_t / l).T.astype(BF16)

    work = [(h, qi) for h in range(heads) for qi in range(n_q)]
    s_q, p_q = {}, {}
    for step in range(len(work) + 2):
        if step < len(work):
            s_q[step] = scores(*work[step])
        if 0 <= step - 1 < len(work):
            p_q[step - 1] = softmax(s_q.pop(step - 1))
        if 0 <= step - 2 < len(work):
            values(*work[step - 2], p_q.pop(step - 2))


def _kv_attn_kernel(*refs, seg_rows):
    n_seg = len(seg_rows)
    wk_ref, wv_ref, gk_ref, q_ref, o_ref, k_scr, v_scr = refs[2 * n_seg:]
    g_nope = gk_ref[:, :QK_NOPE]
    g_rope = gk_ref[:, QK_NOPE:]

    group = 4
    width = group * QK_NOPE
    segs, lo = [], 0
    for s, rows in enumerate(seg_rows):
        c = refs[2 * s][...].astype(BF16)
        kr = refs[2 * s + 1][...]
        segs.append((lo, rows, c, kr, jnp.sum(kr * kr, axis=-1, keepdims=True)))
        lo += rows

    def project(g):
        cols = slice(g * width, (g + 1) * width)
        kn = []
        for lo, rows, c, _, _ in segs:
            v_scr[lo:lo + rows, cols] = _dot(c, wv_ref[:, cols]).astype(BF16)
            kn.append(_dot(c, wk_ref[:, cols]))
        return kn

    kn_groups = {}

    def key_norm(h):
        g, hh = divmod(h, group)
        for (lo, rows, _, kr, ss_rope), kn in zip(segs, kn_groups[g]):
            knh = kn[:, hh * QK_NOPE:(hh + 1) * QK_NOPE]
            ss = jnp.sum(knh * knh, axis=-1, keepdims=True) + ss_rope
            r = lax.rsqrt(ss * (1.0 / QK_HEAD) + EPS)
            k_scr[h, lo:lo + rows, :QK_NOPE] = (knh * r * g_nope).astype(BF16)
            k_scr[h, lo:lo + rows, QK_NOPE:] = (kr * r * g_rope).astype(BF16)

    def scores(h):
        return _dot_nt(q_ref[h], k_scr[h])

    def softmax(s):
        m = jnp.max(s, axis=-1, keepdims=True)
        p = jnp.exp2(s - m)
        return p.astype(BF16), jnp.sum(p, axis=-1, keepdims=True)

    def values(h, pl_):
        p, l = pl_
        cols = slice(h * V_HEAD, (h + 1) * V_HEAD)
        o_ref[:, cols] = (_dot(p, v_scr[:, cols]) / l).astype(BF16)

    s_q, p_q = {}, {}
    kn_groups[0] = project(0)
    for step in range(MLA_HEADS + 3):
        if step % group == 0 and step // group + 1 < MLA_HEADS // group:
            kn_groups[step // group + 1] = project(step // group + 1)
        if step < MLA_HEADS:
            key_norm(step)
        if 0 <= step - 1 < MLA_HEADS:
            s_q[step - 1] = scores(step - 1)
        if 0 <= step - 2 < MLA_HEADS:
            p_q[step - 2] = softmax(s_q.pop(step - 2))
        if 0 <= step - 3 < MLA_HEADS:
            values(step - 3, p_q.pop(step - 3))


def _kv_attention(segments, wk, wv, gk, q, *, batch):
    seg_rows = tuple(c.shape[0] // batch for c, _ in segments)
    t_kv = sum(seg_rows)
    t_q = q.shape[1] // batch
    in_specs, args = [], []
    for (c, kr), rows in zip(segments, seg_rows):
        in_specs += [pl.BlockSpec((rows, KV_LORA), lambda b: (b, 0)),
                     pl.BlockSpec((rows, QK_ROPE), lambda b: (b, 0))]
        args += [c, kr]
    in_specs += [pl.BlockSpec(wk.shape, lambda b: (0, 0)),
                 pl.BlockSpec(wv.shape, lambda b: (0, 0)),
                 pl.BlockSpec((1, QK_HEAD), lambda b: (0, 0)),
                 pl.BlockSpec((MLA_HEADS, t_q, QK_HEAD), lambda b: (0, b, 0))]
    return pl.pallas_call(
        functools.partial(_kv_attn_kernel, seg_rows=seg_rows),
        out_shape=jax.ShapeDtypeStruct((batch * t_q, MLA_HEADS * V_HEAD), BF16),
        grid=(batch,),
        in_specs=in_specs,
        out_specs=pl.BlockSpec((t_q, MLA_HEADS * V_HEAD), lambda b: (b, 0)),
        scratch_shapes=[pltpu.VMEM((MLA_HEADS, t_kv, QK_HEAD), BF16),
                        pltpu.VMEM((t_kv, MLA_HEADS * V_HEAD), BF16)],
        compiler_params=_params("parallel"),
        name="kv_attention",
    )(*args, wk, wv, gk.reshape(1, QK_HEAD), q)


def _attention(q, k, v, *, batch, t_q, t_kv, heads, tq):
    hg = MLA_HEADS // heads
    kern = functools.partial(_attn_causal_kernel, heads=heads, tq=tq)
    return pl.pallas_call(
        kern,
        out_shape=jax.ShapeDtypeStruct((batch * t_q, MLA_HEADS * V_HEAD), BF16),
        grid=(batch, hg),
        in_specs=[
            pl.BlockSpec((heads, t_q, QK_HEAD), lambda b, g: (g, b, 0)),
            pl.BlockSpec((heads, t_kv, QK_HEAD), lambda b, g: (g, b, 0)),
            pl.BlockSpec((t_kv, heads * V_HEAD), lambda b, g: (b, g)),
        ],
        out_specs=pl.BlockSpec((t_q, heads * V_HEAD), lambda b, g: (b, g)),
        compiler_params=_params("parallel", "parallel"),
        name="attention",
    )(q, k, v)


def _rope_tables(pos, half, reps, rows):
    inv = jnp.power(ROPE_BASE, -jnp.arange(half, dtype=F32) / half)
    ang = pos.astype(F32)[:, None] * inv[None, :]
    cos = jnp.tile(jnp.cos(ang), (max(rows // pos.shape[0], 1), reps))
    sin = jnp.tile(jnp.sin(ang), (max(rows // pos.shape[0], 1), reps))
    return cos, sin


def _mm_res_and_ffn2(a, w, h, layer, p):
    key = ("ffn2_bf16", layer)
    if key not in p:
        h, p[key] = _mm_res(a, w, h, tm=512, cast=((p["ffn2_w_in"], layer), (p["ffn2_w_out"], layer)))
    else:
        h, _ = _mm_res(a, w, h, tm=512)
    w_in, w_out = p[key]
    return _ffn(h, p["ffn2_norm"][layer], w_in, w_out, 0, tm=FFN_TM, tf=FFN_TF)


def _trunk(x, pos, ret_state, ckv_past, krope_past, p):
    batch, seq, d = x.shape
    m = batch * seq
    tm = 1024
    h = x.reshape(m, d)

    h = _ffn(h, p["ffn1_norm"][0], p["ffn1_w_in_l0"], p["ffn1_w_out_l0"], 0, tm=FFN_TM, tf=FFN_TF)
    cos, sin = _rope_tables(pos, RET_DK // 2, 1, tm)
    key = "cast_by_ret_in"
    cast = () if key in p else ((p["ffn1_w_in"], 1), (p["ffn1_w_out"], 1), (p["ret_w_out"], 0), (p["w_o"], 0))
    qkvg, cast_out = _ret_in(h, p["mix_norm"][0], p["ret_w_in"][0], cos, sin, tm=tm, tn=2048, cast=cast)
    if cast:
        p[key] = cast_out
    ffn1_l1_w_in, ffn1_l1_w_out, ret_w_out, w_o = p[key]
    log_gamma = jnp.log1p(-jnp.exp2(-5.0 - jnp.arange(RET_HEADS, dtype=F32)))
    chunk = min(seq, 256)
    y, s_fin = _ret_core(qkvg, log_gamma, p["ret_gn_gain"][0],
                         None if ret_state is None else ret_state[0],
                         batch=batch, seq=seq, chunk=chunk,
                         heads=max(1, min(RET_HEADS, 512 // seq)))
    h = _mm_res_and_ffn2(y, ret_w_out[0], h, 0, p)

    cos, sin = _rope_tables(pos, QK_ROPE // 2, 1, tm)
    c_new, kr_new = _latent(h, p["kv_norm"], p["w_dkv"], p["kv_lat_norm"], cos, sin, tm=tm)
    if ckv_past is None:
        k_sh, v_sh = _kv_up([(c_new, kr_new)], p["w_uk"], p["w_uv"], p["k_norm"], steps=m // tm)
    else:
        past = ckv_past.shape[1]
        segments = [(ckv_past.reshape(batch * past, KV_LORA), krope_past.reshape(batch * past, QK_ROPE)),
                    (c_new, kr_new)]

    h = _ffn(h, p["ffn1_norm"][1], ffn1_l1_w_in, ffn1_l1_w_out, 0, tm=FFN_TM, tf=FFN_TF)
    cos, sin = _rope_tables(pos, QK_ROPE // 2, 1, tm)
    zeros = jnp.zeros_like(cos)
    cos = jnp.concatenate([cos, cos, zeros, zeros], axis=-1)
    sin = jnp.concatenate([-sin, sin, zeros, zeros], axis=-1)
    q = _q_proj(h, p["mix_norm"][1], p["w_dq"][0], p["q_lat_norm"][0], p["w_uq"][0], cos, sin,
                p["q_norm"][0], tm=tm)
    if ckv_past is None:
        o = _attention(q, k_sh, v_sh, batch=batch, t_q=seq, t_kv=seq, heads=2, tq=512)
    else:
        o = _kv_attention(segments, p["w_uk"], p["w_uv"], p["k_norm"], q, batch=batch)
    h = _mm_res_and_ffn2(o, w_o[0], h, 1, p)

    return (h.reshape(batch, seq, d), s_fin[None],
            c_new.reshape(batch, seq, KV_LORA), kr_new.reshape(batch, seq, QK_ROPE))


def kernel(x_prompt, x_sample, state_ret, cache_ckv, cache_krope, ffn1_norm, ffn1_w_in, ffn1_w_out, mix_norm, ffn2_norm, ffn2_w_in, ffn2_w_out, ret_w_in, ret_gn_gain, ret_w_out, kv_norm, w_dkv, kv_lat_norm, w_ukv, k_norm, w_dq, q_lat_norm, w_uq, q_norm, w_o):
    bf = lambda w: w.astype(BF16)
    w_ukv_h = w_ukv.reshape(KV_LORA, MLA_HEADS, QK_NOPE + V_HEAD)
    w_uk = w_ukv_h[:, :, :QK_NOPE].reshape(KV_LORA, MLA_HEADS * QK_NOPE)
    w_uv = w_ukv_h[:, :, QK_NOPE:].reshape(KV_LORA, MLA_HEADS * V_HEAD)
    half = QK_ROPE // 2
    w_uq_h = w_uq.reshape(w_uq.shape[0], Q_LORA, MLA_HEADS, QK_HEAD)
    w_lo = w_uq_h[..., QK_NOPE:QK_NOPE + half]
    w_hi = w_uq_h[..., QK_NOPE + half:]
    w_uq_p = jnp.concatenate([w_uq_h[..., :QK_NOPE], w_lo, w_hi, w_hi, w_lo], axis=-1)
    w_uq_p = w_uq_p.reshape(w_uq.shape[0], Q_LORA, MLA_HEADS * Q_SLOT)
    p = dict(ffn1_norm=ffn1_norm, ffn1_w_in=ffn1_w_in, ffn1_w_out=ffn1_w_out,
             ffn1_w_in_l0=bf(ffn1_w_in[:1]), ffn1_w_out_l0=bf(ffn1_w_out[:1]),
             mix_norm=mix_norm, ffn2_norm=ffn2_norm, ffn2_w_in=ffn2_w_in, ffn2_w_out=ffn2_w_out,
             ret_w_in=bf(ret_w_in), ret_gn_gain=ret_gn_gain, ret_w_out=ret_w_out,
             kv_norm=kv_norm, w_dkv=bf(w_dkv), kv_lat_norm=kv_lat_norm,
             w_uk=bf(w_uk), w_uv=bf(w_uv), k_norm=k_norm,
             w_dq=bf(w_dq), q_lat_norm=q_lat_norm, w_uq=bf(w_uq_p), q_norm=q_norm, w_o=w_o)
    pos_prompt = jnp.arange(x_prompt.shape[1])
    past = cache_ckv.shape[1]
    pos_sample = past + jnp.arange(x_sample.shape[1])
    y_p, ret_p, ckv_p, kr_p = _trunk(x_prompt, pos_prompt, None, None, None, p)
    y_s, ret_s, ckv_s, kr_s = _trunk(x_sample, pos_sample, state_ret, cache_ckv, cache_krope, p)
    return (y_p, y_s, ret_p, ckv_p, kr_p, ret_s, ckv_s, kr_s)
```

```python
import functools

import jax
import jax.numpy as jnp
from jax import lax
from jax.experimental import pallas as pl
from jax.experimental.pallas import tpu as pltpu

F32 = jnp.float32
BF16 = jnp.bfloat16

D_MODEL = 2048
CHUNK = 64
D_FF = 5632
RET_HEADS = 8
RET_DK = D_MODEL // RET_HEADS
RET_DV = 2 * D_MODEL // RET_HEADS
MLA_HEADS = 16
QK_NOPE = 128
QK_ROPE = 64
QK_HEAD = QK_NOPE + QK_ROPE
V_HEAD = 128
KV_LORA = 512
Q_LORA = 512
ROPE_BASE = 10000.0
EPS = 1e-6
NEG_INF = -1e30
LOG2E = 1.4426950408889634

VMEM_LIMIT_BYTES = 60 * 1024 * 1024

FFN_TM = 1024
FFN_TF = 512
FFN_NORM_CHUNKS = 4
FFN_DMA_BLOCKS = 2


def _params(*semantics):
    return pltpu.CompilerParams(dimension_semantics=semantics,
                                vmem_limit_bytes=VMEM_LIMIT_BYTES)


def _rms_scale(x):
    return lax.rsqrt(jnp.mean(x * x, axis=-1, keepdims=True) + EPS)


def _dot(a, b):
    return jnp.dot(a, b, preferred_element_type=F32)


def _dot_nt(a, b):
    return lax.dot_general(a, b, (((1,), (1,)), ((), ())), preferred_element_type=F32)


def _dot_tn(a, b):
    return lax.dot_general(a, b, (((0,), (0,)), ((), ())), preferred_element_type=F32)


def _silu(x):
    return x * jax.nn.sigmoid(x)


def _ffn_kernel(*refs, layer, tf, cast_pieces):
    if cast_pieces:
        (g_ref, x_hbm, win_hbm, wout_hbm, cast_src, o_hbm, cast_dst, acc, xn_ref, wa_buf, wb_buf, wo_buf,
         wsem, xsem, osem, stage_in, stage_out, csem_in, csem_out) = refs
    else:
        g_ref, x_hbm, win_hbm, wout_hbm, o_hbm, acc, xn_ref, wa_buf, wb_buf, wo_buf, wsem, xsem, osem = refs
    i = pl.program_id(0)
    n_tiles = pl.num_programs(0)
    tm = acc.shape[1]
    dma_cols = FFN_DMA_BLOCKS * tf
    n_pos = 1 + (D_FF - tf) // dma_cols
    last = n_pos - 1
    a_slot = i % 2

    def weight_copies(pos, slot):
        col = pl.multiple_of(jnp.maximum(pos * dma_cols - (dma_cols - tf), 0), tf)
        return (
            pltpu.make_async_copy(win_hbm.at[layer, :, pl.ds(col, dma_cols)], wa_buf.at[slot], wsem.at[slot, 0]),
            pltpu.make_async_copy(win_hbm.at[layer, :, pl.ds(D_FF + col, dma_cols)], wb_buf.at[slot], wsem.at[slot, 1]),
            pltpu.make_async_copy(wout_hbm.at[layer, pl.ds(col, dma_cols), :], wo_buf.at[slot], wsem.at[slot, 2]),
        )

    def x_copy(tile, slot):
        src = x_hbm.at[pl.ds(pl.multiple_of(tile * tm, tm), tm), :]
        return pltpu.make_async_copy(src, acc.at[slot], xsem.at[slot])

    def out_copy(tile, slot):
        dst = o_hbm.at[pl.ds(pl.multiple_of(tile * tm, tm), tm), :]
        return pltpu.make_async_copy(acc.at[slot], dst, osem.at[slot])

    def update(xn, slot, cols):
        a = _dot(xn, wa_buf[slot, :, :cols])
        b = _dot(xn, wb_buf[slot, :, :cols])
        h = (_silu(a) * (0.5 * b)).astype(BF16)
        return _dot(h, wo_buf[slot, :cols, :])

    @pl.when(i == 0)
    def _():
        x_copy(0, 0).start()
        for cp in weight_copies(0, 0):
            cp.start()

    x_copy(i, a_slot).wait()

    def sync(pos, w_slot):
        for cp in weight_copies(pos, w_slot):
            cp.wait()

        nxt = jnp.where(pos == last, 0, pos + 1)

        @pl.when(jnp.logical_or(pos < last, i + 1 < n_tiles))
        def _():
            for cp in weight_copies(nxt, 1 - w_slot):
                cp.start()

        @pl.when(pos == 1)
        def _():
            @pl.when(i > 0)
            def _():
                out_copy(i - 1, 1 - a_slot).wait()

            @pl.when(i + 1 < n_tiles)
            def _():
                x_copy(i + 1, 1 - a_slot).start()

        if cast_pieces:
            side_cast(pos)

    def cast_copies(n, slot):
        piece = stage_in.shape[1]
        rows = pl.ds(pl.multiple_of((i * cast_pieces + n) * piece, piece), piece)
        return (pltpu.make_async_copy(cast_src.at[0, rows, :], stage_in.at[slot], csem_in.at[slot]),
                pltpu.make_async_copy(stage_out.at[slot], cast_dst.at[0, rows, :], csem_out.at[slot]))

    def side_cast(pos):
        @pl.when(pos == 0)
        def _():
            cast_copies(0, 0)[0].start()

        @pl.when((pos >= 1) & (pos <= cast_pieces))
        def _():
            n = pos - 1
            slot = n % 2
            cast_copies(n, slot)[0].wait()

            @pl.when(n + 1 < cast_pieces)
            def _():
                cast_copies(n + 1, 1 - slot)[0].start()

            @pl.when(n >= 2)
            def _():
                cast_copies(n - 2, slot)[1].wait()

            stage_out[slot] = stage_in[slot].astype(BF16)
            cast_copies(n, slot)[1].start()

        @pl.when(pos == cast_pieces + 1)
        def _():
            for n in (cast_pieces - 2, cast_pieces - 1):
                cast_copies(n, n % 2)[1].wait()

    sync(0, 0)
    tc = tm // FFN_NORM_CHUNKS

    def norm(c):
        x = acc[a_slot, c * tc:(c + 1) * tc, :]
        xn = (x * _rms_scale(x) * g_ref[...]).astype(BF16)
        xn_ref[c * tc:(c + 1) * tc, :] = xn
        return x, xn

    cur = norm(0)
    for c in range(FFN_NORM_CHUNKS):
        nxt_c = norm(c + 1) if c + 1 < FFN_NORM_CHUNKS else None
        x, xn = cur
        acc[a_slot, c * tc:(c + 1) * tc, :] = x + update(xn, 0, tf)
        cur = nxt_c

    def position(pos, carry):
        w_slot = pos % 2
        sync(pos, w_slot)
        acc[a_slot] += update(xn_ref[...], w_slot, dma_cols)
        return carry

    lax.fori_loop(1, n_pos, position, 0)

    out_copy(i, a_slot).start()

    @pl.when(i + 1 == n_tiles)
    def _():
        out_copy(i, a_slot).wait()


def _ffn(x, g, w_in, w_out, layer, *, tm, tf, cast=None):
    m, d = x.shape
    n_tiles = m // tm
    dma_cols = FFN_DMA_BLOCKS * tf
    n_pos = 1 + (D_FF - tf) // dma_cols
    assert (D_FF - tf) % dma_cols == 0 and n_pos % 2 == 0
    in_specs = [pl.BlockSpec((1, d), lambda i: (0, 0))] + [pl.BlockSpec(memory_space=pl.ANY)] * 3
    out_shape = [jax.ShapeDtypeStruct((m, d), F32)]
    scratch = [
        pltpu.VMEM((2, tm, d), F32),
        pltpu.VMEM((tm, d), BF16),
        pltpu.VMEM((2, d, dma_cols), BF16),
        pltpu.VMEM((2, d, dma_cols), BF16),
        pltpu.VMEM((2, dma_cols, d), BF16),
        pltpu.SemaphoreType.DMA((2, 3)),
        pltpu.SemaphoreType.DMA((2,)),
        pltpu.SemaphoreType.DMA((2,)),
    ]
    args = [g.reshape(1, d), x, w_in, w_out]
    cast_pieces = 0
    if cast is not None:
        cast_pieces = n_pos - 2
        _, rows, cols = cast.shape
        piece = rows // (n_tiles * cast_pieces)
        assert piece * n_tiles * cast_pieces == rows and piece % 16 == 0 and cast_pieces >= 2
        in_specs.append(pl.BlockSpec(memory_space=pl.ANY))
        out_shape.append(jax.ShapeDtypeStruct(cast.shape, BF16))
        scratch += [pltpu.VMEM((2, piece, cols), F32), pltpu.VMEM((2, piece, cols), BF16),
                    pltpu.SemaphoreType.DMA((2,)), pltpu.SemaphoreType.DMA((2,))]
        args.append(cast)
    outs = pl.pallas_call(
        functools.partial(_ffn_kernel, layer=layer, tf=tf, cast_pieces=cast_pieces),
        out_shape=out_shape,
        grid=(n_tiles,),
        in_specs=in_specs,
        out_specs=[pl.BlockSpec(memory_space=pl.ANY)] * len(out_shape),
        scratch_shapes=scratch,
        compiler_params=_params("arbitrary"),
        name="ffn",
    )(*args)
    return outs[0] if cast is None else (outs[0], outs[1])


def _ret_in_kernel(x_ref, g_ref, w_ref, cos_ref, sin_ref, *rest, rope_blocks, heads_per_block, cast_steps):
    n_cast = len(cast_steps)
    o_ref = rest[n_cast]
    xn_ref = rest[-1]
    j = pl.program_id(1)

    for src, dst, steps in zip(rest[:n_cast], rest[n_cast + 1:-1], cast_steps):
        @pl.when(j < steps)
        def _(src=src, dst=dst):
            dst[...] = src[...].astype(BF16)

    @pl.when(j == 0)
    def _():
        x = x_ref[...]
        xn_ref[...] = (x * _rms_scale(x) * g_ref[...]).astype(BF16)

    half = RET_DK // 2

    @pl.when(j < 2 * rope_blocks)
    def _():
        scale = jnp.where(j < rope_blocks, RET_DK ** -0.5, 1.0).astype(F32)
        cos = cos_ref[...] * scale
        sin = sin_ref[...] * scale
        xn = xn_ref[...]
        for h in range(heads_per_block):
            lo = h * RET_DK
            y = _dot(xn, w_ref[:, lo:lo + RET_DK])
            x1 = y[:, :half]
            x2 = y[:, half:]
            o_ref[:, lo:lo + half] = (x1 * cos - x2 * sin).astype(BF16)
            o_ref[:, lo + half:lo + RET_DK] = (x1 * sin + x2 * cos).astype(BF16)

    @pl.when(j >= 2 * rope_blocks)
    def _():
        o_ref[...] = _dot(xn_ref[...], w_ref[...]).astype(BF16)


def _ret_in(x, g, w, cos, sin, *, tm, tn, cast=()):
    m, d = x.shape
    n = w.shape[1]
    tab_blocks = cos.shape[0] // tm
    n_i, n_j = m // tm, n // tn
    in_specs = [
        pl.BlockSpec((tm, d), lambda i, j: (i, 0)),
        pl.BlockSpec((1, d), lambda i, j: (0, 0)),
        pl.BlockSpec((d, tn), lambda i, j: (0, j)),
        pl.BlockSpec((tm, RET_DK // 2), lambda i, j: (i % tab_blocks, 0)),
        pl.BlockSpec((tm, RET_DK // 2), lambda i, j: (i % tab_blocks, 0)),
    ]
    out_shape = [jax.ShapeDtypeStruct((m, n), BF16)]
    out_specs = [pl.BlockSpec((tm, tn), lambda i, j: (i, j))]
    cast_steps = []
    for wf, layer in cast:
        _, rows, cols = wf.shape
        slab = rows // n_i
        assert slab * n_i == rows and slab % 16 == 0
        steps = max(s for s in range(1, n_j + 1) if cols % (128 * s) == 0)
        cast_steps.append(steps)
        in_specs.append(pl.BlockSpec((None, slab, cols // steps),
                                     lambda i, j, layer=layer, steps=steps: (layer, i, jnp.minimum(j, steps - 1))))
        out_shape.append(jax.ShapeDtypeStruct((1, rows, cols), BF16))
        out_specs.append(pl.BlockSpec((None, slab, cols // steps),
                                      lambda i, j, steps=steps: (0, i, jnp.minimum(j, steps - 1))))
    kern = functools.partial(_ret_in_kernel, rope_blocks=(RET_HEADS * RET_DK) // tn,
                             heads_per_block=tn // RET_DK, cast_steps=tuple(cast_steps))
    outs = pl.pallas_call(
        kern,
        out_shape=out_shape,
        grid=(n_i, n_j),
        in_specs=in_specs,
        out_specs=out_specs,
        scratch_shapes=[pltpu.VMEM((tm, d), BF16)],
        compiler_params=_params("parallel", "arbitrary"),
        name="ret_in",
    )(x, g.reshape(1, d), w, cos, sin, *[wf for wf, _ in cast])
    return outs[0], tuple(outs[1:])


def _ret_core_kernel(lg_ref, q_ref, k_ref, v_ref, g_ref, gain_ref, *rest, chunk, heads, has_state):
    if has_state:
        s0_ref, y_ref, s_ref = rest
    else:
        y_ref, s_ref = rest
    head0 = pl.program_id(1) * heads
    n_chunks = q_ref.shape[0] // chunk
    row = lax.broadcasted_iota(jnp.int32, (chunk, chunk), 0)
    col = lax.broadcasted_iota(jnp.int32, (chunk, chunk), 1)
    diff = (row - col).astype(F32)
    n = lax.broadcasted_iota(jnp.int32, (chunk, 1), 0).astype(F32)

    def head_consts(hh):
        lg = lg_ref[head0 + hh]
        decay = jnp.where(diff >= 0, jnp.exp(lg * jnp.maximum(diff, 0.0)), 0.0)
        cross = jnp.exp(lg * (n + 1.0))
        kdec = jnp.exp(lg * (chunk - 1.0 - n))
        carry = jnp.exp(lg * jnp.full((1, RET_DV), float(chunk), F32))
        return decay, cross, kdec, carry

    def mix(hh, c, consts):
        decay, cross, kdec, carry = consts
        rows = slice(c * chunk, (c + 1) * chunk)
        q = q_ref[rows, hh * RET_DK:(hh + 1) * RET_DK]
        k = k_ref[rows, hh * RET_DK:(hh + 1) * RET_DK]
        v = v_ref[rows, hh * RET_DV:(hh + 1) * RET_DV]
        inner = (_dot_nt(q, k) * decay).astype(BF16)
        if c > 0:
            s_old = s_ref[0, hh]
        elif has_state:
            s_old = s0_ref[0, hh]
        else:
            s_old = None
        kd = (k.astype(F32) * kdec).astype(BF16)
        o = _dot(inner, v)
        s_new = _dot_tn(kd, v)
        if s_old is not None:
            o = o + cross * _dot(q, s_old.astype(BF16))
            s_new = carry * s_old + s_new
        s_ref[0, hh] = s_new
        return o

    def norm_gate(hh, c, o):
        rows = slice(c * chunk, (c + 1) * chunk)
        cols = slice(hh * RET_DV, (hh + 1) * RET_DV)
        mu = jnp.mean(o, axis=-1, keepdims=True)
        dev = o - mu
        var = jnp.mean(dev * dev, axis=-1, keepdims=True)
        of = dev * lax.rsqrt(var + EPS) * gain_ref[:, cols]
        y_ref[rows, cols] = (_silu(g_ref[rows, cols].astype(F32)) * of).astype(BF16)

    consts = None
    pending = None
    for hh in range(heads):
        for c in range(n_chunks):
            if c == 0:
                consts = head_consts(hh)
            o = mix(hh, c, consts)
            if pending is not None:
                norm_gate(*pending)
            pending = (hh, c, o)
    norm_gate(*pending)


def _ret_core(qkvg, log_gamma, gain, state, *, batch, seq, chunk, heads):
    m = qkvg.shape[0]
    hk = RET_HEADS * RET_DK
    hv = RET_HEADS * RET_DV
    wk = heads * RET_DK
    wv = heads * RET_DV
    kblk = hk // wk
    vblk = 2 * hk // wv
    gblk = vblk + hv // wv
    has_state = state is not None
    in_specs = [
        pl.BlockSpec(memory_space=pltpu.SMEM),
        pl.BlockSpec((seq, wk), lambda b, g: (b, g)),
        pl.BlockSpec((seq, wk), lambda b, g: (b, kblk + g)),
        pl.BlockSpec((seq, wv), lambda b, g: (b, vblk + g)),
        pl.BlockSpec((seq, wv), lambda b, g: (b, gblk + g)),
        pl.BlockSpec((1, wv), lambda b, g: (0, g)),
    ]
    args = [log_gamma, qkvg, qkvg, qkvg, qkvg, gain.reshape(1, hv)]
    state_spec = pl.BlockSpec((1, heads, RET_DK, RET_DV), lambda b, g: (b, g, 0, 0))
    if has_state:
        in_specs.append(state_spec)
        args.append(state)
    kern = functools.partial(_ret_core_kernel, chunk=chunk, heads=heads, has_state=has_state)
    return pl.pallas_call(
        kern,
        out_shape=(jax.ShapeDtypeStruct((m, hv), BF16),
                   jax.ShapeDtypeStruct((batch, RET_HEADS, RET_DK, RET_DV), F32)),
        grid=(batch, RET_HEADS // heads),
        in_specs=in_specs,
        out_specs=(pl.BlockSpec((seq, wv), lambda b, g: (b, g)), state_spec),
        compiler_params=_params("parallel", "parallel"),
        name="ret_core",
    )(*args)


def _mm_res_kernel(a_ref, w_ref, r_ref, *rest):
    n_cast = (len(rest) - 1) // 2
    o_ref = rest[n_cast]
    o_ref[...] = r_ref[...] + _dot(a_ref[...], w_ref[...])
    for src, dst in zip(rest[:n_cast], rest[n_cast + 1:]):
        dst[...] = src[...].astype(BF16)


def _mm_res(a, w, res, *, tm, cast=()):
    m, k = a.shape
    n = w.shape[1]
    steps = m // tm
    in_specs = [
        pl.BlockSpec((tm, k), lambda i: (i, 0)),
        pl.BlockSpec((k, n), lambda i: (0, 0), pipeline_mode=pl.Buffered(1)),
        pl.BlockSpec((tm, n), lambda i: (i, 0)),
    ]
    out_shape = [jax.ShapeDtypeStruct((m, n), F32)]
    out_specs = [pl.BlockSpec((tm, n), lambda i: (i, 0))]
    for wf, layer in cast:
        _, rows, cols = wf.shape
        slab = rows // steps
        assert slab * steps == rows and slab % 16 == 0
        in_specs.append(pl.BlockSpec((None, slab, cols), lambda i, layer=layer: (layer, i, 0)))
        out_shape.append(jax.ShapeDtypeStruct((1, rows, cols), BF16))
        out_specs.append(pl.BlockSpec((None, slab, cols), lambda i: (0, i, 0)))
    outs = pl.pallas_call(
        _mm_res_kernel,
        out_shape=out_shape,
        grid=(steps,),
        in_specs=in_specs,
        out_specs=out_specs,
        compiler_params=_params("parallel"),
        name="mm_res",
    )(a, w, res, *[wf for wf, _ in cast])
    return outs[0], tuple(outs[1:])


def _latent_kernel(x_ref, g_ref, w_ref, gl_ref, cos_ref, sin_ref, c_ref, kr_ref, *, row_chunks):
    half = QK_ROPE // 2
    tc = x_ref.shape[0] // row_chunks

    def norm(c):
        x = x_ref[c * tc:(c + 1) * tc, :]
        return (x * _rms_scale(x) * g_ref[...]).astype(BF16)

    def project(c, xn):
        rows = slice(c * tc, (c + 1) * tc)
        y = _dot(xn, w_ref[...])
        cl = y[:, :KV_LORA]
        c_ref[rows, :] = cl * _rms_scale(cl) * gl_ref[...]
        x1 = y[:, KV_LORA:KV_LORA + half]
        x2 = y[:, KV_LORA + half:KV_LORA + QK_ROPE]
        cos = cos_ref[rows, :]
        sin = sin_ref[rows, :]
        kr_ref[rows, :] = jnp.concatenate([x1 * cos - x2 * sin, x1 * sin + x2 * cos], axis=-1)

    xn = norm(0)
    for c in range(row_chunks):
        xn_next = norm(c + 1) if c + 1 < row_chunks else None
        project(c, xn)
        xn = xn_next


def _latent(x, g, w, g_lat, cos, sin, *, tm):
    m, d = x.shape
    n = w.shape[1]
    half = QK_ROPE // 2
    tab_blocks = cos.shape[0] // tm
    return pl.pallas_call(
        functools.partial(_latent_kernel, row_chunks=tm // 128),
        out_shape=(jax.ShapeDtypeStruct((m, KV_LORA), F32),
                   jax.ShapeDtypeStruct((m, QK_ROPE), F32)),
        grid=(m // tm,),
        in_specs=[
            pl.BlockSpec((tm, d), lambda i: (i, 0)),
            pl.BlockSpec((1, d), lambda i: (0, 0)),
            pl.BlockSpec((d, n), lambda i: (0, 0)),
            pl.BlockSpec((1, KV_LORA), lambda i: (0, 0)),
            pl.BlockSpec((tm, half), lambda i: (i % tab_blocks, 0)),
            pl.BlockSpec((tm, half), lambda i: (i % tab_blocks, 0)),
        ],
        out_specs=(pl.BlockSpec((tm, KV_LORA), lambda i: (i, 0)),
                   pl.BlockSpec((tm, QK_ROPE), lambda i: (i, 0))),
        compiler_params=_params("parallel"),
        name="latent",
    )(x, g.reshape(1, d), w, g_lat.reshape(1, KV_LORA), cos, sin)


def _kv_up_body(seg_refs, seg_rows, wk_ref, wv_ref, gk_ref, k_ref, v_ref):
    refs = seg_refs
    g_nope = gk_ref[:, :QK_NOPE]
    g_rope = gk_ref[:, QK_NOPE:]
    lo = 0
    for s, rows in enumerate(seg_rows):
        c = refs[2 * s][...].astype(BF16)
        kr = refs[2 * s + 1][...]
        kn = _dot(c, wk_ref[...])
        v_ref[lo:lo + rows, :] = _dot(c, wv_ref[...]).astype(BF16)
        ss_rope = jnp.sum(kr * kr, axis=-1, keepdims=True)
        for h in range(MLA_HEADS):
            knh = kn[:, h * QK_NOPE:(h + 1) * QK_NOPE]
            ss = jnp.sum(knh * knh, axis=-1, keepdims=True) + ss_rope
            r = lax.rsqrt(ss * (1.0 / QK_HEAD) + EPS)
            k_ref[h, lo:lo + rows, :QK_NOPE] = (knh * r * g_nope).astype(BF16)
            k_ref[h, lo:lo + rows, QK_NOPE:] = (kr * r * g_rope).astype(BF16)
        lo += rows


def _kv_up_kernel(*refs, seg_rows):
    n_seg = len(seg_rows)
    wk_ref, wv_ref, gk_ref, k_ref, v_ref = refs[2 * n_seg:]
    _kv_up_body(refs[:2 * n_seg], seg_rows, wk_ref, wv_ref, gk_ref, k_ref, v_ref)


def _kv_up(segments, wk, wv, gk, *, steps):
    seg_rows = tuple(c.shape[0] // steps for c, _ in segments)
    tm = sum(seg_rows)
    m = steps * tm
    in_specs, args = [], []
    for (c, kr), rows in zip(segments, seg_rows):
        in_specs += [pl.BlockSpec((rows, KV_LORA), lambda i: (i, 0)),
                     pl.BlockSpec((rows, QK_ROPE), lambda i: (i, 0))]
        args += [c, kr]
    in_specs += [pl.BlockSpec(wk.shape, lambda i: (0, 0)),
                 pl.BlockSpec(wv.shape, lambda i: (0, 0)),
                 pl.BlockSpec((1, QK_HEAD), lambda i: (0, 0))]
    return pl.pallas_call(
        functools.partial(_kv_up_kernel, seg_rows=seg_rows),
        out_shape=(jax.ShapeDtypeStruct((MLA_HEADS, m, QK_HEAD), BF16),
                   jax.ShapeDtypeStruct((m, MLA_HEADS * V_HEAD), BF16)),
        grid=(steps,),
        in_specs=in_specs,
        out_specs=(pl.BlockSpec((MLA_HEADS, tm, QK_HEAD), lambda i: (0, i, 0)),
                   pl.BlockSpec((tm, MLA_HEADS * V_HEAD), lambda i: (i, 0))),
        compiler_params=_params("parallel"),
        name="kv_up",
    )(*args, wk, wv, gk.reshape(1, QK_HEAD))


Q_SLOT = 2 * QK_NOPE


def _q_proj_kernel(x_ref, g_ref, wdq_ref, gl_ref, wuq_ref, cos_ref, sin_ref, gq_ref, q_ref, *, row_chunks):
    g_nope = gq_ref[:, :QK_NOPE]
    g_rope = gq_ref[:, QK_NOPE:]
    tc = x_ref.shape[0] // row_chunks

    def project(c):
        rows = slice(c * tc, (c + 1) * tc)
        x = x_ref[rows, :]
        hn = (x * _rms_scale(x) * g_ref[...]).astype(BF16)
        ql = _dot(hn, wdq_ref[...])
        qn = (ql * _rms_scale(ql) * gl_ref[...]).astype(BF16)
        return _dot(qn, wuq_ref[...])

    def per_head(c, q):
        rows = slice(c * tc, (c + 1) * tc)
        cos = cos_ref[rows, :]
        sin = sin_ref[rows, :]
        for h in range(MLA_HEADS):
            nh = q[:, h * Q_SLOT:h * Q_SLOT + QK_NOPE]
            t = q[:, h * Q_SLOT + QK_NOPE:(h + 1) * Q_SLOT]
            rp = t * cos + pltpu.roll(t, QK_ROPE, 1) * sin
            ss = jnp.sum(nh * nh + rp * rp, axis=-1, keepdims=True)
            r = lax.rsqrt(ss * (1.0 / QK_HEAD) + EPS) * (QK_HEAD ** -0.5 * LOG2E)
            q_ref[h, rows, :QK_NOPE] = (nh * r * g_nope).astype(BF16)
            q_ref[h, rows, QK_NOPE:] = (rp[:, :QK_ROPE] * r * g_rope).astype(BF16)

    q = project(0)
    for c in range(row_chunks):
        q_next = project(c + 1) if c + 1 < row_chunks else None
        per_head(c, q)
        q = q_next


def _q_proj(x, g, wdq, g_lat, wuq, cos, sin, gq, *, tm):
    m, d = x.shape
    tab_blocks = cos.shape[0] // tm
    tw = cos.shape[1]
    return pl.pallas_call(
        functools.partial(_q_proj_kernel, row_chunks=tm // 256),
        out_shape=jax.ShapeDtypeStruct((MLA_HEADS, m, QK_HEAD), BF16),
        grid=(m // tm,),
        in_specs=[
            pl.BlockSpec((tm, d), lambda i: (i, 0)),
            pl.BlockSpec((1, d), lambda i: (0, 0)),
            pl.BlockSpec(wdq.shape, lambda i: (0, 0)),
            pl.BlockSpec((1, Q_LORA), lambda i: (0, 0)),
            pl.BlockSpec(wuq.shape, lambda i: (0, 0)),
            pl.BlockSpec((tm, tw), lambda i: (i % tab_blocks, 0)),
            pl.BlockSpec((tm, tw), lambda i: (i % tab_blocks, 0)),
            pl.BlockSpec((1, QK_HEAD), lambda i: (0, 0)),
        ],
        out_specs=pl.BlockSpec((MLA_HEADS, tm, QK_HEAD), lambda i: (0, i, 0)),
        compiler_params=_params("parallel"),
        name="q_proj",
    )(x, g.reshape(1, d), wdq, g_lat.reshape(1, Q_LORA), wuq, cos, sin, gq.reshape(1, QK_HEAD))


def _attn_causal_kernel(q_ref, k_ref, v_ref, o_ref, *, heads, tq):
    n_q = q_ref.shape[1] // tq
    hq = tq // 2
    krow = lax.broadcasted_iota(jnp.int32, (hq, tq), 0) // CHUNK
    qcol = lax.broadcasted_iota(jnp.int32, (hq, tq), 1) // CHUNK
    keep_a = krow <= qcol
    keep_b = keep_a[:, :hq]

    def right_half(full, fn, part):
        return jnp.concatenate([full[:, :hq], fn(full[:, hq:], part)], axis=1)

    def scores(h, qi):
        lo = qi * tq
        q = q_ref[h, lo:lo + tq, :]
        st_a = jnp.where(keep_a, _dot_nt(k_ref[h, lo:lo + hq, :], q), NEG_INF)
        st_b = jnp.where(keep_b, _dot_nt(k_ref[h, lo + hq:lo + tq, :], q[hq:, :]), NEG_INF)
        st_p = _dot_nt(k_ref[h, :lo, :], q) if qi > 0 else None
        return st_a, st_b, st_p

    def softmax(st):
        st_a, st_b, st_p = st
        m = jnp.max(st_a, axis=0, keepdims=True)
        if st_p is not None:
            m = jnp.maximum(m, jnp.max(st_p, axis=0, keepdims=True))
        m = right_half(m, jnp.maximum, jnp.max(st_b, axis=0, keepdims=True))
        pt_a = jnp.exp2(st_a - m)
        pt_b = jnp.exp2(st_b - m[:, hq:])
        l = jnp.sum(pt_a, axis=0, keepdims=True)
        pt_p = None
        if st_p is not None:
            pt_p = jnp.exp2(st_p - m)
            l = l + jnp.sum(pt_p, axis=0, keepdims=True)
            pt_p = pt_p.astype(BF16)
        l = right_half(l, jnp.add, jnp.sum(pt_b, axis=0, keepdims=True))
        return pt_a.astype(BF16), pt_b.astype(BF16), pt_p, l

    def values(h, qi, p):
        pt_a, pt_b, pt_p, l = p
        lo = qi * tq
        cols = slice(h * V_HEAD, (h + 1) * V_HEAD)
        acc_t = _dot_tn(v_ref[lo:lo + hq, cols], pt_a)
        if pt_p is not None:
            acc_t = acc_t + _dot_tn(v_ref[:lo, cols], pt_p)
        acc_t = right_half(acc_t, jnp.add, _dot_tn(v_ref[lo + hq:lo + tq, cols], pt_b))
        o_ref[lo:lo + tq, cols] = (acc_t / l).T.astype(BF16)

    work = [(h, qi) for h in range(heads) for qi in range(n_q)]
    s_q, p_q = {}, {}
    for step in range(len(work) + 2):
        if step < len(work):
            s_q[step] = scores(*work[step])
        if 0 <= step - 1 < len(work):
            p_q[step - 1] = softmax(s_q.pop(step - 1))
        if 0 <= step - 2 < len(work):
            values(*work[step - 2], p_q.pop(step - 2))


def _kv_attn_kernel(*refs, seg_rows):
    n_seg = len(seg_rows)
    wk_ref, wv_ref, gk_ref, q_ref, o_ref, k_scr, v_scr = refs[2 * n_seg:]
    g_nope = gk_ref[:, :QK_NOPE]
    g_rope = gk_ref[:, QK_NOPE:]

    group = 4
    width = group * QK_NOPE
    segs, lo = [], 0
    for s, rows in enumerate(seg_rows):
        c = refs[2 * s][...].astype(BF16)
        kr = refs[2 * s + 1][...]
        segs.append((lo, rows, c, kr, jnp.sum(kr * kr, axis=-1, keepdims=True)))
        lo += rows

    def project(g):
        cols = slice(g * width, (g + 1) * width)
        kn = []
        for lo, rows, c, _, _ in segs:
            v_scr[lo:lo + rows, cols] = _dot(c, wv_ref[:, cols]).astype(BF16)
            kn.append(_dot(c, wk_ref[:, cols]))
        return kn

    kn_groups = {}

    def key_norm(h):
        g, hh = divmod(h, group)
        for (lo, rows, _, kr, ss_rope), kn in zip(segs, kn_groups[g]):
            knh = kn[:, hh * QK_NOPE:(hh + 1) * QK_NOPE]
            ss = jnp.sum(knh * knh, axis=-1, keepdims=True) + ss_rope
            r = lax.rsqrt(ss * (1.0 / QK_HEAD) + EPS)
            k_scr[h, lo:lo + rows, :QK_NOPE] = (knh * r * g_nope).astype(BF16)
            k_scr[h, lo:lo + rows, QK_NOPE:] = (kr * r * g_rope).astype(BF16)

    def scores(h):
        return _dot_nt(q_ref[h], k_scr[h])

    def softmax(s):
        m = jnp.max(s, axis=-1, keepdims=True)
        p = jnp.exp2(s - m)
        return p.astype(BF16), jnp.sum(p, axis=-1, keepdims=True)

    def values(h, pl_):
        p, l = pl_
        cols = slice(h * V_HEAD, (h + 1) * V_HEAD)
        o_ref[:, cols] = (_dot(p, v_scr[:, cols]) / l).astype(BF16)

    s_q, p_q = {}, {}
    kn_groups[0] = project(0)
    for step in range(MLA_HEADS + 3):
        if step % group == 0 and step // group + 1 < MLA_HEADS // group:
            kn_groups[step // group + 1] = project(step // group + 1)
        if step < MLA_HEADS:
            key_norm(step)
        if 0 <= step - 1 < MLA_HEADS:
            s_q[step - 1] = scores(step - 1)
        if 0 <= step - 2 < MLA_HEADS:
            p_q[step - 2] = softmax(s_q.pop(step - 2))
        if 0 <= step - 3 < MLA_HEADS:
            values(step - 3, p_q.pop(step - 3))


def _kv_attention(segments, wk, wv, gk, q, *, batch):
    seg_rows = tuple(c.shape[0] // batch for c, _ in segments)
    t_kv = sum(seg_rows)
    t_q = q.shape[1] // batch
    in_specs, args = [], []
    for (c, kr), rows in zip(segments, seg_rows):
        in_specs += [pl.BlockSpec((rows, KV_LORA), lambda b: (b, 0)),
                     pl.BlockSpec((rows, QK_ROPE), lambda b: (b, 0))]
        args += [c, kr]
    in_specs += [pl.BlockSpec(wk.shape, lambda b: (0, 0)),
                 pl.BlockSpec(wv.shape, lambda b: (0, 0)),
                 pl.BlockSpec((1, QK_HEAD), lambda b: (0, 0)),
                 pl.BlockSpec((MLA_HEADS, t_q, QK_HEAD), lambda b: (0, b, 0))]
    return pl.pallas_call(
        functools.partial(_kv_attn_kernel, seg_rows=seg_rows),
        out_shape=jax.ShapeDtypeStruct((batch * t_q, MLA_HEADS * V_HEAD), BF16),
        grid=(batch,),
        in_specs=in_specs,
        out_specs=pl.BlockSpec((t_q, MLA_HEADS * V_HEAD), lambda b: (b, 0)),
        scratch_shapes=[pltpu.VMEM((MLA_HEADS, t_kv, QK_HEAD), BF16),
                        pltpu.VMEM((t_kv, MLA_HEADS * V_HEAD), BF16)],
        compiler_params=_params("parallel"),
        name="kv_attention",
    )(*args, wk, wv, gk.reshape(1, QK_HEAD), q)


def _attention(q, k, v, *, batch, t_q, t_kv, heads, tq):
    hg = MLA_HEADS // heads
    kern = functools.partial(_attn_causal_kernel, heads=heads, tq=tq)
    return pl.pallas_call(
        kern,
        out_shape=jax.ShapeDtypeStruct((batch * t_q, MLA_HEADS * V_HEAD), BF16),
        grid=(batch, hg),
        in_specs=[
            pl.BlockSpec((heads, t_q, QK_HEAD), lambda b, g: (g, b, 0)),
            pl.BlockSpec((heads, t_kv, QK_HEAD), lambda b, g: (g, b, 0)),
            pl.BlockSpec((t_kv, heads * V_HEAD), lambda b, g: (b, g)),
        ],
        out_specs=pl.BlockSpec((t_q, heads * V_HEAD), lambda b, g: (b, g)),
        compiler_params=_params("parallel", "parallel"),
        name="attention",
    )(q, k, v)


def _rope_tables(pos, half, reps, rows):
    inv = jnp.power(ROPE_BASE, -jnp.arange(half, dtype=F32) / half)
    ang = pos.astype(F32)[:, None] * inv[None, :]
    cos = jnp.tile(jnp.cos(ang), (max(rows // pos.shape[0], 1), reps))
    sin = jnp.tile(jnp.sin(ang), (max(rows // pos.shape[0], 1), reps))
    return cos, sin


def _mm_res_and_ffn2(a, w, h, layer, p):
    key = ("ffn2_bf16", layer)
    if key not in p:
        h, p[key] = _mm_res(a, w, h, tm=512, cast=((p["ffn2_w_in"], layer), (p["ffn2_w_out"], layer)))
    else:
        h, _ = _mm_res(a, w, h, tm=512)
    w_in, w_out = p[key]
    return _ffn(h, p["ffn2_norm"][layer], w_in, w_out, 0, tm=FFN_TM, tf=FFN_TF)


def _trunk(x, pos, ret_state, ckv_past, krope_past, p):
    batch, seq, d = x.shape
    m = batch * seq
    tm = 1024
    h = x.reshape(m, d)

    if "ret_w_in_bf16" not in p:
        h, p["ret_w_in_bf16"] = _ffn(h, p["ffn1_norm"][0], p["ffn1_w_in_l0"], p["ffn1_w_out_l0"], 0,
                                     tm=FFN_TM, tf=FFN_TF, cast=p["ret_w_in"])
    else:
        h = _ffn(h, p["ffn1_norm"][0], p["ffn1_w_in_l0"], p["ffn1_w_out_l0"], 0, tm=FFN_TM, tf=FFN_TF)
    cos, sin = _rope_tables(pos, RET_DK // 2, 1, tm)
    key = "cast_by_ret_in"
    cast = () if key in p else ((p["ffn1_w_in"], 1), (p["ffn1_w_out"], 1), (p["ret_w_out"], 0), (p["w_o"], 0))
    qkvg, cast_out = _ret_in(h, p["mix_norm"][0], p["ret_w_in_bf16"][0], cos, sin, tm=tm, tn=2048, cast=cast)
    if cast:
        p[key] = cast_out
    ffn1_l1_w_in, ffn1_l1_w_out, ret_w_out, w_o = p[key]
    log_gamma = jnp.log1p(-jnp.exp2(-5.0 - jnp.arange(RET_HEADS, dtype=F32)))
    chunk = min(seq, 256)
    y, s_fin = _ret_core(qkvg, log_gamma, p["ret_gn_gain"][0],
                         None if ret_state is None else ret_state[0],
                         batch=batch, seq=seq, chunk=chunk,
                         heads=max(1, min(RET_HEADS, 512 // seq)))
    h = _mm_res_and_ffn2(y, ret_w_out[0], h, 0, p)

    cos, sin = _rope_tables(pos, QK_ROPE // 2, 1, tm)
    c_new, kr_new = _latent(h, p["kv_norm"], p["w_dkv"], p["kv_lat_norm"], cos, sin, tm=tm)
    if ckv_past is None:
        k_sh, v_sh = _kv_up([(c_new, kr_new)], p["w_uk"], p["w_uv"], p["k_norm"], steps=m // tm)
    else:
        past = ckv_past.shape[1]
        segments = [(ckv_past.reshape(batch * past, KV_LORA), krope_past.reshape(batch * past, QK_ROPE)),
                    (c_new, kr_new)]

    h = _ffn(h, p["ffn1_norm"][1], ffn1_l1_w_in, ffn1_l1_w_out, 0, tm=FFN_TM, tf=FFN_TF)
    cos, sin = _rope_tables(pos, QK_ROPE // 2, 1, tm)
    zeros = jnp.zeros_like(cos)
    cos = jnp.concatenate([cos, cos, zeros, zeros], axis=-1)
    sin = jnp.concatenate([-sin, sin, zeros, zeros], axis=-1)
    q = _q_proj(h, p["mix_norm"][1], p["w_dq"][0], p["q_lat_norm"][0], p["w_uq"][0], cos, sin,
                p["q_norm"][0], tm=tm)
    if ckv_past is None:
        o = _attention(q, k_sh, v_sh, batch=batch, t_q=seq, t_kv=seq, heads=2, tq=512)
    else:
        o = _kv_attention(segments, p["w_uk"], p["w_uv"], p["k_norm"], q, batch=batch)
    h = _mm_res_and_ffn2(o, w_o[0], h, 1, p)

    return (h.reshape(batch, seq, d), s_fin[None],
            c_new.reshape(batch, seq, KV_LORA), kr_new.reshape(batch, seq, QK_ROPE))


def kernel(x_prompt, x_sample, state_ret, cache_ckv, cache_krope, ffn1_norm, ffn1_w_in, ffn1_w_out, mix_norm, ffn2_norm, ffn2_w_in, ffn2_w_out, ret_w_in, ret_gn_gain, ret_w_out, kv_norm, w_dkv, kv_lat_norm, w_ukv, k_norm, w_dq, q_lat_norm, w_uq, q_norm, w_o):
    bf = lambda w: w.astype(BF16)
    w_ukv_h = w_ukv.reshape(KV_LORA, MLA_HEADS, QK_NOPE + V_HEAD)
    w_uk = w_ukv_h[:, :, :QK_NOPE].reshape(KV_LORA, MLA_HEADS * QK_NOPE)
    w_uv = w_ukv_h[:, :, QK_NOPE:].reshape(KV_LORA, MLA_HEADS * V_HEAD)
    half = QK_ROPE // 2
    w_uq_h = w_uq.reshape(w_uq.shape[0], Q_LORA, MLA_HEADS, QK_HEAD)
    w_lo = w_uq_h[..., QK_NOPE:QK_NOPE + half]
    w_hi = w_uq_h[..., QK_NOPE + half:]
    w_uq_p = jnp.concatenate([w_uq_h[..., :QK_NOPE], w_lo, w_hi, w_hi, w_lo], axis=-1)
    w_uq_p = w_uq_p.reshape(w_uq.shape[0], Q_LORA, MLA_HEADS * Q_SLOT)
    p = dict(ffn1_norm=ffn1_norm, ffn1_w_in=ffn1_w_in, ffn1_w_out=ffn1_w_out,
             ffn1_w_in_l0=bf(ffn1_w_in[:1]), ffn1_w_out_l0=bf(ffn1_w_out[:1]),
             mix_norm=mix_norm, ffn2_norm=ffn2_norm, ffn2_w_in=ffn2_w_in, ffn2_w_out=ffn2_w_out,
             ret_w_in=ret_w_in, ret_gn_gain=ret_gn_gain, ret_w_out=ret_w_out,
             kv_norm=kv_norm, w_dkv=bf(w_dkv), kv_lat_norm=kv_lat_norm,
             w_uk=bf(w_uk), w_uv=bf(w_uv), k_norm=k_norm,
             w_dq=bf(w_dq), q_lat_norm=q_lat_norm, w_uq=bf(w_uq_p), q_norm=q_norm, w_o=w_o)
    pos_prompt = jnp.arange(x_prompt.shape[1])
    past = cache_ckv.shape[1]
    pos_sample = past + jnp.arange(x_sample.shape[1])
    y_p, ret_p, ckv_p, kr_p = _trunk(x_prompt, pos_prompt, None, None, None, p)
    y_s, ret_s, ckv_s, kr_s = _trunk(x_sample, pos_sample, state_ret, cache_ckv, cache_krope, p)
    return (y_p, y_s, ret_p, ckv_p, kr_p, ret_s, ckv_s, kr_s)
```

```python
import functools

import jax
import jax.numpy as jnp
from jax import lax
from jax.experimental import pallas as pl
from jax.experimental.pallas import tpu as pltpu

F32 = jnp.float32
BF16 = jnp.bfloat16

D_MODEL = 2048
CHUNK = 64
D_FF = 5632
RET_HEADS = 8
RET_DK = D_MODEL // RET_HEADS
RET_DV = 2 * D_MODEL // RET_HEADS
MLA_HEADS = 16
QK_NOPE = 128
QK_ROPE = 64
QK_HEAD = QK_NOPE + QK_ROPE
V_HEAD = 128
KV_LORA = 512
Q_LORA = 512
ROPE_BASE = 10000.0
EPS = 1e-6
NEG_INF = -1e30
LOG2E = 1.4426950408889634

VMEM_LIMIT_BYTES = 60 * 1024 * 1024

FFN_TM = 1024
FFN_TF = 512
FFN_NORM_CHUNKS = 4
FFN_DMA_BLOCKS = 2


def _params(*semantics):
    return pltpu.CompilerParams(dimension_semantics=semantics,
                                vmem_limit_bytes=VMEM_LIMIT_BYTES)


def _rms_scale(x):
    return lax.rsqrt(jnp.mean(x * x, axis=-1, keepdims=True) + EPS)


def _dot(a, b):
    return jnp.dot(a, b, preferred_element_type=F32)


def _dot_nt(a, b):
    return lax.dot_general(a, b, (((1,), (1,)), ((), ())), preferred_element_type=F32)


def _dot_tn(a, b):
    return lax.dot_general(a, b, (((0,), (0,)), ((), ())), preferred_element_type=F32)


def _silu(x):
    return x * jax.nn.sigmoid(x)


def _ffn_kernel(*refs, layer, tf, cast_pieces):
    if cast_pieces:
        (g_ref, x_hbm, win_hbm, wout_hbm, cast_src, o_hbm, cast_dst, acc, xn_ref, wa_buf, wb_buf, wo_buf,
         wsem, xsem, osem, stage_in, stage_out, csem_in, csem_out) = refs
    else:
        g_ref, x_hbm, win_hbm, wout_hbm, o_hbm, acc, xn_ref, wa_buf, wb_buf, wo_buf, wsem, xsem, osem = refs
    i = pl.program_id(0)
    n_tiles = pl.num_programs(0)
    tm = acc.shape[1]
    dma_cols = FFN_DMA_BLOCKS * tf
    n_pos = 1 + (D_FF - tf) // dma_cols
    last = n_pos - 1
    a_slot = i % 2

    def weight_copies(pos, slot):
        col = pl.multiple_of(jnp.maximum(pos * dma_cols - (dma_cols - tf), 0), tf)
        return (
            pltpu.make_async_copy(win_hbm.at[layer, :, pl.ds(col, dma_cols)], wa_buf.at[slot], wsem.at[slot, 0]),
            pltpu.make_async_copy(win_hbm.at[layer, :, pl.ds(D_FF + col, dma_cols)], wb_buf.at[slot], wsem.at[slot, 1]),
            pltpu.make_async_copy(wout_hbm.at[layer, pl.ds(col, dma_cols), :], wo_buf.at[slot], wsem.at[slot, 2]),
        )

    def x_copy(tile, slot):
        src = x_hbm.at[pl.ds(pl.multiple_of(tile * tm, tm), tm), :]
        return pltpu.make_async_copy(src, acc.at[slot], xsem.at[slot])

    def out_copy(tile, slot):
        dst = o_hbm.at[pl.ds(pl.multiple_of(tile * tm, tm), tm), :]
        return pltpu.make_async_copy(acc.at[slot], dst, osem.at[slot])

    def update(xn, slot, cols):
        a = _dot(xn, wa_buf[slot, :, :cols])
        b = _dot(xn, wb_buf[slot, :, :cols])
        h = (_silu(a) * (0.5 * b)).astype(BF16)
        return _dot(h, wo_buf[slot, :cols, :])

    @pl.when(i == 0)
    def _():
        x_copy(0, 0).start()
        for cp in weight_copies(0, 0):
            cp.start()

    x_copy(i, a_slot).wait()

    def sync(pos, w_slot):
        for cp in weight_copies(pos, w_slot):
            cp.wait()

        nxt = jnp.where(pos == last, 0, pos + 1)

        @pl.when(jnp.logical_or(pos < last, i + 1 < n_tiles))
        def _():
            for cp in weight_copies(nxt, 1 - w_slot):
                cp.start()

        @pl.when(pos == 1)
        def _():
            @pl.when(i > 0)
            def _():
                out_copy(i - 1, 1 - a_slot).wait()

            @pl.when(i + 1 < n_tiles)
            def _():
                x_copy(i + 1, 1 - a_slot).start()

        if cast_pieces:
            side_cast(pos)

    def cast_copies(n, slot):
        piece = stage_in.shape[1]
        rows = pl.ds(pl.multiple_of((i * cast_pieces + n) * piece, piece), piece)
        return (pltpu.make_async_copy(cast_src.at[0, rows, :], stage_in.at[slot], csem_in.at[slot]),
                pltpu.make_async_copy(stage_out.at[slot], cast_dst.at[0, rows, :], csem_out.at[slot]))

    def side_cast(pos):
        @pl.when(pos == 0)
        def _():
            cast_copies(0, 0)[0].start()

        @pl.when((pos >= 1) & (pos <= cast_pieces))
        def _():
            n = pos - 1
            slot = n % 2
            cast_copies(n, slot)[0].wait()

            @pl.when(n + 1 < cast_pieces)
            def _():
                cast_copies(n + 1, 1 - slot)[0].start()

            @pl.when(n >= 2)
            def _():
                cast_copies(n - 2, slot)[1].wait()

            stage_out[slot] = stage_in[slot].astype(BF16)
            cast_copies(n, slot)[1].start()

        @pl.when(pos == cast_pieces + 1)
        def _():
            for n in (cast_pieces - 2, cast_pieces - 1):
                cast_copies(n, n % 2)[1].wait()

    sync(0, 0)
    tc = tm // FFN_NORM_CHUNKS

    def norm(c):
        x = acc[a_slot, c * tc:(c + 1) * tc, :]
        xn = (x * _rms_scale(x) * g_ref[...]).astype(BF16)
        xn_ref[c * tc:(c + 1) * tc, :] = xn
        return x, xn

    cur = norm(0)
    for c in range(FFN_NORM_CHUNKS):
        nxt_c = norm(c + 1) if c + 1 < FFN_NORM_CHUNKS else None
        x, xn = cur
        acc[a_slot, c * tc:(c + 1) * tc, :] = x + update(xn, 0, tf)
        cur = nxt_c

    def position(pos, carry):
        w_slot = pos % 2
        sync(pos, w_slot)
        acc[a_slot] += update(xn_ref[...], w_slot, dma_cols)
        return carry

    lax.fori_loop(1, n_pos, position, 0)

    out_copy(i, a_slot).start()

    @pl.when(i + 1 == n_tiles)
    def _():
        out_copy(i, a_slot).wait()


def _ffn(x, g, w_in, w_out, layer, *, tm, tf, cast=None):
    m, d = x.shape
    n_tiles = m // tm
    dma_cols = FFN_DMA_BLOCKS * tf
    n_pos = 1 + (D_FF - tf) // dma_cols
    assert (D_FF - tf) % dma_cols == 0 and n_pos % 2 == 0
    in_specs = [pl.BlockSpec((1, d), lambda i: (0, 0))] + [pl.BlockSpec(memory_space=pl.ANY)] * 3
    out_shape = [jax.ShapeDtypeStruct((m, d), F32)]
    scratch = [
        pltpu.VMEM((2, tm, d), F32),
        pltpu.VMEM((tm, d), BF16),
        pltpu.VMEM((2, d, dma_cols), BF16),
        pltpu.VMEM((2, d, dma_cols), BF16),
        pltpu.VMEM((2, dma_cols, d), BF16),
        pltpu.SemaphoreType.DMA((2, 3)),
        pltpu.SemaphoreType.DMA((2,)),
        pltpu.SemaphoreType.DMA((2,)),
    ]
    args = [g.reshape(1, d), x, w_in, w_out]
    cast_pieces = 0
    if cast is not None:
        cast_pieces = n_pos - 2
        _, rows, cols = cast.shape
        piece = rows // (n_tiles * cast_pieces)
        assert piece * n_tiles * cast_pieces == rows and piece % 16 == 0 and cast_pieces >= 2
        in_specs.append(pl.BlockSpec(memory_space=pl.ANY))
        out_shape.append(jax.ShapeDtypeStruct(cast.shape, BF16))
        scratch += [pltpu.VMEM((2, piece, cols), F32), pltpu.VMEM((2, piece, cols), BF16),
                    pltpu.SemaphoreType.DMA((2,)), pltpu.SemaphoreType.DMA((2,))]
        args.append(cast)
    outs = pl.pallas_call(
        functools.partial(_ffn_kernel, layer=layer, tf=tf, cast_pieces=cast_pieces),
        out_shape=out_shape,
        grid=(n_tiles,),
        in_specs=in_specs,
        out_specs=[pl.BlockSpec(memory_space=pl.ANY)] * len(out_shape),
        scratch_shapes=scratch,
        compiler_params=_params("arbitrary"),
        name="ffn",
    )(*args)
    return outs[0] if cast is None else (outs[0], outs[1])


def _ret_in_kernel(x_ref, g_ref, w_ref, cos_ref, sin_ref, *rest, rope_blocks, heads_per_block, cast_steps):
    n_cast = len(cast_steps)
    o_ref = rest[n_cast]
    xn_ref = rest[-1]
    j = pl.program_id(1)

    for src, dst, steps in zip(rest[:n_cast], rest[n_cast + 1:-1], cast_steps):
        @pl.when(j < steps)
        def _(src=src, dst=dst):
            dst[...] = src[...].astype(BF16)

    @pl.when(j == 0)
    def _():
        x = x_ref[...]
        xn_ref[...] = (x * _rms_scale(x) * g_ref[...]).astype(BF16)

    half = RET_DK // 2

    @pl.when(j < 2 * rope_blocks)
    def _():
        scale = jnp.where(j < rope_blocks, RET_DK ** -0.5, 1.0).astype(F32)
        cos = cos_ref[...] * scale
        sin = sin_ref[...] * scale
        xn = xn_ref[...]
        for h in range(heads_per_block):
            lo = h * RET_DK
            y = _dot(xn, w_ref[:, lo:lo + RET_DK])
            x1 = y[:, :half]
            x2 = y[:, half:]
            o_ref[:, lo:lo + half] = (x1 * cos - x2 * sin).astype(BF16)
            o_ref[:, lo + half:lo + RET_DK] = (x1 * sin + x2 * cos).astype(BF16)

    @pl.when(j >= 2 * rope_blocks)
    def _():
        o_ref[...] = _dot(xn_ref[...], w_ref[...]).astype(BF16)


def _ret_in(x, g, w, cos, sin, *, tm, tn, cast=()):
    m, d = x.shape
    n = w.shape[1]
    tab_blocks = cos.shape[0] // tm
    n_i, n_j = m // tm, n // tn
    in_specs = [
        pl.BlockSpec((tm, d), lambda i, j: (i, 0)),
        pl.BlockSpec((1, d), lambda i, j: (0, 0)),
        pl.BlockSpec((d, tn), lambda i, j: (0, j)),
        pl.BlockSpec((tm, RET_DK // 2), lambda i, j: (i % tab_blocks, 0)),
        pl.BlockSpec((tm, RET_DK // 2), lambda i, j: (i % tab_blocks, 0)),
    ]
    out_shape = [jax.ShapeDtypeStruct((m, n), BF16)]
    out_specs = [pl.BlockSpec((tm, tn), lambda i, j: (i, j))]
    cast_steps = []
    for wf, layer in cast:
        _, rows, cols = wf.shape
        slab = rows // n_i
        assert slab * n_i == rows and slab % 16 == 0
        steps = max(s for s in range(1, n_j + 1) if cols % (128 * s) == 0)
        cast_steps.append(steps)
        in_specs.append(pl.BlockSpec((None, slab, cols // steps),
                                     lambda i, j, layer=layer, steps=steps: (layer, i, jnp.minimum(j, steps - 1))))
        out_shape.append(jax.ShapeDtypeStruct((1, rows, cols), BF16))
        out_specs.append(pl.BlockSpec((None, slab, cols // steps),
                                      lambda i, j, steps=steps: (0, i, jnp.minimum(j, steps - 1))))
    kern = functools.partial(_ret_in_kernel, rope_blocks=(RET_HEADS * RET_DK) // tn,
                             heads_per_block=tn // RET_DK, cast_steps=tuple(cast_steps))
    outs = pl.pallas_call(
        kern,
        out_shape=out_shape,
        grid=(n_i, n_j),
        in_specs=in_specs,
        out_specs=out_specs,
        scratch_shapes=[pltpu.VMEM((tm, d), BF16)],
        compiler_params=_params("parallel", "arbitrary"),
        name="ret_in",
    )(x, g.reshape(1, d), w, cos, sin, *[wf for wf, _ in cast])
    return outs[0], tuple(outs[1:])


def _ret_core_kernel(lg_ref, q_ref, k_ref, v_ref, g_ref, gain_ref, *rest, chunk, heads, has_state):
    if has_state:
        s0_ref, y_ref, s_ref = rest
    else:
        y_ref, s_ref = rest
    head0 = pl.program_id(1) * heads
    n_chunks = q_ref.shape[0] // chunk
    row = lax.broadcasted_iota(jnp.int32, (chunk, chunk), 0)
    col = lax.broadcasted_iota(jnp.int32, (chunk, chunk), 1)
    diff = (row - col).astype(F32)
    n = lax.broadcasted_iota(jnp.int32, (chunk, 1), 0).astype(F32)

    def head_consts(hh):
        lg = lg_ref[head0 + hh]
        decay = jnp.where(diff >= 0, jnp.exp(lg * jnp.maximum(diff, 0.0)), 0.0)
        cross = jnp.exp(lg * (n + 1.0))
        kdec = jnp.exp(lg * (chunk - 1.0 - n))
        carry = jnp.exp(lg * jnp.full((1, RET_DV), float(chunk), F32))
        return decay, cross, kdec, carry

    def mix(hh, c, consts):
        decay, cross, kdec, carry = consts
        rows = slice(c * chunk, (c + 1) * chunk)
        q = q_ref[rows, hh * RET_DK:(hh + 1) * RET_DK]
        k = k_ref[rows, hh * RET_DK:(hh + 1) * RET_DK]
        v = v_ref[rows, hh * RET_DV:(hh + 1) * RET_DV]
        inner = (_dot_nt(q, k) * decay).astype(BF16)
        if c > 0:
            s_old = s_ref[0, hh]
        elif has_state:
            s_old = s0_ref[0, hh]
        else:
            s_old = None
        kd = (k.astype(F32) * kdec).astype(BF16)
        o = _dot(inner, v)
        s_new = _dot_tn(kd, v)
        if s_old is not None:
            o = o + cross * _dot(q, s_old.astype(BF16))
            s_new = carry * s_old + s_new
        s_ref[0, hh] = s_new
        return o

    def norm_gate(hh, c, o):
        rows = slice(c * chunk, (c + 1) * chunk)
        cols = slice(hh * RET_DV, (hh + 1) * RET_DV)
        mu = jnp.mean(o, axis=-1, keepdims=True)
        dev = o - mu
        var = jnp.mean(dev * dev, axis=-1, keepdims=True)
        of = dev * lax.rsqrt(var + EPS) * gain_ref[:, cols]
        y_ref[rows, cols] = (_silu(g_ref[rows, cols].astype(F32)) * of).astype(BF16)

    consts = None
    pending = None
    for hh in range(heads):
        for c in range(n_chunks):
            if c == 0:
                consts = head_consts(hh)
            o = mix(hh, c, consts)
            if pending is not None:
                norm_gate(*pending)
            pending = (hh, c, o)
    norm_gate(*pending)


def _ret_core(qkvg, log_gamma, gain, state, *, batch, seq, chunk, heads):
    m = qkvg.shape[0]
    hk = RET_HEADS * RET_DK
    hv = RET_HEADS * RET_DV
    wk = heads * RET_DK
    wv = heads * RET_DV
    kblk = hk // wk
    vblk = 2 * hk // wv
    gblk = vblk + hv // wv
    has_state = state is not None
    in_specs = [
        pl.BlockSpec(memory_space=pltpu.SMEM),
        pl.BlockSpec((seq, wk), lambda b, g: (b, g)),
        pl.BlockSpec((seq, wk), lambda b, g: (b, kblk + g)),
        pl.BlockSpec((seq, wv), lambda b, g: (b, vblk + g)),
        pl.BlockSpec((seq, wv), lambda b, g: (b, gblk + g)),
        pl.BlockSpec((1, wv), lambda b, g: (0, g)),
    ]
    args = [log_gamma, qkvg, qkvg, qkvg, qkvg, gain.reshape(1, hv)]
    state_spec = pl.BlockSpec((1, heads, RET_DK, RET_DV), lambda b, g: (b, g, 0, 0))
    if has_state:
        in_specs.append(state_spec)
        args.append(state)
    kern = functools.partial(_ret_core_kernel, chunk=chunk, heads=heads, has_state=has_state)
    return pl.pallas_call(
        kern,
        out_shape=(jax.ShapeDtypeStruct((m, hv), BF16),
                   jax.ShapeDtypeStruct((batch, RET_HEADS, RET_DK, RET_DV), F32)),
        grid=(batch, RET_HEADS // heads),
        in_specs=in_specs,
        out_specs=(pl.BlockSpec((seq, wv), lambda b, g: (b, g)), state_spec),
        compiler_params=_params("parallel", "parallel"),
        name="ret_core",
    )(*args)


def _mm_res_kernel(a_ref, w_ref, r_ref, *rest):
    n_cast = (len(rest) - 1) // 2
    o_ref = rest[n_cast]
    o_ref[...] = r_ref[...] + _dot(a_ref[...], w_ref[...])
    for src, dst in zip(rest[:n_cast], rest[n_cast + 1:]):
        dst[...] = src[...].astype(BF16)


def _mm_res(a, w, res, *, tm, cast=()):
    m, k = a.shape
    n = w.shape[1]
    steps = m // tm
    in_specs = [
        pl.BlockSpec((tm, k), lambda i: (i, 0)),
        pl.BlockSpec((k, n), lambda i: (0, 0), pipeline_mode=pl.Buffered(1)),
        pl.BlockSpec((tm, n), lambda i: (i, 0)),
    ]
    out_shape = [jax.ShapeDtypeStruct((m, n), F32)]
    out_specs = [pl.BlockSpec((tm, n), lambda i: (i, 0))]
    for wf, layer in cast:
        _, rows, cols = wf.shape
        slab = rows // steps
        assert slab * steps == rows and slab % 16 == 0
        in_specs.append(pl.BlockSpec((None, slab, cols), lambda i, layer=layer: (layer, i, 0)))
        out_shape.append(jax.ShapeDtypeStruct((1, rows, cols), BF16))
        out_specs.append(pl.BlockSpec((None, slab, cols), lambda i: (0, i, 0)))
    outs = pl.pallas_call(
        _mm_res_kernel,
        out_shape=out_shape,
        grid=(steps,),
        in_specs=in_specs,
        out_specs=out_specs,
        compiler_params=_params("parallel"),
        name="mm_res",
    )(a, w, res, *[wf for wf, _ in cast])
    return outs[0], tuple(outs[1:])


def _latent_kernel(x_ref, g_ref, w_ref, gl_ref, cos_ref, sin_ref, c_ref, kr_ref, *, row_chunks):
    half = QK_ROPE // 2
    tc = x_ref.shape[0] // row_chunks

    def norm(c):
        x = x_ref[c * tc:(c + 1) * tc, :]
        return (x * _rms_scale(x) * g_ref[...]).astype(BF16)

    def project(c, xn):
        rows = slice(c * tc, (c + 1) * tc)
        y = _dot(xn, w_ref[...])
        cl = y[:, :KV_LORA]
        c_ref[rows, :] = cl * _rms_scale(cl) * gl_ref[...]
        x1 = y[:, KV_LORA:KV_LORA + half]
        x2 = y[:, KV_LORA + half:KV_LORA + QK_ROPE]
        cos = cos_ref[rows, :]
        sin = sin_ref[rows, :]
        kr_ref[rows, :] = jnp.concatenate([x1 * cos - x2 * sin, x1 * sin + x2 * cos], axis=-1)

    xn = norm(0)
    for c in range(row_chunks):
        xn_next = norm(c + 1) if c + 1 < row_chunks else None
        project(c, xn)
        xn = xn_next


def _latent(x, g, w, g_lat, cos, sin, *, tm):
    m, d = x.shape
    n = w.shape[1]
    half = QK_ROPE // 2
    tab_blocks = cos.shape[0] // tm
    return pl.pallas_call(
        functools.partial(_latent_kernel, row_chunks=tm // 128),
        out_shape=(jax.ShapeDtypeStruct((m, KV_LORA), F32),
                   jax.ShapeDtypeStruct((m, QK_ROPE), F32)),
        grid=(m // tm,),
        in_specs=[
            pl.BlockSpec((tm, d), lambda i: (i, 0)),
            pl.BlockSpec((1, d), lambda i: (0, 0)),
            pl.BlockSpec((d, n), lambda i: (0, 0)),
            pl.BlockSpec((1, KV_LORA), lambda i: (0, 0)),
            pl.BlockSpec((tm, half), lambda i: (i % tab_blocks, 0)),
            pl.BlockSpec((tm, half), lambda i: (i % tab_blocks, 0)),
        ],
        out_specs=(pl.BlockSpec((tm, KV_LORA), lambda i: (i, 0)),
                   pl.BlockSpec((tm, QK_ROPE), lambda i: (i, 0))),
        compiler_params=_params("parallel"),
        name="latent",
    )(x, g.reshape(1, d), w, g_lat.reshape(1, KV_LORA), cos, sin)


def _kv_up_body(seg_refs, seg_rows, wk_ref, wv_ref, gk_ref, k_ref, v_ref):
    refs = seg_refs
    g_nope = gk_ref[:, :QK_NOPE]
    g_rope = gk_ref[:, QK_NOPE:]
    lo = 0
    for s, rows in enumerate(seg_rows):
        c = refs[2 * s][...].astype(BF16)
        kr = refs[2 * s + 1][...]
        kn = _dot(c, wk_ref[...])
        v_ref[lo:lo + rows, :] = _dot(c, wv_ref[...]).astype(BF16)
        ss_rope = jnp.sum(kr * kr, axis=-1, keepdims=True)
        for h in range(MLA_HEADS):
            knh = kn[:, h * QK_NOPE:(h + 1) * QK_NOPE]
            ss = jnp.sum(knh * knh, axis=-1, keepdims=True) + ss_rope
            r = lax.rsqrt(ss * (1.0 / QK_HEAD) + EPS)
            k_ref[h, lo:lo + rows, :QK_NOPE] = (knh * r * g_nope).astype(BF16)
            k_ref[h, lo:lo + rows, QK_NOPE:] = (kr * r * g_rope).astype(BF16)
        lo += rows


def _kv_up_kernel(*refs, seg_rows):
    n_seg = len(seg_rows)
    wk_ref, wv_ref, gk_ref, k_ref, v_ref = refs[2 * n_seg:]
    _kv_up_body(refs[:2 * n_seg], seg_rows, wk_ref, wv_ref, gk_ref, k_ref, v_ref)


def _kv_up(segments, wk, wv, gk, *, steps):
    seg_rows = tuple(c.shape[0] // steps for c, _ in segments)
    tm = sum(seg_rows)
    m = steps * tm
    in_specs, args = [], []
    for (c, kr), rows in zip(segments, seg_rows):
        in_specs += [pl.BlockSpec((rows, KV_LORA), lambda i: (i, 0)),
                     pl.BlockSpec((rows, QK_ROPE), lambda i: (i, 0))]
        args += [c, kr]
    in_specs += [pl.BlockSpec(wk.shape, lambda i: (0, 0)),
                 pl.BlockSpec(wv.shape, lambda i: (0, 0)),
                 pl.BlockSpec((1, QK_HEAD), lambda i: (0, 0))]
    return pl.pallas_call(
        functools.partial(_kv_up_kernel, seg_rows=seg_rows),
        out_shape=(jax.ShapeDtypeStruct((MLA_HEADS, m, QK_HEAD), BF16),
                   jax.ShapeDtypeStruct((m, MLA_HEADS * V_HEAD), BF16)),
        grid=(steps,),
        in_specs=in_specs,
        out_specs=(pl.BlockSpec((MLA_HEADS, tm, QK_HEAD), lambda i: (0, i, 0)),
                   pl.BlockSpec((tm, MLA_HEADS * V_HEAD), lambda i: (i, 0))),
        compiler_params=_params("parallel"),
        name="kv_up",
    )(*args, wk, wv, gk.reshape(1, QK_HEAD))


Q_SLOT = 2 * QK_NOPE


def _q_proj_kernel(x_ref, g_ref, wdq_ref, gl_ref, wuq_ref, cos_ref, sin_ref, gq_ref, q_ref, *, row_chunks):
    g_nope = gq_ref[:, :QK_NOPE]
    g_rope = gq_ref[:, QK_NOPE:]
    tc = x_ref.shape[0] // row_chunks

    def project(c):
        rows = slice(c * tc, (c + 1) * tc)
        x = x_ref[rows, :]
        hn = (x * _rms_scale(x) * g_ref[...]).astype(BF16)
        ql = _dot(hn, wdq_ref[...])
        qn = (ql * _rms_scale(ql) * gl_ref[...]).astype(BF16)
        return _dot(qn, wuq_ref[...])

    def per_head(c, q):
        rows = slice(c * tc, (c + 1) * tc)
        cos = cos_ref[rows, :]
        sin = sin_ref[rows, :]
        for h in range(MLA_HEADS):
            nh = q[:, h * Q_SLOT:h * Q_SLOT + QK_NOPE]
            t = q[:, h * Q_SLOT + QK_NOPE:(h + 1) * Q_SLOT]
            rp = t * cos + pltpu.roll(t, QK_ROPE, 1) * sin
            ss = jnp.sum(nh * nh + rp * rp, axis=-1, keepdims=True)
            r = lax.rsqrt(ss * (1.0 / QK_HEAD) + EPS) * (QK_HEAD ** -0.5 * LOG2E)
            q_ref[h, rows, :QK_NOPE] = (nh * r * g_nope).astype(BF16)
            q_ref[h, rows, QK_NOPE:] = (rp[:, :QK_ROPE] * r * g_rope).astype(BF16)

    q = project(0)
    for c in range(row_chunks):
        q_next = project(c + 1) if c + 1 < row_chunks else None
        per_head(c, q)
        q = q_next


def _q_proj(x, g, wdq, g_lat, wuq, cos, sin, gq, *, tm):
    m, d = x.shape
    tab_blocks = cos.shape[0] // tm
    tw = cos.shape[1]
    return pl.pallas_call(
        functools.partial(_q_proj_kernel, row_chunks=tm // 256),
        out_shape=jax.ShapeDtypeStruct((MLA_HEADS, m, QK_HEAD), BF16),
        grid=(m // tm,),
        in_specs=[
            pl.BlockSpec((tm, d), lambda i: (i, 0)),
            pl.BlockSpec((1, d), lambda i: (0, 0)),
            pl.BlockSpec(wdq.shape, lambda i: (0, 0)),
            pl.BlockSpec((1, Q_LORA), lambda i: (0, 0)),
            pl.BlockSpec(wuq.shape, lambda i: (0, 0)),
            pl.BlockSpec((tm, tw), lambda i: (i % tab_blocks, 0)),
            pl.BlockSpec((tm, tw), lambda i: (i % tab_blocks, 0)),
            pl.BlockSpec((1, QK_HEAD), lambda i: (0, 0)),
        ],
        out_specs=pl.BlockSpec((MLA_HEADS, tm, QK_HEAD), lambda i: (0, i, 0)),
        compiler_params=_params("parallel"),
        name="q_proj",
    )(x, g.reshape(1, d), wdq, g_lat.reshape(1, Q_LORA), wuq, cos, sin, gq.reshape(1, QK_HEAD))


def _attn_causal_kernel(q_ref, k_ref, v_ref, *rest, heads, tq):
    n_cast = (len(rest) - 1) // 2
    o_ref = rest[n_cast]
    for src, dst in zip(rest[:n_cast], rest[n_cast + 1:]):
        dst[...] = src[...].astype(BF16)
    n_q = q_ref.shape[1] // tq
    hq = tq // 2
    krow = lax.broadcasted_iota(jnp.int32, (hq, tq), 0) // CHUNK
    qcol = lax.broadcasted_iota(jnp.int32, (hq, tq), 1) // CHUNK
    keep_a = krow <= qcol
    keep_b = keep_a[:, :hq]

    def right_half(full, fn, part):
        return jnp.concatenate([full[:, :hq], fn(full[:, hq:], part)], axis=1)

    def scores(h, qi):
        lo = qi * tq
        q = q_ref[h, lo:lo + tq, :]
        st_a = jnp.where(keep_a, _dot_nt(k_ref[h, lo:lo + hq, :], q), NEG_INF)
        st_b = jnp.where(keep_b, _dot_nt(k_ref[h, lo + hq:lo + tq, :], q[hq:, :]), NEG_INF)
        st_p = _dot_nt(k_ref[h, :lo, :], q) if qi > 0 else None
        return st_a, st_b, st_p

    def softmax(st):
        st_a, st_b, st_p = st
        m = jnp.max(st_a, axis=0, keepdims=True)
        if st_p is not None:
            m = jnp.maximum(m, jnp.max(st_p, axis=0, keepdims=True))
        m = right_half(m, jnp.maximum, jnp.max(st_b, axis=0, keepdims=True))
        pt_a = jnp.exp2(st_a - m)
        pt_b = jnp.exp2(st_b - m[:, hq:])
        l = jnp.sum(pt_a, axis=0, keepdims=True)
        pt_p = None
        if st_p is not None:
            pt_p = jnp.exp2(st_p - m)
            l = l + jnp.sum(pt_p, axis=0, keepdims=True)
            pt_p = pt_p.astype(BF16)
        l = right_half(l, jnp.add, jnp.sum(pt_b, axis=0, keepdims=True))
        return pt_a.astype(BF16), pt_b.astype(BF16), pt_p, l

    def values(h, qi, p):
        pt_a, pt_b, pt_p, l = p
        lo = qi * tq
        cols = slice(h * V_HEAD, (h + 1) * V_HEAD)
        acc_t = _dot_tn(v_ref[lo:lo + hq, cols], pt_a)
        if pt_p is not None:
            acc_t = acc_t + _dot_tn(v_ref[:lo, cols], pt_p)
        acc_t = right_half(acc_t, jnp.add, _dot_tn(v_ref[lo + hq:lo + tq, cols], pt_b))
        o_ref[lo:lo + tq, cols] = (acc_t / l).T.astype(BF16)

    work = [(h, qi) for h in range(heads) for qi in range(n_q)]
    s_q, p_q = {}, {}
    for step in range(len(work) + 2):
        if step < len(work):
            s_q[step] = scores(*work[step])
        if 0 <= step - 1 < len(work):
            p_q[step - 1] = softmax(s_q.pop(step - 1))
        if 0 <= step - 2 < len(work):
            values(*work[step - 2], p_q.pop(step - 2))


def _kv_attn_kernel(*refs, seg_rows):
    n_seg = len(seg_rows)
    wk_ref, wv_ref, gk_ref, q_ref, o_ref, k_scr, v_scr = refs[2 * n_seg:]
    g_nope = gk_ref[:, :QK_NOPE]
    g_rope = gk_ref[:, QK_NOPE:]

    group = 4
    width = group * QK_NOPE
    segs, lo = [], 0
    for s, rows in enumerate(seg_rows):
        c = refs[2 * s][...].astype(BF16)
        kr = refs[2 * s + 1][...]
        segs.append((lo, rows, c, kr, jnp.sum(kr * kr, axis=-1, keepdims=True)))
        lo += rows

    def project(g):
        cols = slice(g * width, (g + 1) * width)
        kn = []
        for lo, rows, c, _, _ in segs:
            v_scr[lo:lo + rows, cols] = _dot(c, wv_ref[:, cols]).astype(BF16)
            kn.append(_dot(c, wk_ref[:, cols]))
        return kn

    kn_groups = {}

    def key_norm(h):
        g, hh = divmod(h, group)
        for (lo, rows, _, kr, ss_rope), kn in zip(segs, kn_groups[g]):
            knh = kn[:, hh * QK_NOPE:(hh + 1) * QK_NOPE]
            ss = jnp.sum(knh * knh, axis=-1, keepdims=True) + ss_rope
            r = lax.rsqrt(ss * (1.0 / QK_HEAD) + EPS)
            k_scr[h, lo:lo + rows, :QK_NOPE] = (knh * r * g_nope).astype(BF16)
            k_scr[h, lo:lo + rows, QK_NOPE:] = (kr * r * g_rope).astype(BF16)

    def scores(h):
        return _dot_nt(q_ref[h], k_scr[h])

    def softmax(s):
        m = jnp.max(s, axis=-1, keepdims=True)
        p = jnp.exp2(s - m)
        return p.astype(BF16), jnp.sum(p, axis=-1, keepdims=True)

    def values(h, pl_):
        p, l = pl_
        cols = slice(h * V_HEAD, (h + 1) * V_HEAD)
        o_ref[:, cols] = (_dot(p, v_scr[:, cols]) / l).astype(BF16)

    s_q, p_q = {}, {}
    kn_groups[0] = project(0)
    for step in range(MLA_HEADS + 3):
        if step % group == 0 and step // group + 1 < MLA_HEADS // group:
            kn_groups[step // group + 1] = project(step // group + 1)
        if step < MLA_HEADS:
            key_norm(step)
        if 0 <= step - 1 < MLA_HEADS:
            s_q[step - 1] = scores(step - 1)
        if 0 <= step - 2 < MLA_HEADS:
            p_q[step - 2] = softmax(s_q.pop(step - 2))
        if 0 <= step - 3 < MLA_HEADS:
            values(step - 3, p_q.pop(step - 3))


def _kv_attention(segments, wk, wv, gk, q, *, batch):
    seg_rows = tuple(c.shape[0] // batch for c, _ in segments)
    t_kv = sum(seg_rows)
    t_q = q.shape[1] // batch
    in_specs, args = [], []
    for (c, kr), rows in zip(segments, seg_rows):
        in_specs += [pl.BlockSpec((rows, KV_LORA), lambda b: (b, 0)),
                     pl.BlockSpec((rows, QK_ROPE), lambda b: (b, 0))]
        args += [c, kr]
    in_specs += [pl.BlockSpec(wk.shape, lambda b: (0, 0)),
                 pl.BlockSpec(wv.shape, lambda b: (0, 0)),
                 pl.BlockSpec((1, QK_HEAD), lambda b: (0, 0)),
                 pl.BlockSpec((MLA_HEADS, t_q, QK_HEAD), lambda b: (0, b, 0))]
    return pl.pallas_call(
        functools.partial(_kv_attn_kernel, seg_rows=seg_rows),
        out_shape=jax.ShapeDtypeStruct((batch * t_q, MLA_HEADS * V_HEAD), BF16),
        grid=(batch,),
        in_specs=in_specs,
        out_specs=pl.BlockSpec((t_q, MLA_HEADS * V_HEAD), lambda b: (b, 0)),
        scratch_shapes=[pltpu.VMEM((MLA_HEADS, t_kv, QK_HEAD), BF16),
                        pltpu.VMEM((t_kv, MLA_HEADS * V_HEAD), BF16)],
        compiler_params=_params("parallel"),
        name="kv_attention",
    )(*args, wk, wv, gk.reshape(1, QK_HEAD), q)


def _attention(q, k, v, *, batch, t_q, t_kv, heads, tq, cast=()):
    hg = MLA_HEADS // heads
    steps = batch * hg
    kern = functools.partial(_attn_causal_kernel, heads=heads, tq=tq)
    in_specs = [
        pl.BlockSpec((heads, t_q, QK_HEAD), lambda b, g: (g, b, 0)),
        pl.BlockSpec((heads, t_kv, QK_HEAD), lambda b, g: (g, b, 0)),
        pl.BlockSpec((t_kv, heads * V_HEAD), lambda b, g: (b, g)),
    ]
    out_shape = [jax.ShapeDtypeStruct((batch * t_q, MLA_HEADS * V_HEAD), BF16)]
    out_specs = [pl.BlockSpec((t_q, heads * V_HEAD), lambda b, g: (b, g))]
    for wf, layer in cast:
        _, rows, cols = wf.shape
        slab = rows // steps
        assert slab * steps == rows and slab % 16 == 0
        in_specs.append(pl.BlockSpec((None, slab, cols), lambda b, g, layer=layer: (layer, b * hg + g, 0)))
        out_shape.append(jax.ShapeDtypeStruct((1, rows, cols), BF16))
        out_specs.append(pl.BlockSpec((None, slab, cols), lambda b, g: (0, b * hg + g, 0)))
    outs = pl.pallas_call(
        kern,
        out_shape=out_shape,
        grid=(batch, hg),
        in_specs=in_specs,
        out_specs=out_specs,
        compiler_params=_params("parallel", "parallel"),
        name="attention",
    )(q, k, v, *[wf for wf, _ in cast])
    return outs[0], tuple(outs[1:])


def _rope_tables(pos, half, reps, rows):
    inv = jnp.power(ROPE_BASE, -jnp.arange(half, dtype=F32) / half)
    ang = pos.astype(F32)[:, None] * inv[None, :]
    cos = jnp.tile(jnp.cos(ang), (max(rows // pos.shape[0], 1), reps))
    sin = jnp.tile(jnp.sin(ang), (max(rows // pos.shape[0], 1), reps))
    return cos, sin


def _mm_res_and_ffn2(a, w, h, layer, p):
    key = ("ffn2_bf16", layer)
    if key not in p:
        w_in = p.get(("ffn2_w_in_bf16", layer))
        todo = [(p["ffn2_w_out"], layer)] if w_in is not None else [(p["ffn2_w_in"], layer), (p["ffn2_w_out"], layer)]
        h, done = _mm_res(a, w, h, tm=512, cast=tuple(todo))
        p[key] = (w_in, done[0]) if w_in is not None else done
    else:
        h, _ = _mm_res(a, w, h, tm=512)
    w_in, w_out = p[key]
    return _ffn(h, p["ffn2_norm"][layer], w_in, w_out, 0, tm=FFN_TM, tf=FFN_TF)


def _trunk(x, pos, ret_state, ckv_past, krope_past, p):
    batch, seq, d = x.shape
    m = batch * seq
    tm = 1024
    h = x.reshape(m, d)

    if "ret_w_in_bf16" not in p:
        h, p["ret_w_in_bf16"] = _ffn(h, p["ffn1_norm"][0], p["ffn1_w_in_l0"], p["ffn1_w_out_l0"], 0,
                                     tm=FFN_TM, tf=FFN_TF, cast=p["ret_w_in"])
    else:
        h = _ffn(h, p["ffn1_norm"][0], p["ffn1_w_in_l0"], p["ffn1_w_out_l0"], 0, tm=FFN_TM, tf=FFN_TF)
    cos, sin = _rope_tables(pos, RET_DK // 2, 1, tm)
    key = "cast_by_ret_in"
    cast = () if key in p else ((p["ffn1_w_in"], 1), (p["ffn1_w_out"], 1), (p["ret_w_out"], 0), (p["w_o"], 0))
    qkvg, cast_out = _ret_in(h, p["mix_norm"][0], p["ret_w_in_bf16"][0], cos, sin, tm=tm, tn=2048, cast=cast)
    if cast:
        p[key] = cast_out
    ffn1_l1_w_in, ffn1_l1_w_out, ret_w_out, w_o = p[key]
    log_gamma = jnp.log1p(-jnp.exp2(-5.0 - jnp.arange(RET_HEADS, dtype=F32)))
    chunk = min(seq, 256)
    y, s_fin = _ret_core(qkvg, log_gamma, p["ret_gn_gain"][0],
                         None if ret_state is None else ret_state[0],
                         batch=batch, seq=seq, chunk=chunk,
                         heads=max(1, min(RET_HEADS, 512 // seq)))
    h = _mm_res_and_ffn2(y, ret_w_out[0], h, 0, p)

    cos, sin = _rope_tables(pos, QK_ROPE // 2, 1, tm)
    c_new, kr_new = _latent(h, p["kv_norm"], p["w_dkv"], p["kv_lat_norm"], cos, sin, tm=tm)
    if ckv_past is None:
        k_sh, v_sh = _kv_up([(c_new, kr_new)], p["w_uk"], p["w_uv"], p["k_norm"], steps=m // tm)
    else:
        past = ckv_past.shape[1]
        segments = [(ckv_past.reshape(batch * past, KV_LORA), krope_past.reshape(batch * past, QK_ROPE)),
                    (c_new, kr_new)]

    h = _ffn(h, p["ffn1_norm"][1], ffn1_l1_w_in, ffn1_l1_w_out, 0, tm=FFN_TM, tf=FFN_TF)
    cos, sin = _rope_tables(pos, QK_ROPE // 2, 1, tm)
    zeros = jnp.zeros_like(cos)
    cos = jnp.concatenate([cos, cos, zeros, zeros], axis=-1)
    sin = jnp.concatenate([-sin, sin, zeros, zeros], axis=-1)
    q = _q_proj(h, p["mix_norm"][1], p["w_dq"][0], p["q_lat_norm"][0], p["w_uq"][0], cos, sin,
                p["q_norm"][0], tm=tm)
    if ckv_past is None:
        key = ("ffn2_w_in_bf16", 1)
        o, cast_out = _attention(q, k_sh, v_sh, batch=batch, t_q=seq, t_kv=seq, heads=2, tq=512,
                                 cast=() if key in p else ((p["ffn2_w_in"], 1),))
        if cast_out:
            p[key] = cast_out[0]
    else:
        o = _kv_attention(segments, p["w_uk"], p["w_uv"], p["k_norm"], q, batch=batch)
    h = _mm_res_and_ffn2(o, w_o[0], h, 1, p)

    return (h.reshape(batch, seq, d), s_fin[None],
            c_new.reshape(batch, seq, KV_LORA), kr_new.reshape(batch, seq, QK_ROPE))


def kernel(x_prompt, x_sample, state_ret, cache_ckv, cache_krope, ffn1_norm, ffn1_w_in, ffn1_w_out, mix_norm, ffn2_norm, ffn2_w_in, ffn2_w_out, ret_w_in, ret_gn_gain, ret_w_out, kv_norm, w_dkv, kv_lat_norm, w_ukv, k_norm, w_dq, q_lat_norm, w_uq, q_norm, w_o):
    bf = lambda w: w.astype(BF16)
    w_ukv_h = w_ukv.reshape(KV_LORA, MLA_HEADS, QK_NOPE + V_HEAD)
    w_uk = w_ukv_h[:, :, :QK_NOPE].reshape(KV_LORA, MLA_HEADS * QK_NOPE)
    w_uv = w_ukv_h[:, :, QK_NOPE:].reshape(KV_LORA, MLA_HEADS * V_HEAD)
    half = QK_ROPE // 2
    w_uq_h = w_uq.reshape(w_uq.shape[0], Q_LORA, MLA_HEADS, QK_HEAD)
    w_lo = w_uq_h[..., QK_NOPE:QK_NOPE + half]
    w_hi = w_uq_h[..., QK_NOPE + half:]
    w_uq_p = jnp.concatenate([w_uq_h[..., :QK_NOPE], w_lo, w_hi, w_hi, w_lo], axis=-1)
    w_uq_p = w_uq_p.reshape(w_uq.shape[0], Q_LORA, MLA_HEADS * Q_SLOT)
    p = dict(ffn1_norm=ffn1_norm, ffn1_w_in=ffn1_w_in, ffn1_w_out=ffn1_w_out,
             ffn1_w_in_l0=bf(ffn1_w_in[:1]), ffn1_w_out_l0=bf(ffn1_w_out[:1]),
             mix_norm=mix_norm, ffn2_norm=ffn2_norm, ffn2_w_in=ffn2_w_in, ffn2_w_out=ffn2_w_out,
             ret_w_in=ret_w_in, ret_gn_gain=ret_gn_gain, ret_w_out=ret_w_out,
             kv_norm=kv_norm, w_dkv=bf(w_dkv), kv_lat_norm=kv_lat_norm,
             w_uk=bf(w_uk), w_uv=bf(w_uv), k_norm=k_norm,
             w_dq=bf(w_dq), q_lat_norm=q_lat_norm, w_uq=bf(w_uq_p), q_norm=q_norm, w_o=w_o)
    pos_prompt = jnp.arange(x_prompt.shape[1])
    past = cache_ckv.shape[1]
    pos_sample = past + jnp.arange(x_sample.shape[1])
    y_p, ret_p, ckv_p, kr_p = _trunk(x_prompt, pos_prompt, None, None, None, p)
    y_s, ret_s, ckv_s, kr_s = _trunk(x_sample, pos_sample, state_ret, cache_ckv, cache_krope, p)
    return (y_p, y_s, ret_p, ckv_p, kr_p, ret_s, ckv_s, kr_s)
```

```python
import functools

import jax
import jax.numpy as jnp
from jax import lax
from jax.experimental import pallas as pl
from jax.experimental.pallas import tpu as pltpu

F32 = jnp.float32
BF16 = jnp.bfloat16

D_MODEL = 2048
CHUNK = 64
D_FF = 5632
RET_HEADS = 8
RET_DK = D_MODEL // RET_HEADS
RET_DV = 2 * D_MODEL // RET_HEADS
MLA_HEADS = 16
QK_NOPE = 128
QK_ROPE = 64
QK_HEAD = QK_NOPE + QK_ROPE
V_HEAD = 128
KV_LORA = 512
Q_LORA = 512
ROPE_BASE = 10000.0
EPS = 1e-6
NEG_INF = -1e30
LOG2E = 1.4426950408889634

VMEM_LIMIT_BYTES = 60 * 1024 * 1024

FFN_TM = 1024
FFN_TF = 512
FFN_NORM_CHUNKS = 4
FFN_DMA_BLOCKS = 2


def _params(*semantics):
    return pltpu.CompilerParams(dimension_semantics=semantics,
                                vmem_limit_bytes=VMEM_LIMIT_BYTES)


def _rms_scale(x):
    return lax.rsqrt(jnp.mean(x * x, axis=-1, keepdims=True) + EPS)


def _dot(a, b):
    return jnp.dot(a, b, preferred_element_type=F32)


def _dot_nt(a, b):
    return lax.dot_general(a, b, (((1,), (1,)), ((), ())), preferred_element_type=F32)


def _dot_tn(a, b):
    return lax.dot_general(a, b, (((0,), (0,)), ((), ())), preferred_element_type=F32)


def _silu(x):
    return x * jax.nn.sigmoid(x)


def _ffn_kernel(*refs, layer, tf, cast_pieces):
    if cast_pieces:
        (g_ref, x_hbm, win_hbm, wout_hbm, cast_src, o_hbm, cast_dst, acc, xn_ref, wa_buf, wb_buf, wo_buf,
         wsem, xsem, osem, stage_in, stage_out, csem_in, csem_out) = refs
    else:
        g_ref, x_hbm, win_hbm, wout_hbm, o_hbm, acc, xn_ref, wa_buf, wb_buf, wo_buf, wsem, xsem, osem = refs
    i = pl.program_id(0)
    n_tiles = pl.num_programs(0)
    tm = acc.shape[1]
    dma_cols = FFN_DMA_BLOCKS * tf
    n_pos = 1 + (D_FF - tf) // dma_cols
    last = n_pos - 1
    a_slot = i % 2

    def weight_copies(pos, slot):
        col = pl.multiple_of(jnp.maximum(pos * dma_cols - (dma_cols - tf), 0), tf)
        return (
            pltpu.make_async_copy(win_hbm.at[layer, :, pl.ds(col, dma_cols)], wa_buf.at[slot], wsem.at[slot, 0]),
            pltpu.make_async_copy(win_hbm.at[layer, :, pl.ds(D_FF + col, dma_cols)], wb_buf.at[slot], wsem.at[slot, 1]),
            pltpu.make_async_copy(wout_hbm.at[layer, pl.ds(col, dma_cols), :], wo_buf.at[slot], wsem.at[slot, 2]),
        )

    def x_copy(tile, slot):
        src = x_hbm.at[pl.ds(pl.multiple_of(tile * tm, tm), tm), :]
        return pltpu.make_async_copy(src, acc.at[slot], xsem.at[slot])

    def out_copy(tile, slot):
        dst = o_hbm.at[pl.ds(pl.multiple_of(tile * tm, tm), tm), :]
        return pltpu.make_async_copy(acc.at[slot], dst, osem.at[slot])

    def update(xn, slot, cols):
        a = _dot(xn, wa_buf[slot, :, :cols])
        b = _dot(xn, wb_buf[slot, :, :cols])
        h = (_silu(a) * (0.5 * b)).astype(BF16)
        return _dot(h, wo_buf[slot, :cols, :])

    @pl.when(i == 0)
    def _():
        x_copy(0, 0).start()
        for cp in weight_copies(0, 0):
            cp.start()

    x_copy(i, a_slot).wait()

    def sync(pos, w_slot):
        for cp in weight_copies(pos, w_slot):
            cp.wait()

        nxt = jnp.where(pos == last, 0, pos + 1)

        @pl.when(jnp.logical_or(pos < last, i + 1 < n_tiles))
        def _():
            for cp in weight_copies(nxt, 1 - w_slot):
                cp.start()

        @pl.when(pos == 1)
        def _():
            @pl.when(i > 0)
            def _():
                out_copy(i - 1, 1 - a_slot).wait()

            @pl.when(i + 1 < n_tiles)
            def _():
                x_copy(i + 1, 1 - a_slot).start()

        if cast_pieces:
            side_cast(pos)

    def cast_copies(n, slot):
        piece = stage_in.shape[1]
        rows = pl.ds(pl.multiple_of((i * cast_pieces + n) * piece, piece), piece)
        return (pltpu.make_async_copy(cast_src.at[0, rows, :], stage_in.at[slot], csem_in.at[slot]),
                pltpu.make_async_copy(stage_out.at[slot], cast_dst.at[0, rows, :], csem_out.at[slot]))

    def side_cast(pos):
        @pl.when(pos == 0)
        def _():
            cast_copies(0, 0)[0].start()

        @pl.when((pos >= 1) & (pos <= cast_pieces))
        def _():
            n = pos - 1
            slot = n % 2
            cast_copies(n, slot)[0].wait()

            @pl.when(n + 1 < cast_pieces)
            def _():
                cast_copies(n + 1, 1 - slot)[0].start()

            @pl.when(n >= 2)
            def _():
                cast_copies(n - 2, slot)[1].wait()

            stage_out[slot] = stage_in[slot].astype(BF16)
            cast_copies(n, slot)[1].start()

        @pl.when(pos == cast_pieces + 1)
        def _():
            for n in (cast_pieces - 2, cast_pieces - 1):
                cast_copies(n, n % 2)[1].wait()

    sync(0, 0)
    tc = tm // FFN_NORM_CHUNKS

    def norm(c):
        x = acc[a_slot, c * tc:(c + 1) * tc, :]
        xn = (x * _rms_scale(x) * g_ref[...]).astype(BF16)
        xn_ref[c * tc:(c + 1) * tc, :] = xn
        return x, xn

    cur = norm(0)
    for c in range(FFN_NORM_CHUNKS):
        nxt_c = norm(c + 1) if c + 1 < FFN_NORM_CHUNKS else None
        x, xn = cur
        acc[a_slot, c * tc:(c + 1) * tc, :] = x + update(xn, 0, tf)
        cur = nxt_c

    def position(pos, carry):
        w_slot = pos % 2
        sync(pos, w_slot)
        acc[a_slot] += update(xn_ref[...], w_slot, dma_cols)
        return carry

    lax.fori_loop(1, n_pos, position, 0)

    out_copy(i, a_slot).start()

    @pl.when(i + 1 == n_tiles)
    def _():
        out_copy(i, a_slot).wait()


def _ffn(x, g, w_in, w_out, layer, *, tm, tf, cast=None):
    m, d = x.shape
    n_tiles = m // tm
    dma_cols = FFN_DMA_BLOCKS * tf
    n_pos = 1 + (D_FF - tf) // dma_cols
    assert (D_FF - tf) % dma_cols == 0 and n_pos % 2 == 0
    in_specs = [pl.BlockSpec((1, d), lambda i: (0, 0))] + [pl.BlockSpec(memory_space=pl.ANY)] * 3
    out_shape = [jax.ShapeDtypeStruct((m, d), F32)]
    scratch = [
        pltpu.VMEM((2, tm, d), F32),
        pltpu.VMEM((tm, d), BF16),
        pltpu.VMEM((2, d, dma_cols), BF16),
        pltpu.VMEM((2, d, dma_cols), BF16),
        pltpu.VMEM((2, dma_cols, d), BF16),
        pltpu.SemaphoreType.DMA((2, 3)),
        pltpu.SemaphoreType.DMA((2,)),
        pltpu.SemaphoreType.DMA((2,)),
    ]
    args = [g.reshape(1, d), x, w_in, w_out]
    cast_pieces = 0
    if cast is not None:
        cast_pieces = n_pos - 2
        _, rows, cols = cast.shape
        piece = rows // (n_tiles * cast_pieces)
        assert piece * n_tiles * cast_pieces == rows and piece % 16 == 0 and cast_pieces >= 2
        in_specs.append(pl.BlockSpec(memory_space=pl.ANY))
        out_shape.append(jax.ShapeDtypeStruct(cast.shape, BF16))
        scratch += [pltpu.VMEM((2, piece, cols), F32), pltpu.VMEM((2, piece, cols), BF16),
                    pltpu.SemaphoreType.DMA((2,)), pltpu.SemaphoreType.DMA((2,))]
        args.append(cast)
    outs = pl.pallas_call(
        functools.partial(_ffn_kernel, layer=layer, tf=tf, cast_pieces=cast_pieces),
        out_shape=out_shape,
        grid=(n_tiles,),
        in_specs=in_specs,
        out_specs=[pl.BlockSpec(memory_space=pl.ANY)] * len(out_shape),
        scratch_shapes=scratch,
        compiler_params=_params("arbitrary"),
        name="ffn",
    )(*args)
    return outs[0] if cast is None else (outs[0], outs[1])


def _ret_in_kernel(x_ref, g_ref, w_ref, cos_ref, sin_ref, *rest, rope_blocks, heads_per_block, cast_steps):
    n_cast = len(cast_steps)
    o_ref = rest[n_cast]
    xn_ref = rest[-1]
    j = pl.program_id(1)

    for src, dst, steps in zip(rest[:n_cast], rest[n_cast + 1:-1], cast_steps):
        @pl.when(j < steps)
        def _(src=src, dst=dst):
            dst[...] = src[...].astype(BF16)

    @pl.when(j == 0)
    def _():
        x = x_ref[...]
        xn_ref[...] = (x * _rms_scale(x) * g_ref[...]).astype(BF16)

    half = RET_DK // 2

    @pl.when(j < 2 * rope_blocks)
    def _():
        scale = jnp.where(j < rope_blocks, RET_DK ** -0.5, 1.0).astype(F32)
        cos = cos_ref[...] * scale
        sin = sin_ref[...] * scale
        xn = xn_ref[...]
        for h in range(heads_per_block):
            lo = h * RET_DK
            y = _dot(xn, w_ref[:, lo:lo + RET_DK])
            x1 = y[:, :half]
            x2 = y[:, half:]
            o_ref[:, lo:lo + half] = (x1 * cos - x2 * sin).astype(BF16)
            o_ref[:, lo + half:lo + RET_DK] = (x1 * sin + x2 * cos).astype(BF16)

    @pl.when(j >= 2 * rope_blocks)
    def _():
        o_ref[...] = _dot(xn_ref[...], w_ref[...]).astype(BF16)


def _ret_in(x, g, w, cos, sin, *, tm, tn, cast=()):
    m, d = x.shape
    n = w.shape[1]
    tab_blocks = cos.shape[0] // tm
    n_i, n_j = m // tm, n // tn
    in_specs = [
        pl.BlockSpec((tm, d), lambda i, j: (i, 0)),
        pl.BlockSpec((1, d), lambda i, j: (0, 0)),
        pl.BlockSpec((d, tn), lambda i, j: (0, j)),
        pl.BlockSpec((tm, RET_DK // 2), lambda i, j: (i % tab_blocks, 0)),
        pl.BlockSpec((tm, RET_DK // 2), lambda i, j: (i % tab_blocks, 0)),
    ]
    out_shape = [jax.ShapeDtypeStruct((m, n), BF16)]
    out_specs = [pl.BlockSpec((tm, tn), lambda i, j: (i, j))]
    cast_steps = []
    for wf, layer in cast:
        _, rows, cols = wf.shape
        slab = rows // n_i
        assert slab * n_i == rows and slab % 16 == 0
        steps = max(s for s in range(1, n_j + 1) if cols % (128 * s) == 0)
        cast_steps.append(steps)
        in_specs.append(pl.BlockSpec((None, slab, cols // steps),
                                     lambda i, j, layer=layer, steps=steps: (layer, i, jnp.minimum(j, steps - 1))))
        out_shape.append(jax.ShapeDtypeStruct((1, rows, cols), BF16))
        out_specs.append(pl.BlockSpec((None, slab, cols // steps),
                                      lambda i, j, steps=steps: (0, i, jnp.minimum(j, steps - 1))))
    kern = functools.partial(_ret_in_kernel, rope_blocks=(RET_HEADS * RET_DK) // tn,
                             heads_per_block=tn // RET_DK, cast_steps=tuple(cast_steps))
    outs = pl.pallas_call(
        kern,
        out_shape=out_shape,
        grid=(n_i, n_j),
        in_specs=in_specs,
        out_specs=out_specs,
        scratch_shapes=[pltpu.VMEM((tm, d), BF16)],
        compiler_params=_params("parallel", "arbitrary"),
        name="ret_in",
    )(x, g.reshape(1, d), w, cos, sin, *[wf for wf, _ in cast])
    return outs[0], tuple(outs[1:])


def _ret_core_kernel(lg_ref, q_ref, k_ref, v_ref, g_ref, gain_ref, *rest, chunk, heads, has_state):
    if has_state:
        s0_ref, y_ref, s_ref = rest
    else:
        y_ref, s_ref = rest
    head0 = pl.program_id(1) * heads
    n_chunks = q_ref.shape[0] // chunk
    row = lax.broadcasted_iota(jnp.int32, (chunk, chunk), 0)
    col = lax.broadcasted_iota(jnp.int32, (chunk, chunk), 1)
    diff = (row - col).astype(F32)
    n = lax.broadcasted_iota(jnp.int32, (chunk, 1), 0).astype(F32)

    def head_consts(hh):
        lg = lg_ref[head0 + hh]
        decay = jnp.where(diff >= 0, jnp.exp(lg * jnp.maximum(diff, 0.0)), 0.0)
        cross = jnp.exp(lg * (n + 1.0))
        kdec = jnp.exp(lg * (chunk - 1.0 - n))
        carry = jnp.exp(lg * jnp.full((1, RET_DV), float(chunk), F32))
        return decay, cross, kdec, carry

    def mix(hh, c, consts):
        decay, cross, kdec, carry = consts
        rows = slice(c * chunk, (c + 1) * chunk)
        q = q_ref[rows, hh * RET_DK:(hh + 1) * RET_DK]
        k = k_ref[rows, hh * RET_DK:(hh + 1) * RET_DK]
        v = v_ref[rows, hh * RET_DV:(hh + 1) * RET_DV]
        inner = (_dot_nt(q, k) * decay).astype(BF16)
        if c > 0:
            s_old = s_ref[0, hh]
        elif has_state:
            s_old = s0_ref[0, hh]
        else:
            s_old = None
        kd = (k.astype(F32) * kdec).astype(BF16)
        o = _dot(inner, v)
        s_new = _dot_tn(kd, v)
        if s_old is not None:
            o = o + cross * _dot(q, s_old.astype(BF16))
            s_new = carry * s_old + s_new
        s_ref[0, hh] = s_new
        return o

    def norm_gate(hh, c, o):
        rows = slice(c * chunk, (c + 1) * chunk)
        cols = slice(hh * RET_DV, (hh + 1) * RET_DV)
        mu = jnp.mean(o, axis=-1, keepdims=True)
        dev = o - mu
        var = jnp.mean(dev * dev, axis=-1, keepdims=True)
        of = dev * lax.rsqrt(var + EPS) * gain_ref[:, cols]
        y_ref[rows, cols] = (_silu(g_ref[rows, cols].astype(F32)) * of).astype(BF16)

    consts = None
    pending = None
    for hh in range(heads):
        for c in range(n_chunks):
            if c == 0:
                consts = head_consts(hh)
            o = mix(hh, c, consts)
            if pending is not None:
                norm_gate(*pending)
            pending = (hh, c, o)
    norm_gate(*pending)


def _ret_core(qkvg, log_gamma, gain, state, *, batch, seq, chunk, heads):
    m = qkvg.shape[0]
    hk = RET_HEADS * RET_DK
    hv = RET_HEADS * RET_DV
    wk = heads * RET_DK
    wv = heads * RET_DV
    kblk = hk // wk
    vblk = 2 * hk // wv
    gblk = vblk + hv // wv
    has_state = state is not None
    in_specs = [
        pl.BlockSpec(memory_space=pltpu.SMEM),
        pl.BlockSpec((seq, wk), lambda b, g: (b, g)),
        pl.BlockSpec((seq, wk), lambda b, g: (b, kblk + g)),
        pl.BlockSpec((seq, wv), lambda b, g: (b, vblk + g)),
        pl.BlockSpec((seq, wv), lambda b, g: (b, gblk + g)),
        pl.BlockSpec((1, wv), lambda b, g: (0, g)),
    ]
    args = [log_gamma, qkvg, qkvg, qkvg, qkvg, gain.reshape(1, hv)]
    state_spec = pl.BlockSpec((1, heads, RET_DK, RET_DV), lambda b, g: (b, g, 0, 0))
    if has_state:
        in_specs.append(state_spec)
        args.append(state)
    kern = functools.partial(_ret_core_kernel, chunk=chunk, heads=heads, has_state=has_state)
    return pl.pallas_call(
        kern,
        out_shape=(jax.ShapeDtypeStruct((m, hv), BF16),
                   jax.ShapeDtypeStruct((batch, RET_HEADS, RET_DK, RET_DV), F32)),
        grid=(batch, RET_HEADS // heads),
        in_specs=in_specs,
        out_specs=(pl.BlockSpec((seq, wv), lambda b, g: (b, g)), state_spec),
        compiler_params=_params("parallel", "parallel"),
        name="ret_core",
    )(*args)


def _mm_res_kernel(a_ref, w_ref, r_ref, *rest):
    n_cast = (len(rest) - 1) // 2
    o_ref = rest[n_cast]
    o_ref[...] = r_ref[...] + _dot(a_ref[...], w_ref[...])
    for src, dst in zip(rest[:n_cast], rest[n_cast + 1:]):
        dst[...] = src[...].astype(BF16)


def _mm_res(a, w, res, *, tm, cast=()):
    m, k = a.shape
    n = w.shape[1]
    steps = m // tm
    in_specs = [
        pl.BlockSpec((tm, k), lambda i: (i, 0)),
        pl.BlockSpec((k, n), lambda i: (0, 0), pipeline_mode=pl.Buffered(1)),
        pl.BlockSpec((tm, n), lambda i: (i, 0)),
    ]
    out_shape = [jax.ShapeDtypeStruct((m, n), F32)]
    out_specs = [pl.BlockSpec((tm, n), lambda i: (i, 0))]
    for wf, layer in cast:
        _, rows, cols = wf.shape
        slab = rows // steps
        assert slab * steps == rows and slab % 16 == 0
        in_specs.append(pl.BlockSpec((None, slab, cols), lambda i, layer=layer: (layer, i, 0)))
        out_shape.append(jax.ShapeDtypeStruct((1, rows, cols), BF16))
        out_specs.append(pl.BlockSpec((None, slab, cols), lambda i: (0, i, 0)))
    outs = pl.pallas_call(
        _mm_res_kernel,
        out_shape=out_shape,
        grid=(steps,),
        in_specs=in_specs,
        out_specs=out_specs,
        compiler_params=_params("parallel"),
        name="mm_res",
    )(a, w, res, *[wf for wf, _ in cast])
    return outs[0], tuple(outs[1:])


def _latent_kernel(x_ref, g_ref, w_ref, gl_ref, cos_ref, sin_ref, c_ref, kr_ref, *, row_chunks):
    half = QK_ROPE // 2
    tc = x_ref.shape[0] // row_chunks

    def norm(c):
        x = x_ref[c * tc:(c + 1) * tc, :]
        return (x * _rms_scale(x) * g_ref[...]).astype(BF16)

    def project(c, xn):
        rows = slice(c * tc, (c + 1) * tc)
        y = _dot(xn, w_ref[...])
        cl = y[:, :KV_LORA]
        c_ref[rows, :] = cl * _rms_scale(cl) * gl_ref[...]
        x1 = y[:, KV_LORA:KV_LORA + half]
        x2 = y[:, KV_LORA + half:KV_LORA + QK_ROPE]
        cos = cos_ref[rows, :]
        sin = sin_ref[rows, :]
        kr_ref[rows, :] = jnp.concatenate([x1 * cos - x2 * sin, x1 * sin + x2 * cos], axis=-1)

    xn = norm(0)
    for c in range(row_chunks):
        xn_next = norm(c + 1) if c + 1 < row_chunks else None
        project(c, xn)
        xn = xn_next


def _latent(x, g, w, g_lat, cos, sin, *, tm):
    m, d = x.shape
    n = w.shape[1]
    half = QK_ROPE // 2
    tab_blocks = cos.shape[0] // tm
    return pl.pallas_call(
        functools.partial(_latent_kernel, row_chunks=tm // 128),
        out_shape=(jax.ShapeDtypeStruct((m, KV_LORA), F32),
                   jax.ShapeDtypeStruct((m, QK_ROPE), F32)),
        grid=(m // tm,),
        in_specs=[
            pl.BlockSpec((tm, d), lambda i: (i, 0)),
            pl.BlockSpec((1, d), lambda i: (0, 0)),
            pl.BlockSpec((d, n), lambda i: (0, 0)),
            pl.BlockSpec((1, KV_LORA), lambda i: (0, 0)),
            pl.BlockSpec((tm, half), lambda i: (i % tab_blocks, 0)),
            pl.BlockSpec((tm, half), lambda i: (i % tab_blocks, 0)),
        ],
        out_specs=(pl.BlockSpec((tm, KV_LORA), lambda i: (i, 0)),
                   pl.BlockSpec((tm, QK_ROPE), lambda i: (i, 0))),
        compiler_params=_params("parallel"),
        name="latent",
    )(x, g.reshape(1, d), w, g_lat.reshape(1, KV_LORA), cos, sin)


def _kv_up_body(seg_refs, seg_rows, wk_ref, wv_ref, gk_ref, k_ref, v_ref):
    refs = seg_refs
    g_nope = gk_ref[:, :QK_NOPE]
    g_rope = gk_ref[:, QK_NOPE:]
    lo = 0
    for s, rows in enumerate(seg_rows):
        c = refs[2 * s][...].astype(BF16)
        kr = refs[2 * s + 1][...]
        kn = _dot(c, wk_ref[...])
        v_ref[lo:lo + rows, :] = _dot(c, wv_ref[...]).astype(BF16)
        ss_rope = jnp.sum(kr * kr, axis=-1, keepdims=True)
        for h in range(MLA_HEADS):
            knh = kn[:, h * QK_NOPE:(h + 1) * QK_NOPE]
            ss = jnp.sum(knh * knh, axis=-1, keepdims=True) + ss_rope
            r = lax.rsqrt(ss * (1.0 / QK_HEAD) + EPS)
            k_ref[h, lo:lo + rows, :QK_NOPE] = (knh * r * g_nope).astype(BF16)
            k_ref[h, lo:lo + rows, QK_NOPE:] = (kr * r * g_rope).astype(BF16)
        lo += rows


def _kv_up_kernel(*refs, seg_rows):
    n_seg = len(seg_rows)
    wk_ref, wv_ref, gk_ref, k_ref, v_ref = refs[2 * n_seg:]
    _kv_up_body(refs[:2 * n_seg], seg_rows, wk_ref, wv_ref, gk_ref, k_ref, v_ref)


def _kv_up(segments, wk, wv, gk, *, steps):
    seg_rows = tuple(c.shape[0] // steps for c, _ in segments)
    tm = sum(seg_rows)
    m = steps * tm
    in_specs, args = [], []
    for (c, kr), rows in zip(segments, seg_rows):
        in_specs += [pl.BlockSpec((rows, KV_LORA), lambda i: (i, 0)),
                     pl.BlockSpec((rows, QK_ROPE), lambda i: (i, 0))]
        args += [c, kr]
    in_specs += [pl.BlockSpec(wk.shape, lambda i: (0, 0)),
                 pl.BlockSpec(wv.shape, lambda i: (0, 0)),
                 pl.BlockSpec((1, QK_HEAD), lambda i: (0, 0))]
    return pl.pallas_call(
        functools.partial(_kv_up_kernel, seg_rows=seg_rows),
        out_shape=(jax.ShapeDtypeStruct((MLA_HEADS, m, QK_HEAD), BF16),
                   jax.ShapeDtypeStruct((m, MLA_HEADS * V_HEAD), BF16)),
        grid=(steps,),
        in_specs=in_specs,
        out_specs=(pl.BlockSpec((MLA_HEADS, tm, QK_HEAD), lambda i: (0, i, 0)),
                   pl.BlockSpec((tm, MLA_HEADS * V_HEAD), lambda i: (i, 0))),
        compiler_params=_params("parallel"),
        name="kv_up",
    )(*args, wk, wv, gk.reshape(1, QK_HEAD))


Q_SLOT = 2 * QK_NOPE


def _q_proj_kernel(x_ref, g_ref, wdq_ref, gl_ref, wuq_ref, cos_ref, sin_ref, gq_ref, q_ref, *, row_chunks):
    g_nope = gq_ref[:, :QK_NOPE]
    g_rope = gq_ref[:, QK_NOPE:]
    tc = x_ref.shape[0] // row_chunks

    def project(c):
        rows = slice(c * tc, (c + 1) * tc)
        x = x_ref[rows, :]
        hn = (x * _rms_scale(x) * g_ref[...]).astype(BF16)
        ql = _dot(hn, wdq_ref[...])
        qn = (ql * _rms_scale(ql) * gl_ref[...]).astype(BF16)
        return _dot(qn, wuq_ref[...])

    def per_head(c, q):
        rows = slice(c * tc, (c + 1) * tc)
        cos = cos_ref[rows, :]
        sin = sin_ref[rows, :]
        for h in range(MLA_HEADS):
            nh = q[:, h * Q_SLOT:h * Q_SLOT + QK_NOPE]
            t = q[:, h * Q_SLOT + QK_NOPE:(h + 1) * Q_SLOT]
            rp = t * cos + pltpu.roll(t, QK_ROPE, 1) * sin
            ss = jnp.sum(nh * nh + rp * rp, axis=-1, keepdims=True)
            r = lax.rsqrt(ss * (1.0 / QK_HEAD) + EPS) * (QK_HEAD ** -0.5 * LOG2E)
            q_ref[h, rows, :QK_NOPE] = (nh * r * g_nope).astype(BF16)
            q_ref[h, rows, QK_NOPE:] = (rp[:, :QK_ROPE] * r * g_rope).astype(BF16)

    q = project(0)
    for c in range(row_chunks):
        q_next = project(c + 1) if c + 1 < row_chunks else None
        per_head(c, q)
        q = q_next


def _q_proj(x, g, wdq, g_lat, wuq, cos, sin, gq, *, tm):
    m, d = x.shape
    tab_blocks = cos.shape[0] // tm
    tw = cos.shape[1]
    return pl.pallas_call(
        functools.partial(_q_proj_kernel, row_chunks=tm // 256),
        out_shape=jax.ShapeDtypeStruct((MLA_HEADS, m, QK_HEAD), BF16),
        grid=(m // tm,),
        in_specs=[
            pl.BlockSpec((tm, d), lambda i: (i, 0)),
            pl.BlockSpec((1, d), lambda i: (0, 0)),
            pl.BlockSpec(wdq.shape, lambda i: (0, 0)),
            pl.BlockSpec((1, Q_LORA), lambda i: (0, 0)),
            pl.BlockSpec(wuq.shape, lambda i: (0, 0)),
            pl.BlockSpec((tm, tw), lambda i: (i % tab_blocks, 0)),
            pl.BlockSpec((tm, tw), lambda i: (i % tab_blocks, 0)),
            pl.BlockSpec((1, QK_HEAD), lambda i: (0, 0)),
        ],
        out_specs=pl.BlockSpec((MLA_HEADS, tm, QK_HEAD), lambda i: (0, i, 0)),
        compiler_params=_params("parallel"),
        name="q_proj",
    )(x, g.reshape(1, d), wdq, g_lat.reshape(1, Q_LORA), wuq, cos, sin, gq.reshape(1, QK_HEAD))


def _attn_causal_kernel(q_ref, k_ref, v_ref, *rest, heads, tq):
    n_cast = (len(rest) - 1) // 2
    o_ref = rest[n_cast]
    for src, dst in zip(rest[:n_cast], rest[n_cast + 1:]):
        dst[...] = src[...].astype(BF16)
    n_q = q_ref.shape[1] // tq
    hq = tq // 2
    krow = lax.broadcasted_iota(jnp.int32, (hq, tq), 0) // CHUNK
    qcol = lax.broadcasted_iota(jnp.int32, (hq, tq), 1) // CHUNK
    keep_a = krow <= qcol
    keep_b = keep_a[:, :hq]

    def right_half(full, fn, part):
        return jnp.concatenate([full[:, :hq], fn(full[:, hq:], part)], axis=1)

    def scores(h, qi):
        lo = qi * tq
        q = q_ref[h, lo:lo + tq, :]
        st_a = jnp.where(keep_a, _dot_nt(k_ref[h, lo:lo + hq, :], q), NEG_INF)
        st_b = jnp.where(keep_b, _dot_nt(k_ref[h, lo + hq:lo + tq, :], q[hq:, :]), NEG_INF)
        st_p = _dot_nt(k_ref[h, :lo, :], q) if qi > 0 else None
        return st_a, st_b, st_p

    def softmax(st):
        st_a, st_b, st_p = st
        m = jnp.max(st_a, axis=0, keepdims=True)
        if st_p is not None:
            m = jnp.maximum(m, jnp.max(st_p, axis=0, keepdims=True))
        m = right_half(m, jnp.maximum, jnp.max(st_b, axis=0, keepdims=True))
        pt_a = jnp.exp2(st_a - m)
        pt_b = jnp.exp2(st_b - m[:, hq:])
        l = jnp.sum(pt_a, axis=0, keepdims=True)
        pt_p = None
        if st_p is not None:
            pt_p = jnp.exp2(st_p - m)
            l = l + jnp.sum(pt_p, axis=0, keepdims=True)
            pt_p = pt_p.astype(BF16)
        l = right_half(l, jnp.add, jnp.sum(pt_b, axis=0, keepdims=True))
        return pt_a.astype(BF16), pt_b.astype(BF16), pt_p, l

    def values(h, qi, p):
        pt_a, pt_b, pt_p, l = p
        lo = qi * tq
        cols = slice(h * V_HEAD, (h + 1) * V_HEAD)
        acc_t = _dot_tn(v_ref[lo:lo + hq, cols], pt_a)
        if pt_p is not None:
            acc_t = acc_t + _dot_tn(v_ref[:lo, cols], pt_p)
        acc_t = right_half(acc_t, jnp.add, _dot_tn(v_ref[lo + hq:lo + tq, cols], pt_b))
        o_ref[lo:lo + tq, cols] = (acc_t / l).T.astype(BF16)

    work = [(h, qi) for h in range(heads) for qi in range(n_q)]
    s_q, p_q = {}, {}
    for step in range(len(work) + 2):
        if step < len(work):
            s_q[step] = scores(*work[step])
        if 0 <= step - 1 < len(work):
            p_q[step - 1] = softmax(s_q.pop(step - 1))
        if 0 <= step - 2 < len(work):
            values(*work[step - 2], p_q.pop(step - 2))


def _kv_attn_kernel(*refs, seg_rows):
    n_seg = len(seg_rows)
    wk_ref, wv_ref, gk_ref, q_ref, o_ref, k_scr, v_scr = refs[2 * n_seg:]
    g_nope = gk_ref[:, :QK_NOPE]
    g_rope = gk_ref[:, QK_NOPE:]

    group = 2
    width = group * QK_NOPE
    segs, lo = [], 0
    for s, rows in enumerate(seg_rows):
        c = refs[2 * s][...].astype(BF16)
        kr = refs[2 * s + 1][...]
        segs.append((lo, rows, c, kr, jnp.sum(kr * kr, axis=-1, keepdims=True)))
        lo += rows

    def project(g):
        cols = slice(g * width, (g + 1) * width)
        kn = []
        for lo, rows, c, _, _ in segs:
            v_scr[lo:lo + rows, cols] = _dot(c, wv_ref[:, cols]).astype(BF16)
            kn.append(_dot(c, wk_ref[:, cols]))
        return kn

    kn_groups = {}

    def key_norm(h):
        g, hh = divmod(h, group)
        for (lo, rows, _, kr, ss_rope), kn in zip(segs, kn_groups[g]):
            knh = kn[:, hh * QK_NOPE:(hh + 1) * QK_NOPE]
            ss = jnp.sum(knh * knh, axis=-1, keepdims=True) + ss_rope
            r = lax.rsqrt(ss * (1.0 / QK_HEAD) + EPS)
            k_scr[h, lo:lo + rows, :QK_NOPE] = (knh * r * g_nope).astype(BF16)
            k_scr[h, lo:lo + rows, QK_NOPE:] = (kr * r * g_rope).astype(BF16)

    def scores(h):
        return _dot_nt(q_ref[h], k_scr[h])

    def softmax(s):
        m = jnp.max(s, axis=-1, keepdims=True)
        p = jnp.exp2(s - m)
        return p.astype(BF16), jnp.sum(p, axis=-1, keepdims=True)

    def values(h, pl_):
        p, l = pl_
        cols = slice(h * V_HEAD, (h + 1) * V_HEAD)
        o_ref[:, cols] = (_dot(p, v_scr[:, cols]) / l).astype(BF16)

    s_q, p_q = {}, {}
    kn_groups[0] = project(0)
    for step in range(MLA_HEADS + 3):
        if step % group == 0 and step // group + 1 < MLA_HEADS // group:
            kn_groups[step // group + 1] = project(step // group + 1)
        if step < MLA_HEADS:
            key_norm(step)
        if 0 <= step - 1 < MLA_HEADS:
            s_q[step - 1] = scores(step - 1)
        if 0 <= step - 2 < MLA_HEADS:
            p_q[step - 2] = softmax(s_q.pop(step - 2))
        if 0 <= step - 3 < MLA_HEADS:
            values(step - 3, p_q.pop(step - 3))


def _kv_attention(segments, wk, wv, gk, q, *, batch):
    seg_rows = tuple(c.shape[0] // batch for c, _ in segments)
    t_kv = sum(seg_rows)
    t_q = q.shape[1] // batch
    in_specs, args = [], []
    for (c, kr), rows in zip(segments, seg_rows):
        in_specs += [pl.BlockSpec((rows, KV_LORA), lambda b: (b, 0)),
                     pl.BlockSpec((rows, QK_ROPE), lambda b: (b, 0))]
        args += [c, kr]
    in_specs += [pl.BlockSpec(wk.shape, lambda b: (0, 0)),
                 pl.BlockSpec(wv.shape, lambda b: (0, 0)),
                 pl.BlockSpec((1, QK_HEAD), lambda b: (0, 0)),
                 pl.BlockSpec((MLA_HEADS, t_q, QK_HEAD), lambda b: (0, b, 0))]
    return pl.pallas_call(
        functools.partial(_kv_attn_kernel, seg_rows=seg_rows),
        out_shape=jax.ShapeDtypeStruct((batch * t_q, MLA_HEADS * V_HEAD), BF16),
        grid=(batch,),
        in_specs=in_specs,
        out_specs=pl.BlockSpec((t_q, MLA_HEADS * V_HEAD), lambda b: (b, 0)),
        scratch_shapes=[pltpu.VMEM((MLA_HEADS, t_kv, QK_HEAD), BF16),
                        pltpu.VMEM((t_kv, MLA_HEADS * V_HEAD), BF16)],
        compiler_params=_params("parallel"),
        name="kv_attention",
    )(*args, wk, wv, gk.reshape(1, QK_HEAD), q)


def _attention(q, k, v, *, batch, t_q, t_kv, heads, tq, cast=()):
    hg = MLA_HEADS // heads
    steps = batch * hg
    kern = functools.partial(_attn_causal_kernel, heads=heads, tq=tq)
    in_specs = [
        pl.BlockSpec((heads, t_q, QK_HEAD), lambda b, g: (g, b, 0)),
        pl.BlockSpec((heads, t_kv, QK_HEAD), lambda b, g: (g, b, 0)),
        pl.BlockSpec((t_kv, heads * V_HEAD), lambda b, g: (b, g)),
    ]
    out_shape = [jax.ShapeDtypeStruct((batch * t_q, MLA_HEADS * V_HEAD), BF16)]
    out_specs = [pl.BlockSpec((t_q, heads * V_HEAD), lambda b, g: (b, g))]
    for wf, layer in cast:
        _, rows, cols = wf.shape
        slab = rows // steps
        assert slab * steps == rows and slab % 16 == 0
        in_specs.append(pl.BlockSpec((None, slab, cols), lambda b, g, layer=layer: (layer, b * hg + g, 0)))
        out_shape.append(jax.ShapeDtypeStruct((1, rows, cols), BF16))
        out_specs.append(pl.BlockSpec((None, slab, cols), lambda b, g: (0, b * hg + g, 0)))
    outs = pl.pallas_call(
        kern,
        out_shape=out_shape,
        grid=(batch, hg),
        in_specs=in_specs,
        out_specs=out_specs,
        compiler_params=_params("parallel", "parallel"),
        name="attention",
    )(q, k, v, *[wf for wf, _ in cast])
    return outs[0], tuple(outs[1:])


def _rope_tables(pos, half, reps, rows):
    inv = jnp.power(ROPE_BASE, -jnp.arange(half, dtype=F32) / half)
    ang = pos.astype(F32)[:, None] * inv[None, :]
    cos = jnp.tile(jnp.cos(ang), (max(rows // pos.shape[0], 1), reps))
    sin = jnp.tile(jnp.sin(ang), (max(rows // pos.shape[0], 1), reps))
    return cos, sin


def _mm_res_and_ffn2(a, w, h, layer, p):
    key = ("ffn2_bf16", layer)
    if key not in p:
        w_in = p.get(("ffn2_w_in_bf16", layer))
        todo = [(p["ffn2_w_out"], layer)] if w_in is not None else [(p["ffn2_w_in"], layer), (p["ffn2_w_out"], layer)]
        h, done = _mm_res(a, w, h, tm=512, cast=tuple(todo))
        p[key] = (w_in, done[0]) if w_in is not None else done
    else:
        h, _ = _mm_res(a, w, h, tm=512)
    w_in, w_out = p[key]
    return _ffn(h, p["ffn2_norm"][layer], w_in, w_out, 0, tm=FFN_TM, tf=FFN_TF)


def _trunk(x, pos, ret_state, ckv_past, krope_past, p):
    batch, seq, d = x.shape
    m = batch * seq
    tm = 1024
    h = x.reshape(m, d)

    if "ret_w_in_bf16" not in p:
        h, p["ret_w_in_bf16"] = _ffn(h, p["ffn1_norm"][0], p["ffn1_w_in_l0"], p["ffn1_w_out_l0"], 0,
                                     tm=FFN_TM, tf=FFN_TF, cast=p["ret_w_in"])
    else:
        h = _ffn(h, p["ffn1_norm"][0], p["ffn1_w_in_l0"], p["ffn1_w_out_l0"], 0, tm=FFN_TM, tf=FFN_TF)
    cos, sin = _rope_tables(pos, RET_DK // 2, 1, tm)
    key = "cast_by_ret_in"
    cast = () if key in p else ((p["ffn1_w_in"], 1), (p["ffn1_w_out"], 1), (p["ret_w_out"], 0), (p["w_o"], 0))
    qkvg, cast_out = _ret_in(h, p["mix_norm"][0], p["ret_w_in_bf16"][0], cos, sin, tm=tm, tn=2048, cast=cast)
    if cast:
        p[key] = cast_out
    ffn1_l1_w_in, ffn1_l1_w_out, ret_w_out, w_o = p[key]
    log_gamma = jnp.log1p(-jnp.exp2(-5.0 - jnp.arange(RET_HEADS, dtype=F32)))
    chunk = min(seq, 256)
    y, s_fin = _ret_core(qkvg, log_gamma, p["ret_gn_gain"][0],
                         None if ret_state is None else ret_state[0],
                         batch=batch, seq=seq, chunk=chunk,
                         heads=max(1, min(RET_HEADS, 512 // seq)))
    h = _mm_res_and_ffn2(y, ret_w_out[0], h, 0, p)

    cos, sin = _rope_tables(pos, QK_ROPE // 2, 1, tm)
    c_new, kr_new = _latent(h, p["kv_norm"], p["w_dkv"], p["kv_lat_norm"], cos, sin, tm=tm)
    if ckv_past is None:
        k_sh, v_sh = _kv_up([(c_new, kr_new)], p["w_uk"], p["w_uv"], p["k_norm"], steps=m // tm)
    else:
        past = ckv_past.shape[1]
        segments = [(ckv_past.reshape(batch * past, KV_LORA), krope_past.reshape(batch * past, QK_ROPE)),
                    (c_new, kr_new)]

    h = _ffn(h, p["ffn1_norm"][1], ffn1_l1_w_in, ffn1_l1_w_out, 0, tm=FFN_TM, tf=FFN_TF)
    cos, sin = _rope_tables(pos, QK_ROPE // 2, 1, tm)
    zeros = jnp.zeros_like(cos)
    cos = jnp.concatenate([cos, cos, zeros, zeros], axis=-1)
    sin = jnp.concatenate([-sin, sin, zeros, zeros], axis=-1)
    q = _q_proj(h, p["mix_norm"][1], p["w_dq"][0], p["q_lat_norm"][0], p["w_uq"][0], cos, sin,
                p["q_norm"][0], tm=tm)
    if ckv_past is None:
        key = ("ffn2_w_in_bf16", 1)
        o, cast_out = _attention(q, k_sh, v_sh, batch=batch, t_q=seq, t_kv=seq, heads=4, tq=512,
                                 cast=() if key in p else ((p["ffn2_w_in"], 1),))
        if cast_out:
            p[key] = cast_out[0]
    else:
        o = _kv_attention(segments, p["w_uk"], p["w_uv"], p["k_norm"], q, batch=batch)
    h = _mm_res_and_ffn2(o, w_o[0], h, 1, p)

    return (h.reshape(batch, seq, d), s_fin[None],
            c_new.reshape(batch, seq, KV_LORA), kr_new.reshape(batch, seq, QK_ROPE))


def kernel(x_prompt, x_sample, state_ret, cache_ckv, cache_krope, ffn1_norm, ffn1_w_in, ffn1_w_out, mix_norm, ffn2_norm, ffn2_w_in, ffn2_w_out, ret_w_in, ret_gn_gain, ret_w_out, kv_norm, w_dkv, kv_lat_norm, w_ukv, k_norm, w_dq, q_lat_norm, w_uq, q_norm, w_o):
    bf = lambda w: w.astype(BF16)
    w_ukv_h = w_ukv.reshape(KV_LORA, MLA_HEADS, QK_NOPE + V_HEAD)
    w_uk = w_ukv_h[:, :, :QK_NOPE].reshape(KV_LORA, MLA_HEADS * QK_NOPE)
    w_uv = w_ukv_h[:, :, QK_NOPE:].reshape(KV_LORA, MLA_HEADS * V_HEAD)
    half = QK_ROPE // 2
    w_uq_h = w_uq.reshape(w_uq.shape[0], Q_LORA, MLA_HEADS, QK_HEAD)
    w_lo = w_uq_h[..., QK_NOPE:QK_NOPE + half]
    w_hi = w_uq_h[..., QK_NOPE + half:]
    w_uq_p = jnp.concatenate([w_uq_h[..., :QK_NOPE], w_lo, w_hi, w_hi, w_lo], axis=-1)
    w_uq_p = w_uq_p.reshape(w_uq.shape[0], Q_LORA, MLA_HEADS * Q_SLOT)
    p = dict(ffn1_norm=ffn1_norm, ffn1_w_in=ffn1_w_in, ffn1_w_out=ffn1_w_out,
             ffn1_w_in_l0=bf(ffn1_w_in[:1]), ffn1_w_out_l0=bf(ffn1_w_out[:1]),
             mix_norm=mix_norm, ffn2_norm=ffn2_norm, ffn2_w_in=ffn2_w_in, ffn2_w_out=ffn2_w_out,
             ret_w_in=ret_w_in, ret_gn_gain=ret_gn_gain, ret_w_out=ret_w_out,
             kv_norm=kv_norm, w_dkv=bf(w_dkv), kv_lat_norm=kv_lat_norm,
             w_uk=bf(w_uk), w_uv=bf(w_uv), k_norm=k_norm,
             w_dq=bf(w_dq), q_lat_norm=q_lat_norm, w_uq=bf(w_uq_p), q_norm=q_norm, w_o=w_o)
    pos_prompt = jnp.arange(x_prompt.shape[1])
    past = cache_ckv.shape[1]
    pos_sample = past + jnp.arange(x_sample.shape[1])
    y_p, ret_p, ckv_p, kr_p = _trunk(x_prompt, pos_prompt, None, None, None, p)
    y_s, ret_s, ckv_s, kr_s = _trunk(x_sample, pos_sample, state_ret, cache_ckv, cache_krope, p)
    return (y_p, y_s, ret_p, ckv_p, kr_p, ret_s, ckv_s, kr_s)
```

```python
import functools

import jax
import jax.numpy as jnp
from jax import lax
from jax.experimental import pallas as pl
from jax.experimental.pallas import tpu as pltpu

F32 = jnp.float32
BF16 = jnp.bfloat16

D_MODEL = 2048
CHUNK = 64
D_FF = 5632
RET_HEADS = 8
RET_DK = D_MODEL // RET_HEADS
RET_DV = 2 * D_MODEL // RET_HEADS
MLA_HEADS = 16
QK_NOPE = 128
QK_ROPE = 64
QK_HEAD = QK_NOPE + QK_ROPE
V_HEAD = 128
KV_LORA = 512
Q_LORA = 512
ROPE_BASE = 10000.0
EPS = 1e-6
NEG_INF = -1e30
LOG2E = 1.4426950408889634

VMEM_LIMIT_BYTES = 60 * 1024 * 1024

FFN_TM = 1024
FFN_TF = 512
FFN_NORM_CHUNKS = 4
FFN_DMA_BLOCKS = 2


def _params(*semantics):
    return pltpu.CompilerParams(dimension_semantics=semantics,
                                vmem_limit_bytes=VMEM_LIMIT_BYTES)


def _rms_scale(x):
    return lax.rsqrt(jnp.mean(x * x, axis=-1, keepdims=True) + EPS)


def _dot(a, b):
    return jnp.dot(a, b, preferred_element_type=F32)


def _dot_nt(a, b):
    return lax.dot_general(a, b, (((1,), (1,)), ((), ())), preferred_element_type=F32)


def _dot_tn(a, b):
    return lax.dot_general(a, b, (((0,), (0,)), ((), ())), preferred_element_type=F32)


def _silu(x):
    return x * jax.nn.sigmoid(x)


def _ffn_kernel(*refs, layer, tf, cast_pieces):
    if cast_pieces:
        (g_ref, x_hbm, win_hbm, wout_hbm, cast_src, o_hbm, cast_dst, acc, xn_ref, wa_buf, wb_buf, wo_buf,
         wsem, xsem, osem, stage_in, stage_out, csem_in, csem_out) = refs
    else:
        g_ref, x_hbm, win_hbm, wout_hbm, o_hbm, acc, xn_ref, wa_buf, wb_buf, wo_buf, wsem, xsem, osem = refs
    i = pl.program_id(0)
    n_tiles = pl.num_programs(0)
    tm = acc.shape[1]
    dma_cols = FFN_DMA_BLOCKS * tf
    n_pos = 1 + (D_FF - tf) // dma_cols
    last = n_pos - 1
    a_slot = i % 2

    def weight_copies(pos, slot):
        col = pl.multiple_of(jnp.maximum(pos * dma_cols - (dma_cols - tf), 0), tf)
        return (
            pltpu.make_async_copy(win_hbm.at[layer, :, pl.ds(col, dma_cols)], wa_buf.at[slot], wsem.at[slot, 0]),
            pltpu.make_async_copy(win_hbm.at[layer, :, pl.ds(D_FF + col, dma_cols)], wb_buf.at[slot], wsem.at[slot, 1]),
            pltpu.make_async_copy(wout_hbm.at[layer, pl.ds(col, dma_cols), :], wo_buf.at[slot], wsem.at[slot, 2]),
        )

    def x_copy(tile, slot):
        src = x_hbm.at[pl.ds(pl.multiple_of(tile * tm, tm), tm), :]
        return pltpu.make_async_copy(src, acc.at[slot], xsem.at[slot])

    def out_copy(tile, slot):
        dst = o_hbm.at[pl.ds(pl.multiple_of(tile * tm, tm), tm), :]
        return pltpu.make_async_copy(acc.at[slot], dst, osem.at[slot])

    def update(xn, slot, cols):
        a = _dot(xn, wa_buf[slot, :, :cols])
        b = _dot(xn, wb_buf[slot, :, :cols])
        h = (_silu(a) * (0.5 * b)).astype(BF16)
        return _dot(h, wo_buf[slot, :cols, :])

    @pl.when(i == 0)
    def _():
        x_copy(0, 0).start()
        for cp in weight_copies(0, 0):
            cp.start()

    x_copy(i, a_slot).wait()

    def sync(pos, w_slot):
        for cp in weight_copies(pos, w_slot):
            cp.wait()

        nxt = jnp.where(pos == last, 0, pos + 1)

        @pl.when(jnp.logical_or(pos < last, i + 1 < n_tiles))
        def _():
            for cp in weight_copies(nxt, 1 - w_slot):
                cp.start()

        @pl.when(pos == 1)
        def _():
            @pl.when(i > 0)
            def _():
                out_copy(i - 1, 1 - a_slot).wait()

            @pl.when(i + 1 < n_tiles)
            def _():
                x_copy(i + 1, 1 - a_slot).start()

        if cast_pieces:
            side_cast(pos)

    def cast_copies(n, slot):
        piece = stage_in.shape[1]
        rows = pl.ds(pl.multiple_of((i * cast_pieces + n) * piece, piece), piece)
        return (pltpu.make_async_copy(cast_src.at[0, rows, :], stage_in.at[slot], csem_in.at[slot]),
                pltpu.make_async_copy(stage_out.at[slot], cast_dst.at[0, rows, :], csem_out.at[slot]))

    def side_cast(pos):
        @pl.when(pos == 0)
        def _():
            cast_copies(0, 0)[0].start()

        @pl.when((pos >= 1) & (pos <= cast_pieces))
        def _():
            n = pos - 1
            slot = n % 2
            cast_copies(n, slot)[0].wait()

            @pl.when(n + 1 < cast_pieces)
            def _():
                cast_copies(n + 1, 1 - slot)[0].start()

            @pl.when(n >= 2)
            def _():
                cast_copies(n - 2, slot)[1].wait()

            stage_out[slot] = stage_in[slot].astype(BF16)
            cast_copies(n, slot)[1].start()

        @pl.when(pos == cast_pieces + 1)
        def _():
            for n in (cast_pieces - 2, cast_pieces - 1):
                cast_copies(n, n % 2)[1].wait()

    sync(0, 0)
    tc = tm // FFN_NORM_CHUNKS

    def norm(c):
        x = acc[a_slot, c * tc:(c + 1) * tc, :]
        xn = (x * _rms_scale(x) * g_ref[...]).astype(BF16)
        xn_ref[c * tc:(c + 1) * tc, :] = xn
        return x, xn

    cur = norm(0)
    for c in range(FFN_NORM_CHUNKS):
        nxt_c = norm(c + 1) if c + 1 < FFN_NORM_CHUNKS else None
        x, xn = cur
        acc[a_slot, c * tc:(c + 1) * tc, :] = x + update(xn, 0, tf)
        cur = nxt_c

    def position(pos, carry):
        w_slot = pos % 2
        sync(pos, w_slot)
        acc[a_slot] += update(xn_ref[...], w_slot, dma_cols)
        return carry

    lax.fori_loop(1, n_pos, position, 0)

    out_copy(i, a_slot).start()

    @pl.when(i + 1 == n_tiles)
    def _():
        out_copy(i, a_slot).wait()


def _ffn(x, g, w_in, w_out, layer, *, tm, tf, cast=None):
    m, d = x.shape
    n_tiles = m // tm
    dma_cols = FFN_DMA_BLOCKS * tf
    n_pos = 1 + (D_FF - tf) // dma_cols
    assert (D_FF - tf) % dma_cols == 0 and n_pos % 2 == 0
    in_specs = [pl.BlockSpec((1, d), lambda i: (0, 0))] + [pl.BlockSpec(memory_space=pl.ANY)] * 3
    out_shape = [jax.ShapeDtypeStruct((m, d), F32)]
    scratch = [
        pltpu.VMEM((2, tm, d), F32),
        pltpu.VMEM((tm, d), BF16),
        pltpu.VMEM((2, d, dma_cols), BF16),
        pltpu.VMEM((2, d, dma_cols), BF16),
        pltpu.VMEM((2, dma_cols, d), BF16),
        pltpu.SemaphoreType.DMA((2, 3)),
        pltpu.SemaphoreType.DMA((2,)),
        pltpu.SemaphoreType.DMA((2,)),
    ]
    args = [g.reshape(1, d), x, w_in, w_out]
    cast_pieces = 0
    if cast is not None:
        cast_pieces = n_pos - 2
        _, rows, cols = cast.shape
        piece = rows // (n_tiles * cast_pieces)
        assert piece * n_tiles * cast_pieces == rows and piece % 16 == 0 and cast_pieces >= 2
        in_specs.append(pl.BlockSpec(memory_space=pl.ANY))
        out_shape.append(jax.ShapeDtypeStruct(cast.shape, BF16))
        scratch += [pltpu.VMEM((2, piece, cols), F32), pltpu.VMEM((2, piece, cols), BF16),
                    pltpu.SemaphoreType.DMA((2,)), pltpu.SemaphoreType.DMA((2,))]
        args.append(cast)
    outs = pl.pallas_call(
        functools.partial(_ffn_kernel, layer=layer, tf=tf, cast_pieces=cast_pieces),
        out_shape=out_shape,
        grid=(n_tiles,),
        in_specs=in_specs,
        out_specs=[pl.BlockSpec(memory_space=pl.ANY)] * len(out_shape),
        scratch_shapes=scratch,
        compiler_params=_params("arbitrary"),
        name="ffn",
    )(*args)
    return outs[0] if cast is None else (outs[0], outs[1])


def _ret_in_kernel(x_ref, g_ref, w_ref, cos_ref, sin_ref, *rest, rope_blocks, heads_per_block, cast_steps):
    n_cast = len(cast_steps)
    o_ref = rest[n_cast]
    xn_ref = rest[-1]
    j = pl.program_id(1)

    for src, dst, steps in zip(rest[:n_cast], rest[n_cast + 1:-1], cast_steps):
        @pl.when(j < steps)
        def _(src=src, dst=dst):
            dst[...] = src[...].astype(BF16)

    @pl.when(j == 0)
    def _():
        x = x_ref[...]
        xn_ref[...] = (x * _rms_scale(x) * g_ref[...]).astype(BF16)

    half = RET_DK // 2

    @pl.when(j < 2 * rope_blocks)
    def _():
        scale = jnp.where(j < rope_blocks, RET_DK ** -0.5, 1.0).astype(F32)
        cos = cos_ref[...] * scale
        sin = sin_ref[...] * scale
        xn = xn_ref[...]
        for h in range(heads_per_block):
            lo = h * RET_DK
            y = _dot(xn, w_ref[:, lo:lo + RET_DK])
            x1 = y[:, :half]
            x2 = y[:, half:]
            o_ref[:, lo:lo + half] = (x1 * cos - x2 * sin).astype(BF16)
            o_ref[:, lo + half:lo + RET_DK] = (x1 * sin + x2 * cos).astype(BF16)

    @pl.when(j >= 2 * rope_blocks)
    def _():
        o_ref[...] = _dot(xn_ref[...], w_ref[...]).astype(BF16)


def _ret_in(x, g, w, cos, sin, *, tm, tn, cast=()):
    m, d = x.shape
    n = w.shape[1]
    tab_blocks = cos.shape[0] // tm
    n_i, n_j = m // tm, n // tn
    in_specs = [
        pl.BlockSpec((tm, d), lambda i, j: (i, 0)),
        pl.BlockSpec((1, d), lambda i, j: (0, 0)),
        pl.BlockSpec((d, tn), lambda i, j: (0, j)),
        pl.BlockSpec((tm, RET_DK // 2), lambda i, j: (i % tab_blocks, 0)),
        pl.BlockSpec((tm, RET_DK // 2), lambda i, j: (i % tab_blocks, 0)),
    ]
    out_shape = [jax.ShapeDtypeStruct((m, n), BF16)]
    out_specs = [pl.BlockSpec((tm, tn), lambda i, j: (i, j))]
    cast_steps = []
    for wf, layer in cast:
        _, rows, cols = wf.shape
        slab = rows // n_i
        assert slab * n_i == rows and slab % 16 == 0
        steps = max(s for s in range(1, n_j + 1) if cols % (128 * s) == 0)
        cast_steps.append(steps)
        in_specs.append(pl.BlockSpec((None, slab, cols // steps),
                                     lambda i, j, layer=layer, steps=steps: (layer, i, jnp.minimum(j, steps - 1))))
        out_shape.append(jax.ShapeDtypeStruct((1, rows, cols), BF16))
        out_specs.append(pl.BlockSpec((None, slab, cols // steps),
                                      lambda i, j, steps=steps: (0, i, jnp.minimum(j, steps - 1))))
    kern = functools.partial(_ret_in_kernel, rope_blocks=(RET_HEADS * RET_DK) // tn,
                             heads_per_block=tn // RET_DK, cast_steps=tuple(cast_steps))
    outs = pl.pallas_call(
        kern,
        out_shape=out_shape,
        grid=(n_i, n_j),
        in_specs=in_specs,
        out_specs=out_specs,
        scratch_shapes=[pltpu.VMEM((tm, d), BF16)],
        compiler_params=_params("parallel", "arbitrary"),
        name="ret_in",
    )(x, g.reshape(1, d), w, cos, sin, *[wf for wf, _ in cast])
    return outs[0], tuple(outs[1:])


def _ret_core_kernel(lg_ref, q_ref, k_ref, v_ref, g_ref, gain_ref, *rest, chunk, heads, has_state):
    if has_state:
        s0_ref, y_ref, s_ref = rest
    else:
        y_ref, s_ref = rest
    head0 = pl.program_id(1) * heads
    n_chunks = q_ref.shape[0] // chunk
    row = lax.broadcasted_iota(jnp.int32, (chunk, chunk), 0)
    col = lax.broadcasted_iota(jnp.int32, (chunk, chunk), 1)
    diff = (row - col).astype(F32)
    n = lax.broadcasted_iota(jnp.int32, (chunk, 1), 0).astype(F32)

    def head_consts(hh):
        lg = lg_ref[head0 + hh]
        decay = jnp.where(diff >= 0, jnp.exp(lg * jnp.maximum(diff, 0.0)), 0.0)
        cross = jnp.exp(lg * (n + 1.0))
        kdec = jnp.exp(lg * (chunk - 1.0 - n))
        carry = jnp.exp(lg * jnp.full((1, RET_DV), float(chunk), F32))
        return decay, cross, kdec, carry

    def mix(hh, c, consts):
        decay, cross, kdec, carry = consts
        rows = slice(c * chunk, (c + 1) * chunk)
        q = q_ref[rows, hh * RET_DK:(hh + 1) * RET_DK]
        k = k_ref[rows, hh * RET_DK:(hh + 1) * RET_DK]
        v = v_ref[rows, hh * RET_DV:(hh + 1) * RET_DV]
        inner = (_dot_nt(q, k) * decay).astype(BF16)
        if c > 0:
            s_old = s_ref[0, hh]
        elif has_state:
            s_old = s0_ref[0, hh]
        else:
            s_old = None
        kd = (k.astype(F32) * kdec).astype(BF16)
        o = _dot(inner, v)
        s_new = _dot_tn(kd, v)
        if s_old is not None:
            o = o + cross * _dot(q, s_old.astype(BF16))
            s_new = carry * s_old + s_new
        s_ref[0, hh] = s_new
        return o

    def norm_gate(hh, c, o):
        rows = slice(c * chunk, (c + 1) * chunk)
        cols = slice(hh * RET_DV, (hh + 1) * RET_DV)
        mu = jnp.mean(o, axis=-1, keepdims=True)
        dev = o - mu
        var = jnp.mean(dev * dev, axis=-1, keepdims=True)
        of = dev * lax.rsqrt(var + EPS) * gain_ref[:, cols]
        y_ref[rows, cols] = (_silu(g_ref[rows, cols].astype(F32)) * of).astype(BF16)

    consts = None
    pending = None
    for hh in range(heads):
        for c in range(n_chunks):
            if c == 0:
                consts = head_consts(hh)
            o = mix(hh, c, consts)
            if pending is not None:
                norm_gate(*pending)
            pending = (hh, c, o)
    norm_gate(*pending)


def _ret_core(qkvg, log_gamma, gain, state, *, batch, seq, chunk, heads):
    m = qkvg.shape[0]
    hk = RET_HEADS * RET_DK
    hv = RET_HEADS * RET_DV
    wk = heads * RET_DK
    wv = heads * RET_DV
    kblk = hk // wk
    vblk = 2 * hk // wv
    gblk = vblk + hv // wv
    has_state = state is not None
    in_specs = [
        pl.BlockSpec(memory_space=pltpu.SMEM),
        pl.BlockSpec((seq, wk), lambda b, g: (b, g)),
        pl.BlockSpec((seq, wk), lambda b, g: (b, kblk + g)),
        pl.BlockSpec((seq, wv), lambda b, g: (b, vblk + g)),
        pl.BlockSpec((seq, wv), lambda b, g: (b, gblk + g)),
        pl.BlockSpec((1, wv), lambda b, g: (0, g)),
    ]
    args = [log_gamma, qkvg, qkvg, qkvg, qkvg, gain.reshape(1, hv)]
    state_spec = pl.BlockSpec((1, heads, RET_DK, RET_DV), lambda b, g: (b, g, 0, 0))
    if has_state:
        in_specs.append(state_spec)
        args.append(state)
    kern = functools.partial(_ret_core_kernel, chunk=chunk, heads=heads, has_state=has_state)
    return pl.pallas_call(
        kern,
        out_shape=(jax.ShapeDtypeStruct((m, hv), BF16),
                   jax.ShapeDtypeStruct((batch, RET_HEADS, RET_DK, RET_DV), F32)),
        grid=(batch, RET_HEADS // heads),
        in_specs=in_specs,
        out_specs=(pl.BlockSpec((seq, wv), lambda b, g: (b, g)), state_spec),
        compiler_params=_params("parallel", "parallel"),
        name="ret_core",
    )(*args)


def _mm_res_kernel(a_ref, w_ref, r_ref, *rest):
    n_cast = (len(rest) - 1) // 2
    o_ref = rest[n_cast]
    o_ref[...] = r_ref[...] + _dot(a_ref[...], w_ref[...])
    for src, dst in zip(rest[:n_cast], rest[n_cast + 1:]):
        dst[...] = src[...].astype(BF16)


def _mm_res(a, w, res, *, tm, cast=()):
    m, k = a.shape
    n = w.shape[1]
    steps = m // tm
    in_specs = [
        pl.BlockSpec((tm, k), lambda i: (i, 0)),
        pl.BlockSpec((k, n), lambda i: (0, 0), pipeline_mode=pl.Buffered(1)),
        pl.BlockSpec((tm, n), lambda i: (i, 0)),
    ]
    out_shape = [jax.ShapeDtypeStruct((m, n), F32)]
    out_specs = [pl.BlockSpec((tm, n), lambda i: (i, 0))]
    for wf, layer in cast:
        _, rows, cols = wf.shape
        slab = rows // steps
        assert slab * steps == rows and slab % 16 == 0
        in_specs.append(pl.BlockSpec((None, slab, cols), lambda i, layer=layer: (layer, i, 0)))
        out_shape.append(jax.ShapeDtypeStruct((1, rows, cols), BF16))
        out_specs.append(pl.BlockSpec((None, slab, cols), lambda i: (0, i, 0)))
    outs = pl.pallas_call(
        _mm_res_kernel,
        out_shape=out_shape,
        grid=(steps,),
        in_specs=in_specs,
        out_specs=out_specs,
        compiler_params=_params("parallel"),
        name="mm_res",
    )(a, w, res, *[wf for wf, _ in cast])
    return outs[0], tuple(outs[1:])


def _latent_kernel(x_ref, g_ref, w_ref, gl_ref, cos_ref, sin_ref, c_ref, kr_ref, *, row_chunks):
    half = QK_ROPE // 2
    tc = x_ref.shape[0] // row_chunks

    def norm(c):
        x = x_ref[c * tc:(c + 1) * tc, :]
        return (x * _rms_scale(x) * g_ref[...]).astype(BF16)

    def project(c, xn):
        rows = slice(c * tc, (c + 1) * tc)
        y = _dot(xn, w_ref[...])
        cl = y[:, :KV_LORA]
        c_ref[rows, :] = cl * _rms_scale(cl) * gl_ref[...]
        x1 = y[:, KV_LORA:KV_LORA + half]
        x2 = y[:, KV_LORA + half:KV_LORA + QK_ROPE]
        cos = cos_ref[rows, :]
        sin = sin_ref[rows, :]
        kr_ref[rows, :] = jnp.concatenate([x1 * cos - x2 * sin, x1 * sin + x2 * cos], axis=-1)

    xn = norm(0)
    for c in range(row_chunks):
        xn_next = norm(c + 1) if c + 1 < row_chunks else None
        project(c, xn)
        xn = xn_next


def _latent(x, g, w, g_lat, cos, sin, *, tm):
    m, d = x.shape
    n = w.shape[1]
    half = QK_ROPE // 2
    tab_blocks = cos.shape[0] // tm
    return pl.pallas_call(
        functools.partial(_latent_kernel, row_chunks=tm // 256),
        out_shape=(jax.ShapeDtypeStruct((m, KV_LORA), F32),
                   jax.ShapeDtypeStruct((m, QK_ROPE), F32)),
        grid=(m // tm,),
        in_specs=[
            pl.BlockSpec((tm, d), lambda i: (i, 0)),
            pl.BlockSpec((1, d), lambda i: (0, 0)),
            pl.BlockSpec((d, n), lambda i: (0, 0)),
            pl.BlockSpec((1, KV_LORA), lambda i: (0, 0)),
            pl.BlockSpec((tm, half), lambda i: (i % tab_blocks, 0)),
            pl.BlockSpec((tm, half), lambda i: (i % tab_blocks, 0)),
        ],
        out_specs=(pl.BlockSpec((tm, KV_LORA), lambda i: (i, 0)),
                   pl.BlockSpec((tm, QK_ROPE), lambda i: (i, 0))),
        compiler_params=_params("parallel"),
        name="latent",
    )(x, g.reshape(1, d), w, g_lat.reshape(1, KV_LORA), cos, sin)


def _kv_up_body(seg_refs, seg_rows, wk_ref, wv_ref, gk_ref, k_ref, v_ref):
    refs = seg_refs
    g_nope = gk_ref[:, :QK_NOPE]
    g_rope = gk_ref[:, QK_NOPE:]
    lo = 0
    for s, rows in enumerate(seg_rows):
        c = refs[2 * s][...].astype(BF16)
        kr = refs[2 * s + 1][...]
        kn = _dot(c, wk_ref[...])
        v_ref[lo:lo + rows, :] = _dot(c, wv_ref[...]).astype(BF16)
        ss_rope = jnp.sum(kr * kr, axis=-1, keepdims=True)
        for h in range(MLA_HEADS):
            knh = kn[:, h * QK_NOPE:(h + 1) * QK_NOPE]
            ss = jnp.sum(knh * knh, axis=-1, keepdims=True) + ss_rope
            r = lax.rsqrt(ss * (1.0 / QK_HEAD) + EPS)
            k_ref[h, lo:lo + rows, :QK_NOPE] = (knh * r * g_nope).astype(BF16)
            k_ref[h, lo:lo + rows, QK_NOPE:] = (kr * r * g_rope).astype(BF16)
        lo += rows


def _kv_up_kernel(*refs, seg_rows):
    n_seg = len(seg_rows)
    wk_ref, wv_ref, gk_ref, k_ref, v_ref = refs[2 * n_seg:]
    _kv_up_body(refs[:2 * n_seg], seg_rows, wk_ref, wv_ref, gk_ref, k_ref, v_ref)


def _kv_up(segments, wk, wv, gk, *, steps):
    seg_rows = tuple(c.shape[0] // steps for c, _ in segments)
    tm = sum(seg_rows)
    m = steps * tm
    in_specs, args = [], []
    for (c, kr), rows in zip(segments, seg_rows):
        in_specs += [pl.BlockSpec((rows, KV_LORA), lambda i: (i, 0)),
                     pl.BlockSpec((rows, QK_ROPE), lambda i: (i, 0))]
        args += [c, kr]
    in_specs += [pl.BlockSpec(wk.shape, lambda i: (0, 0)),
                 pl.BlockSpec(wv.shape, lambda i: (0, 0)),
                 pl.BlockSpec((1, QK_HEAD), lambda i: (0, 0))]
    return pl.pallas_call(
        functools.partial(_kv_up_kernel, seg_rows=seg_rows),
        out_shape=(jax.ShapeDtypeStruct((MLA_HEADS, m, QK_HEAD), BF16),
                   jax.ShapeDtypeStruct((m, MLA_HEADS * V_HEAD), BF16)),
        grid=(steps,),
        in_specs=in_specs,
        out_specs=(pl.BlockSpec((MLA_HEADS, tm, QK_HEAD), lambda i: (0, i, 0)),
                   pl.BlockSpec((tm, MLA_HEADS * V_HEAD), lambda i: (i, 0))),
        compiler_params=_params("parallel"),
        name="kv_up",
    )(*args, wk, wv, gk.reshape(1, QK_HEAD))


Q_SLOT = 2 * QK_NOPE


def _q_proj_kernel(x_ref, g_ref, wdq_ref, gl_ref, wuq_ref, cos_ref, sin_ref, gq_ref, q_ref, *, row_chunks):
    g_nope = gq_ref[:, :QK_NOPE]
    g_rope = gq_ref[:, QK_NOPE:]
    tc = x_ref.shape[0] // row_chunks

    def project(c):
        rows = slice(c * tc, (c + 1) * tc)
        x = x_ref[rows, :]
        hn = (x * _rms_scale(x) * g_ref[...]).astype(BF16)
        ql = _dot(hn, wdq_ref[...])
        qn = (ql * _rms_scale(ql) * gl_ref[...]).astype(BF16)
        return _dot(qn, wuq_ref[...])

    def per_head(c, q):
        rows = slice(c * tc, (c + 1) * tc)
        cos = cos_ref[rows, :]
        sin = sin_ref[rows, :]
        for h in range(MLA_HEADS):
            nh = q[:, h * Q_SLOT:h * Q_SLOT + QK_NOPE]
            t = q[:, h * Q_SLOT + QK_NOPE:(h + 1) * Q_SLOT]
            rp = t * cos + pltpu.roll(t, QK_ROPE, 1) * sin
            ss = jnp.sum(nh * nh + rp * rp, axis=-1, keepdims=True)
            r = lax.rsqrt(ss * (1.0 / QK_HEAD) + EPS) * (QK_HEAD ** -0.5 * LOG2E)
            q_ref[h, rows, :QK_NOPE] = (nh * r * g_nope).astype(BF16)
            q_ref[h, rows, QK_NOPE:] = (rp[:, :QK_ROPE] * r * g_rope).astype(BF16)

    q = project(0)
    for c in range(row_chunks):
        q_next = project(c + 1) if c + 1 < row_chunks else None
        per_head(c, q)
        q = q_next


def _q_proj(x, g, wdq, g_lat, wuq, cos, sin, gq, *, tm):
    m, d = x.shape
    tab_blocks = cos.shape[0] // tm
    tw = cos.shape[1]
    return pl.pallas_call(
        functools.partial(_q_proj_kernel, row_chunks=tm // 256),
        out_shape=jax.ShapeDtypeStruct((MLA_HEADS, m, QK_HEAD), BF16),
        grid=(m // tm,),
        in_specs=[
            pl.BlockSpec((tm, d), lambda i: (i, 0)),
            pl.BlockSpec((1, d), lambda i: (0, 0)),
            pl.BlockSpec(wdq.shape, lambda i: (0, 0)),
            pl.BlockSpec((1, Q_LORA), lambda i: (0, 0)),
            pl.BlockSpec(wuq.shape, lambda i: (0, 0)),
            pl.BlockSpec((tm, tw), lambda i: (i % tab_blocks, 0)),
            pl.BlockSpec((tm, tw), lambda i: (i % tab_blocks, 0)),
            pl.BlockSpec((1, QK_HEAD), lambda i: (0, 0)),
        ],
        out_specs=pl.BlockSpec((MLA_HEADS, tm, QK_HEAD), lambda i: (0, i, 0)),
        compiler_params=_params("parallel"),
        name="q_proj",
    )(x, g.reshape(1, d), wdq, g_lat.reshape(1, Q_LORA), wuq, cos, sin, gq.reshape(1, QK_HEAD))


def _attn_causal_kernel(q_ref, k_ref, v_ref, *rest, heads, tq):
    n_cast = (len(rest) - 1) // 2
    o_ref = rest[n_cast]
    for src, dst in zip(rest[:n_cast], rest[n_cast + 1:]):
        dst[...] = src[...].astype(BF16)
    n_q = q_ref.shape[1] // tq
    hq = tq // 2
    krow = lax.broadcasted_iota(jnp.int32, (hq, tq), 0) // CHUNK
    qcol = lax.broadcasted_iota(jnp.int32, (hq, tq), 1) // CHUNK
    keep_a = krow <= qcol
    keep_b = keep_a[:, :hq]

    def right_half(full, fn, part):
        return jnp.concatenate([full[:, :hq], fn(full[:, hq:], part)], axis=1)

    def scores(h, qi):
        lo = qi * tq
        q = q_ref[h, lo:lo + tq, :]
        st_a = jnp.where(keep_a, _dot_nt(k_ref[h, lo:lo + hq, :], q), NEG_INF)
        st_b = jnp.where(keep_b, _dot_nt(k_ref[h, lo + hq:lo + tq, :], q[hq:, :]), NEG_INF)
        st_p = _dot_nt(k_ref[h, :lo, :], q) if qi > 0 else None
        return st_a, st_b, st_p

    def softmax(st):
        st_a, st_b, st_p = st
        m = jnp.max(st_a, axis=0, keepdims=True)
        if st_p is not None:
            m = jnp.maximum(m, jnp.max(st_p, axis=0, keepdims=True))
        m = right_half(m, jnp.maximum, jnp.max(st_b, axis=0, keepdims=True))
        pt_a = jnp.exp2(st_a - m)
        pt_b = jnp.exp2(st_b - m[:, hq:])
        l = jnp.sum(pt_a, axis=0, keepdims=True)
        pt_p = None
        if st_p is not None:
            pt_p = jnp.exp2(st_p - m)
            l = l + jnp.sum(pt_p, axis=0, keepdims=True)
            pt_p = pt_p.astype(BF16)
        l = right_half(l, jnp.add, jnp.sum(pt_b, axis=0, keepdims=True))
        return pt_a.astype(BF16), pt_b.astype(BF16), pt_p, l

    def values(h, qi, p):
        pt_a, pt_b, pt_p, l = p
        lo = qi * tq
        cols = slice(h * V_HEAD, (h + 1) * V_HEAD)
        acc_t = _dot_tn(v_ref[lo:lo + hq, cols], pt_a)
        if pt_p is not None:
            acc_t = acc_t + _dot_tn(v_ref[:lo, cols], pt_p)
        acc_t = right_half(acc_t, jnp.add, _dot_tn(v_ref[lo + hq:lo + tq, cols], pt_b))
        o_ref[lo:lo + tq, cols] = (acc_t / l).T.astype(BF16)

    work = [(h, qi) for h in range(heads) for qi in range(n_q)]
    s_q, p_q = {}, {}
    for step in range(len(work) + 2):
        if step < len(work):
            s_q[step] = scores(*work[step])
        if 0 <= step - 1 < len(work):
            p_q[step - 1] = softmax(s_q.pop(step - 1))
        if 0 <= step - 2 < len(work):
            values(*work[step - 2], p_q.pop(step - 2))


def _kv_attn_kernel(*refs, seg_rows):
    n_seg = len(seg_rows)
    wk_ref, wv_ref, gk_ref, q_ref, o_ref, k_scr, v_scr = refs[2 * n_seg:]
    g_nope = gk_ref[:, :QK_NOPE]
    g_rope = gk_ref[:, QK_NOPE:]

    group = 4
    width = group * QK_NOPE
    segs, lo = [], 0
    for s, rows in enumerate(seg_rows):
        c = refs[2 * s][...].astype(BF16)
        kr = refs[2 * s + 1][...]
        segs.append((lo, rows, c, kr, jnp.sum(kr * kr, axis=-1, keepdims=True)))
        lo += rows

    def project(g):
        cols = slice(g * width, (g + 1) * width)
        kn = []
        for lo, rows, c, _, _ in segs:
            v_scr[lo:lo + rows, cols] = _dot(c, wv_ref[:, cols]).astype(BF16)
            kn.append(_dot(c, wk_ref[:, cols]))
        return kn

    kn_groups = {}

    def key_norm(h):
        g, hh = divmod(h, group)
        for (lo, rows, _, kr, ss_rope), kn in zip(segs, kn_groups[g]):
            knh = kn[:, hh * QK_NOPE:(hh + 1) * QK_NOPE]
            ss = jnp.sum(knh * knh, axis=-1, keepdims=True) + ss_rope
            r = lax.rsqrt(ss * (1.0 / QK_HEAD) + EPS)
            k_scr[h, lo:lo + rows, :QK_NOPE] = (knh * r * g_nope).astype(BF16)
            k_scr[h, lo:lo + rows, QK_NOPE:] = (kr * r * g_rope).astype(BF16)

    def scores(h):
        return _dot_nt(q_ref[h], k_scr[h])

    def softmax(s):
        m = jnp.max(s, axis=-1, keepdims=True)
        p = jnp.exp2(s - m)
        return p.astype(BF16), jnp.sum(p, axis=-1, keepdims=True)

    def values(h, pl_):
        p, l = pl_
        cols = slice(h * V_HEAD, (h + 1) * V_HEAD)
        o_ref[:, cols] = (_dot(p, v_scr[:, cols]) / l).astype(BF16)

    s_q, p_q = {}, {}
    kn_groups[0] = project(0)
    for step in range(MLA_HEADS + 3):
        if step % group == 0 and step // group + 1 < MLA_HEADS // group:
            kn_groups[step // group + 1] = project(step // group + 1)
        if step < MLA_HEADS:
            key_norm(step)
        if 0 <= step - 1 < MLA_HEADS:
            s_q[step - 1] = scores(step - 1)
        if 0 <= step - 2 < MLA_HEADS:
            p_q[step - 2] = softmax(s_q.pop(step - 2))
        if 0 <= step - 3 < MLA_HEADS:
            values(step - 3, p_q.pop(step - 3))


def _kv_attention(segments, wk, wv, gk, q, *, batch):
    seg_rows = tuple(c.shape[0] // batch for c, _ in segments)
    t_kv = sum(seg_rows)
    t_q = q.shape[1] // batch
    in_specs, args = [], []
    for (c, kr), rows in zip(segments, seg_rows):
        in_specs += [pl.BlockSpec((rows, KV_LORA), lambda b: (b, 0)),
                     pl.BlockSpec((rows, QK_ROPE), lambda b: (b, 0))]
        args += [c, kr]
    in_specs += [pl.BlockSpec(wk.shape, lambda b: (0, 0)),
                 pl.BlockSpec(wv.shape, lambda b: (0, 0)),
                 pl.BlockSpec((1, QK_HEAD), lambda b: (0, 0)),
                 pl.BlockSpec((MLA_HEADS, t_q, QK_HEAD), lambda b: (0, b, 0))]
    return pl.pallas_call(
        functools.partial(_kv_attn_kernel, seg_rows=seg_rows),
        out_shape=jax.ShapeDtypeStruct((batch * t_q, MLA_HEADS * V_HEAD), BF16),
        grid=(batch,),
        in_specs=in_specs,
        out_specs=pl.BlockSpec((t_q, MLA_HEADS * V_HEAD), lambda b: (b, 0)),
        scratch_shapes=[pltpu.VMEM((MLA_HEADS, t_kv, QK_HEAD), BF16),
                        pltpu.VMEM((t_kv, MLA_HEADS * V_HEAD), BF16)],
        compiler_params=_params("parallel"),
        name="kv_attention",
    )(*args, wk, wv, gk.reshape(1, QK_HEAD), q)


def _attention(q, k, v, *, batch, t_q, t_kv, heads, tq, cast=()):
    hg = MLA_HEADS // heads
    steps = batch * hg
    kern = functools.partial(_attn_causal_kernel, heads=heads, tq=tq)
    in_specs = [
        pl.BlockSpec((heads, t_q, QK_HEAD), lambda b, g: (g, b, 0)),
        pl.BlockSpec((heads, t_kv, QK_HEAD), lambda b, g: (g, b, 0)),
        pl.BlockSpec((t_kv, heads * V_HEAD), lambda b, g: (b, g)),
    ]
    out_shape = [jax.ShapeDtypeStruct((batch * t_q, MLA_HEADS * V_HEAD), BF16)]
    out_specs = [pl.BlockSpec((t_q, heads * V_HEAD), lambda b, g: (b, g))]
    for wf, layer in cast:
        _, rows, cols = wf.shape
        slab = rows // steps
        assert slab * steps == rows and slab % 16 == 0
        in_specs.append(pl.BlockSpec((None, slab, cols), lambda b, g, layer=layer: (layer, b * hg + g, 0)))
        out_shape.append(jax.ShapeDtypeStruct((1, rows, cols), BF16))
        out_specs.append(pl.BlockSpec((None, slab, cols), lambda b, g: (0, b * hg + g, 0)))
    outs = pl.pallas_call(
        kern,
        out_shape=out_shape,
        grid=(batch, hg),
        in_specs=in_specs,
        out_specs=out_specs,
        compiler_params=_params("parallel", "parallel"),
        name="attention",
    )(q, k, v, *[wf for wf, _ in cast])
    return outs[0], tuple(outs[1:])


def _rope_tables(pos, half, reps, rows):
    inv = jnp.power(ROPE_BASE, -jnp.arange(half, dtype=F32) / half)
    ang = pos.astype(F32)[:, None] * inv[None, :]
    cos = jnp.tile(jnp.cos(ang), (max(rows // pos.shape[0], 1), reps))
    sin = jnp.tile(jnp.sin(ang), (max(rows // pos.shape[0], 1), reps))
    return cos, sin


def _mm_res_and_ffn2(a, w, h, layer, p):
    key = ("ffn2_bf16", layer)
    if key not in p:
        w_in = p.get(("ffn2_w_in_bf16", layer))
        todo = [(p["ffn2_w_out"], layer)] if w_in is not None else [(p["ffn2_w_in"], layer), (p["ffn2_w_out"], layer)]
        h, done = _mm_res(a, w, h, tm=512, cast=tuple(todo))
        p[key] = (w_in, done[0]) if w_in is not None else done
    else:
        h, _ = _mm_res(a, w, h, tm=512)
    w_in, w_out = p[key]
    return _ffn(h, p["ffn2_norm"][layer], w_in, w_out, 0, tm=FFN_TM, tf=FFN_TF)


def _trunk(x, pos, ret_state, ckv_past, krope_past, p):
    batch, seq, d = x.shape
    m = batch * seq
    tm = 1024
    h = x.reshape(m, d)

    if "ret_w_in_bf16" not in p:
        h, p["ret_w_in_bf16"] = _ffn(h, p["ffn1_norm"][0], p["ffn1_w_in_l0"], p["ffn1_w_out_l0"], 0,
                                     tm=FFN_TM, tf=FFN_TF, cast=p["ret_w_in"])
    else:
        h = _ffn(h, p["ffn1_norm"][0], p["ffn1_w_in_l0"], p["ffn1_w_out_l0"], 0, tm=FFN_TM, tf=FFN_TF)
    cos, sin = _rope_tables(pos, RET_DK // 2, 1, tm)
    key = "cast_by_ret_in"
    cast = () if key in p else ((p["ffn1_w_in"], 1), (p["ffn1_w_out"], 1), (p["ret_w_out"], 0), (p["w_o"], 0))
    qkvg, cast_out = _ret_in(h, p["mix_norm"][0], p["ret_w_in_bf16"][0], cos, sin, tm=tm, tn=2048, cast=cast)
    if cast:
        p[key] = cast_out
    ffn1_l1_w_in, ffn1_l1_w_out, ret_w_out, w_o = p[key]
    log_gamma = jnp.log1p(-jnp.exp2(-5.0 - jnp.arange(RET_HEADS, dtype=F32)))
    chunk = min(seq, 256)
    y, s_fin = _ret_core(qkvg, log_gamma, p["ret_gn_gain"][0],
                         None if ret_state is None else ret_state[0],
                         batch=batch, seq=seq, chunk=chunk,
                         heads=max(1, min(RET_HEADS, 512 // seq)))
    h = _mm_res_and_ffn2(y, ret_w_out[0], h, 0, p)

    cos, sin = _rope_tables(pos, QK_ROPE // 2, 1, tm)
    c_new, kr_new = _latent(h, p["kv_norm"], p["w_dkv"], p["kv_lat_norm"], cos, sin, tm=tm)
    if ckv_past is None:
        k_sh, v_sh = _kv_up([(c_new, kr_new)], p["w_uk"], p["w_uv"], p["k_norm"], steps=m // tm)
    else:
        past = ckv_past.shape[1]
        segments = [(ckv_past.reshape(batch * past, KV_LORA), krope_past.reshape(batch * past, QK_ROPE)),
                    (c_new, kr_new)]

    h = _ffn(h, p["ffn1_norm"][1], ffn1_l1_w_in, ffn1_l1_w_out, 0, tm=FFN_TM, tf=FFN_TF)
    cos, sin = _rope_tables(pos, QK_ROPE // 2, 1, tm)
    zeros = jnp.zeros_like(cos)
    cos = jnp.concatenate([cos, cos, zeros, zeros], axis=-1)
    sin = jnp.concatenate([-sin, sin, zeros, zeros], axis=-1)
    q = _q_proj(h, p["mix_norm"][1], p["w_dq"][0], p["q_lat_norm"][0], p["w_uq"][0], cos, sin,
                p["q_norm"][0], tm=tm)
    if ckv_past is None:
        key = ("ffn2_w_in_bf16", 1)
        o, cast_out = _attention(q, k_sh, v_sh, batch=batch, t_q=seq, t_kv=seq, heads=4, tq=512,
                                 cast=() if key in p else ((p["ffn2_w_in"], 1),))
        if cast_out:
            p[key] = cast_out[0]
    else:
        o = _kv_attention(segments, p["w_uk"], p["w_uv"], p["k_norm"], q, batch=batch)
    h = _mm_res_and_ffn2(o, w_o[0], h, 1, p)

    return (h.reshape(batch, seq, d), s_fin[None],
            c_new.reshape(batch, seq, KV_LORA), kr_new.reshape(batch, seq, QK_ROPE))


def kernel(x_prompt, x_sample, state_ret, cache_ckv, cache_krope, ffn1_norm, ffn1_w_in, ffn1_w_out, mix_norm, ffn2_norm, ffn2_w_in, ffn2_w_out, ret_w_in, ret_gn_gain, ret_w_out, kv_norm, w_dkv, kv_lat_norm, w_ukv, k_norm, w_dq, q_lat_norm, w_uq, q_norm, w_o):
    bf = lambda w: w.astype(BF16)
    w_ukv_h = w_ukv.reshape(KV_LORA, MLA_HEADS, QK_NOPE + V_HEAD)
    w_uk = w_ukv_h[:, :, :QK_NOPE].reshape(KV_LORA, MLA_HEADS * QK_NOPE)
    w_uv = w_ukv_h[:, :, QK_NOPE:].reshape(KV_LORA, MLA_HEADS * V_HEAD)
    half = QK_ROPE // 2
    w_uq_h = w_uq.reshape(w_uq.shape[0], Q_LORA, MLA_HEADS, QK_HEAD)
    w_lo = w_uq_h[..., QK_NOPE:QK_NOPE + half]
    w_hi = w_uq_h[..., QK_NOPE + half:]
    w_uq_p = jnp.concatenate([w_uq_h[..., :QK_NOPE], w_lo, w_hi, w_hi, w_lo], axis=-1)
    w_uq_p = w_uq_p.reshape(w_uq.shape[0], Q_LORA, MLA_HEADS * Q_SLOT)
    p = dict(ffn1_norm=ffn1_norm, ffn1_w_in=ffn1_w_in, ffn1_w_out=ffn1_w_out,
             ffn1_w_in_l0=bf(ffn1_w_in[:1]), ffn1_w_out_l0=bf(ffn1_w_out[:1]),
             mix_norm=mix_norm, ffn2_norm=ffn2_norm, ffn2_w_in=ffn2_w_in, ffn2_w_out=ffn2_w_out,
             ret_w_in=ret_w_in, ret_gn_gain=ret_gn_gain, ret_w_out=ret_w_out,
             kv_norm=kv_norm, w_dkv=bf(w_dkv), kv_lat_norm=kv_lat_norm,
             w_uk=bf(w_uk), w_uv=bf(w_uv), k_norm=k_norm,
             w_dq=bf(w_dq), q_lat_norm=q_lat_norm, w_uq=bf(w_uq_p), q_norm=q_norm, w_o=w_o)
    pos_prompt = jnp.arange(x_prompt.shape[1])
    past = cache_ckv.shape[1]
    pos_sample = past + jnp.arange(x_sample.shape[1])
    y_p, ret_p, ckv_p, kr_p = _trunk(x_prompt, pos_prompt, None, None, None, p)
    y_s, ret_s, ckv_s, kr_s = _trunk(x_sample, pos_sample, state_ret, cache_ckv, cache_krope, p)
    return (y_p, y_s, ret_p, ckv_p, kr_p, ret_s, ckv_s, kr_s)
```

```python
import functools

import jax
import jax.numpy as jnp
from jax import lax
from jax.experimental import pallas as pl
from jax.experimental.pallas import tpu as pltpu

F32 = jnp.float32
BF16 = jnp.bfloat16

D_MODEL = 2048
CHUNK = 64
D_FF = 5632
RET_HEADS = 8
RET_DK = D_MODEL // RET_HEADS
RET_DV = 2 * D_MODEL // RET_HEADS
MLA_HEADS = 16
QK_NOPE = 128
QK_ROPE = 64
QK_HEAD = QK_NOPE + QK_ROPE
V_HEAD = 128
KV_LORA = 512
Q_LORA = 512
ROPE_BASE = 10000.0
EPS = 1e-6
NEG_INF = -1e30
LOG2E = 1.4426950408889634

VMEM_LIMIT_BYTES = 60 * 1024 * 1024

FFN_TM = 1024
FFN_TF = 512
FFN_NORM_CHUNKS = 4
FFN_DMA_BLOCKS = 2


def _params(*semantics):
    return pltpu.CompilerParams(dimension_semantics=semantics,
                                vmem_limit_bytes=VMEM_LIMIT_BYTES)


def _rms_scale(x):
    return lax.rsqrt(jnp.mean(x * x, axis=-1, keepdims=True) + EPS)


def _dot(a, b):
    return jnp.dot(a, b, preferred_element_type=F32)


def _dot_nt(a, b):
    return lax.dot_general(a, b, (((1,), (1,)), ((), ())), preferred_element_type=F32)


def _dot_tn(a, b):
    return lax.dot_general(a, b, (((0,), (0,)), ((), ())), preferred_element_type=F32)


def _silu(x):
    return x * jax.nn.sigmoid(x)


def _ffn_kernel(*refs, layer, tf, cast_pieces):
    if cast_pieces:
        (g_ref, x_hbm, win_hbm, wout_hbm, cast_src, o_hbm, cast_dst, acc, xn_ref, wa_buf, wb_buf, wo_buf,
         wsem, xsem, osem, stage_in, stage_out, csem_in, csem_out) = refs
    else:
        g_ref, x_hbm, win_hbm, wout_hbm, o_hbm, acc, xn_ref, wa_buf, wb_buf, wo_buf, wsem, xsem, osem = refs
    i = pl.program_id(0)
    n_tiles = pl.num_programs(0)
    tm = acc.shape[1]
    dma_cols = FFN_DMA_BLOCKS * tf
    n_pos = 1 + (D_FF - tf) // dma_cols
    last = n_pos - 1
    a_slot = i % 2

    def weight_copies(pos, slot):
        if isinstance(pos, int) and pos == 0:
            col, cols = 0, tf
        else:
            col, cols = pos * dma_cols - (dma_cols - tf), dma_cols
            if not isinstance(pos, int):
                col = pl.multiple_of(col, tf)
        return (
            pltpu.make_async_copy(win_hbm.at[layer, :, pl.ds(col, cols)], wa_buf.at[slot, :, pl.ds(0, cols)],
                                  wsem.at[slot, 0]),
            pltpu.make_async_copy(win_hbm.at[layer, :, pl.ds(D_FF + col, cols)], wb_buf.at[slot, :, pl.ds(0, cols)],
                                  wsem.at[slot, 1]),
            pltpu.make_async_copy(wout_hbm.at[layer, pl.ds(col, cols), :], wo_buf.at[slot, pl.ds(0, cols), :],
                                  wsem.at[slot, 2]),
        )

    def x_copy(tile, slot):
        src = x_hbm.at[pl.ds(pl.multiple_of(tile * tm, tm), tm), :]
        return pltpu.make_async_copy(src, acc.at[slot], xsem.at[slot])

    def out_copy(tile, slot):
        dst = o_hbm.at[pl.ds(pl.multiple_of(tile * tm, tm), tm), :]
        return pltpu.make_async_copy(acc.at[slot], dst, osem.at[slot])

    def update(xn, slot, cols):
        a = _dot(xn, wa_buf[slot, :, :cols])
        b = _dot(xn, wb_buf[slot, :, :cols])
        h = (_silu(a) * (0.5 * b)).astype(BF16)
        return _dot(h, wo_buf[slot, :cols, :])

    @pl.when(i == 0)
    def _():
        x_copy(0, 0).start()
        for cp in weight_copies(0, 0):
            cp.start()

    x_copy(i, a_slot).wait()

    def sync(pos, w_slot):
        for cp in weight_copies(pos, w_slot):
            cp.wait()

        @pl.when(pos < last)
        def _():
            for cp in weight_copies(pos + 1, 1 - w_slot):
                cp.start()

        @pl.when((pos == last) & (i + 1 < n_tiles))
        def _():
            for cp in weight_copies(0, 0):
                cp.start()

        @pl.when(pos == 1)
        def _():
            @pl.when(i > 0)
            def _():
                out_copy(i - 1, 1 - a_slot).wait()

            @pl.when(i + 1 < n_tiles)
            def _():
                x_copy(i + 1, 1 - a_slot).start()

        if cast_pieces:
            side_cast(pos)

    def cast_copies(n, slot):
        piece = stage_in.shape[1]
        rows = pl.ds(pl.multiple_of((i * cast_pieces + n) * piece, piece), piece)
        return (pltpu.make_async_copy(cast_src.at[0, rows, :], stage_in.at[slot], csem_in.at[slot]),
                pltpu.make_async_copy(stage_out.at[slot], cast_dst.at[0, rows, :], csem_out.at[slot]))

    def side_cast(pos):
        @pl.when(pos == 0)
        def _():
            cast_copies(0, 0)[0].start()

        @pl.when((pos >= 1) & (pos <= cast_pieces))
        def _():
            n = pos - 1
            slot = n % 2
            cast_copies(n, slot)[0].wait()

            @pl.when(n + 1 < cast_pieces)
            def _():
                cast_copies(n + 1, 1 - slot)[0].start()

            @pl.when(n >= 2)
            def _():
                cast_copies(n - 2, slot)[1].wait()

            stage_out[slot] = stage_in[slot].astype(BF16)
            cast_copies(n, slot)[1].start()

        @pl.when(pos == cast_pieces + 1)
        def _():
            for n in (cast_pieces - 2, cast_pieces - 1):
                cast_copies(n, n % 2)[1].wait()

    sync(0, 0)
    tc = tm // FFN_NORM_CHUNKS

    def norm(c):
        x = acc[a_slot, c * tc:(c + 1) * tc, :]
        xn = (x * _rms_scale(x) * g_ref[...]).astype(BF16)
        xn_ref[c * tc:(c + 1) * tc, :] = xn
        return x, xn

    cur = norm(0)
    for c in range(FFN_NORM_CHUNKS):
        nxt_c = norm(c + 1) if c + 1 < FFN_NORM_CHUNKS else None
        x, xn = cur
        acc[a_slot, c * tc:(c + 1) * tc, :] = x + update(xn, 0, tf)
        cur = nxt_c

    def position(pos, carry):
        w_slot = pos % 2
        sync(pos, w_slot)
        acc[a_slot] += update(xn_ref[...], w_slot, dma_cols)
        return carry

    lax.fori_loop(1, n_pos, position, 0)

    out_copy(i, a_slot).start()

    @pl.when(i + 1 == n_tiles)
    def _():
        out_copy(i, a_slot).wait()


def _ffn(x, g, w_in, w_out, layer, *, tm, tf, cast=None):
    m, d = x.shape
    n_tiles = m // tm
    dma_cols = FFN_DMA_BLOCKS * tf
    n_pos = 1 + (D_FF - tf) // dma_cols
    assert (D_FF - tf) % dma_cols == 0 and n_pos % 2 == 0
    in_specs = [pl.BlockSpec((1, d), lambda i: (0, 0))] + [pl.BlockSpec(memory_space=pl.ANY)] * 3
    out_shape = [jax.ShapeDtypeStruct((m, d), F32)]
    scratch = [
        pltpu.VMEM((2, tm, d), F32),
        pltpu.VMEM((tm, d), BF16),
        pltpu.VMEM((2, d, dma_cols), BF16),
        pltpu.VMEM((2, d, dma_cols), BF16),
        pltpu.VMEM((2, dma_cols, d), BF16),
        pltpu.SemaphoreType.DMA((2, 3)),
        pltpu.SemaphoreType.DMA((2,)),
        pltpu.SemaphoreType.DMA((2,)),
    ]
    args = [g.reshape(1, d), x, w_in, w_out]
    cast_pieces = 0
    if cast is not None:
        cast_pieces = n_pos - 2
        _, rows, cols = cast.shape
        piece = rows // (n_tiles * cast_pieces)
        assert piece * n_tiles * cast_pieces == rows and piece % 16 == 0 and cast_pieces >= 2
        in_specs.append(pl.BlockSpec(memory_space=pl.ANY))
        out_shape.append(jax.ShapeDtypeStruct(cast.shape, BF16))
        scratch += [pltpu.VMEM((2, piece, cols), F32), pltpu.VMEM((2, piece, cols), BF16),
                    pltpu.SemaphoreType.DMA((2,)), pltpu.SemaphoreType.DMA((2,))]
        args.append(cast)
    outs = pl.pallas_call(
        functools.partial(_ffn_kernel, layer=layer, tf=tf, cast_pieces=cast_pieces),
        out_shape=out_shape,
        grid=(n_tiles,),
        in_specs=in_specs,
        out_specs=[pl.BlockSpec(memory_space=pl.ANY)] * len(out_shape),
        scratch_shapes=scratch,
        compiler_params=_params("arbitrary"),
        name="ffn",
    )(*args)
    return outs[0] if cast is None else (outs[0], outs[1])


def _ret_in_kernel(x_ref, g_ref, w_ref, cos_ref, sin_ref, *rest, rope_blocks, heads_per_block, cast_steps):
    n_cast = len(cast_steps)
    o_ref = rest[n_cast]
    xn_ref = rest[-1]
    j = pl.program_id(1)

    for src, dst, steps in zip(rest[:n_cast], rest[n_cast + 1:-1], cast_steps):
        @pl.when(j < steps)
        def _(src=src, dst=dst):
            dst[...] = src[...].astype(BF16)

    @pl.when(j == 0)
    def _():
        x = x_ref[...]
        xn_ref[...] = (x * _rms_scale(x) * g_ref[...]).astype(BF16)

    half = RET_DK // 2

    @pl.when(j < 2 * rope_blocks)
    def _():
        scale = jnp.where(j < rope_blocks, RET_DK ** -0.5, 1.0).astype(F32)
        cos = cos_ref[...] * scale
        sin = sin_ref[...] * scale
        xn = xn_ref[...]
        for h in range(heads_per_block):
            lo = h * RET_DK
            y = _dot(xn, w_ref[:, lo:lo + RET_DK])
            x1 = y[:, :half]
            x2 = y[:, half:]
            o_ref[:, lo:lo + half] = (x1 * cos - x2 * sin).astype(BF16)
            o_ref[:, lo + half:lo + RET_DK] = (x1 * sin + x2 * cos).astype(BF16)

    @pl.when(j >= 2 * rope_blocks)
    def _():
        o_ref[...] = _dot(xn_ref[...], w_ref[...]).astype(BF16)


def _ret_in(x, g, w, cos, sin, *, tm, tn, cast=()):
    m, d = x.shape
    n = w.shape[1]
    tab_blocks = cos.shape[0] // tm
    n_i, n_j = m // tm, n // tn
    in_specs = [
        pl.BlockSpec((tm, d), lambda i, j: (i, 0)),
        pl.BlockSpec((1, d), lambda i, j: (0, 0)),
        pl.BlockSpec((d, tn), lambda i, j: (0, j)),
        pl.BlockSpec((tm, RET_DK // 2), lambda i, j: (i % tab_blocks, 0)),
        pl.BlockSpec((tm, RET_DK // 2), lambda i, j: (i % tab_blocks, 0)),
    ]
    out_shape = [jax.ShapeDtypeStruct((m, n), BF16)]
    out_specs = [pl.BlockSpec((tm, tn), lambda i, j: (i, j))]
    cast_steps = []
    for wf, layer in cast:
        _, rows, cols = wf.shape
        slab = rows // n_i
        assert slab * n_i == rows and slab % 16 == 0
        steps = max(s for s in range(1, n_j + 1) if cols % (128 * s) == 0)
        cast_steps.append(steps)
        in_specs.append(pl.BlockSpec((None, slab, cols // steps),
                                     lambda i, j, layer=layer, steps=steps: (layer, i, jnp.minimum(j, steps - 1))))
        out_shape.append(jax.ShapeDtypeStruct((1, rows, cols), BF16))
        out_specs.append(pl.BlockSpec((None, slab, cols // steps),
                                      lambda i, j, steps=steps: (0, i, jnp.minimum(j, steps - 1))))
    kern = functools.partial(_ret_in_kernel, rope_blocks=(RET_HEADS * RET_DK) // tn,
                             heads_per_block=tn // RET_DK, cast_steps=tuple(cast_steps))
    outs = pl.pallas_call(
        kern,
        out_shape=out_shape,
        grid=(n_i, n_j),
        in_specs=in_specs,
        out_specs=out_specs,
        scratch_shapes=[pltpu.VMEM((tm, d), BF16)],
        compiler_params=_params("parallel", "arbitrary"),
        name="ret_in",
    )(x, g.reshape(1, d), w, cos, sin, *[wf for wf, _ in cast])
    return outs[0], tuple(outs[1:])


def _ret_core_kernel(lg_ref, q_ref, k_ref, v_ref, g_ref, gain_ref, *rest, chunk, heads, has_state):
    if has_state:
        s0_ref, y_ref, s_ref = rest
    else:
        y_ref, s_ref = rest
    head0 = pl.program_id(1) * heads
    n_chunks = q_ref.shape[0] // chunk
    row = lax.broadcasted_iota(jnp.int32, (chunk, chunk), 0)
    col = lax.broadcasted_iota(jnp.int32, (chunk, chunk), 1)
    diff = (row - col).astype(F32)
    n = lax.broadcasted_iota(jnp.int32, (chunk, 1), 0).astype(F32)

    def head_consts(hh):
        lg = lg_ref[head0 + hh]
        decay = jnp.where(diff >= 0, jnp.exp(lg * jnp.maximum(diff, 0.0)), 0.0)
        cross = jnp.exp(lg * (n + 1.0))
        kdec = jnp.exp(lg * (chunk - 1.0 - n))
        carry = jnp.exp(lg * jnp.full((1, RET_DV), float(chunk), F32))
        return decay, cross, kdec, carry

    def mix(hh, c, consts):
        decay, cross, kdec, carry = consts
        rows = slice(c * chunk, (c + 1) * chunk)
        q = q_ref[rows, hh * RET_DK:(hh + 1) * RET_DK]
        k = k_ref[rows, hh * RET_DK:(hh + 1) * RET_DK]
        v = v_ref[rows, hh * RET_DV:(hh + 1) * RET_DV]
        inner = (_dot_nt(q, k) * decay).astype(BF16)
        if c > 0:
            s_old = s_ref[0, hh]
        elif has_state:
            s_old = s0_ref[0, hh]
        else:
            s_old = None
        kd = (k.astype(F32) * kdec).astype(BF16)
        o = _dot(inner, v)
        s_new = _dot_tn(kd, v)
        if s_old is not None:
            o = o + cross * _dot(q, s_old.astype(BF16))
            s_new = carry * s_old + s_new
        s_ref[0, hh] = s_new
        return o

    def norm_gate(hh, c, o):
        rows = slice(c * chunk, (c + 1) * chunk)
        cols = slice(hh * RET_DV, (hh + 1) * RET_DV)
        mu = jnp.mean(o, axis=-1, keepdims=True)
        dev = o - mu
        var = jnp.mean(dev * dev, axis=-1, keepdims=True)
        of = dev * lax.rsqrt(var + EPS) * gain_ref[:, cols]
        y_ref[rows, cols] = (_silu(g_ref[rows, cols].astype(F32)) * of).astype(BF16)

    consts = None
    pending = None
    for hh in range(heads):
        for c in range(n_chunks):
            if c == 0:
                consts = head_consts(hh)
            o = mix(hh, c, consts)
            if pending is not None:
                norm_gate(*pending)
            pending = (hh, c, o)
    norm_gate(*pending)


def _ret_core(qkvg, log_gamma, gain, state, *, batch, seq, chunk, heads):
    m = qkvg.shape[0]
    hk = RET_HEADS * RET_DK
    hv = RET_HEADS * RET_DV
    wk = heads * RET_DK
    wv = heads * RET_DV
    kblk = hk // wk
    vblk = 2 * hk // wv
    gblk = vblk + hv // wv
    has_state = state is not None
    in_specs = [
        pl.BlockSpec(memory_space=pltpu.SMEM),
        pl.BlockSpec((seq, wk), lambda b, g: (b, g)),
        pl.BlockSpec((seq, wk), lambda b, g: (b, kblk + g)),
        pl.BlockSpec((seq, wv), lambda b, g: (b, vblk + g)),
        pl.BlockSpec((seq, wv), lambda b, g: (b, gblk + g)),
        pl.BlockSpec((1, wv), lambda b, g: (0, g)),
    ]
    args = [log_gamma, qkvg, qkvg, qkvg, qkvg, gain.reshape(1, hv)]
    state_spec = pl.BlockSpec((1, heads, RET_DK, RET_DV), lambda b, g: (b, g, 0, 0))
    if has_state:
        in_specs.append(state_spec)
        args.append(state)
    kern = functools.partial(_ret_core_kernel, chunk=chunk, heads=heads, has_state=has_state)
    return pl.pallas_call(
        kern,
        out_shape=(jax.ShapeDtypeStruct((m, hv), BF16),
                   jax.ShapeDtypeStruct((batch, RET_HEADS, RET_DK, RET_DV), F32)),
        grid=(batch, RET_HEADS // heads),
        in_specs=in_specs,
        out_specs=(pl.BlockSpec((seq, wv), lambda b, g: (b, g)), state_spec),
        compiler_params=_params("parallel", "parallel"),
        name="ret_core",
    )(*args)


def _mm_res_kernel(a_ref, w_ref, r_ref, *rest):
    n_cast = (len(rest) - 1) // 2
    o_ref = rest[n_cast]
    o_ref[...] = r_ref[...] + _dot(a_ref[...], w_ref[...])
    for src, dst in zip(rest[:n_cast], rest[n_cast + 1:]):
        dst[...] = src[...].astype(BF16)


def _mm_res(a, w, res, *, tm, cast=()):
    m, k = a.shape
    n = w.shape[1]
    steps = m // tm
    in_specs = [
        pl.BlockSpec((tm, k), lambda i: (i, 0)),
        pl.BlockSpec((k, n), lambda i: (0, 0), pipeline_mode=pl.Buffered(1)),
        pl.BlockSpec((tm, n), lambda i: (i, 0)),
    ]
    out_shape = [jax.ShapeDtypeStruct((m, n), F32)]
    out_specs = [pl.BlockSpec((tm, n), lambda i: (i, 0))]
    for wf, layer in cast:
        _, rows, cols = wf.shape
        slab = rows // steps
        assert slab * steps == rows and slab % 16 == 0
        in_specs.append(pl.BlockSpec((None, slab, cols), lambda i, layer=layer: (layer, i, 0)))
        out_shape.append(jax.ShapeDtypeStruct((1, rows, cols), BF16))
        out_specs.append(pl.BlockSpec((None, slab, cols), lambda i: (0, i, 0)))
    outs = pl.pallas_call(
        _mm_res_kernel,
        out_shape=out_shape,
        grid=(steps,),
        in_specs=in_specs,
        out_specs=out_specs,
        compiler_params=_params("parallel"),
        name="mm_res",
    )(a, w, res, *[wf for wf, _ in cast])
    return outs[0], tuple(outs[1:])


def _latent_kernel(x_ref, g_ref, w_ref, gl_ref, cos_ref, sin_ref, c_ref, kr_ref, *, row_chunks):
    half = QK_ROPE // 2
    tc = x_ref.shape[0] // row_chunks

    def norm(c):
        x = x_ref[c * tc:(c + 1) * tc, :]
        return (x * _rms_scale(x) * g_ref[...]).astype(BF16)

    def project(c, xn):
        rows = slice(c * tc, (c + 1) * tc)
        y = _dot(xn, w_ref[...])
        cl = y[:, :KV_LORA]
        c_ref[rows, :] = cl * _rms_scale(cl) * gl_ref[...]
        x1 = y[:, KV_LORA:KV_LORA + half]
        x2 = y[:, KV_LORA + half:KV_LORA + QK_ROPE]
        cos = cos_ref[rows, :]
        sin = sin_ref[rows, :]
        kr_ref[rows, :] = jnp.concatenate([x1 * cos - x2 * sin, x1 * sin + x2 * cos], axis=-1)

    xn = norm(0)
    for c in range(row_chunks):
        xn_next = norm(c + 1) if c + 1 < row_chunks else None
        project(c, xn)
        xn = xn_next


def _latent(x, g, w, g_lat, cos, sin, *, tm):
    m, d = x.shape
    n = w.shape[1]
    half = QK_ROPE // 2
    tab_blocks = cos.shape[0] // tm
    return pl.pallas_call(
        functools.partial(_latent_kernel, row_chunks=tm // 256),
        out_shape=(jax.ShapeDtypeStruct((m, KV_LORA), F32),
                   jax.ShapeDtypeStruct((m, QK_ROPE), F32)),
        grid=(m // tm,),
        in_specs=[
            pl.BlockSpec((tm, d), lambda i: (i, 0)),
            pl.BlockSpec((1, d), lambda i: (0, 0)),
            pl.BlockSpec((d, n), lambda i: (0, 0)),
            pl.BlockSpec((1, KV_LORA), lambda i: (0, 0)),
            pl.BlockSpec((tm, half), lambda i: (i % tab_blocks, 0)),
            pl.BlockSpec((tm, half), lambda i: (i % tab_blocks, 0)),
        ],
        out_specs=(pl.BlockSpec((tm, KV_LORA), lambda i: (i, 0)),
                   pl.BlockSpec((tm, QK_ROPE), lambda i: (i, 0))),
        compiler_params=_params("parallel"),
        name="latent",
    )(x, g.reshape(1, d), w, g_lat.reshape(1, KV_LORA), cos, sin)


def _kv_up_body(seg_refs, seg_rows, wk_ref, wv_ref, gk_ref, k_ref, v_ref):
    refs = seg_refs
    g_nope = gk_ref[:, :QK_NOPE]
    g_rope = gk_ref[:, QK_NOPE:]
    lo = 0
    for s, rows in enumerate(seg_rows):
        c = refs[2 * s][...].astype(BF16)
        kr = refs[2 * s + 1][...]
        kn = _dot(c, wk_ref[...])
        v_ref[lo:lo + rows, :] = _dot(c, wv_ref[...]).astype(BF16)
        ss_rope = jnp.sum(kr * kr, axis=-1, keepdims=True)
        for h in range(MLA_HEADS):
            knh = kn[:, h * QK_NOPE:(h + 1) * QK_NOPE]
            ss = jnp.sum(knh * knh, axis=-1, keepdims=True) + ss_rope
            r = lax.rsqrt(ss * (1.0 / QK_HEAD) + EPS)
            k_ref[h, lo:lo + rows, :QK_NOPE] = (knh * r * g_nope).astype(BF16)
            k_ref[h, lo:lo + rows, QK_NOPE:] = (kr * r * g_rope).astype(BF16)
        lo += rows


def _kv_up_kernel(*refs, seg_rows):
    n_seg = len(seg_rows)
    wk_ref, wv_ref, gk_ref, k_ref, v_ref = refs[2 * n_seg:]
    _kv_up_body(refs[:2 * n_seg], seg_rows, wk_ref, wv_ref, gk_ref, k_ref, v_ref)


def _kv_up(segments, wk, wv, gk, *, steps):
    seg_rows = tuple(c.shape[0] // steps for c, _ in segments)
    tm = sum(seg_rows)
    m = steps * tm
    in_specs, args = [], []
    for (c, kr), rows in zip(segments, seg_rows):
        in_specs += [pl.BlockSpec((rows, KV_LORA), lambda i: (i, 0)),
                     pl.BlockSpec((rows, QK_ROPE), lambda i: (i, 0))]
        args += [c, kr]
    in_specs += [pl.BlockSpec(wk.shape, lambda i: (0, 0)),
                 pl.BlockSpec(wv.shape, lambda i: (0, 0)),
                 pl.BlockSpec((1, QK_HEAD), lambda i: (0, 0))]
    return pl.pallas_call(
        functools.partial(_kv_up_kernel, seg_rows=seg_rows),
        out_shape=(jax.ShapeDtypeStruct((MLA_HEADS, m, QK_HEAD), BF16),
                   jax.ShapeDtypeStruct((m, MLA_HEADS * V_HEAD), BF16)),
        grid=(steps,),
        in_specs=in_specs,
        out_specs=(pl.BlockSpec((MLA_HEADS, tm, QK_HEAD), lambda i: (0, i, 0)),
                   pl.BlockSpec((tm, MLA_HEADS * V_HEAD), lambda i: (i, 0))),
        compiler_params=_params("parallel"),
        name="kv_up",
    )(*args, wk, wv, gk.reshape(1, QK_HEAD))


Q_SLOT = 2 * QK_NOPE


def _q_proj_kernel(x_ref, g_ref, wdq_ref, gl_ref, wuq_ref, cos_ref, sin_ref, gq_ref, q_ref, *, row_chunks):
    g_nope = gq_ref[:, :QK_NOPE]
    g_rope = gq_ref[:, QK_NOPE:]
    tc = x_ref.shape[0] // row_chunks

    def project(c):
        rows = slice(c * tc, (c + 1) * tc)
        x = x_ref[rows, :]
        hn = (x * _rms_scale(x) * g_ref[...]).astype(BF16)
        ql = _dot(hn, wdq_ref[...])
        qn = (ql * _rms_scale(ql) * gl_ref[...]).astype(BF16)
        return _dot(qn, wuq_ref[...])

    def per_head(c, q):
        rows = slice(c * tc, (c + 1) * tc)
        cos = cos_ref[rows, :]
        sin = sin_ref[rows, :]
        for h in range(MLA_HEADS):
            nh = q[:, h * Q_SLOT:h * Q_SLOT + QK_NOPE]
            t = q[:, h * Q_SLOT + QK_NOPE:(h + 1) * Q_SLOT]
            rp = t * cos + pltpu.roll(t, QK_ROPE, 1) * sin
            ss = jnp.sum(nh * nh + rp * rp, axis=-1, keepdims=True)
            r = lax.rsqrt(ss * (1.0 / QK_HEAD) + EPS) * (QK_HEAD ** -0.5 * LOG2E)
            q_ref[h, rows, :QK_NOPE] = (nh * r * g_nope).astype(BF16)
            q_ref[h, rows, QK_NOPE:] = (rp[:, :QK_ROPE] * r * g_rope).astype(BF16)

    q = project(0)
    for c in range(row_chunks):
        q_next = project(c + 1) if c + 1 < row_chunks else None
        per_head(c, q)
        q = q_next


def _q_proj(x, g, wdq, g_lat, wuq, cos, sin, gq, *, tm):
    m, d = x.shape
    tab_blocks = cos.shape[0] // tm
    tw = cos.shape[1]
    return pl.pallas_call(
        functools.partial(_q_proj_kernel, row_chunks=tm // 256),
        out_shape=jax.ShapeDtypeStruct((MLA_HEADS, m, QK_HEAD), BF16),
        grid=(m // tm,),
        in_specs=[
            pl.BlockSpec((tm, d), lambda i: (i, 0)),
            pl.BlockSpec((1, d), lambda i: (0, 0)),
            pl.BlockSpec(wdq.shape, lambda i: (0, 0)),
            pl.BlockSpec((1, Q_LORA), lambda i: (0, 0)),
            pl.BlockSpec(wuq.shape, lambda i: (0, 0)),
            pl.BlockSpec((tm, tw), lambda i: (i % tab_blocks, 0)),
            pl.BlockSpec((tm, tw), lambda i: (i % tab_blocks, 0)),
            pl.BlockSpec((1, QK_HEAD), lambda i: (0, 0)),
        ],
        out_specs=pl.BlockSpec((MLA_HEADS, tm, QK_HEAD), lambda i: (0, i, 0)),
        compiler_params=_params("parallel"),
        name="q_proj",
    )(x, g.reshape(1, d), wdq, g_lat.reshape(1, Q_LORA), wuq, cos, sin, gq.reshape(1, QK_HEAD))


def _attn_causal_kernel(q_ref, k_ref, v_ref, *rest, heads, tq):
    n_cast = (len(rest) - 1) // 2
    o_ref = rest[n_cast]
    for src, dst in zip(rest[:n_cast], rest[n_cast + 1:]):
        dst[...] = src[...].astype(BF16)
    n_q = q_ref.shape[1] // tq
    hq = tq // 2
    krow = lax.broadcasted_iota(jnp.int32, (hq, tq), 0) // CHUNK
    qcol = lax.broadcasted_iota(jnp.int32, (hq, tq), 1) // CHUNK
    keep_a = krow <= qcol
    keep_b = keep_a[:, :hq]

    def right_half(full, fn, part):
        return jnp.concatenate([full[:, :hq], fn(full[:, hq:], part)], axis=1)

    def scores(h, qi):
        lo = qi * tq
        q = q_ref[h, lo:lo + tq, :]
        st_a = jnp.where(keep_a, _dot_nt(k_ref[h, lo:lo + hq, :], q), NEG_INF)
        st_b = jnp.where(keep_b, _dot_nt(k_ref[h, lo + hq:lo + tq, :], q[hq:, :]), NEG_INF)
        st_p = _dot_nt(k_ref[h, :lo, :], q) if qi > 0 else None
        return st_a, st_b, st_p

    def softmax(st):
        st_a, st_b, st_p = st
        m = jnp.max(st_a, axis=0, keepdims=True)
        if st_p is not None:
            m = jnp.maximum(m, jnp.max(st_p, axis=0, keepdims=True))
        m = right_half(m, jnp.maximum, jnp.max(st_b, axis=0, keepdims=True))
        pt_a = jnp.exp2(st_a - m)
        pt_b = jnp.exp2(st_b - m[:, hq:])
        l = jnp.sum(pt_a, axis=0, keepdims=True)
        pt_p = None
        if st_p is not None:
            pt_p = jnp.exp2(st_p - m)
            l = l + jnp.sum(pt_p, axis=0, keepdims=True)
            pt_p = pt_p.astype(BF16)
        l = right_half(l, jnp.add, jnp.sum(pt_b, axis=0, keepdims=True))
        return pt_a.astype(BF16), pt_b.astype(BF16), pt_p, l

    def values(h, qi, p):
        pt_a, pt_b, pt_p, l = p
        lo = qi * tq
        cols = slice(h * V_HEAD, (h + 1) * V_HEAD)
        acc_t = _dot_tn(v_ref[lo:lo + hq, cols], pt_a)
        if pt_p is not None:
            acc_t = acc_t + _dot_tn(v_ref[:lo, cols], pt_p)
        acc_t = right_half(acc_t, jnp.add, _dot_tn(v_ref[lo + hq:lo + tq, cols], pt_b))
        o_ref[lo:lo + tq, cols] = (acc_t / l).T.astype(BF16)

    work = [(h, qi) for h in range(heads) for qi in range(n_q)]
    s_q, p_q = {}, {}
    for step in range(len(work) + 2):
        if step < len(work):
            s_q[step] = scores(*work[step])
        if 0 <= step - 1 < len(work):
            p_q[step - 1] = softmax(s_q.pop(step - 1))
        if 0 <= step - 2 < len(work):
            values(*work[step - 2], p_q.pop(step - 2))


def _kv_attn_kernel(*refs, seg_rows):
    n_seg = len(seg_rows)
    wk_ref, wv_ref, gk_ref, q_ref, o_ref, k_scr, v_scr = refs[2 * n_seg:]
    g_nope = gk_ref[:, :QK_NOPE]
    g_rope = gk_ref[:, QK_NOPE:]

    group = 4
    width = group * QK_NOPE
    segs, lo = [], 0
    for s, rows in enumerate(seg_rows):
        c = refs[2 * s][...].astype(BF16)
        kr = refs[2 * s + 1][...]
        segs.append((lo, rows, c, kr, jnp.sum(kr * kr, axis=-1, keepdims=True)))
        lo += rows

    def project(g):
        cols = slice(g * width, (g + 1) * width)
        kn = []
        for lo, rows, c, _, _ in segs:
            v_scr[lo:lo + rows, cols] = _dot(c, wv_ref[:, cols]).astype(BF16)
            kn.append(_dot(c, wk_ref[:, cols]))
        return kn

    kn_groups = {}

    def key_norm(h):
        g, hh = divmod(h, group)
        for (lo, rows, _, kr, ss_rope), kn in zip(segs, kn_groups[g]):
            knh = kn[:, hh * QK_NOPE:(hh + 1) * QK_NOPE]
            ss = jnp.sum(knh * knh, axis=-1, keepdims=True) + ss_rope
            r = lax.rsqrt(ss * (1.0 / QK_HEAD) + EPS)
            k_scr[h, lo:lo + rows, :QK_NOPE] = (knh * r * g_nope).astype(BF16)
            k_scr[h, lo:lo + rows, QK_NOPE:] = (kr * r * g_rope).astype(BF16)

    def scores(h):
        return _dot_nt(q_ref[h], k_scr[h])

    def softmax(s):
        m = jnp.max(s, axis=-1, keepdims=True)
        p = jnp.exp2(s - m)
        return p.astype(BF16), jnp.sum(p, axis=-1, keepdims=True)

    def values(h, pl_):
        p, l = pl_
        cols = slice(h * V_HEAD, (h + 1) * V_HEAD)
        o_ref[:, cols] = (_dot(p, v_scr[:, cols]) / l).astype(BF16)

    s_q, p_q = {}, {}
    kn_groups[0] = project(0)
    for step in range(MLA_HEADS + 3):
        if step % group == 0 and step // group + 1 < MLA_HEADS // group:
            kn_groups[step // group + 1] = project(step // group + 1)
        if step < MLA_HEADS:
            key_norm(step)
        if 0 <= step - 1 < MLA_HEADS:
            s_q[step - 1] = scores(step - 1)
        if 0 <= step - 2 < MLA_HEADS:
            p_q[step - 2] = softmax(s_q.pop(step - 2))
        if 0 <= step - 3 < MLA_HEADS:
            values(step - 3, p_q.pop(step - 3))


def _kv_attention(segments, wk, wv, gk, q, *, batch):
    seg_rows = tuple(c.shape[0] // batch for c, _ in segments)
    t_kv = sum(seg_rows)
    t_q = q.shape[1] // batch
    in_specs, args = [], []
    for (c, kr), rows in zip(segments, seg_rows):
        in_specs += [pl.BlockSpec((rows, KV_LORA), lambda b: (b, 0)),
                     pl.BlockSpec((rows, QK_ROPE), lambda b: (b, 0))]
        args += [c, kr]
    in_specs += [pl.BlockSpec(wk.shape, lambda b: (0, 0)),
                 pl.BlockSpec(wv.shape, lambda b: (0, 0)),
                 pl.BlockSpec((1, QK_HEAD), lambda b: (0, 0)),
                 pl.BlockSpec((MLA_HEADS, t_q, QK_HEAD), lambda b: (0, b, 0))]
    return pl.pallas_call(
        functools.partial(_kv_attn_kernel, seg_rows=seg_rows),
        out_shape=jax.ShapeDtypeStruct((batch * t_q, MLA_HEADS * V_HEAD), BF16),
        grid=(batch,),
        in_specs=in_specs,
        out_specs=pl.BlockSpec((t_q, MLA_HEADS * V_HEAD), lambda b: (b, 0)),
        scratch_shapes=[pltpu.VMEM((MLA_HEADS, t_kv, QK_HEAD), BF16),
                        pltpu.VMEM((t_kv, MLA_HEADS * V_HEAD), BF16)],
        compiler_params=_params("parallel"),
        name="kv_attention",
    )(*args, wk, wv, gk.reshape(1, QK_HEAD), q)


def _attention(q, k, v, *, batch, t_q, t_kv, heads, tq, cast=()):
    hg = MLA_HEADS // heads
    steps = batch * hg
    kern = functools.partial(_attn_causal_kernel, heads=heads, tq=tq)
    in_specs = [
        pl.BlockSpec((heads, t_q, QK_HEAD), lambda b, g: (g, b, 0)),
        pl.BlockSpec((heads, t_kv, QK_HEAD), lambda b, g: (g, b, 0)),
        pl.BlockSpec((t_kv, heads * V_HEAD), lambda b, g: (b, g)),
    ]
    out_shape = [jax.ShapeDtypeStruct((batch * t_q, MLA_HEADS * V_HEAD), BF16)]
    out_specs = [pl.BlockSpec((t_q, heads * V_HEAD), lambda b, g: (b, g))]
    for wf, layer in cast:
        _, rows, cols = wf.shape
        slab = rows // steps
        assert slab * steps == rows and slab % 16 == 0
        in_specs.append(pl.BlockSpec((None, slab, cols), lambda b, g, layer=layer: (layer, b * hg + g, 0)))
        out_shape.append(jax.ShapeDtypeStruct((1, rows, cols), BF16))
        out_specs.append(pl.BlockSpec((None, slab, cols), lambda b, g: (0, b * hg + g, 0)))
    outs = pl.pallas_call(
        kern,
        out_shape=out_shape,
        grid=(batch, hg),
        in_specs=in_specs,
        out_specs=out_specs,
        compiler_params=_params("parallel", "parallel"),
        name="attention",
    )(q, k, v, *[wf for wf, _ in cast])
    return outs[0], tuple(outs[1:])


def _rope_tables(pos, half, reps, rows):
    inv = jnp.power(ROPE_BASE, -jnp.arange(half, dtype=F32) / half)
    ang = pos.astype(F32)[:, None] * inv[None, :]
    cos = jnp.tile(jnp.cos(ang), (max(rows // pos.shape[0], 1), reps))
    sin = jnp.tile(jnp.sin(ang), (max(rows // pos.shape[0], 1), reps))
    return cos, sin


def _mm_res_and_ffn2(a, w, h, layer, p):
    key = ("ffn2_bf16", layer)
    if key not in p:
        w_in = p.get(("ffn2_w_in_bf16", layer))
        todo = [(p["ffn2_w_out"], layer)] if w_in is not None else [(p["ffn2_w_in"], layer), (p["ffn2_w_out"], layer)]
        h, done = _mm_res(a, w, h, tm=512, cast=tuple(todo))
        p[key] = (w_in, done[0]) if w_in is not None else done
    else:
        h, _ = _mm_res(a, w, h, tm=512)
    w_in, w_out = p[key]
    return _ffn(h, p["ffn2_norm"][layer], w_in, w_out, 0, tm=FFN_TM, tf=FFN_TF)


def _trunk(x, pos, ret_state, ckv_past, krope_past, p):
    batch, seq, d = x.shape
    m = batch * seq
    tm = 1024
    h = x.reshape(m, d)

    if "ret_w_in_bf16" not in p:
        h, p["ret_w_in_bf16"] = _ffn(h, p["ffn1_norm"][0], p["ffn1_w_in_l0"], p["ffn1_w_out_l0"], 0,
                                     tm=FFN_TM, tf=FFN_TF, cast=p["ret_w_in"])
    else:
        h = _ffn(h, p["ffn1_norm"][0], p["ffn1_w_in_l0"], p["ffn1_w_out_l0"], 0, tm=FFN_TM, tf=FFN_TF)
    cos, sin = _rope_tables(pos, RET_DK // 2, 1, tm)
    key = "cast_by_ret_in"
    cast = () if key in p else ((p["ffn1_w_in"], 1), (p["ffn1_w_out"], 1), (p["ret_w_out"], 0), (p["w_o"], 0))
    qkvg, cast_out = _ret_in(h, p["mix_norm"][0], p["ret_w_in_bf16"][0], cos, sin, tm=tm, tn=2048, cast=cast)
    if cast:
        p[key] = cast_out
    ffn1_l1_w_in, ffn1_l1_w_out, ret_w_out, w_o = p[key]
    log_gamma = jnp.log1p(-jnp.exp2(-5.0 - jnp.arange(RET_HEADS, dtype=F32)))
    chunk = min(seq, 256)
    y, s_fin = _ret_core(qkvg, log_gamma, p["ret_gn_gain"][0],
                         None if ret_state is None else ret_state[0],
                         batch=batch, seq=seq, chunk=chunk,
                         heads=max(1, min(RET_HEADS, 256 // seq)))
    h = _mm_res_and_ffn2(y, ret_w_out[0], h, 0, p)

    cos, sin = _rope_tables(pos, QK_ROPE // 2, 1, tm)
    c_new, kr_new = _latent(h, p["kv_norm"], p["w_dkv"], p["kv_lat_norm"], cos, sin, tm=tm)
    if ckv_past is None:
        k_sh, v_sh = _kv_up([(c_new, kr_new)], p["w_uk"], p["w_uv"], p["k_norm"], steps=m // tm)
    else:
        past = ckv_past.shape[1]
        segments = [(ckv_past.reshape(batch * past, KV_LORA), krope_past.reshape(batch * past, QK_ROPE)),
                    (c_new, kr_new)]

    h = _ffn(h, p["ffn1_norm"][1], ffn1_l1_w_in, ffn1_l1_w_out, 0, tm=FFN_TM, tf=FFN_TF)
    cos, sin = _rope_tables(pos, QK_ROPE // 2, 1, tm)
    zeros = jnp.zeros_like(cos)
    cos = jnp.concatenate([cos, cos, zeros, zeros], axis=-1)
    sin = jnp.concatenate([-sin, sin, zeros, zeros], axis=-1)
    q = _q_proj(h, p["mix_norm"][1], p["w_dq"][0], p["q_lat_norm"][0], p["w_uq"][0], cos, sin,
                p["q_norm"][0], tm=tm)
    if ckv_past is None:
        key = ("ffn2_w_in_bf16", 1)
        o, cast_out = _attention(q, k_sh, v_sh, batch=batch, t_q=seq, t_kv=seq, heads=4, tq=512,
                                 cast=() if key in p else ((p["ffn2_w_in"], 1),))
        if cast_out:
            p[key] = cast_out[0]
    else:
        o = _kv_attention(segments, p["w_uk"], p["w_uv"], p["k_norm"], q, batch=batch)
    h = _mm_res_and_ffn2(o, w_o[0], h, 1, p)

    return (h.reshape(batch, seq, d), s_fin[None],
            c_new.reshape(batch, seq, KV_LORA), kr_new.reshape(batch, seq, QK_ROPE))


def kernel(x_prompt, x_sample, state_ret, cache_ckv, cache_krope, ffn1_norm, ffn1_w_in, ffn1_w_out, mix_norm, ffn2_norm, ffn2_w_in, ffn2_w_out, ret_w_in, ret_gn_gain, ret_w_out, kv_norm, w_dkv, kv_lat_norm, w_ukv, k_norm, w_dq, q_lat_norm, w_uq, q_norm, w_o):
    bf = lambda w: w.astype(BF16)
    w_ukv_h = w_ukv.reshape(KV_LORA, MLA_HEADS, QK_NOPE + V_HEAD)
    w_uk = w_ukv_h[:, :, :QK_NOPE].reshape(KV_LORA, MLA_HEADS * QK_NOPE)
    w_uv = w_ukv_h[:, :, QK_NOPE:].reshape(KV_LORA, MLA_HEADS * V_HEAD)
    half = QK_ROPE // 2
    w_uq_h = w_uq.reshape(w_uq.shape[0], Q_LORA, MLA_HEADS, QK_HEAD)
    w_lo = w_uq_h[..., QK_NOPE:QK_NOPE + half]
    w_hi = w_uq_h[..., QK_NOPE + half:]
    w_uq_p = jnp.concatenate([w_uq_h[..., :QK_NOPE], w_lo, w_hi, w_hi, w_lo], axis=-1)
    w_uq_p = w_uq_p.reshape(w_uq.shape[0], Q_LORA, MLA_HEADS * Q_SLOT)
    p = dict(ffn1_norm=ffn1_norm, ffn1_w_in=ffn1_w_in, ffn1_w_out=ffn1_w_out,
             ffn1_w_in_l0=bf(ffn1_w_in[:1]), ffn1_w_out_l0=bf(ffn1_w_out[:1]),
             mix_norm=mix_norm, ffn2_norm=ffn2_norm, ffn2_w_in=ffn2_w_in, ffn2_w_out=ffn2_w_out,
             ret_w_in=ret_w_in, ret_gn_gain=ret_gn_gain, ret_w_out=ret_w_out,
             kv_norm=kv_norm, w_dkv=bf(w_dkv), kv_lat_norm=kv_lat_norm,
             w_uk=bf(w_uk), w_uv=bf(w_uv), k_norm=k_norm,
             w_dq=bf(w_dq), q_lat_norm=q_lat_norm, w_uq=bf(w_uq_p), q_norm=q_norm, w_o=w_o)
    pos_prompt = jnp.arange(x_prompt.shape[1])
    past = cache_ckv.shape[1]
    pos_sample = past + jnp.arange(x_sample.shape[1])
    y_p, ret_p, ckv_p, kr_p = _trunk(x_prompt, pos_prompt, None, None, None, p)
    y_s, ret_s, ckv_s, kr_s = _trunk(x_sample, pos_sample, state_ret, cache_ckv, cache_krope, p)
    return (y_p, y_s, ret_p, ckv_p, kr_p, ret_s, ckv_s, kr_s)
```

```python
import functools

import jax
import jax.numpy as jnp
from jax import lax
from jax.experimental import pallas as pl
from jax.experimental.pallas import tpu as pltpu

F32 = jnp.float32
BF16 = jnp.bfloat16

D_MODEL = 2048
CHUNK = 64
D_FF = 5632
RET_HEADS = 8
RET_DK = D_MODEL // RET_HEADS
RET_DV = 2 * D_MODEL // RET_HEADS
MLA_HEADS = 16
QK_NOPE = 128
QK_ROPE = 64
QK_HEAD = QK_NOPE + QK_ROPE
V_HEAD = 128
KV_LORA = 512
Q_LORA = 512
ROPE_BASE = 10000.0
EPS = 1e-6
NEG_INF = -1e30
LOG2E = 1.4426950408889634

VMEM_LIMIT_BYTES = 60 * 1024 * 1024

FFN_TM = 1024
FFN_TF = 512
FFN_NORM_CHUNKS = 4
FFN_DMA_BLOCKS = 2


def _params(*semantics):
    return pltpu.CompilerParams(dimension_semantics=semantics,
                                vmem_limit_bytes=VMEM_LIMIT_BYTES)


def _rms_scale(x):
    return lax.rsqrt(jnp.mean(x * x, axis=-1, keepdims=True) + EPS)


def _dot(a, b):
    return jnp.dot(a, b, preferred_element_type=F32)


def _dot_nt(a, b):
    return lax.dot_general(a, b, (((1,), (1,)), ((), ())), preferred_element_type=F32)


def _dot_tn(a, b):
    return lax.dot_general(a, b, (((0,), (0,)), ((), ())), preferred_element_type=F32)


def _silu(x):
    return x * jax.nn.sigmoid(x)


def _ffn_kernel(*refs, layer, tf, cast_pieces):
    if cast_pieces:
        (g_ref, x_hbm, win_hbm, wout_hbm, cast_src, o_hbm, cast_dst, acc, xn_ref, wa_buf, wb_buf, wo_buf,
         wsem, xsem, osem, stage_in, stage_out, csem_in, csem_out) = refs
    else:
        g_ref, x_hbm, win_hbm, wout_hbm, o_hbm, acc, xn_ref, wa_buf, wb_buf, wo_buf, wsem, xsem, osem = refs
    i = pl.program_id(0)
    n_tiles = pl.num_programs(0)
    tm = acc.shape[1]
    dma_cols = FFN_DMA_BLOCKS * tf
    n_pos = 1 + (D_FF - tf) // dma_cols
    last = n_pos - 1
    a_slot = i % 2

    def weight_copies(pos, slot):
        if isinstance(pos, int) and pos == 0:
            col, cols = 0, tf
        else:
            col, cols = pos * dma_cols - (dma_cols - tf), dma_cols
            if not isinstance(pos, int):
                col = pl.multiple_of(col, tf)
        return (
            pltpu.make_async_copy(win_hbm.at[layer, :, pl.ds(col, cols)], wa_buf.at[slot, :, pl.ds(0, cols)],
                                  wsem.at[slot, 0]),
            pltpu.make_async_copy(win_hbm.at[layer, :, pl.ds(D_FF + col, cols)], wb_buf.at[slot, :, pl.ds(0, cols)],
                                  wsem.at[slot, 1]),
            pltpu.make_async_copy(wout_hbm.at[layer, pl.ds(col, cols), :], wo_buf.at[slot, pl.ds(0, cols), :],
                                  wsem.at[slot, 2]),
        )

    def x_copy(tile, slot):
        src = x_hbm.at[pl.ds(pl.multiple_of(tile * tm, tm), tm), :]
        return pltpu.make_async_copy(src, acc.at[slot], xsem.at[slot])

    def out_copy(tile, slot):
        dst = o_hbm.at[pl.ds(pl.multiple_of(tile * tm, tm), tm), :]
        return pltpu.make_async_copy(acc.at[slot], dst, osem.at[slot])

    def update(xn, slot, cols):
        a = _dot(xn, wa_buf[slot, :, :cols])
        b = _dot(xn, wb_buf[slot, :, :cols])
        h = (_silu(a) * (0.5 * b)).astype(BF16)
        return _dot(h, wo_buf[slot, :cols, :])

    @pl.when(i == 0)
    def _():
        x_copy(0, 0).start()
        for cp in weight_copies(0, 0):
            cp.start()

    x_copy(i, a_slot).wait()

    def sync(pos, w_slot):
        for cp in weight_copies(pos, w_slot):
            cp.wait()

        @pl.when(pos < last)
        def _():
            for cp in weight_copies(pos + 1, 1 - w_slot):
                cp.start()

        @pl.when((pos == last) & (i + 1 < n_tiles))
        def _():
            for cp in weight_copies(0, 0):
                cp.start()

        @pl.when(pos == 1)
        def _():
            @pl.when(i > 0)
            def _():
                out_copy(i - 1, 1 - a_slot).wait()

            @pl.when(i + 1 < n_tiles)
            def _():
                x_copy(i + 1, 1 - a_slot).start()

        if cast_pieces:
            side_cast(pos)

    def cast_copies(n, slot):
        piece = stage_in.shape[1]
        rows = pl.ds(pl.multiple_of((i * cast_pieces + n) * piece, piece), piece)
        return (pltpu.make_async_copy(cast_src.at[0, rows, :], stage_in.at[slot], csem_in.at[slot]),
                pltpu.make_async_copy(stage_out.at[slot], cast_dst.at[0, rows, :], csem_out.at[slot]))

    def side_cast(pos):
        @pl.when(pos == 0)
        def _():
            cast_copies(0, 0)[0].start()

        @pl.when((pos >= 1) & (pos <= cast_pieces))
        def _():
            n = pos - 1
            slot = n % 2
            cast_copies(n, slot)[0].wait()

            @pl.when(n + 1 < cast_pieces)
            def _():
                cast_copies(n + 1, 1 - slot)[0].start()

            @pl.when(n >= 2)
            def _():
                cast_copies(n - 2, slot)[1].wait()

            stage_out[slot] = stage_in[slot].astype(BF16)
            cast_copies(n, slot)[1].start()

        @pl.when(pos == cast_pieces + 1)
        def _():
            for n in (cast_pieces - 2, cast_pieces - 1):
                cast_copies(n, n % 2)[1].wait()

    sync(0, 0)
    tc = tm // FFN_NORM_CHUNKS

    def norm(c):
        x = acc[a_slot, c * tc:(c + 1) * tc, :]
        xn = (x * _rms_scale(x) * g_ref[...]).astype(BF16)
        xn_ref[c * tc:(c + 1) * tc, :] = xn
        return x, xn

    cur = norm(0)
    for c in range(FFN_NORM_CHUNKS):
        nxt_c = norm(c + 1) if c + 1 < FFN_NORM_CHUNKS else None
        x, xn = cur
        acc[a_slot, c * tc:(c + 1) * tc, :] = x + update(xn, 0, tf)
        cur = nxt_c

    def position(pos, carry):
        w_slot = pos % 2
        sync(pos, w_slot)
        acc[a_slot] += update(xn_ref[...], w_slot, dma_cols)
        return carry

    lax.fori_loop(1, n_pos, position, 0)

    out_copy(i, a_slot).start()

    @pl.when(i + 1 == n_tiles)
    def _():
        out_copy(i, a_slot).wait()


def _ffn(x, g, w_in, w_out, layer, *, tm, tf, cast=None):
    m, d = x.shape
    n_tiles = m // tm
    dma_cols = FFN_DMA_BLOCKS * tf
    n_pos = 1 + (D_FF - tf) // dma_cols
    assert (D_FF - tf) % dma_cols == 0 and n_pos % 2 == 0
    in_specs = [pl.BlockSpec((1, d), lambda i: (0, 0))] + [pl.BlockSpec(memory_space=pl.ANY)] * 3
    out_shape = [jax.ShapeDtypeStruct((m, d), F32)]
    scratch = [
        pltpu.VMEM((2, tm, d), F32),
        pltpu.VMEM((tm, d), BF16),
        pltpu.VMEM((2, d, dma_cols), BF16),
        pltpu.VMEM((2, d, dma_cols), BF16),
        pltpu.VMEM((2, dma_cols, d), BF16),
        pltpu.SemaphoreType.DMA((2, 3)),
        pltpu.SemaphoreType.DMA((2,)),
        pltpu.SemaphoreType.DMA((2,)),
    ]
    args = [g.reshape(1, d), x, w_in, w_out]
    cast_pieces = 0
    if cast is not None:
        cast_pieces = n_pos - 2
        _, rows, cols = cast.shape
        piece = rows // (n_tiles * cast_pieces)
        assert piece * n_tiles * cast_pieces == rows and piece % 16 == 0 and cast_pieces >= 2
        in_specs.append(pl.BlockSpec(memory_space=pl.ANY))
        out_shape.append(jax.ShapeDtypeStruct(cast.shape, BF16))
        scratch += [pltpu.VMEM((2, piece, cols), F32), pltpu.VMEM((2, piece, cols), BF16),
                    pltpu.SemaphoreType.DMA((2,)), pltpu.SemaphoreType.DMA((2,))]
        args.append(cast)
    outs = pl.pallas_call(
        functools.partial(_ffn_kernel, layer=layer, tf=tf, cast_pieces=cast_pieces),
        out_shape=out_shape,
        grid=(n_tiles,),
        in_specs=in_specs,
        out_specs=[pl.BlockSpec(memory_space=pl.ANY)] * len(out_shape),
        scratch_shapes=scratch,
        compiler_params=_params("arbitrary"),
        name="ffn",
    )(*args)
    return outs[0] if cast is None else (outs[0], outs[1])


def _ret_in_kernel(x_ref, g_ref, w_ref, cos_ref, sin_ref, *rest, rope_blocks, heads_per_block, cast_steps):
    n_cast = len(cast_steps)
    o_ref = rest[n_cast]
    xn_ref = rest[-1]
    j = pl.program_id(1)

    for src, dst, steps in zip(rest[:n_cast], rest[n_cast + 1:-1], cast_steps):
        @pl.when(j < steps)
        def _(src=src, dst=dst):
            dst[...] = src[...].astype(BF16)

    @pl.when(j == 0)
    def _():
        x = x_ref[...]
        xn_ref[...] = (x * _rms_scale(x) * g_ref[...]).astype(BF16)

    half = RET_DK // 2

    @pl.when(j < 2 * rope_blocks)
    def _():
        scale = jnp.where(j < rope_blocks, RET_DK ** -0.5, 1.0).astype(F32)
        cos = cos_ref[...] * scale
        sin = sin_ref[...] * scale
        xn = xn_ref[...]
        for h in range(heads_per_block):
            lo = h * RET_DK
            y = _dot(xn, w_ref[:, lo:lo + RET_DK])
            x1 = y[:, :half]
            x2 = y[:, half:]
            o_ref[:, lo:lo + half] = (x1 * cos - x2 * sin).astype(BF16)
            o_ref[:, lo + half:lo + RET_DK] = (x1 * sin + x2 * cos).astype(BF16)

    @pl.when(j >= 2 * rope_blocks)
    def _():
        o_ref[...] = _dot(xn_ref[...], w_ref[...]).astype(BF16)


def _ret_in(x, g, w, cos, sin, *, tm, tn, cast=()):
    m, d = x.shape
    n = w.shape[1]
    tab_blocks = cos.shape[0] // tm
    n_i, n_j = m // tm, n // tn
    in_specs = [
        pl.BlockSpec((tm, d), lambda i, j: (i, 0)),
        pl.BlockSpec((1, d), lambda i, j: (0, 0)),
        pl.BlockSpec((d, tn), lambda i, j: (0, j)),
        pl.BlockSpec((tm, RET_DK // 2), lambda i, j: (i % tab_blocks, 0)),
        pl.BlockSpec((tm, RET_DK // 2), lambda i, j: (i % tab_blocks, 0)),
    ]
    out_shape = [jax.ShapeDtypeStruct((m, n), BF16)]
    out_specs = [pl.BlockSpec((tm, tn), lambda i, j: (i, j))]
    cast_steps = []
    for wf, layer in cast:
        _, rows, cols = wf.shape
        slab = rows // n_i
        assert slab * n_i == rows and slab % 16 == 0
        steps = max(s for s in range(1, n_j + 1) if cols % (128 * s) == 0)
        cast_steps.append(steps)
        in_specs.append(pl.BlockSpec((None, slab, cols // steps),
                                     lambda i, j, layer=layer, steps=steps: (layer, i, jnp.minimum(j, steps - 1))))
        out_shape.append(jax.ShapeDtypeStruct((1, rows, cols), BF16))
        out_specs.append(pl.BlockSpec((None, slab, cols // steps),
                                      lambda i, j, steps=steps: (0, i, jnp.minimum(j, steps - 1))))
    kern = functools.partial(_ret_in_kernel, rope_blocks=(RET_HEADS * RET_DK) // tn,
                             heads_per_block=tn // RET_DK, cast_steps=tuple(cast_steps))
    outs = pl.pallas_call(
        kern,
        out_shape=out_shape,
        grid=(n_i, n_j),
        in_specs=in_specs,
        out_specs=out_specs,
        scratch_shapes=[pltpu.VMEM((tm, d), BF16)],
        compiler_params=_params("parallel", "arbitrary"),
        name="ret_in",
    )(x, g.reshape(1, d), w, cos, sin, *[wf for wf, _ in cast])
    return outs[0], tuple(outs[1:])


def _ret_core_kernel(lg_ref, q_ref, k_ref, v_ref, g_ref, gain_ref, *rest, chunk, heads, has_state):
    if has_state:
        s0_ref, y_ref, s_ref = rest
    else:
        y_ref, s_ref = rest
    head0 = pl.program_id(1) * heads
    n_chunks = q_ref.shape[0] // chunk
    row = lax.broadcasted_iota(jnp.int32, (chunk, chunk), 0)
    col = lax.broadcasted_iota(jnp.int32, (chunk, chunk), 1)
    diff = (row - col).astype(F32)
    n = lax.broadcasted_iota(jnp.int32, (chunk, 1), 0).astype(F32)

    def head_consts(hh):
        lg = lg_ref[head0 + hh]
        decay = jnp.where(diff >= 0, jnp.exp(lg * jnp.maximum(diff, 0.0)), 0.0)
        cross = jnp.exp(lg * (n + 1.0))
        kdec = jnp.exp(lg * (chunk - 1.0 - n))
        carry = jnp.exp(lg * jnp.full((1, RET_DV), float(chunk), F32))
        return decay, cross, kdec, carry

    def mix(hh, c, consts):
        decay, cross, kdec, carry = consts
        rows = slice(c * chunk, (c + 1) * chunk)
        q = q_ref[rows, hh * RET_DK:(hh + 1) * RET_DK]
        k = k_ref[rows, hh * RET_DK:(hh + 1) * RET_DK]
        v = v_ref[rows, hh * RET_DV:(hh + 1) * RET_DV]
        inner = (_dot_nt(q, k) * decay).astype(BF16)
        if c > 0:
            s_old = s_ref[0, hh]
        elif has_state:
            s_old = s0_ref[0, hh]
        else:
            s_old = None
        kd = (k.astype(F32) * kdec).astype(BF16)
        o = _dot(inner, v)
        s_new = _dot_tn(kd, v)
        if s_old is not None:
            o = o + cross * _dot(q, s_old.astype(BF16))
            s_new = carry * s_old + s_new
        s_ref[0, hh] = s_new
        return o

    def norm_gate(hh, c, o):
        rows = slice(c * chunk, (c + 1) * chunk)
        cols = slice(hh * RET_DV, (hh + 1) * RET_DV)
        mu = jnp.mean(o, axis=-1, keepdims=True)
        dev = o - mu
        var = jnp.mean(dev * dev, axis=-1, keepdims=True)
        of = dev * lax.rsqrt(var + EPS) * gain_ref[:, cols]
        y_ref[rows, cols] = (_silu(g_ref[rows, cols].astype(F32)) * of).astype(BF16)

    consts = None
    pending = None
    for hh in range(heads):
        for c in range(n_chunks):
            if c == 0:
                consts = head_consts(hh)
            o = mix(hh, c, consts)
            if pending is not None:
                norm_gate(*pending)
            pending = (hh, c, o)
    norm_gate(*pending)


def _ret_core(qkvg, log_gamma, gain, state, *, batch, seq, chunk, heads):
    m = qkvg.shape[0]
    hk = RET_HEADS * RET_DK
    hv = RET_HEADS * RET_DV
    wk = heads * RET_DK
    wv = heads * RET_DV
    kblk = hk // wk
    vblk = 2 * hk // wv
    gblk = vblk + hv // wv
    has_state = state is not None
    in_specs = [
        pl.BlockSpec(memory_space=pltpu.SMEM),
        pl.BlockSpec((seq, wk), lambda b, g: (b, g)),
        pl.BlockSpec((seq, wk), lambda b, g: (b, kblk + g)),
        pl.BlockSpec((seq, wv), lambda b, g: (b, vblk + g)),
        pl.BlockSpec((seq, wv), lambda b, g: (b, gblk + g)),
        pl.BlockSpec((1, wv), lambda b, g: (0, g)),
    ]
    args = [log_gamma, qkvg, qkvg, qkvg, qkvg, gain.reshape(1, hv)]
    state_spec = pl.BlockSpec((1, heads, RET_DK, RET_DV), lambda b, g: (b, g, 0, 0))
    if has_state:
        in_specs.append(state_spec)
        args.append(state)
    kern = functools.partial(_ret_core_kernel, chunk=chunk, heads=heads, has_state=has_state)
    return pl.pallas_call(
        kern,
        out_shape=(jax.ShapeDtypeStruct((m, hv), BF16),
                   jax.ShapeDtypeStruct((batch, RET_HEADS, RET_DK, RET_DV), F32)),
        grid=(batch, RET_HEADS // heads),
        in_specs=in_specs,
        out_specs=(pl.BlockSpec((seq, wv), lambda b, g: (b, g)), state_spec),
        compiler_params=_params("parallel", "parallel"),
        name="ret_core",
    )(*args)


def _mm_res_kernel(a_ref, w_ref, r_ref, *rest):
    n_cast = (len(rest) - 1) // 2
    o_ref = rest[n_cast]
    o_ref[...] = r_ref[...] + _dot(a_ref[...], w_ref[...])
    for src, dst in zip(rest[:n_cast], rest[n_cast + 1:]):
        dst[...] = src[...].astype(BF16)


def _mm_res(a, w, res, *, tm, cast=()):
    m, k = a.shape
    n = w.shape[1]
    steps = m // tm
    in_specs = [
        pl.BlockSpec((tm, k), lambda i: (i, 0)),
        pl.BlockSpec((k, n), lambda i: (0, 0), pipeline_mode=pl.Buffered(1)),
        pl.BlockSpec((tm, n), lambda i: (i, 0)),
    ]
    out_shape = [jax.ShapeDtypeStruct((m, n), F32)]
    out_specs = [pl.BlockSpec((tm, n), lambda i: (i, 0))]
    for wf, layer in cast:
        _, rows, cols = wf.shape
        slab = rows // steps
        assert slab * steps == rows and slab % 16 == 0
        in_specs.append(pl.BlockSpec((None, slab, cols), lambda i, layer=layer: (layer, i, 0)))
        out_shape.append(jax.ShapeDtypeStruct((1, rows, cols), BF16))
        out_specs.append(pl.BlockSpec((None, slab, cols), lambda i: (0, i, 0)))
    outs = pl.pallas_call(
        _mm_res_kernel,
        out_shape=out_shape,
        grid=(steps,),
        in_specs=in_specs,
        out_specs=out_specs,
        compiler_params=_params("parallel"),
        name="mm_res",
    )(a, w, res, *[wf for wf, _ in cast])
    return outs[0], tuple(outs[1:])


def _latent_kernel(x_ref, g_ref, w_ref, gl_ref, cos_ref, sin_ref, c_ref, kr_ref, *, row_chunks):
    half = QK_ROPE // 2
    tc = x_ref.shape[0] // row_chunks

    def norm(c):
        x = x_ref[c * tc:(c + 1) * tc, :]
        return (x * _rms_scale(x) * g_ref[...]).astype(BF16)

    def project(c, xn):
        rows = slice(c * tc, (c + 1) * tc)
        y = _dot(xn, w_ref[...])
        cl = y[:, :KV_LORA]
        c_ref[rows, :] = cl * _rms_scale(cl) * gl_ref[...]
        x1 = y[:, KV_LORA:KV_LORA + half]
        x2 = y[:, KV_LORA + half:KV_LORA + QK_ROPE]
        cos = cos_ref[rows, :]
        sin = sin_ref[rows, :]
        kr_ref[rows, :] = jnp.concatenate([x1 * cos - x2 * sin, x1 * sin + x2 * cos], axis=-1)

    xn = norm(0)
    for c in range(row_chunks):
        xn_next = norm(c + 1) if c + 1 < row_chunks else None
        project(c, xn)
        xn = xn_next


def _latent(x, g, w, g_lat, cos, sin, *, tm):
    m, d = x.shape
    n = w.shape[1]
    half = QK_ROPE // 2
    tab_blocks = cos.shape[0] // tm
    return pl.pallas_call(
        functools.partial(_latent_kernel, row_chunks=tm // 256),
        out_shape=(jax.ShapeDtypeStruct((m, KV_LORA), F32),
                   jax.ShapeDtypeStruct((m, QK_ROPE), F32)),
        grid=(m // tm,),
        in_specs=[
            pl.BlockSpec((tm, d), lambda i: (i, 0)),
            pl.BlockSpec((1, d), lambda i: (0, 0)),
            pl.BlockSpec((d, n), lambda i: (0, 0)),
            pl.BlockSpec((1, KV_LORA), lambda i: (0, 0)),
            pl.BlockSpec((tm, half), lambda i: (i % tab_blocks, 0)),
            pl.BlockSpec((tm, half), lambda i: (i % tab_blocks, 0)),
        ],
        out_specs=(pl.BlockSpec((tm, KV_LORA), lambda i: (i, 0)),
                   pl.BlockSpec((tm, QK_ROPE), lambda i: (i, 0))),
        compiler_params=_params("parallel"),
        name="latent",
    )(x, g.reshape(1, d), w, g_lat.reshape(1, KV_LORA), cos, sin)


def _kv_up_body(seg_refs, seg_rows, wk_ref, wv_ref, gk_ref, k_ref, v_ref):
    refs = seg_refs
    g_nope = gk_ref[:, :QK_NOPE]
    g_rope = gk_ref[:, QK_NOPE:]
    lo = 0
    for s, rows in enumerate(seg_rows):
        c = refs[2 * s][...].astype(BF16)
        kr = refs[2 * s + 1][...]
        kn = _dot(c, wk_ref[...])
        v_ref[lo:lo + rows, :] = _dot(c, wv_ref[...]).astype(BF16)
        ss_rope = jnp.sum(kr * kr, axis=-1, keepdims=True)
        for h in range(MLA_HEADS):
            knh = kn[:, h * QK_NOPE:(h + 1) * QK_NOPE]
            ss = jnp.sum(knh * knh, axis=-1, keepdims=True) + ss_rope
            r = lax.rsqrt(ss * (1.0 / QK_HEAD) + EPS)
            k_ref[h, lo:lo + rows, :QK_NOPE] = (knh * r * g_nope).astype(BF16)
            k_ref[h, lo:lo + rows, QK_NOPE:] = (kr * r * g_rope).astype(BF16)
        lo += rows


def _kv_up_kernel(*refs, seg_rows):
    n_seg = len(seg_rows)
    wk_ref, wv_ref, gk_ref, k_ref, v_ref = refs[2 * n_seg:]
    _kv_up_body(refs[:2 * n_seg], seg_rows, wk_ref, wv_ref, gk_ref, k_ref, v_ref)


def _kv_up(segments, wk, wv, gk, *, steps):
    seg_rows = tuple(c.shape[0] // steps for c, _ in segments)
    tm = sum(seg_rows)
    m = steps * tm
    in_specs, args = [], []
    for (c, kr), rows in zip(segments, seg_rows):
        in_specs += [pl.BlockSpec((rows, KV_LORA), lambda i: (i, 0)),
                     pl.BlockSpec((rows, QK_ROPE), lambda i: (i, 0))]
        args += [c, kr]
    in_specs += [pl.BlockSpec(wk.shape, lambda i: (0, 0)),
                 pl.BlockSpec(wv.shape, lambda i: (0, 0)),
                 pl.BlockSpec((1, QK_HEAD), lambda i: (0, 0))]
    return pl.pallas_call(
        functools.partial(_kv_up_kernel, seg_rows=seg_rows),
        out_shape=(jax.ShapeDtypeStruct((MLA_HEADS, m, QK_HEAD), BF16),
                   jax.ShapeDtypeStruct((m, MLA_HEADS * V_HEAD), BF16)),
        grid=(steps,),
        in_specs=in_specs,
        out_specs=(pl.BlockSpec((MLA_HEADS, tm, QK_HEAD), lambda i: (0, i, 0)),
                   pl.BlockSpec((tm, MLA_HEADS * V_HEAD), lambda i: (i, 0))),
        compiler_params=_params("parallel"),
        name="kv_up",
    )(*args, wk, wv, gk.reshape(1, QK_HEAD))


Q_SLOT = 2 * QK_NOPE


def _q_proj_kernel(x_ref, g_ref, wdq_ref, gl_ref, wuq_ref, cos_ref, sin_ref, gq_ref, q_ref, *, row_chunks):
    g_nope = gq_ref[:, :QK_NOPE]
    g_rope = gq_ref[:, QK_NOPE:]
    tc = x_ref.shape[0] // row_chunks

    def project(c):
        rows = slice(c * tc, (c + 1) * tc)
        x = x_ref[rows, :]
        hn = (x * _rms_scale(x) * g_ref[...]).astype(BF16)
        ql = _dot(hn, wdq_ref[...])
        qn = (ql * _rms_scale(ql) * gl_ref[...]).astype(BF16)
        return _dot(qn, wuq_ref[...])

    def per_head(c, q):
        rows = slice(c * tc, (c + 1) * tc)
        cos = cos_ref[rows, :]
        sin = sin_ref[rows, :]
        for h in range(MLA_HEADS):
            nh = q[:, h * Q_SLOT:h * Q_SLOT + QK_NOPE]
            t = q[:, h * Q_SLOT + QK_NOPE:(h + 1) * Q_SLOT]
            rp = t * cos + pltpu.roll(t, QK_ROPE, 1) * sin
            ss = jnp.sum(nh * nh + rp * rp, axis=-1, keepdims=True)
            r = lax.rsqrt(ss * (1.0 / QK_HEAD) + EPS) * (QK_HEAD ** -0.5 * LOG2E)
            q_ref[h, rows, :QK_NOPE] = (nh * r * g_nope).astype(BF16)
            q_ref[h, rows, QK_NOPE:] = (rp[:, :QK_ROPE] * r * g_rope).astype(BF16)

    q = project(0)
    for c in range(row_chunks):
        q_next = project(c + 1) if c + 1 < row_chunks else None
        per_head(c, q)
        q = q_next


def _q_proj(x, g, wdq, g_lat, wuq, cos, sin, gq, *, tm):
    m, d = x.shape
    tab_blocks = cos.shape[0] // tm
    tw = cos.shape[1]
    return pl.pallas_call(
        functools.partial(_q_proj_kernel, row_chunks=tm // 256),
        out_shape=jax.ShapeDtypeStruct((MLA_HEADS, m, QK_HEAD), BF16),
        grid=(m // tm,),
        in_specs=[
            pl.BlockSpec((tm, d), lambda i: (i, 0)),
            pl.BlockSpec((1, d), lambda i: (0, 0)),
            pl.BlockSpec(wdq.shape, lambda i: (0, 0)),
            pl.BlockSpec((1, Q_LORA), lambda i: (0, 0)),
            pl.BlockSpec(wuq.shape, lambda i: (0, 0)),
            pl.BlockSpec((tm, tw), lambda i: (i % tab_blocks, 0)),
            pl.BlockSpec((tm, tw), lambda i: (i % tab_blocks, 0)),
            pl.BlockSpec((1, QK_HEAD), lambda i: (0, 0)),
        ],
        out_specs=pl.BlockSpec((MLA_HEADS, tm, QK_HEAD), lambda i: (0, i, 0)),
        compiler_params=_params("parallel"),
        name="q_proj",
    )(x, g.reshape(1, d), wdq, g_lat.reshape(1, Q_LORA), wuq, cos, sin, gq.reshape(1, QK_HEAD))


def _attn_causal_kernel(q_ref, k_ref, v_ref, *rest, heads, tq):
    n_cast = (len(rest) - 1) // 2
    o_ref = rest[n_cast]
    for src, dst in zip(rest[:n_cast], rest[n_cast + 1:]):
        dst[...] = src[...].astype(BF16)
    n_q = q_ref.shape[1] // tq
    hq = tq // 2
    krow = lax.broadcasted_iota(jnp.int32, (hq, tq), 0) // CHUNK
    qcol = lax.broadcasted_iota(jnp.int32, (hq, tq), 1) // CHUNK
    keep_a = krow <= qcol
    keep_b = keep_a[:, :hq]

    def right_half(full, fn, part):
        return jnp.concatenate([full[:, :hq], fn(full[:, hq:], part)], axis=1)

    def scores(h, qi):
        lo = qi * tq
        q = q_ref[h, lo:lo + tq, :]
        st_a = jnp.where(keep_a, _dot_nt(k_ref[h, lo:lo + hq, :], q), NEG_INF)
        st_b = jnp.where(keep_b, _dot_nt(k_ref[h, lo + hq:lo + tq, :], q[hq:, :]), NEG_INF)
        st_p = _dot_nt(k_ref[h, :lo, :], q) if qi > 0 else None
        return st_a, st_b, st_p

    def softmax(st):
        st_a, st_b, st_p = st
        m = jnp.max(st_a, axis=0, keepdims=True)
        if st_p is not None:
            m = jnp.maximum(m, jnp.max(st_p, axis=0, keepdims=True))
        m = right_half(m, jnp.maximum, jnp.max(st_b, axis=0, keepdims=True))
        pt_a = jnp.exp2(st_a - m)
        pt_b = jnp.exp2(st_b - m[:, hq:])
        l = jnp.sum(pt_a, axis=0, keepdims=True)
        pt_p = None
        if st_p is not None:
            pt_p = jnp.exp2(st_p - m)
            l = l + jnp.sum(pt_p, axis=0, keepdims=True)
            pt_p = pt_p.astype(BF16)
        l = right_half(l, jnp.add, jnp.sum(pt_b, axis=0, keepdims=True))
        return pt_a.astype(BF16), pt_b.astype(BF16), pt_p, l

    def values(h, qi, p):
        pt_a, pt_b, pt_p, l = p
        lo = qi * tq
        cols = slice(h * V_HEAD, (h + 1) * V_HEAD)
        acc_t = _dot_tn(v_ref[lo:lo + hq, cols], pt_a)
        if pt_p is not None:
            acc_t = acc_t + _dot_tn(v_ref[:lo, cols], pt_p)
        acc_t = right_half(acc_t, jnp.add, _dot_tn(v_ref[lo + hq:lo + tq, cols], pt_b))
        o_ref[lo:lo + tq, cols] = (acc_t / l).T.astype(BF16)

    work = [(h, qi) for h in range(heads) for qi in range(n_q)]
    s_q, p_q = {}, {}
    for step in range(len(work) + 2):
        if step < len(work):
            s_q[step] = scores(*work[step])
        if 0 <= step - 1 < len(work):
            p_q[step - 1] = softmax(s_q.pop(step - 1))
        if 0 <= step - 2 < len(work):
            values(*work[step - 2], p_q.pop(step - 2))


def _kv_attn_kernel(*refs, seg_rows):
    n_seg = len(seg_rows)
    wk_ref, wv_ref, gk_ref, q_ref, o_ref, k_scr, v_scr = refs[2 * n_seg:]
    g_nope = gk_ref[:, :QK_NOPE]
    g_rope = gk_ref[:, QK_NOPE:]

    group = 4
    width = group * QK_NOPE
    segs, lo = [], 0
    for s, rows in enumerate(seg_rows):
        c = refs[2 * s][...].astype(BF16)
        kr = refs[2 * s + 1][...]
        segs.append((lo, rows, c, kr, jnp.sum(kr * kr, axis=-1, keepdims=True)))
        lo += rows

    def project(g):
        cols = slice(g * width, (g + 1) * width)
        kn = []
        for lo, rows, c, _, _ in segs:
            v_scr[lo:lo + rows, cols] = _dot(c, wv_ref[:, cols]).astype(BF16)
            kn.append(_dot(c, wk_ref[:, cols]))
        return kn

    kn_groups = {}

    def key_norm(h):
        g, hh = divmod(h, group)
        for (lo, rows, _, kr, ss_rope), kn in zip(segs, kn_groups[g]):
            knh = kn[:, hh * QK_NOPE:(hh + 1) * QK_NOPE]
            ss = jnp.sum(knh * knh, axis=-1, keepdims=True) + ss_rope
            r = lax.rsqrt(ss * (1.0 / QK_HEAD) + EPS)
            k_scr[h, lo:lo + rows, :QK_NOPE] = (knh * r * g_nope).astype(BF16)
            k_scr[h, lo:lo + rows, QK_NOPE:] = (kr * r * g_rope).astype(BF16)

    def scores(h):
        return _dot_nt(q_ref[h], k_scr[h])

    def softmax(s):
        m = jnp.max(s, axis=-1, keepdims=True)
        p = jnp.exp2(s - m)
        return p.astype(BF16), jnp.sum(p, axis=-1, keepdims=True)

    def values(h, pl_):
        p, l = pl_
        cols = slice(h * V_HEAD, (h + 1) * V_HEAD)
        o_ref[:, cols] = (_dot(p, v_scr[:, cols]) / l).astype(BF16)

    s_q, p_q = {}, {}
    kn_groups[0] = project(0)
    for step in range(MLA_HEADS + 3):
        if step % group == 0 and step // group + 1 < MLA_HEADS // group:
            kn_groups[step // group + 1] = project(step // group + 1)
        if step < MLA_HEADS:
            key_norm(step)
        if 0 <= step - 1 < MLA_HEADS:
            s_q[step - 1] = scores(step - 1)
        if 0 <= step - 2 < MLA_HEADS:
            p_q[step - 2] = softmax(s_q.pop(step - 2))
        if 0 <= step - 3 < MLA_HEADS:
            values(step - 3, p_q.pop(step - 3))


def _kv_attention(segments, wk, wv, gk, q, *, batch):
    seg_rows = tuple(c.shape[0] // batch for c, _ in segments)
    t_kv = sum(seg_rows)
    t_q = q.shape[1] // batch
    in_specs, args = [], []
    for (c, kr), rows in zip(segments, seg_rows):
        in_specs += [pl.BlockSpec((rows, KV_LORA), lambda b: (b, 0)),
                     pl.BlockSpec((rows, QK_ROPE), lambda b: (b, 0))]
        args += [c, kr]
    in_specs += [pl.BlockSpec(wk.shape, lambda b: (0, 0)),
                 pl.BlockSpec(wv.shape, lambda b: (0, 0)),
                 pl.BlockSpec((1, QK_HEAD), lambda b: (0, 0)),
                 pl.BlockSpec((MLA_HEADS, t_q, QK_HEAD), lambda b: (0, b, 0))]
    return pl.pallas_call(
        functools.partial(_kv_attn_kernel, seg_rows=seg_rows),
        out_shape=jax.ShapeDtypeStruct((batch * t_q, MLA_HEADS * V_HEAD), BF16),
        grid=(batch,),
        in_specs=in_specs,
        out_specs=pl.BlockSpec((t_q, MLA_HEADS * V_HEAD), lambda b: (b, 0)),
        scratch_shapes=[pltpu.VMEM((MLA_HEADS, t_kv, QK_HEAD), BF16),
                        pltpu.VMEM((t_kv, MLA_HEADS * V_HEAD), BF16)],
        compiler_params=_params("parallel"),
        name="kv_attention",
    )(*args, wk, wv, gk.reshape(1, QK_HEAD), q)


def _attention(q, k, v, *, batch, t_q, t_kv, heads, tq, cast=()):
    hg = MLA_HEADS // heads
    steps = batch * hg
    kern = functools.partial(_attn_causal_kernel, heads=heads, tq=tq)
    in_specs = [
        pl.BlockSpec((heads, t_q, QK_HEAD), lambda b, g: (g, b, 0)),
        pl.BlockSpec((heads, t_kv, QK_HEAD), lambda b, g: (g, b, 0)),
        pl.BlockSpec((t_kv, heads * V_HEAD), lambda b, g: (b, g)),
    ]
    out_shape = [jax.ShapeDtypeStruct((batch * t_q, MLA_HEADS * V_HEAD), BF16)]
    out_specs = [pl.BlockSpec((t_q, heads * V_HEAD), lambda b, g: (b, g))]
    for wf, layer in cast:
        _, rows, cols = wf.shape
        slab = rows // steps
        assert slab * steps == rows and slab % 16 == 0
        in_specs.append(pl.BlockSpec((None, slab, cols), lambda b, g, layer=layer: (layer, b * hg + g, 0)))
        out_shape.append(jax.ShapeDtypeStruct((1, rows, cols), BF16))
        out_specs.append(pl.BlockSpec((None, slab, cols), lambda b, g: (0, b * hg + g, 0)))
    outs = pl.pallas_call(
        kern,
        out_shape=out_shape,
        grid=(batch, hg),
        in_specs=in_specs,
        out_specs=out_specs,
        compiler_params=_params("parallel", "parallel"),
        name="attention",
    )(q, k, v, *[wf for wf, _ in cast])
    return outs[0], tuple(outs[1:])


def _rope_tables(pos, half, reps, rows):
    inv = jnp.power(ROPE_BASE, -jnp.arange(half, dtype=F32) / half)
    ang = pos.astype(F32)[:, None] * inv[None, :]
    cos = jnp.tile(jnp.cos(ang), (max(rows // pos.shape[0], 1), reps))
    sin = jnp.tile(jnp.sin(ang), (max(rows // pos.shape[0], 1), reps))
    return cos, sin


def _mm_res_and_ffn2(a, w, h, layer, p):
    key = ("ffn2_bf16", layer)
    if key not in p:
        w_in = p.get(("ffn2_w_in_bf16", layer))
        todo = [(p["ffn2_w_out"], layer)] if w_in is not None else [(p["ffn2_w_in"], layer), (p["ffn2_w_out"], layer)]
        h, done = _mm_res(a, w, h, tm=512, cast=tuple(todo))
        p[key] = (w_in, done[0]) if w_in is not None else done
    else:
        h, _ = _mm_res(a, w, h, tm=512)
    w_in, w_out = p[key]
    return _ffn(h, p["ffn2_norm"][layer], w_in, w_out, 0, tm=FFN_TM, tf=FFN_TF)


def _trunk(x, pos, ret_state, ckv_past, krope_past, p):
    batch, seq, d = x.shape
    m = batch * seq
    tm = 1024
    h = x.reshape(m, d)

    if "ret_w_in_bf16" not in p:
        h, p["ret_w_in_bf16"] = _ffn(h, p["ffn1_norm"][0], p["ffn1_w_in_l0"], p["ffn1_w_out_l0"], 0,
                                     tm=FFN_TM, tf=FFN_TF, cast=p["ret_w_in"])
    else:
        h = _ffn(h, p["ffn1_norm"][0], p["ffn1_w_in_l0"], p["ffn1_w_out_l0"], 0, tm=FFN_TM, tf=FFN_TF)
    cos, sin = _rope_tables(pos, RET_DK // 2, 1, tm)
    key = "cast_by_ret_in"
    cast = () if key in p else ((p["ffn1_w_in"], 1), (p["ffn1_w_out"], 1), (p["ret_w_out"], 0), (p["w_o"], 0))
    qkvg, cast_out = _ret_in(h, p["mix_norm"][0], p["ret_w_in_bf16"][0], cos, sin, tm=tm, tn=2048, cast=cast)
    if cast:
        p[key] = cast_out
    ffn1_l1_w_in, ffn1_l1_w_out, ret_w_out, w_o = p[key]
    log_gamma = jnp.log1p(-jnp.exp2(-5.0 - jnp.arange(RET_HEADS, dtype=F32)))
    chunk = min(seq, 256)
    y, s_fin = _ret_core(qkvg, log_gamma, p["ret_gn_gain"][0],
                         None if ret_state is None else ret_state[0],
                         batch=batch, seq=seq, chunk=chunk,
                         heads=max(1, min(RET_HEADS, 512 // seq)))
    h = _mm_res_and_ffn2(y, ret_w_out[0], h, 0, p)

    cos, sin = _rope_tables(pos, QK_ROPE // 2, 1, tm)
    c_new, kr_new = _latent(h, p["kv_norm"], p["w_dkv"], p["kv_lat_norm"], cos, sin, tm=tm)
    if ckv_past is None:
        k_sh, v_sh = _kv_up([(c_new, kr_new)], p["w_uk"], p["w_uv"], p["k_norm"], steps=m // tm)
    else:
        past = ckv_past.shape[1]
        segments = [(ckv_past.reshape(batch * past, KV_LORA), krope_past.reshape(batch * past, QK_ROPE)),
                    (c_new, kr_new)]

    h = _ffn(h, p["ffn1_norm"][1], ffn1_l1_w_in, ffn1_l1_w_out, 0, tm=FFN_TM, tf=FFN_TF)
    cos, sin = _rope_tables(pos, QK_ROPE // 2, 1, tm)
    zeros = jnp.zeros_like(cos)
    cos = jnp.concatenate([cos, cos, zeros, zeros], axis=-1)
    sin = jnp.concatenate([-sin, sin, zeros, zeros], axis=-1)
    q = _q_proj(h, p["mix_norm"][1], p["w_dq"][0], p["q_lat_norm"][0], p["w_uq"][0], cos, sin,
                p["q_norm"][0], tm=tm)
    if ckv_past is None:
        key = ("ffn2_w_in_bf16", 1)
        o, cast_out = _attention(q, k_sh, v_sh, batch=batch, t_q=seq, t_kv=seq, heads=4, tq=512,
                                 cast=() if key in p else ((p["ffn2_w_in"], 1),))
        if cast_out:
            p[key] = cast_out[0]
    else:
        o = _kv_attention(segments, p["w_uk"], p["w_uv"], p["k_norm"], q, batch=batch)
    h = _mm_res_and_ffn2(o, w_o[0], h, 1, p)

    return (h.reshape(batch, seq, d), s_fin[None],
            c_new.reshape(batch, seq, KV_LORA), kr_new.reshape(batch, seq, QK_ROPE))


def kernel(x_prompt, x_sample, state_ret, cache_ckv, cache_krope, ffn1_norm, ffn1_w_in, ffn1_w_out, mix_norm, ffn2_norm, ffn2_w_in, ffn2_w_out, ret_w_in, ret_gn_gain, ret_w_out, kv_norm, w_dkv, kv_lat_norm, w_ukv, k_norm, w_dq, q_lat_norm, w_uq, q_norm, w_o):
    bf = lambda w: w.astype(BF16)
    w_ukv_h = w_ukv.reshape(KV_LORA, MLA_HEADS, QK_NOPE + V_HEAD)
    w_uk = w_ukv_h[:, :, :QK_NOPE].reshape(KV_LORA, MLA_HEADS * QK_NOPE)
    w_uv = w_ukv_h[:, :, QK_NOPE:].reshape(KV_LORA, MLA_HEADS * V_HEAD)
    half = QK_ROPE // 2
    w_uq_h = w_uq.reshape(w_uq.shape[0], Q_LORA, MLA_HEADS, QK_HEAD)
    w_lo = w_uq_h[..., QK_NOPE:QK_NOPE + half]
    w_hi = w_uq_h[..., QK_NOPE + half:]
    w_uq_p = jnp.concatenate([w_uq_h[..., :QK_NOPE], w_lo, w_hi, w_hi, w_lo], axis=-1)
    w_uq_p = w_uq_p.reshape(w_uq.shape[0], Q_LORA, MLA_HEADS * Q_SLOT)
    p = dict(ffn1_norm=ffn1_norm, ffn1_w_in=ffn1_w_in, ffn1_w_out=ffn1_w_out,
             ffn1_w_in_l0=bf(ffn1_w_in[:1]), ffn1_w_out_l0=bf(ffn1_w_out[:1]),
             mix_norm=mix_norm, ffn2_norm=ffn2_norm, ffn2_w_in=ffn2_w_in, ffn2_w_out=ffn2_w_out,
             ret_w_in=ret_w_in, ret_gn_gain=ret_gn_gain, ret_w_out=ret_w_out,
             kv_norm=kv_norm, w_dkv=bf(w_dkv), kv_lat_norm=kv_lat_norm,
             w_uk=bf(w_uk), w_uv=bf(w_uv), k_norm=k_norm,
             w_dq=bf(w_dq), q_lat_norm=q_lat_norm, w_uq=bf(w_uq_p), q_norm=q_norm, w_o=w_o)
    pos_prompt = jnp.arange(x_prompt.shape[1])
    past = cache_ckv.shape[1]
    pos_sample = past + jnp.arange(x_sample.shape[1])
    y_p, ret_p, ckv_p, kr_p = _trunk(x_prompt, pos_prompt, None, None, None, p)
    y_s, ret_s, ckv_s, kr_s = _trunk(x_sample, pos_sample, state_ret, cache_ckv, cache_krope, p)
    return (y_p, y_s, ret_p, ckv_p, kr_p, ret_s, ckv_s, kr_s)
```
